```python
import math
import jax, jax.numpy as jnp
from jax import lax
import numpy as np

D_MODEL = 1024
BATCH = 8
SEQ = 2048
DEPTH = 2
DEC_BATCH = 128
DEC_SEQ = 8
PAST_LEN = 16384
PAGE_SIZE = 128

D_MIX = D_MODEL
DN_HEADS = 4
DN_DK = 128
DN_DV = 128
DN_WIDTH = DN_HEADS * DN_DV
LRU_WIDTH = D_MIX - DN_WIDTH
LRU_BLOCKS = 8
LRU_BLOCK = LRU_WIDTH // LRU_BLOCKS
LRU_C = 8.0
CONV_W = 4
DN_CHUNK = 64
D_FF = 2816
N_EXPERTS = 8
TOP_K = 2
D_FF_EXPERT = 3584
DEEPNORM_ALPHA = (2 * DEPTH) ** 0.25
DEEPNORM_BETA = (8 * DEPTH) ** -0.25
LN_EPS = 1e-5
NORM_EPS = 1e-6
N_MOD = 6
K_OFF = DN_HEADS * DN_DK
V_OFF = 2 * DN_HEADS * DN_DK
DN_QKV = V_OFF + DN_WIDTH
A_OFF = DN_QKV
B_OFF = A_OFF + DN_HEADS
Z_OFF = B_OFF + DN_HEADS
X_OFF = Z_OFF + DN_WIDTH
Y_OFF = X_OFF + LRU_WIDTH
IN_COLS = Y_OFF + LRU_WIDTH

kernel_name = 'hybrid_gdn_rglru_adaln_step'


def layer_norm(x, g, b):
    xf = x.astype(jnp.float32)
    mu = jnp.mean(xf, -1, keepdims=True)
    var = jnp.mean(jnp.square(xf - mu), -1, keepdims=True)
    return ((xf - mu) * lax.rsqrt(var + LN_EPS) * g + b).astype(x.dtype)


def l2norm(x):
    return x * lax.rsqrt(jnp.sum(x * x, -1, keepdims=True) + NORM_EPS)


def causal_conv(u, buf, w):
    t = u.shape[1]
    full = jnp.concatenate([buf.astype(u.dtype), u], axis=1)
    out = full[:, CONV_W - 1:] * w[CONV_W - 1]
    for j in range(CONV_W - 1):
        out = out + full[:, j:j + t] * w[j]
    return out, full[:, t:]


def gated_delta_rule(q, k, v, g, beta, s0):
    b, t = q.shape[0], q.shape[1]
    c = min(DN_CHUNK, t)
    n = -(-t // c)
    pad = n * c - t

    def blocks(a):
        a = jnp.pad(a, [(0, 0), (0, pad)] + [(0, 0)] * (a.ndim - 2))
        a = a.reshape((b, n, c) + a.shape[2:])
        return jnp.moveaxis(jnp.moveaxis(a, 1, 0), 2, 3)

    qb, kb, vb, gb, bb = blocks(q), blocks(k), blocks(v), blocks(g), blocks(beta)
    gc = jnp.cumsum(gb, axis=-1)
    incl = jnp.tril(jnp.ones((c, c), dtype=bool))
    strict = jnp.tril(jnp.ones((c, c), dtype=bool), -1)
    diff = gc[..., :, None] - gc[..., None, :]
    decay = jnp.where(incl, jnp.exp(jnp.where(incl, diff, 0.0)), 0.0)
    k_beta = kb * bb[..., None]
    v_beta = vb * bb[..., None]
    lmat = jnp.where(strict, jnp.einsum('nbhid,nbhjd->nbhij', k_beta, kb) * decay, 0.0)
    eye = jnp.eye(c, dtype=lmat.dtype)
    tmat = lax.linalg.triangular_solve(eye + lmat, jnp.broadcast_to(eye, lmat.shape),
                                       left_side=True, lower=True)
    u = jnp.einsum('nbhij,nbhje->nbhie', tmat, v_beta)
    w = jnp.einsum('nbhij,nbhjd->nbhid', tmat, k_beta * jnp.exp(gc)[..., None])
    qk = jnp.einsum('nbhid,nbhjd->nbhij', qb, kb) * decay

    def step(s, blk):
        q_c, k_c, u_c, w_c, g_c, qk_c = blk
        v_new = u_c - jnp.einsum('bhcd,bhde->bhce', w_c, s)
        o = (jnp.einsum('bhcd,bhde->bhce', q_c * jnp.exp(g_c)[..., None], s)
             + jnp.einsum('bhij,bhje->bhie', qk_c, v_new))
        g_last = g_c[..., -1]
        k_dec = k_c * jnp.exp(g_last[..., None] - g_c)[..., None]
        s = s * jnp.exp(g_last)[..., None, None] + jnp.einsum('bhcd,bhce->bhde', k_dec, v_new)
        return s, o

    s_final, o = lax.scan(step, s0, (qb, kb, u, w, gc, qk))
    o = jnp.moveaxis(o, 0, 1)
    o = jnp.swapaxes(o, 2, 3).reshape(b, n * c, o.shape[2], o.shape[-1])[:, :t]
    return o, s_final


def rg_lru(xc, h0, w_r, b_r, w_i, b_i, lam):
    b, t, _ = xc.shape
    xg = xc.reshape(b, t, LRU_BLOCKS, LRU_BLOCK)
    r = jax.nn.sigmoid(jnp.einsum('btnc,ncd->btnd', xg, w_r).reshape(b, t, LRU_WIDTH) + b_r)
    i = jax.nn.sigmoid(jnp.einsum('btnc,ncd->btnd', xg, w_i).reshape(b, t, LRU_WIDTH) + b_i)
    log_a = -LRU_C * r * jax.nn.softplus(-lam)
    a = jnp.exp(log_a)
    inp = jnp.sqrt(-jnp.expm1(2.0 * log_a)) * (i * xc)

    def combine(left, right):
        a_l, b_l = left
        a_r, b_rr = right
        return a_l * a_r, a_r * b_l + b_rr

    a_cum, b_cum = lax.associative_scan(combine, (a, inp), axis=1)
    h = a_cum * h0[:, None, :] + b_cum
    return h, h[:, -1]


def token_mix(h, conv_dn, s_dn, conv_lru, s_lru, p, l):
    b, t, _ = h.shape
    f32 = jnp.float32
    proj = h @ p['w_in'][l]
    qkv = proj[..., :DN_QKV]
    a_in = proj[..., A_OFF:A_OFF + DN_HEADS].astype(f32)
    b_in = proj[..., B_OFF:B_OFF + DN_HEADS].astype(f32)
    z = proj[..., Z_OFF:Z_OFF + DN_WIDTH].astype(f32).reshape(b, t, DN_HEADS, DN_DV)
    x_lru = proj[..., X_OFF:X_OFF + LRU_WIDTH]
    y_lru = proj[..., Y_OFF:Y_OFF + LRU_WIDTH].astype(f32)
    qkv_c, new_conv_dn = causal_conv(qkv, conv_dn, p['dn_conv_w'][l])
    qkv_c = jax.nn.silu(qkv_c.astype(f32))
    q = l2norm(qkv_c[..., :K_OFF].reshape(b, t, DN_HEADS, DN_DK)) * (DN_DK ** -0.5)
    k = l2norm(qkv_c[..., K_OFF:V_OFF].reshape(b, t, DN_HEADS, DN_DK))
    v = qkv_c[..., V_OFF:DN_QKV].reshape(b, t, DN_HEADS, DN_DV)
    beta = jax.nn.sigmoid(b_in)
    g = -jnp.exp(p['dn_a_log'][l].astype(f32)) * jax.nn.softplus(a_in + p['dn_dt_bias'][l])
    o, s_dn_new = gated_delta_rule(q, k, v, g, beta, s_dn.astype(f32))
    o = o * lax.rsqrt(jnp.mean(o * o, -1, keepdims=True) + NORM_EPS) * p['dn_norm_w'][l] * jax.nn.silu(z)
    o_a = o.reshape(b, t, DN_WIDTH)
    xc, new_conv_lru = causal_conv(x_lru, conv_lru, p['lru_conv_w'][l])
    xc = xc.astype(f32) + p['lru_conv_b'][l]
    hs, h_last = rg_lru(xc, s_lru.astype(f32), p['lru_w_r'][l], p['lru_b_r'][l],
                        p['lru_w_i'][l], p['lru_b_i'][l], p['lru_lambda'][l])
    o_b = hs * jax.nn.gelu(y_lru)
    mixed = jnp.concatenate([o_a, o_b], axis=-1).astype(h.dtype) @ p['w_out'][l]
    return mixed, new_conv_dn, s_dn_new.astype(h.dtype), new_conv_lru, h_last.astype(h.dtype)


def swiglu(h, w_gate, w_up, w_down):
    return (jax.nn.silu(h @ w_gate) * (h @ w_up)) @ w_down


def moe_swiglu(h, w_router, w_gate, w_up, w_down):
    logits = (h @ w_router).astype(jnp.float32)
    top_v, top_i = lax.top_k(logits, TOP_K)
    top_w = jax.nn.softmax(top_v, axis=-1)
    gates = jnp.sum(jax.nn.one_hot(top_i, N_EXPERTS, dtype=jnp.float32) * top_w[..., None],
                    axis=-2).astype(h.dtype)
    y = jnp.zeros_like(h)
    for e in range(N_EXPERTS):
        y = y + gates[..., e:e + 1] * swiglu(h, w_gate[e], w_up[e], w_down[e])
    return y


def trunk(x, c, conv_dn, s_dn, conv_lru, s_lru, p):
    new_conv_dn, new_dn, new_conv_lru, new_lru = [], [], [], []
    for l in range(DEPTH):
        mod = (jax.nn.silu(c) @ p['w_ada'][l] + p['b_ada'][l]).reshape(c.shape[0], N_MOD, 1, D_MODEL)
        shift1, scale1, gate1 = mod[:, 0], mod[:, 1], mod[:, 2]
        shift2, scale2, gate2 = mod[:, 3], mod[:, 4], mod[:, 5]
        h = x * (1 + scale1) + shift1
        mixed, cd, sd, cl, sl = token_mix(h, conv_dn[l], s_dn[l], conv_lru[l], s_lru[l], p, l)
        x = layer_norm(DEEPNORM_ALPHA * x + (1 + gate1) * mixed, p['ln1_g'][l], p['ln1_b'][l])
        h = x * (1 + scale2) + shift2
        if l % 2 == 0:
            j = l // 2
            ff = swiglu(h, p['ffn_w_gate'][j], p['ffn_w_up'][j], p['ffn_w_down'][j])
        else:
            j = l // 2
            ff = moe_swiglu(h, p['moe_w_router'][j], p['moe_w_gate'][j], p['moe_w_up'][j], p['moe_w_down'][j])
        x = layer_norm(DEEPNORM_ALPHA * x + (1 + gate2) * ff, p['ln2_g'][l], p['ln2_b'][l])
        new_conv_dn.append(cd)
        new_dn.append(sd)
        new_conv_lru.append(cl)
        new_lru.append(sl)
    return x, jnp.stack(new_conv_dn), jnp.stack(new_dn), jnp.stack(new_conv_lru), jnp.stack(new_lru)


def setup_inputs(seed: int = 0) -> dict:
    key = jax.random.key(seed)
    ks = iter(jax.random.split(key, 48))

    def nrm(shape, s):
        return jax.random.normal(next(ks), shape, jnp.float32) * s

    n_dense = (DEPTH + 1) // 2
    n_moe = DEPTH // 2
    x_prompt = nrm((BATCH, SEQ, D_MODEL), 1.0)
    x_sample = nrm((DEC_BATCH, DEC_SEQ, D_MODEL), 1.0)
    cache_dn_conv = nrm((DEPTH, DEC_BATCH, CONV_W - 1, DN_QKV), 1.0)
    state_dn = nrm((DEPTH, DEC_BATCH, DN_HEADS, DN_DK, DN_DV), 0.1)
    cache_lru_conv = nrm((DEPTH, DEC_BATCH, CONV_W - 1, LRU_WIDTH), 1.0)
    state_lru = nrm((DEPTH, DEC_BATCH, LRU_WIDTH), 0.5)
    c_prompt = nrm((BATCH, D_MODEL), 1.0)
    c_sample = nrm((DEC_BATCH, D_MODEL), 1.0)
    w_ada = nrm((DEPTH, D_MODEL, N_MOD * D_MODEL), D_MODEL ** -0.5)
    b_ada = nrm((DEPTH, N_MOD * D_MODEL), 0.01)
    col_scale = jnp.ones((IN_COLS,), jnp.float32).at[V_OFF:DN_QKV].set(DEEPNORM_BETA)
    w_in = nrm((DEPTH, D_MODEL, IN_COLS), D_MODEL ** -0.5) * col_scale
    dn_conv_w = nrm((DEPTH, CONV_W, DN_QKV), CONV_W ** -0.5)
    dn_a_log = jnp.log(jax.random.uniform(next(ks), (DEPTH, DN_HEADS), jnp.float32, 1.0, 16.0))
    dt = jnp.exp(jax.random.uniform(next(ks), (DEPTH, DN_HEADS), jnp.float32,
                                    math.log(1e-3), math.log(1e-1)))
    dn_dt_bias = dt + jnp.log(-jnp.expm1(-dt))
    dn_norm_w = 1.0 + nrm((DEPTH, DN_DV), 0.02)
    lru_conv_w = nrm((DEPTH, CONV_W, LRU_WIDTH), CONV_W ** -0.5)
    lru_conv_b = nrm((DEPTH, LRU_WIDTH), 0.01)
    lru_w_r = nrm((DEPTH, LRU_BLOCKS, LRU_BLOCK, LRU_BLOCK), LRU_BLOCK ** -0.5)
    lru_b_r = nrm((DEPTH, LRU_WIDTH), 0.01)
    lru_w_i = nrm((DEPTH, LRU_BLOCKS, LRU_BLOCK, LRU_BLOCK), LRU_BLOCK ** -0.5)
    lru_b_i = nrm((DEPTH, LRU_WIDTH), 0.01)
    a_c = jax.random.uniform(next(ks), (DEPTH, LRU_WIDTH), jnp.float32, 0.9, 0.999)
    a0 = a_c ** (1.0 / LRU_C)
    lru_lambda = jnp.log(a0) - jnp.log1p(-a0)
    w_out = nrm((DEPTH, D_MIX, D_MODEL), D_MIX ** -0.5 * DEEPNORM_BETA)
    ln1_g = 1.0 + nrm((DEPTH, D_MODEL), 0.02)
    ln1_b = nrm((DEPTH, D_MODEL), 0.01)
    ln2_g = 1.0 + nrm((DEPTH, D_MODEL), 0.02)
    ln2_b = nrm((DEPTH, D_MODEL), 0.01)
    ffn_w_gate = nrm((n_dense, D_MODEL, D_FF), D_MODEL ** -0.5)
    ffn_w_up = nrm((n_dense, D_MODEL, D_FF), D_MODEL ** -0.5 * DEEPNORM_BETA)
    ffn_w_down = nrm((n_dense, D_FF, D_MODEL), D_FF ** -0.5 * DEEPNORM_BETA)
    moe_w_router = nrm((n_moe, D_MODEL, N_EXPERTS), D_MODEL ** -0.5)
    moe_w_gate = nrm((n_moe, N_EXPERTS, D_MODEL, D_FF_EXPERT), D_MODEL ** -0.5)
    moe_w_up = nrm((n_moe, N_EXPERTS, D_MODEL, D_FF_EXPERT), D_MODEL ** -0.5 * DEEPNORM_BETA)
    moe_w_down = nrm((n_moe, N_EXPERTS, D_FF_EXPERT, D_MODEL), D_FF_EXPERT ** -0.5 * DEEPNORM_BETA)
    return {'x_prompt': x_prompt, 'x_sample': x_sample,
            'cache_dn_conv': cache_dn_conv, 'state_dn': state_dn,
            'cache_lru_conv': cache_lru_conv, 'state_lru': state_lru,
            'c_prompt': c_prompt, 'c_sample': c_sample,
            'w_ada': w_ada, 'b_ada': b_ada, 'w_in': w_in,
            'dn_conv_w': dn_conv_w, 'dn_a_log': dn_a_log, 'dn_dt_bias': dn_dt_bias, 'dn_norm_w': dn_norm_w,
            'lru_conv_w': lru_conv_w, 'lru_conv_b': lru_conv_b, 'lru_w_r': lru_w_r, 'lru_b_r': lru_b_r,
            'lru_w_i': lru_w_i, 'lru_b_i': lru_b_i, 'lru_lambda': lru_lambda, 'w_out': w_out,
            'ln1_g': ln1_g, 'ln1_b': ln1_b, 'ln2_g': ln2_g, 'ln2_b': ln2_b,
            'ffn_w_gate': ffn_w_gate, 'ffn_w_up': ffn_w_up, 'ffn_w_down': ffn_w_down,
            'moe_w_router': moe_w_router, 'moe_w_gate': moe_w_gate, 'moe_w_up': moe_w_up,
            'moe_w_down': moe_w_down}


def reference(x_prompt, x_sample, cache_dn_conv, state_dn, cache_lru_conv, state_lru, c_prompt, c_sample,
              w_ada, b_ada, w_in, dn_conv_w, dn_a_log, dn_dt_bias, dn_norm_w,
              lru_conv_w, lru_conv_b, lru_w_r, lru_b_r, lru_w_i, lru_b_i, lru_lambda, w_out,
              ln1_g, ln1_b, ln2_g, ln2_b, ffn_w_gate, ffn_w_up, ffn_w_down,
              moe_w_router, moe_w_gate, moe_w_up, moe_w_down):
    p = {'w_ada': w_ada, 'b_ada': b_ada, 'w_in': w_in,
         'dn_conv_w': dn_conv_w, 'dn_a_log': dn_a_log, 'dn_dt_bias': dn_dt_bias, 'dn_norm_w': dn_norm_w,
         'lru_conv_w': lru_conv_w, 'lru_conv_b': lru_conv_b, 'lru_w_r': lru_w_r, 'lru_b_r': lru_b_r,
         'lru_w_i': lru_w_i, 'lru_b_i': lru_b_i, 'lru_lambda': lru_lambda, 'w_out': w_out,
         'ln1_g': ln1_g, 'ln1_b': ln1_b, 'ln2_g': ln2_g, 'ln2_b': ln2_b,
         'ffn_w_gate': ffn_w_gate, 'ffn_w_up': ffn_w_up, 'ffn_w_down': ffn_w_down,
         'moe_w_router': moe_w_router, 'moe_w_gate': moe_w_gate, 'moe_w_up': moe_w_up,
         'moe_w_down': moe_w_down}
    bp = x_prompt.shape[0]
    dtp = x_prompt.dtype
    z_conv_dn = jnp.zeros((DEPTH, bp, CONV_W - 1, DN_QKV), dtp)
    z_dn = jnp.zeros((DEPTH, bp, DN_HEADS, DN_DK, DN_DV), dtp)
    z_conv_lru = jnp.zeros((DEPTH, bp, CONV_W - 1, LRU_WIDTH), dtp)
    z_lru = jnp.zeros((DEPTH, bp, LRU_WIDTH), dtp)
    y_prompt, p_conv_dn, p_dn, p_conv_lru, p_lru = trunk(
        x_prompt, c_prompt, z_conv_dn, z_dn, z_conv_lru, z_lru, p)
    y_sample, s_conv_dn, s_dn, s_conv_lru, s_lru = trunk(
        x_sample, c_sample, cache_dn_conv, state_dn, cache_lru_conv, state_lru, p)
    return (y_prompt, y_sample, p_conv_dn, p_dn, p_conv_lru, p_lru, s_conv_dn, s_dn, s_conv_lru, s_lru)
```

```python
import functools
import math

import jax
import jax.numpy as jnp
from jax import lax
from jax.experimental import pallas as pl
from jax.experimental.pallas import tpu as pltpu

F32 = jnp.float32
BF16 = jnp.bfloat16

DN_HEADS = 4
DN_DK = 128
DN_DV = 128
DN_WIDTH = DN_HEADS * DN_DV
DN_QKV = 3 * DN_WIDTH
LRU_BLOCKS = 8
LRU_C = 8.0
CONV_W = 4
N_MOD = 6
TOP_K = 2
LN_EPS = 1e-5
NORM_EPS = 1e-6

SUBLANES = 8
LANES = 128
MXU_DIM = 256
VMEM_LIMIT_BYTES = 56 * 1024 * 1024

PROMPT_CHUNK = 64
INV_BASE_BLOCK = 16


def _sigmoid(x):
    return 1.0 / (1.0 + jnp.exp(-x))


def _silu(x):
    return x * _sigmoid(x)


def _softplus(x):
    return jnp.maximum(x, 0.0) + jnp.log1p(jnp.exp(-jnp.abs(x)))


def _gelu_tanh(x):
    return 0.5 * x * (1.0 + jnp.tanh(math.sqrt(2.0 / math.pi) * (x + 0.044715 * (x * x * x))))


def _mm(a, b):
    return jnp.dot(a.astype(BF16), b.astype(BF16), preferred_element_type=F32)


def _layer_norm(x, g, b):
    mu = jnp.mean(x, axis=-1, keepdims=True)
    xc = x - mu
    var = jnp.mean(xc * xc, axis=-1, keepdims=True)
    return xc * lax.rsqrt(var + LN_EPS) * g + b


def _params(sem):
    return pltpu.CompilerParams(dimension_semantics=sem, vmem_limit_bytes=VMEM_LIMIT_BYTES)


def _mod_kernel(c_ref, w_ref, b_ref, o_ref):
    sc = _silu(c_ref[...])
    o_ref[0] = _mm(sc, w_ref[0]) + b_ref[0]


def _modulation(c_all, w_ada, b_ada):
    depth, d, n = w_ada.shape
    rows = c_all.shape[0]
    tn = 1536 if n % 1536 == 0 else n
    return pl.pallas_call(
        _mod_kernel,
        out_shape=jax.ShapeDtypeStruct((depth, rows, n), F32),
        grid=(depth, n // tn),
        in_specs=[
            pl.BlockSpec((rows, d), lambda l, j: (0, 0)),
            pl.BlockSpec((1, d, tn), lambda l, j: (l, 0, j)),
            pl.BlockSpec((1, 1, tn), lambda l, j: (l, 0, j)),
        ],
        out_specs=pl.BlockSpec((1, rows, tn), lambda l, j: (l, 0, j)),
        compiler_params=_params(("arbitrary", "arbitrary")),
        name="adaln_modulation",
    )(c_all, w_ada, b_ada.reshape(depth, 1, n))


def _causal_conv(u, hist, w, tt):
    rows, c = u.shape
    if tt == SUBLANES:
        hfull = hist.reshape(rows, c)
    else:
        hfull = jnp.concatenate([hist.reshape(SUBLANES, c), jnp.zeros((rows - SUBLANES, c), F32)], axis=0)
    t = lax.broadcasted_iota(jnp.int32, (rows, c), 0) & (tt - 1)
    out = u * w[CONV_W - 1:CONV_W, :]
    for j in range(1, CONV_W):
        prev = jnp.where(t >= j, pltpu.roll(u, j, 0), pltpu.roll(hfull, (rows + j - SUBLANES) % rows, 0))
        out = out + prev * w[CONV_W - 1 - j:CONV_W - j, :]
    return out


def _l2norm_heads(x, scale):
    outs = []
    for h in range(DN_HEADS):
        xh = x[:, h * DN_DK:(h + 1) * DN_DK]
        ss = jnp.sum(xh * xh, axis=-1, keepdims=True)
        outs.append(xh * (lax.rsqrt(ss + NORM_EPS) * scale))
    return jnp.concatenate(outs, axis=-1)


def _inproj_kernel(x_ref, mod_ref, hdn_ref, hlru_ref, w_ref, cwdn_ref, cwlru_ref, cblru_ref,
                   alog_ref, dtb_ref, wr_ref, wi_ref, br_ref, bi_ref, lam_ref,
                   qkv_ref, gz_ref, lru_ref, gb_ref, tdn_ref, tlru_ref,
                   hdn_sc, hlru_sc, *, nb, tt):
    d = x_ref.shape[-1]
    lw = hlru_ref.shape[-1]
    rows = nb * tt

    @pl.when(pl.program_id(1) == 0)
    def _():
        hdn_sc[...] = hdn_ref[...]
        hlru_sc[...] = hlru_ref[...]

    m = mod_ref[...]
    shift = m[:, :, 0:d]
    scale = m[:, :, d:2 * d]
    h = (x_ref[...] * (1.0 + scale) + shift).reshape(rows, d)
    proj = _mm(h, w_ref[...])
    u_dn = proj[:, 0:DN_QKV]
    z = proj[:, DN_QKV:DN_QKV + DN_WIDTH]
    u_lru = proj[:, DN_QKV + DN_WIDTH:DN_QKV + DN_WIDTH + lw]
    y = proj[:, DN_QKV + DN_WIDTH + lw:DN_QKV + DN_WIDTH + 2 * lw]
    ab = proj[:, DN_QKV + DN_WIDTH + 2 * lw:]

    qkv_c = _silu(_causal_conv(u_dn, hdn_sc[...], cwdn_ref[...], tt))
    q = _l2norm_heads(qkv_c[:, 0:DN_WIDTH], DN_DK ** -0.5)
    k = _l2norm_heads(qkv_c[:, DN_WIDTH:2 * DN_WIDTH], 1.0)
    v = qkv_c[:, 2 * DN_WIDTH:DN_QKV]
    qkv_ref[...] = jnp.concatenate([q, k, v], axis=-1).reshape(nb, tt, DN_QKV)
    gz_ref[...] = _silu(z).reshape(nb, tt, DN_WIDTH)
    lane = lax.broadcasted_iota(jnp.int32, ab.shape, 1)
    g_full = -jnp.exp(alog_ref[...]) * _softplus(ab + dtb_ref[...])
    gb_ref[...] = jnp.where(lane < DN_HEADS, g_full, _sigmoid(ab)).reshape(nb, tt, LANES)

    xc = _causal_conv(u_lru, hlru_sc[...], cwlru_ref[...], tt) + cblru_ref[...]
    half = lw // 2
    r_pre = jnp.concatenate([_mm(xc[:, :half], wr_ref[0]), _mm(xc[:, half:], wr_ref[1])], axis=-1)
    i_pre = jnp.concatenate([_mm(xc[:, :half], wi_ref[0]), _mm(xc[:, half:], wi_ref[1])], axis=-1)
    r = _sigmoid(r_pre + br_ref[...])
    i = _sigmoid(i_pre + bi_ref[...])
    log_a = -LRU_C * r * _softplus(-lam_ref[...])
    a = jnp.exp(log_a)
    th = jnp.tanh(log_a)
    inp = jnp.sqrt(-2.0 * th / (1.0 - th)) * (i * xc)
    lru_ref[...] = jnp.concatenate([a, inp, _gelu_tanh(y)], axis=-1).reshape(nb, tt, 3 * lw)

    tail_dn = u_dn.reshape(nb, tt, DN_QKV)[:, tt - SUBLANES:, :]
    tail_lru = u_lru.reshape(nb, tt, lw)[:, tt - SUBLANES:, :]
    hdn_sc[...] = tail_dn
    hlru_sc[...] = tail_lru
    tdn_ref[...] = tail_dn
    tlru_ref[...] = tail_lru


def _inproj(x, mod, hist_dn, hist_lru, lw_, *, nb, tt):
    b, t, d = x.shape
    lw = hist_lru.shape[-1]
    grid = (b // nb, t // tt)
    seq_blk = lambda c: pl.BlockSpec((nb, tt, c), lambda i, j: (i, j, 0))
    per_seq = lambda r, c: pl.BlockSpec((nb, r, c), lambda i, j: (i, 0, 0))
    whole = lambda a: pl.BlockSpec(a.shape, lambda i, j: (0,) * a.ndim)
    weights = [lw_[n] for n in ("w_cat", "dn_conv_w", "lru_conv_w", "lru_conv_b", "a_log", "dt_bias",
                                "w_r", "w_i", "b_r", "b_i", "lam")]
    return pl.pallas_call(
        functools.partial(_inproj_kernel, nb=nb, tt=tt),
        out_shape=(
            jax.ShapeDtypeStruct((b, t, DN_QKV), F32),
            jax.ShapeDtypeStruct((b, t, DN_WIDTH), F32),
            jax.ShapeDtypeStruct((b, t, 3 * lw), F32),
            jax.ShapeDtypeStruct((b, t, LANES), F32),
            jax.ShapeDtypeStruct((b, SUBLANES, DN_QKV), F32),
            jax.ShapeDtypeStruct((b, SUBLANES, lw), F32),
        ),
        grid=grid,
        in_specs=[seq_blk(d), per_seq(1, N_MOD * d), per_seq(SUBLANES, DN_QKV), per_seq(SUBLANES, lw)]
        + [whole(a) for a in weights],
        out_specs=(seq_blk(DN_QKV), seq_blk(DN_WIDTH), seq_blk(3 * lw), seq_blk(LANES),
                   per_seq(SUBLANES, DN_QKV), per_seq(SUBLANES, lw)),
        scratch_shapes=[pltpu.VMEM((nb, SUBLANES, DN_QKV), F32), pltpu.VMEM((nb, SUBLANES, lw), F32)],
        compiler_params=_params(("arbitrary", "arbitrary")),
        name="in_projection",
    )(x, mod, hist_dn, hist_lru, *weights)


def _unit_lower_inverse(l, row, col, chunk, base):
    def same_block(s):
        k = s.bit_length() - 1
        return (row >> k) == (col >> k)

    eye = jnp.where(row == col, 1.0, 0.0)
    dblk = jnp.where(same_block(base), l, 0.0)
    inv = eye - dblk
    power = dblk
    p = 2
    while p < base:
        power = _mm(power, power)
        inv = inv + _mm(inv, power)
        p *= 2
    s = base
    while s < chunk:
        off = jnp.where(same_block(2 * s) & jnp.logical_not(same_block(s)), l, 0.0)
        inv = inv - _mm(inv, _mm(off, inv))
        s *= 2
    return inv


def _mixer_kernel(qkv_ref, gz_ref, lru_ref, gb_ref, x_ref, mod_ref, s0_ref, h0_ref,
                  wout_ref, nw_ref, lng_ref, lnb_ref,
                  y_ref, sout_ref, hout_ref, *, nb, tt, chunk, alpha):
    d = x_ref.shape[-1]
    lw = h0_ref.shape[-1]
    rows = nb * tt
    n_chunks = rows // chunk
    chunks_per_seq = tt // chunk
    log_chunk = chunk.bit_length() - 1

    @pl.when(pl.program_id(1) == 0)
    def _():
        sout_ref[...] = s0_ref[...]
        hout_ref[...] = h0_ref[...]

    qkv = qkv_ref[...].reshape(rows, DN_QKV)
    gz = gz_ref[...].reshape(rows, DN_WIDTH)
    gb = gb_ref[...].reshape(rows, LANES)

    row = lax.broadcasted_iota(jnp.int32, (rows, rows), 0)
    col = lax.broadcasted_iota(jnp.int32, (rows, rows), 1)
    incl = ((row >> log_chunk) == (col >> log_chunk)) & (col <= row)
    strict = incl & (col < row)
    gc = jnp.dot(jnp.where(incl, 1.0, 0.0), gb, precision=lax.Precision.HIGHEST, preferred_element_type=F32)
    gc_t = gc.T

    o_heads = []
    for h in range(DN_HEADS):
        qh = qkv[:, h * DN_DK:(h + 1) * DN_DK]
        kh = qkv[:, DN_WIDTH + h * DN_DK:DN_WIDTH + (h + 1) * DN_DK]
        vh = qkv[:, 2 * DN_WIDTH + h * DN_DV:2 * DN_WIDTH + (h + 1) * DN_DV]
        gcol = gc[:, h:h + 1]
        grow = gc_t[h:h + 1, :]
        beta = gb[:, DN_HEADS + h:DN_HEADS + h + 1]
        decay = jnp.where(incl, jnp.exp(jnp.where(incl, gcol - grow, 0.0)), 0.0)
        kb = kh * beta
        vb = vh * beta
        qk_kk = lax.dot_general(jnp.concatenate([qh, kb], axis=0).astype(BF16), kh.astype(BF16),
                                (((1,), (1,)), ((), ())), preferred_element_type=F32)
        qk = qk_kk[:rows] * decay
        lmat = jnp.where(strict, qk_kk[rows:] * decay, 0.0)
        tmat = _unit_lower_inverse(lmat, row, col, chunk, min(INV_BASE_BLOCK, chunk))
        egc = jnp.exp(gcol)
        uw = _mm(tmat, jnp.concatenate([vb, kb * egc], axis=-1))
        u = uw[:, :DN_DV]
        w = uw[:, DN_DV:]
        qe = qh * egc

        v_new_parts = []
        o_inter_parts = []
        state = None
        for c in range(n_chunks):
            seq = c // chunks_per_seq
            lo, hi = c * chunk, (c + 1) * chunk
            if c % chunks_per_seq == 0:
                state = sout_ref[seq, h]
            wq = _mm(jnp.concatenate([w[lo:hi], qe[lo:hi]], axis=0), state)
            v_new = u[lo:hi] - wq[:chunk]
            v_new_parts.append(v_new)
            o_inter_parts.append(wq[chunk:])
            g_last = gcol[hi - 1:hi, :]
            k_dec = kh[lo:hi] * jnp.exp(g_last - gcol[lo:hi])
            state = state * jnp.exp(g_last) + lax.dot_general(
                k_dec.astype(BF16), v_new.astype(BF16), (((0,), (0,)), ((), ())), preferred_element_type=F32)
            if (c + 1) % chunks_per_seq == 0:
                sout_ref[seq, h] = state
        v_new_all = jnp.concatenate(v_new_parts, axis=0)
        o = jnp.concatenate(o_inter_parts, axis=0) + _mm(qk, v_new_all)
        ms = jnp.mean(o * o, axis=-1, keepdims=True)
        o_heads.append(o * lax.rsqrt(ms + NORM_EPS) * nw_ref[...] * gz[:, h * DN_DV:(h + 1) * DN_DV])
    o_a = jnp.concatenate(o_heads, axis=-1)

    lru = lru_ref[...].reshape(rows, 3 * lw)
    a = lru[:, 0:lw]
    bacc = lru[:, lw:2 * lw]
    gy = lru[:, 2 * lw:3 * lw]
    t = lax.broadcasted_iota(jnp.int32, (rows, lw), 0) & (tt - 1)
    s = 1
    while s < tt:
        keep = t >= s
        a_prev = jnp.where(keep, pltpu.roll(a, s, 0), 1.0)
        b_prev = jnp.where(keep, pltpu.roll(bacc, s, 0), 0.0)
        bacc = a * b_prev + bacc
        a = a * a_prev
        s *= 2
    hs = (a.reshape(nb, tt, lw) * hout_ref[...] + bacc.reshape(nb, tt, lw))
    hout_ref[...] = hs[:, tt - 1:tt, :]
    o_b = hs.reshape(rows, lw) * gy

    mixed = _mm(jnp.concatenate([o_a, o_b], axis=-1), wout_ref[...]).reshape(nb, tt, d)
    gate = mod_ref[...][:, :, 2 * d:3 * d]
    y_ref[...] = _layer_norm(alpha * x_ref[...] + (1.0 + gate) * mixed, lng_ref[...], lnb_ref[...])


def _mixer(qkv, gz, lru, gb, x, mod, s0, h0, lw_, *, nb, tt, chunk, alpha):
    b, t, d = x.shape
    lw = h0.shape[-1]
    grid = (b // nb, t // tt)
    seq_blk = lambda c: pl.BlockSpec((nb, tt, c), lambda i, j: (i, j, 0))
    whole = lambda a: pl.BlockSpec(a.shape, lambda i, j: (0,) * a.ndim)
    state_spec = pl.BlockSpec((nb, DN_HEADS, DN_DK, DN_DV), lambda i, j: (i, 0, 0, 0))
    h_spec = pl.BlockSpec((nb, 1, lw), lambda i, j: (i, 0, 0))
    weights = [lw_[n] for n in ("w_out", "dn_norm_w", "ln1_g", "ln1_b")]
    return pl.pallas_call(
        functools.partial(_mixer_kernel, nb=nb, tt=tt, chunk=chunk, alpha=alpha),
        out_shape=(
            jax.ShapeDtypeStruct((b, t, d), F32),
            jax.ShapeDtypeStruct(s0.shape, F32),
            jax.ShapeDtypeStruct(h0.shape, F32),
        ),
        grid=grid,
        in_specs=[seq_blk(DN_QKV), seq_blk(DN_WIDTH), seq_blk(3 * lw), seq_blk(LANES), seq_blk(d),
                  pl.BlockSpec((nb, 1, N_MOD * d), lambda i, j: (i, 0, 0)), state_spec, h_spec]
        + [whole(a) for a in weights],
        out_specs=(seq_blk(d), state_spec, h_spec),
        compiler_params=_params(("arbitrary", "arbitrary")),
        name="token_mixer",
    )(qkv, gz, lru, gb, x, mod, s0, h0, *weights)


def _ffn_kernel(x_ref, mod_ref, wg_ref, wu_ref, wd_ref, lng_ref, lnb_ref, y_ref, h_sc, acc_sc, *, alpha):
    nb, tt, d = x_ref.shape
    f = pl.program_id(1)

    @pl.when(f == 0)
    def _():
        m = mod_ref[...]
        h = x_ref[...] * (1.0 + m[:, :, 4 * d:5 * d]) + m[:, :, 3 * d:4 * d]
        h_sc[...] = h.reshape(nb * tt, d).astype(BF16)
        acc_sc[...] = jnp.zeros_like(acc_sc)

    h = h_sc[...]
    act = _silu(jnp.dot(h, wg_ref[...], preferred_element_type=F32)) * jnp.dot(
        h, wu_ref[...], preferred_element_type=F32)
    acc_sc[...] += _mm(act, wd_ref[...])

    @pl.when(f == pl.num_programs(1) - 1)
    def _():
        gate = mod_ref[...][:, :, 5 * d:6 * d]
        ff = acc_sc[...].reshape(nb, tt, d)
        y_ref[...] = _layer_norm(alpha * x_ref[...] + (1.0 + gate) * ff, lng_ref[...], lnb_ref[...])


def _ff_tile(ff):
    for n in (2, 4, 7, 8, 11, 14, 16, 22, 28):
        if ff % n == 0 and (ff // n) % LANES == 0 and ff // n <= 2048:
            return ff // n
    return ff


def _dense_ffn(x, mod, lw_, *, nb, tt, alpha):
    b, t, d = x.shape
    ff = lw_["w_gate"].shape[-1]
    tf = _ff_tile(ff)
    grid = ((b // nb) * (t // tt), ff // tf)
    nt = t // tt
    seq_blk = pl.BlockSpec((nb, tt, d), lambda i, f: (i // nt, i % nt, 0))
    whole = lambda a: pl.BlockSpec(a.shape, lambda i, f: (0,) * a.ndim)
    return pl.pallas_call(
        functools.partial(_ffn_kernel, alpha=alpha),
        out_shape=jax.ShapeDtypeStruct((b, t, d), F32),
        grid=grid,
        in_specs=[seq_blk, pl.BlockSpec((nb, 1, N_MOD * d), lambda i, f: (i // nt, 0, 0)),
                  pl.BlockSpec((d, tf), lambda i, f: (0, f)), pl.BlockSpec((d, tf), lambda i, f: (0, f)),
                  pl.BlockSpec((tf, d), lambda i, f: (f, 0)), whole(lw_["ln2_g"]), whole(lw_["ln2_b"])],
        out_specs=seq_blk,
        scratch_shapes=[pltpu.VMEM((nb * tt, d), BF16), pltpu.VMEM((nb * tt, d), F32)],
        compiler_params=_params(("arbitrary", "arbitrary")),
        name="dense_ffn",
    )(x, mod, lw_["w_gate"], lw_["w_up"], lw_["w_down"], lw_["ln2_g"], lw_["ln2_b"])


def _top2_gates(logits, n_experts):
    lane = lax.broadcasted_iota(jnp.int32, logits.shape, 1)
    neg = jnp.float32(-jnp.inf)
    lg = jnp.where(lane < n_experts, logits, neg)
    m1 = jnp.max(lg, axis=-1, keepdims=True)
    i1 = jnp.min(jnp.where(lg == m1, lane, LANES), axis=-1, keepdims=True)
    lg2 = jnp.where(lane == i1, neg, lg)
    m2 = jnp.max(lg2, axis=-1, keepdims=True)
    i2 = jnp.min(jnp.where(lg2 == m2, lane, LANES), axis=-1, keepdims=True)
    e2 = jnp.exp(m2 - m1)
    den = 1.0 + e2
    return jnp.where(lane == i1, 1.0 / den, 0.0) + jnp.where(lane == i2, e2 / den, 0.0)


def _moe_kernel(x_ref, mod_ref, wr_ref, wg_ref, wu_ref, wd_ref, lng_ref, lnb_ref, y_ref,
                h_sc, gates_sc, acc_sc, *, alpha, n_experts):
    nb, tt, d = x_ref.shape
    e = pl.program_id(1)
    f = pl.program_id(2)

    @pl.when((e == 0) & (f == 0))
    def _():
        m = mod_ref[...]
        h = (x_ref[...] * (1.0 + m[:, :, 4 * d:5 * d]) + m[:, :, 3 * d:4 * d]).reshape(nb * tt, d)
        h_sc[...] = h.astype(BF16)
        logits = jnp.dot(h, wr_ref[...], precision=lax.Precision.HIGHEST, preferred_element_type=F32)
        gates_sc[...] = _top2_gates(logits, n_experts)
        acc_sc[...] = jnp.zeros_like(acc_sc)

    h = h_sc[...]
    lane = lax.broadcasted_iota(jnp.int32, gates_sc.shape, 1)
    gate_e = jnp.sum(jnp.where(lane == e, gates_sc[...], 0.0), axis=-1, keepdims=True)
    act = _silu(jnp.dot(h, wg_ref[0], preferred_element_type=F32)) * jnp.dot(
        h, wu_ref[0], preferred_element_type=F32)
    acc_sc[...] += gate_e * _mm(act, wd_ref[0])

    @pl.when((e == pl.num_programs(1) - 1) & (f == pl.num_programs(2) - 1))
    def _():
        gate = mod_ref[...][:, :, 5 * d:6 * d]
        ff = acc_sc[...].reshape(nb, tt, d)
        y_ref[...] = _layer_norm(alpha * x_ref[...] + (1.0 + gate) * ff, lng_ref[...], lnb_ref[...])


def _moe_ffn(x, mod, lw_, *, nb, tt, alpha):
    b, t, d = x.shape
    n_experts, _, ff = lw_["w_gate"].shape
    tf = _ff_tile(ff)
    nt = t // tt
    grid = ((b // nb) * nt, n_experts, ff // tf)
    seq_blk = pl.BlockSpec((nb, tt, d), lambda i, e, f: (i // nt, i % nt, 0))
    whole = lambda a: pl.BlockSpec(a.shape, lambda i, e, f: (0,) * a.ndim)
    return pl.pallas_call(
        functools.partial(_moe_kernel, alpha=alpha, n_experts=n_experts),
        out_shape=jax.ShapeDtypeStruct((b, t, d), F32),
        grid=grid,
        in_specs=[seq_blk, pl.BlockSpec((nb, 1, N_MOD * d), lambda i, e, f: (i // nt, 0, 0)),
                  whole(lw_["w_router"]),
                  pl.BlockSpec((1, d, tf), lambda i, e, f: (e, 0, f)),
                  pl.BlockSpec((1, d, tf), lambda i, e, f: (e, 0, f)),
                  pl.BlockSpec((1, tf, d), lambda i, e, f: (e, f, 0)),
                  whole(lw_["ln2_g"]), whole(lw_["ln2_b"])],
        out_specs=seq_blk,
        scratch_shapes=[pltpu.VMEM((nb * tt, d), BF16), pltpu.VMEM((nb * tt, LANES), F32),
                        pltpu.VMEM((nb * tt, d), F32)],
        compiler_params=_params(("arbitrary", "arbitrary", "arbitrary")),
        name="expert_ffn",
    )(x, mod, lw_["w_router"], lw_["w_gate"], lw_["w_up"], lw_["w_down"], lw_["ln2_g"], lw_["ln2_b"])


def _pad_lanes(v, width=LANES):
    return jnp.pad(v, ((0, 0), (0, width - v.shape[-1])))


def _block_diag_halves(w):
    nblk, c, _ = w.shape
    half = nblk // 2
    out = jnp.zeros((2, half * c, half * c), w.dtype)
    for i in range(nblk):
        j = i % half
        out = out.at[i // half, j * c:(j + 1) * c, j * c:(j + 1) * c].set(w[i])
    return out


def _layer_weights(l, p, d, lw):
    w_in = p["w_in"][l]
    a_off = DN_QKV
    z_off = a_off + 2 * DN_HEADS
    x_off = z_off + DN_WIDTH
    y_off = x_off + lw
    w_ab = _pad_lanes(w_in[:, a_off:z_off])
    w_cat = jnp.concatenate([w_in[:, :DN_QKV], w_in[:, z_off:y_off + lw], w_ab], axis=1).astype(BF16)
    out = {
        "w_cat": w_cat,
        "dn_conv_w": p["dn_conv_w"][l],
        "lru_conv_w": p["lru_conv_w"][l],
        "lru_conv_b": p["lru_conv_b"][l][None],
        "a_log": _pad_lanes(p["dn_a_log"][l][None]),
        "dt_bias": _pad_lanes(p["dn_dt_bias"][l][None]),
        "w_r": _block_diag_halves(p["lru_w_r"][l]).astype(BF16),
        "w_i": _block_diag_halves(p["lru_w_i"][l]).astype(BF16),
        "b_r": p["lru_b_r"][l][None],
        "b_i": p["lru_b_i"][l][None],
        "lam": p["lru_lambda"][l][None],
        "w_out": p["w_out"][l].astype(BF16),
        "dn_norm_w": p["dn_norm_w"][l][None],
        "ln1_g": p["ln1_g"][l][None],
        "ln1_b": p["ln1_b"][l][None],
        "ln2_g": p["ln2_g"][l][None],
        "ln2_b": p["ln2_b"][l][None],
    }
    j = l // 2
    if l % 2 == 0:
        out.update(w_gate=p["ffn_w_gate"][j].astype(BF16), w_up=p["ffn_w_up"][j].astype(BF16),
                   w_down=p["ffn_w_down"][j].astype(BF16))
    else:
        out.update(w_router=_pad_lanes(p["moe_w_router"][j]),
                   w_gate=p["moe_w_gate"][j].astype(BF16), w_up=p["moe_w_up"][j].astype(BF16),
                   w_down=p["moe_w_down"][j].astype(BF16))
    return out


def _tiling(b, t):
    if t >= MXU_DIM:
        tt = MXU_DIM
        return dict(inproj=(1, tt), mixer=(1, tt, min(PROMPT_CHUNK, tt)), ffn=(1, min(t, 2 * MXU_DIM)))
    assert t == SUBLANES, "short sequences must be exactly one sublane tile long"
    return dict(inproj=(min(b, 32), t), mixer=(min(b, 16), t, t), ffn=(min(b, 64), t))


def _trunk(x, mods, conv_dn, s_dn, conv_lru, s_lru, weights, alpha):
    b, t, d = x.shape
    depth = len(weights)
    lw = conv_lru.shape[-1]
    til = _tiling(b, t)
    pad_hist = lambda c: jnp.pad(c, ((0, 0), (SUBLANES - (CONV_W - 1), 0), (0, 0)))
    new_conv_dn, new_dn, new_conv_lru, new_lru = [], [], [], []
    for l in range(depth):
        lw_ = weights[l]
        mod = mods[l]
        nb, tt = til["inproj"]
        qkv, gz, lru, gb, tail_dn, tail_lru = _inproj(
            x, mod, pad_hist(conv_dn[l]), pad_hist(conv_lru[l]), lw_, nb=nb, tt=tt)
        nb, tt, chunk = til["mixer"]
        x, s_new, h_new = _mixer(qkv, gz, lru, gb, x, mod, s_dn[l], s_lru[l][:, None, :], lw_,
                                 nb=nb, tt=tt, chunk=chunk, alpha=alpha)
        nb, tt = til["ffn"]
        if l % 2 == 0:
            x = _dense_ffn(x, mod, lw_, nb=nb, tt=tt, alpha=alpha)
        else:
            x = _moe_ffn(x, mod, lw_, nb=nb, tt=tt, alpha=alpha)
        new_conv_dn.append(tail_dn[:, SUBLANES - (CONV_W - 1):, :])
        new_dn.append(s_new)
        new_conv_lru.append(tail_lru[:, SUBLANES - (CONV_W - 1):, :])
        new_lru.append(h_new[:, 0, :])
    return x, jnp.stack(new_conv_dn), jnp.stack(new_dn), jnp.stack(new_conv_lru), jnp.stack(new_lru)


def kernel(x_prompt, x_sample, cache_dn_conv, state_dn, cache_lru_conv, state_lru, c_prompt, c_sample,
           w_ada, b_ada, w_in, dn_conv_w, dn_a_log, dn_dt_bias, dn_norm_w,
           lru_conv_w, lru_conv_b, lru_w_r, lru_b_r, lru_w_i, lru_b_i, lru_lambda, w_out,
           ln1_g, ln1_b, ln2_g, ln2_b, ffn_w_gate, ffn_w_up, ffn_w_down,
           moe_w_router, moe_w_gate, moe_w_up, moe_w_down):
    p = dict(w_in=w_in, dn_conv_w=dn_conv_w, dn_a_log=dn_a_log, dn_dt_bias=dn_dt_bias, dn_norm_w=dn_norm_w,
             lru_conv_w=lru_conv_w, lru_conv_b=lru_conv_b, lru_w_r=lru_w_r, lru_b_r=lru_b_r,
             lru_w_i=lru_w_i, lru_b_i=lru_b_i, lru_lambda=lru_lambda, w_out=w_out,
             ln1_g=ln1_g, ln1_b=ln1_b, ln2_g=ln2_g, ln2_b=ln2_b,
             ffn_w_gate=ffn_w_gate, ffn_w_up=ffn_w_up, ffn_w_down=ffn_w_down,
             moe_w_router=moe_w_router, moe_w_gate=moe_w_gate, moe_w_up=moe_w_up, moe_w_down=moe_w_down)
    depth, d, _ = w_ada.shape
    bp = x_prompt.shape[0]
    bs = x_sample.shape[0]
    lw = cache_lru_conv.shape[-1]
    alpha = (2 * depth) ** 0.25
    weights = [_layer_weights(l, p, d, lw) for l in range(depth)]

    c_all = jnp.concatenate([c_prompt, c_sample], axis=0)
    mod_all = _modulation(c_all, w_ada, b_ada)
    mods_p = [mod_all[l, :bp][:, None, :] for l in range(depth)]
    mods_s = [mod_all[l, bp:][:, None, :] for l in range(depth)]

    z_conv_dn = jnp.zeros((depth, bp, CONV_W - 1, DN_QKV), F32)
    z_dn = jnp.zeros((depth, bp, DN_HEADS, DN_DK, DN_DV), F32)
    z_conv_lru = jnp.zeros((depth, bp, CONV_W - 1, lw), F32)
    z_lru = jnp.zeros((depth, bp, lw), F32)
    outs_p = _trunk(x_prompt, mods_p, z_conv_dn, z_dn, z_conv_lru, z_lru, weights, alpha)
    outs_s = _trunk(x_sample, mods_s, cache_dn_conv, state_dn, cache_lru_conv, state_lru, weights, alpha)
    return (outs_p[0], outs_s[0]) + tuple(outs_p[1:]) + tuple(outs_s[1:])
```

```python
import functools
import math

import jax
import jax.numpy as jnp
from jax import lax
from jax.experimental import pallas as pl
from jax.experimental.pallas import tpu as pltpu

F32 = jnp.float32
BF16 = jnp.bfloat16

DN_HEADS = 4
DN_DK = 128
DN_DV = 128
DN_WIDTH = DN_HEADS * DN_DV
DN_QKV = 3 * DN_WIDTH
LRU_BLOCKS = 8
LRU_C = 8.0
CONV_W = 4
N_MOD = 6
TOP_K = 2
LN_EPS = 1e-5
NORM_EPS = 1e-6

SUBLANES = 8
LANES = 128
MXU_DIM = 256
VMEM_LIMIT_BYTES = 56 * 1024 * 1024

PROMPT_CHUNK = 64
INV_BASE_BLOCK = 16


def _sigmoid(x):
    return 1.0 / (1.0 + jnp.exp(-x))


def _silu(x):
    return x * _sigmoid(x)


def _softplus(x):
    return jnp.maximum(x, 0.0) + jnp.log1p(jnp.exp(-jnp.abs(x)))


def _gelu_tanh(x):
    return 0.5 * x * (1.0 + jnp.tanh(math.sqrt(2.0 / math.pi) * (x + 0.044715 * (x * x * x))))


def _mm(a, b):
    return jnp.dot(a.astype(BF16), b.astype(BF16), preferred_element_type=F32)


def _layer_norm(x, g, b):
    mu = jnp.mean(x, axis=-1, keepdims=True)
    xc = x - mu
    var = jnp.mean(xc * xc, axis=-1, keepdims=True)
    return xc * lax.rsqrt(var + LN_EPS) * g + b


def _params(sem):
    return pltpu.CompilerParams(dimension_semantics=sem, vmem_limit_bytes=VMEM_LIMIT_BYTES)


def _mod_kernel(c_ref, w_ref, b_ref, o_ref):
    sc = _silu(c_ref[...])
    o_ref[0] = _mm(sc, w_ref[0]) + b_ref[0]


def _modulation(c_all, w_ada, b_ada):
    depth, d, n = w_ada.shape
    rows = c_all.shape[0]
    tn = 1536 if n % 1536 == 0 else n
    return pl.pallas_call(
        _mod_kernel,
        out_shape=jax.ShapeDtypeStruct((depth, rows, n), F32),
        grid=(depth, n // tn),
        in_specs=[
            pl.BlockSpec((rows, d), lambda l, j: (0, 0)),
            pl.BlockSpec((1, d, tn), lambda l, j: (l, 0, j)),
            pl.BlockSpec((1, 1, tn), lambda l, j: (l, 0, j)),
        ],
        out_specs=pl.BlockSpec((1, rows, tn), lambda l, j: (l, 0, j)),
        compiler_params=_params(("arbitrary", "arbitrary")),
        name="adaln_modulation",
    )(c_all, w_ada, b_ada.reshape(depth, 1, n))


def _causal_conv(u, hist, w, tt):
    rows, c = u.shape
    if tt == SUBLANES:
        hfull = hist.reshape(rows, c)
    else:
        hfull = jnp.concatenate([hist.reshape(SUBLANES, c), jnp.zeros((rows - SUBLANES, c), F32)], axis=0)
    t = lax.broadcasted_iota(jnp.int32, (rows, c), 0) & (tt - 1)
    out = u * w[CONV_W - 1:CONV_W, :]
    for j in range(1, CONV_W):
        prev = jnp.where(t >= j, pltpu.roll(u, j, 0), pltpu.roll(hfull, (rows + j - SUBLANES) % rows, 0))
        out = out + prev * w[CONV_W - 1 - j:CONV_W - j, :]
    return out


def _l2norm_heads(x, scale):
    outs = []
    for h in range(DN_HEADS):
        xh = x[:, h * DN_DK:(h + 1) * DN_DK]
        ss = jnp.sum(xh * xh, axis=-1, keepdims=True)
        outs.append(xh * (lax.rsqrt(ss + NORM_EPS) * scale))
    return jnp.concatenate(outs, axis=-1)


def _inproj_kernel(x_ref, mod_ref, hdn_ref, hlru_ref, w_ref, cwdn_ref, cwlru_ref, cblru_ref,
                   alog_ref, dtb_ref, wr_ref, wi_ref, br_ref, bi_ref, lam_ref,
                   qkv_ref, gz_ref, lru_ref, gb_ref, tdn_ref, tlru_ref,
                   hdn_sc, hlru_sc, *, nb, tt):
    d = x_ref.shape[-1]
    lw = hlru_ref.shape[-1]
    rows = nb * tt

    @pl.when(pl.program_id(1) == 0)
    def _():
        hdn_sc[...] = hdn_ref[...]
        hlru_sc[...] = hlru_ref[...]

    m = mod_ref[...]
    shift = m[:, :, 0:d]
    scale = m[:, :, d:2 * d]
    h = (x_ref[...] * (1.0 + scale) + shift).reshape(rows, d)
    proj = _mm(h, w_ref[...])
    u_dn = proj[:, 0:DN_QKV]
    z = proj[:, DN_QKV:DN_QKV + DN_WIDTH]
    u_lru = proj[:, DN_QKV + DN_WIDTH:DN_QKV + DN_WIDTH + lw]
    y = proj[:, DN_QKV + DN_WIDTH + lw:DN_QKV + DN_WIDTH + 2 * lw]
    ab = proj[:, DN_QKV + DN_WIDTH + 2 * lw:]

    qkv_c = _silu(_causal_conv(u_dn, hdn_sc[...], cwdn_ref[...], tt))
    q = _l2norm_heads(qkv_c[:, 0:DN_WIDTH], DN_DK ** -0.5)
    k = _l2norm_heads(qkv_c[:, DN_WIDTH:2 * DN_WIDTH], 1.0)
    v = qkv_c[:, 2 * DN_WIDTH:DN_QKV]
    qkv_ref[...] = jnp.concatenate([q, k, v], axis=-1).reshape(nb, tt, DN_QKV)
    gz_ref[...] = _silu(z).reshape(nb, tt, DN_WIDTH)
    lane = lax.broadcasted_iota(jnp.int32, ab.shape, 1)
    g_full = -jnp.exp(alog_ref[...]) * _softplus(ab + dtb_ref[...])
    gb_ref[...] = jnp.where(lane < DN_HEADS, g_full, _sigmoid(ab)).reshape(nb, tt, LANES)

    xc = _causal_conv(u_lru, hlru_sc[...], cwlru_ref[...], tt) + cblru_ref[...]
    half = lw // 2
    r_pre = jnp.concatenate([_mm(xc[:, :half], wr_ref[0]), _mm(xc[:, half:], wr_ref[1])], axis=-1)
    i_pre = jnp.concatenate([_mm(xc[:, :half], wi_ref[0]), _mm(xc[:, half:], wi_ref[1])], axis=-1)
    r = _sigmoid(r_pre + br_ref[...])
    i = _sigmoid(i_pre + bi_ref[...])
    log_a = -LRU_C * r * _softplus(-lam_ref[...])
    a = jnp.exp(log_a)
    th = jnp.tanh(log_a)
    inp = jnp.sqrt(-2.0 * th / (1.0 - th)) * (i * xc)
    lru_ref[...] = jnp.concatenate([a, inp, _gelu_tanh(y)], axis=-1).reshape(nb, tt, 3 * lw)

    tail_dn = u_dn.reshape(nb, tt, DN_QKV)[:, tt - SUBLANES:, :]
    tail_lru = u_lru.reshape(nb, tt, lw)[:, tt - SUBLANES:, :]
    hdn_sc[...] = tail_dn
    hlru_sc[...] = tail_lru
    tdn_ref[...] = tail_dn
    tlru_ref[...] = tail_lru


def _inproj(x, mod, hist_dn, hist_lru, lw_, *, nb, tt):
    b, t, d = x.shape
    lw = hist_lru.shape[-1]
    grid = (b // nb, t // tt)
    seq_blk = lambda c: pl.BlockSpec((nb, tt, c), lambda i, j: (i, j, 0))
    per_seq = lambda r, c: pl.BlockSpec((nb, r, c), lambda i, j: (i, 0, 0))
    whole = lambda a: pl.BlockSpec(a.shape, lambda i, j: (0,) * a.ndim)
    weights = [lw_[n] for n in ("w_cat", "dn_conv_w", "lru_conv_w", "lru_conv_b", "a_log", "dt_bias",
                                "w_r", "w_i", "b_r", "b_i", "lam")]
    return pl.pallas_call(
        functools.partial(_inproj_kernel, nb=nb, tt=tt),
        out_shape=(
            jax.ShapeDtypeStruct((b, t, DN_QKV), F32),
            jax.ShapeDtypeStruct((b, t, DN_WIDTH), F32),
            jax.ShapeDtypeStruct((b, t, 3 * lw), F32),
            jax.ShapeDtypeStruct((b, t, LANES), F32),
            jax.ShapeDtypeStruct((b, SUBLANES, DN_QKV), F32),
            jax.ShapeDtypeStruct((b, SUBLANES, lw), F32),
        ),
        grid=grid,
        in_specs=[seq_blk(d), per_seq(1, N_MOD * d), per_seq(SUBLANES, DN_QKV), per_seq(SUBLANES, lw)]
        + [whole(a) for a in weights],
        out_specs=(seq_blk(DN_QKV), seq_blk(DN_WIDTH), seq_blk(3 * lw), seq_blk(LANES),
                   per_seq(SUBLANES, DN_QKV), per_seq(SUBLANES, lw)),
        scratch_shapes=[pltpu.VMEM((nb, SUBLANES, DN_QKV), F32), pltpu.VMEM((nb, SUBLANES, lw), F32)],
        compiler_params=_params(("arbitrary", "arbitrary")),
        name="in_projection",
    )(x, mod, hist_dn, hist_lru, *weights)


def _unit_lower_inverses(ls, row, col, chunk, base):
    def same_block(s):
        k = s.bit_length() - 1
        return (row >> k) == (col >> k)

    eye = jnp.where(row == col, 1.0, 0.0)
    base_mask = same_block(base)
    powers = [jnp.where(base_mask, l, 0.0) for l in ls]
    invs = [eye - d for d in powers]
    p = 2
    while p < base:
        powers = [_mm(d, d) for d in powers]
        invs = [t + _mm(t, d) for t, d in zip(invs, powers)]
        p *= 2
    s = base
    while s < chunk:
        off_mask = same_block(2 * s) & jnp.logical_not(same_block(s))
        tmp = [_mm(jnp.where(off_mask, l, 0.0), t) for l, t in zip(ls, invs)]
        invs = [t - _mm(t, x) for t, x in zip(invs, tmp)]
        s *= 2
    return invs


def _mixer_kernel(qkv_ref, gz_ref, lru_ref, gb_ref, x_ref, mod_ref, s0_ref, h0_ref,
                  wout_ref, nw_ref, lng_ref, lnb_ref,
                  y_ref, sout_ref, hout_ref, *, nb, tt, chunk, alpha):
    d = x_ref.shape[-1]
    lw = h0_ref.shape[-1]
    rows = nb * tt
    n_chunks = rows // chunk
    chunks_per_seq = tt // chunk
    log_chunk = chunk.bit_length() - 1

    @pl.when(pl.program_id(1) == 0)
    def _():
        sout_ref[...] = s0_ref[...]
        hout_ref[...] = h0_ref[...]

    qkv = qkv_ref[...].reshape(rows, DN_QKV)
    gz = gz_ref[...].reshape(rows, DN_WIDTH)
    gb = gb_ref[...].reshape(rows, LANES)

    row = lax.broadcasted_iota(jnp.int32, (rows, rows), 0)
    col = lax.broadcasted_iota(jnp.int32, (rows, rows), 1)
    incl = ((row >> log_chunk) == (col >> log_chunk)) & (col <= row)
    strict = incl & (col < row)
    gc = jnp.dot(jnp.where(incl, 1.0, 0.0), gb, precision=lax.Precision.HIGHEST, preferred_element_type=F32)
    gc_t = gc.T

    heads = range(DN_HEADS)
    qs = [qkv[:, h * DN_DK:(h + 1) * DN_DK] for h in heads]
    ks = [qkv[:, DN_WIDTH + h * DN_DK:DN_WIDTH + (h + 1) * DN_DK] for h in heads]
    vs = [qkv[:, 2 * DN_WIDTH + h * DN_DV:2 * DN_WIDTH + (h + 1) * DN_DV] for h in heads]
    gcols = [gc[:, h:h + 1] for h in heads]
    betas = [gb[:, DN_HEADS + h:DN_HEADS + h + 1] for h in heads]
    decays = [jnp.where(incl, jnp.exp(jnp.where(incl, gcols[h] - gc_t[h:h + 1, :], 0.0)), 0.0) for h in heads]
    kbs = [ks[h] * betas[h] for h in heads]
    qk_kks = [lax.dot_general(jnp.concatenate([qs[h], kbs[h]], axis=0).astype(BF16), ks[h].astype(BF16),
                              (((1,), (1,)), ((), ())), preferred_element_type=F32) for h in heads]
    qks = [qk_kks[h][:rows] * decays[h] for h in heads]
    lmats = [jnp.where(strict, qk_kks[h][rows:] * decays[h], 0.0) for h in heads]
    tmats = _unit_lower_inverses(lmats, row, col, chunk, min(INV_BASE_BLOCK, chunk))
    egcs = [jnp.exp(g) for g in gcols]
    uws = [_mm(tmats[h], jnp.concatenate([vs[h] * betas[h], kbs[h] * egcs[h]], axis=-1)) for h in heads]
    us = [uw[:, :DN_DV] for uw in uws]
    ws = [uw[:, DN_DV:] for uw in uws]
    qes = [qs[h] * egcs[h] for h in heads]

    lru = lru_ref[...].reshape(rows, 3 * lw)
    a = lru[:, 0:lw]
    bacc = lru[:, lw:2 * lw]
    gy = lru[:, 2 * lw:3 * lw]
    t = lax.broadcasted_iota(jnp.int32, (rows, lw), 0) & (tt - 1)
    s = 1
    while s < tt:
        keep = t >= s
        a_prev = jnp.where(keep, pltpu.roll(a, s, 0), 1.0)
        b_prev = jnp.where(keep, pltpu.roll(bacc, s, 0), 0.0)
        bacc = a * b_prev + bacc
        a = a * a_prev
        s *= 2
    hs = (a.reshape(nb, tt, lw) * hout_ref[...] + bacc.reshape(nb, tt, lw))
    hout_ref[...] = hs[:, tt - 1:tt, :]
    o_b = hs.reshape(rows, lw) * gy

    v_new_parts = [[] for _ in heads]
    o_inter_parts = [[] for _ in heads]
    states = [None for _ in heads]
    for c in range(n_chunks):
        seq = c // chunks_per_seq
        lo, hi = c * chunk, (c + 1) * chunk
        if c % chunks_per_seq == 0:
            states = [sout_ref[seq, h] for h in heads]
        wqs = [_mm(jnp.concatenate([ws[h][lo:hi], qes[h][lo:hi]], axis=0), states[h]) for h in heads]
        v_news = [us[h][lo:hi] - wqs[h][:chunk] for h in heads]
        new_states = []
        for h in heads:
            v_new_parts[h].append(v_news[h])
            o_inter_parts[h].append(wqs[h][chunk:])
            g_last = gcols[h][hi - 1:hi, :]
            k_dec = ks[h][lo:hi] * jnp.exp(g_last - gcols[h][lo:hi])
            new_states.append(states[h] * jnp.exp(g_last) + lax.dot_general(
                k_dec.astype(BF16), v_news[h].astype(BF16), (((0,), (0,)), ((), ())),
                preferred_element_type=F32))
        states = new_states
        if (c + 1) % chunks_per_seq == 0:
            for h in heads:
                sout_ref[seq, h] = states[h]
    o_heads = []
    for h in heads:
        o = jnp.concatenate(o_inter_parts[h], axis=0) + _mm(qks[h], jnp.concatenate(v_new_parts[h], axis=0))
        ms = jnp.mean(o * o, axis=-1, keepdims=True)
        o_heads.append(o * lax.rsqrt(ms + NORM_EPS) * nw_ref[...] * gz[:, h * DN_DV:(h + 1) * DN_DV])
    o_a = jnp.concatenate(o_heads, axis=-1)

    mixed =_mm(jnp.concatenate([o_a, o_b], axis=-1), wout_ref[...]).reshape(nb, tt, d)
    gate = mod_ref[...][:, :, 2 * d:3 * d]
    y_ref[...] = _layer_norm(alpha * x_ref[...] + (1.0 + gate) * mixed, lng_ref[...], lnb_ref[...])


def _mixer(qkv, gz, lru, gb, x, mod, s0, h0, lw_, *, nb, tt, chunk, alpha):
    b, t, d = x.shape
    lw = h0.shape[-1]
    grid = (b // nb, t // tt)
    seq_blk = lambda c: pl.BlockSpec((nb, tt, c), lambda i, j: (i, j, 0))
    whole = lambda a: pl.BlockSpec(a.shape, lambda i, j: (0,) * a.ndim)
    state_spec = pl.BlockSpec((nb, DN_HEADS, DN_DK, DN_DV), lambda i, j: (i, 0, 0, 0))
    h_spec = pl.BlockSpec((nb, 1, lw), lambda i, j: (i, 0, 0))
    weights = [lw_[n] for n in ("w_out", "dn_norm_w", "ln1_g", "ln1_b")]
    return pl.pallas_call(
        functools.partial(_mixer_kernel, nb=nb, tt=tt, chunk=chunk, alpha=alpha),
        out_shape=(
            jax.ShapeDtypeStruct((b, t, d), F32),
            jax.ShapeDtypeStruct(s0.shape, F32),
            jax.ShapeDtypeStruct(h0.shape, F32),
        ),
        grid=grid,
        in_specs=[seq_blk(DN_QKV), seq_blk(DN_WIDTH), seq_blk(3 * lw), seq_blk(LANES), seq_blk(d),
                  pl.BlockSpec((nb, 1, N_MOD * d), lambda i, j: (i, 0, 0)), state_spec, h_spec]
        + [whole(a) for a in weights],
        out_specs=(seq_blk(d), state_spec, h_spec),
        compiler_params=_params(("arbitrary", "arbitrary")),
        name="token_mixer",
    )(qkv, gz, lru, gb, x, mod, s0, h0, *weights)


def _ffn_kernel(x_ref, mod_ref, wg_ref, wu_ref, wd_ref, lng_ref, lnb_ref, y_ref, h_sc, acc_sc, *, alpha):
    nb, tt, d = x_ref.shape
    f = pl.program_id(1)

    @pl.when(f == 0)
    def _():
        m = mod_ref[...]
        h = x_ref[...] * (1.0 + m[:, :, 4 * d:5 * d]) + m[:, :, 3 * d:4 * d]
        h_sc[...] = h.reshape(nb * tt, d).astype(BF16)
        acc_sc[...] = jnp.zeros_like(acc_sc)

    h = h_sc[...]
    act = _silu(jnp.dot(h, wg_ref[...], preferred_element_type=F32)) * jnp.dot(
        h, wu_ref[...], preferred_element_type=F32)
    acc_sc[...] += _mm(act, wd_ref[...])

    @pl.when(f == pl.num_programs(1) - 1)
    def _():
        gate = mod_ref[...][:, :, 5 * d:6 * d]
        ff = acc_sc[...].reshape(nb, tt, d)
        y_ref[...] = _layer_norm(alpha * x_ref[...] + (1.0 + gate) * ff, lng_ref[...], lnb_ref[...])


def _ff_tile(ff):
    for n in (2, 4, 7, 8, 11, 14, 16, 22, 28):
        if ff % n == 0 and (ff // n) % LANES == 0 and ff // n <= 2048:
            return ff // n
    return ff


def _dense_ffn(x, mod, lw_, *, nb, tt, alpha):
    b, t, d = x.shape
    ff = lw_["w_gate"].shape[-1]
    tf = _ff_tile(ff)
    grid = ((b // nb) * (t // tt), ff // tf)
    nt = t // tt
    seq_blk = pl.BlockSpec((nb, tt, d), lambda i, f: (i // nt, i % nt, 0))
    whole = lambda a: pl.BlockSpec(a.shape, lambda i, f: (0,) * a.ndim)
    return pl.pallas_call(
        functools.partial(_ffn_kernel, alpha=alpha),
        out_shape=jax.ShapeDtypeStruct((b, t, d), F32),
        grid=grid,
        in_specs=[seq_blk, pl.BlockSpec((nb, 1, N_MOD * d), lambda i, f: (i // nt, 0, 0)),
                  pl.BlockSpec((d, tf), lambda i, f: (0, f)), pl.BlockSpec((d, tf), lambda i, f: (0, f)),
                  pl.BlockSpec((tf, d), lambda i, f: (f, 0)), whole(lw_["ln2_g"]), whole(lw_["ln2_b"])],
        out_specs=seq_blk,
        scratch_shapes=[pltpu.VMEM((nb * tt, d), BF16), pltpu.VMEM((nb * tt, d), F32)],
        compiler_params=_params(("arbitrary", "arbitrary")),
        name="dense_ffn",
    )(x, mod, lw_["w_gate"], lw_["w_up"], lw_["w_down"], lw_["ln2_g"], lw_["ln2_b"])


MOE_CHUNK = 512
MOE_SLOT_TILE = 512
MOE_SUB = 128
MOE_COMBINE_BLOCK = 256
ROUTE_I1, ROUTE_I2, ROUTE_R1, ROUTE_R2, ROUTE_W1, ROUTE_W2 = range(6)


def _router_kernel(x_ref, mod_ref, wr_ref, base_ref, h_ref, meta_ref, blkbase_ref, cnt_ref, run_sc,
                   *, n_experts):
    nb, tt, d = x_ref.shape
    rows = nb * tt

    @pl.when(pl.program_id(0) == 0)
    def _():
        run_sc[...] = base_ref[...]

    m = mod_ref[...]
    h = (x_ref[...] * (1.0 + m[:, :, 4 * d:5 * d]) + m[:, :, 3 * d:4 * d]).reshape(rows, d)
    h_ref[...] = h.astype(BF16)
    logits = jnp.dot(h, wr_ref[...], precision=lax.Precision.HIGHEST, preferred_element_type=F32)
    lane = lax.broadcasted_iota(jnp.int32, logits.shape, 1)
    neg = jnp.float32(-jnp.inf)
    lg = jnp.where(lane < n_experts, logits, neg)
    m1 = jnp.max(lg, axis=-1, keepdims=True)
    i1 = jnp.min(jnp.where(lg == m1, lane, LANES), axis=-1, keepdims=True)
    lg2 = jnp.where(lane == i1, neg, lg)
    m2 = jnp.max(lg2, axis=-1, keepdims=True)
    i2 = jnp.min(jnp.where(lg2 == m2, lane, LANES), axis=-1, keepdims=True)
    e2 = jnp.exp(m2 - m1)
    w1 = 1.0 / (1.0 + e2)
    w2 = e2 / (1.0 + e2)
    sel = jnp.where(lane == i1, 1.0, jnp.where(lane == i2, 1.0, 0.0))
    r = lax.broadcasted_iota(jnp.int32, (rows, rows), 0)
    c = lax.broadcasted_iota(jnp.int32, (rows, rows), 1)
    rank = _mm(jnp.where(c < r, 1.0, 0.0), sel) + run_sc[0:1, :]
    r1 = jnp.sum(jnp.where(lane == i1, rank, 0.0), axis=-1, keepdims=True)
    r2 = jnp.sum(jnp.where(lane == i2, rank, 0.0), axis=-1, keepdims=True)
    fields = (i1.astype(F32), i2.astype(F32), r1, r2, w1, w2)
    meta = jnp.zeros_like(logits)
    for k, v in enumerate(fields):
        meta = jnp.where(lane == k, v, meta)
    meta_ref[...] = meta
    blkbase_ref[0] = run_sc[...]
    run_sc[...] = run_sc[...] + jnp.sum(sel, axis=0, keepdims=True)
    cnt_ref[...] = run_sc[...]


def _router(x, mod, w_router, base, *, nb, tt, n_experts):
    b, t, d = x.shape
    rows = nb * tt
    nt = t // tt
    nblk = (b // nb) * nt
    whole = lambda a: pl.BlockSpec(a.shape, lambda i: (0,) * a.ndim)
    return pl.pallas_call(
        functools.partial(_router_kernel, n_experts=n_experts),
        out_shape=(jax.ShapeDtypeStruct((nblk * rows, d), BF16),
                   jax.ShapeDtypeStruct((nblk * rows, LANES), F32),
                   jax.ShapeDtypeStruct((nblk, SUBLANES, LANES), F32),
                   jax.ShapeDtypeStruct((SUBLANES, LANES), F32)),
        grid=(nblk,),
        in_specs=[pl.BlockSpec((nb, tt, d), lambda i: (i // nt, i % nt, 0)),
                  pl.BlockSpec((nb, 1, N_MOD * d), lambda i: (i // nt, 0, 0)),
                  whole(w_router), whole(base)],
        out_specs=(pl.BlockSpec((rows, d), lambda i: (i, 0)),
                   pl.BlockSpec((rows, LANES), lambda i: (i, 0)),
                   pl.BlockSpec((1, SUBLANES, LANES), lambda i: (i, 0, 0)),
                   pl.BlockSpec((SUBLANES, LANES), lambda i: (0, 0))),
        scratch_shapes=[pltpu.VMEM((SUBLANES, LANES), F32)],
        compiler_params=_params(("arbitrary",)),
        name="moe_router",
    )(x, mod, w_router, base)


def _select(s_ref, off, width, ids):
    hit1 = s_ref[0:1, pl.ds(off, width)] == ids
    hit2 = s_ref[1:2, pl.ds(off, width)] == ids
    return hit1, hit2, jnp.where(hit1, 1.0, jnp.where(hit2, 1.0, 0.0)).astype(BF16)


def _gather_kernel(clo_ref, chi_ref, h_ref, srow_ref, wrow_ref, xs_ref, ws_ref, acc_sc, wacc_sc):
    g = pl.program_id(0)
    n_sub = MOE_SLOT_TILE // MOE_SUB
    for j in range(n_sub):
        q = g * n_sub + j
        ids = g * MOE_SLOT_TILE + j * MOE_SUB + lax.broadcasted_iota(jnp.int32, (MOE_SUB, MOE_CHUNK), 0)
        acc_sc[...] = jnp.zeros_like(acc_sc)
        wacc_sc[...] = jnp.zeros_like(wacc_sc)

        def body(c, carry):
            off = pl.multiple_of(c * MOE_CHUNK, MOE_CHUNK)
            hit1, hit2, p = _select(srow_ref, off, MOE_CHUNK, ids)
            acc_sc[...] += jnp.dot(p, h_ref[pl.ds(off, MOE_CHUNK), :], preferred_element_type=F32)
            w = (jnp.where(hit1, wrow_ref[0:1, pl.ds(off, MOE_CHUNK)], 0.0)
                 + jnp.where(hit2, wrow_ref[1:2, pl.ds(off, MOE_CHUNK)], 0.0))
            wacc_sc[...] += jnp.broadcast_to(jnp.sum(w, axis=-1, keepdims=True), wacc_sc.shape)
            return carry

        lax.fori_loop(clo_ref[q], chi_ref[q], body, 0)
        xs_ref[j * MOE_SUB:(j + 1) * MOE_SUB, :] = acc_sc[...].astype(BF16)
        ws_ref[j * MOE_SUB:(j + 1) * MOE_SUB, :] = wacc_sc[...]


def _gather_slots(c_lo, c_hi, h, srow, wrow, n_tiles):
    n, d = h.shape
    vmem = pl.BlockSpec(memory_space=pltpu.VMEM)
    return pl.pallas_call(
        _gather_kernel,
        out_shape=(jax.ShapeDtypeStruct((n_tiles * MOE_SLOT_TILE, d), BF16),
                   jax.ShapeDtypeStruct((n_tiles * MOE_SLOT_TILE, LANES), F32)),
        grid_spec=pltpu.PrefetchScalarGridSpec(
            num_scalar_prefetch=2, grid=(n_tiles,),
            in_specs=[vmem, vmem, vmem],
            out_specs=(pl.BlockSpec((MOE_SLOT_TILE, d), lambda g, lo, hi: (g, 0)),
                       pl.BlockSpec((MOE_SLOT_TILE, LANES), lambda g, lo, hi: (g, 0))),
            scratch_shapes=[pltpu.VMEM((MOE_SUB, d), F32), pltpu.VMEM((MOE_SUB, LANES), F32)]),
        compiler_params=_params(("arbitrary",)),
        name="moe_gather",
    )(c_lo, c_hi, h, srow, wrow)


def _expert_kernel(te_ref, tv_ref, xs_ref, ws_ref, wg_ref, wu_ref, wd_ref, o_ref, *, ff_chunk):
    g = pl.program_id(0)

    @pl.when(tv_ref[g] != 0)
    def _():
        x = xs_ref[...]
        ff = wg_ref.shape[-1]
        acc = jnp.zeros(o_ref.shape, F32)
        for f0 in range(0, ff, ff_chunk):
            gate = jnp.dot(x, wg_ref[0, :, f0:f0 + ff_chunk], preferred_element_type=F32)
            up = jnp.dot(x, wu_ref[0, :, f0:f0 + ff_chunk], preferred_element_type=F32)
            acc = acc + _mm(_silu(gate) * up, wd_ref[0, f0:f0 + ff_chunk, :])
        o_ref[...] = (ws_ref[:, 0:1] * acc).astype(BF16)

    @pl.when(tv_ref[g] == 0)
    def _():
        o_ref[...] = jnp.zeros_like(o_ref)


def _expert_ffn(tile_expert, tile_valid, xs, ws, lw_):
    s_total, d = xs.shape
    _, _, ff = lw_["w_gate"].shape
    n_tiles = s_total // MOE_SLOT_TILE
    once = pl.Buffered(1)
    return pl.pallas_call(
        functools.partial(_expert_kernel, ff_chunk=_ff_tile(ff)),
        out_shape=jax.ShapeDtypeStruct((s_total, d), BF16),
        grid_spec=pltpu.PrefetchScalarGridSpec(
            num_scalar_prefetch=2, grid=(n_tiles,),
            in_specs=[pl.BlockSpec((MOE_SLOT_TILE, d), lambda g, te, tv: (g, 0)),
                      pl.BlockSpec((MOE_SLOT_TILE, LANES), lambda g, te, tv: (g, 0)),
                      pl.BlockSpec((1, d, ff), lambda g, te, tv: (te[g], 0, 0), pipeline_mode=once),
                      pl.BlockSpec((1, d, ff), lambda g, te, tv: (te[g], 0, 0), pipeline_mode=once),
                      pl.BlockSpec((1, ff, d), lambda g, te, tv: (te[g], 0, 0), pipeline_mode=once)],
            out_specs=pl.BlockSpec((MOE_SLOT_TILE, d), lambda g, te, tv: (g, 0))),
        compiler_params=_params(("arbitrary",)),
        name="expert_ffn",
    )(tile_expert, tile_valid, xs, ws, lw_["w_gate"], lw_["w_up"], lw_["w_down"])


def _combine_kernel(ic_ref, ik_ref, if_ref, x_ref, mod_ref, sc_ref, os_ref, lng_ref, lnb_ref, y_ref, acc_sc,
                    *, alpha):
    nb, tt, d = x_ref.shape
    rows = nb * tt
    w = pl.program_id(0)
    flags = if_ref[w]

    @pl.when((flags & 1) != 0)
    def _():
        acc_sc[...] = jnp.zeros_like(acc_sc)

    @pl.when((flags & 4) != 0)
    def _():
        ids = ik_ref[w] * MOE_COMBINE_BLOCK + lax.broadcasted_iota(jnp.int32, (rows, MOE_COMBINE_BLOCK), 1)
        q = jnp.where(sc_ref[:, 0:1] == ids, 1.0, jnp.where(sc_ref[:, 1:2] == ids, 1.0, 0.0)).astype(BF16)
        acc_sc[...] += jnp.dot(q, os_ref[...], preferred_element_type=F32)

    @pl.when((flags & 2) != 0)
    def _():
        gate = mod_ref[...][:, :, 5 * d:6 * d]
        ff = acc_sc[...].reshape(nb, tt, d)
        y_ref[...] = _layer_norm(alpha * x_ref[...] + (1.0 + gate) * ff, lng_ref[...], lnb_ref[...])


def _combine(items, x, mod, slots_col, out_sorted, lw_, *, nb, tt, alpha):
    item_chunk, item_block, item_flags = items
    b, t, d = x.shape
    rows = nb * tt
    nt = t // tt
    whole = lambda a: pl.BlockSpec(a.shape, lambda w, ic, ik, fl: (0,) * a.ndim)
    seq_blk = pl.BlockSpec((nb, tt, d), lambda w, ic, ik, fl: (ic[w] // nt, ic[w] % nt, 0))
    return pl.pallas_call(
        functools.partial(_combine_kernel, alpha=alpha),
        out_shape=jax.ShapeDtypeStruct((b, t, d), F32),
        grid_spec=pltpu.PrefetchScalarGridSpec(
            num_scalar_prefetch=3, grid=(item_chunk.shape[0],),
            in_specs=[seq_blk,
                      pl.BlockSpec((nb, 1, N_MOD * d), lambda w, ic, ik, fl: (ic[w] // nt, 0, 0)),
                      pl.BlockSpec((rows, TOP_K), lambda w, ic, ik, fl: (ic[w], 0)),
                      pl.BlockSpec((MOE_COMBINE_BLOCK, d), lambda w, ic, ik, fl: (ik[w], 0)),
                      whole(lw_["ln2_g"]), whole(lw_["ln2_b"])],
            out_specs=seq_blk,
            scratch_shapes=[pltpu.VMEM((rows, d), F32)]),
        compiler_params=_params(("arbitrary",)),
        name="moe_combine",
    )(item_chunk, item_block, item_flags, x, mod, slots_col, out_sorted, lw_["ln2_g"], lw_["ln2_b"])


def _count_le(sorted_vals, queries):
    return jnp.sum(sorted_vals[None, :] <= queries[:, None], axis=1).astype(jnp.int32)


def _combine_items(lo, hi, n_items):
    n_chunks, n_experts = lo.shape
    first = lo // MOE_COMBINE_BLOCK
    count = jnp.where(hi > lo, (hi - 1) // MOE_COMBINE_BLOCK - first + 1, 0).reshape(-1)
    ends = jnp.cumsum(count)
    total = ends[-1]
    w = jnp.arange(n_items, dtype=jnp.int32)
    pair = jnp.minimum(_count_le(ends, w), n_chunks * n_experts - 1)
    within = w - (ends[pair] - count[pair])
    valid = w < total
    chunk = pair // n_experts
    block = first.reshape(-1)[pair] + within
    last_valid = jnp.maximum(total - 1, 0)
    chunk = jnp.where(valid, chunk, chunk[last_valid])
    block = jnp.where(valid, block, block[last_valid])
    prev_chunk = jnp.concatenate([jnp.full((1,), -1, jnp.int32), chunk[:-1]])
    next_chunk = jnp.concatenate([chunk[1:], jnp.full((1,), -1, jnp.int32)])
    next_valid = jnp.concatenate([valid[1:], jnp.zeros((1,), bool)])
    is_first = valid & (chunk != prev_chunk)
    is_last = valid & ((chunk != next_chunk) | jnp.logical_not(next_valid))
    flags = is_first.astype(jnp.int32) + 2 * is_last.astype(jnp.int32) + 4 * valid.astype(jnp.int32)
    return chunk.astype(jnp.int32), block.astype(jnp.int32), flags


def _moe_layer(xs_in, mods, lw_, tilings, alpha):
    n_experts = lw_["w_gate"].shape[0]
    d = xs_in[0].shape[-1]
    base = jnp.zeros((SUBLANES, LANES), F32)
    hs, metas, bases = [], [], []
    for x, mod, (nb, tt) in zip(xs_in, mods, tilings):
        assert nb * tt == MOE_CHUNK and (x.shape[0] * x.shape[1]) % MOE_CHUNK == 0
        h, meta, blkbase, base = _router(x, mod, lw_["w_router"], base, nb=nb, tt=tt, n_experts=n_experts)
        hs.append(h)
        metas.append(meta)
        bases.append(blkbase[:, 0, :n_experts])
    h = jnp.concatenate(hs, axis=0)
    meta = jnp.concatenate(metas, axis=0)
    n = h.shape[0]
    n_chunks = n // MOE_CHUNK
    cum = jnp.concatenate(bases + [base[0:1, :n_experts]], axis=0).astype(jnp.int32)
    counts = cum[-1]
    sizes = ((counts + MOE_SLOT_TILE - 1) // MOE_SLOT_TILE) * MOE_SLOT_TILE
    run_end = jnp.cumsum(sizes)
    run_start = run_end - sizes
    col = lambda k: meta[:, k].astype(jnp.int32)
    slot1 = run_start[col(ROUTE_I1)] + col(ROUTE_R1)
    slot2 = run_start[col(ROUTE_I2)] + col(ROUTE_R2)
    pad_rows = lambda a, fill: jnp.concatenate(
        [a, jnp.full((SUBLANES - a.shape[0], n), fill, a.dtype)], axis=0)
    srow = pad_rows(jnp.stack([slot1, slot2]), -1)
    wrow = pad_rows(jnp.stack([meta[:, ROUTE_W1], meta[:, ROUTE_W2]]), 0.0)

    n_tiles = (TOP_K * n + MOE_SLOT_TILE - 1) // MOE_SLOT_TILE + n_experts
    tile_start = jnp.arange(n_tiles, dtype=jnp.int32) * MOE_SLOT_TILE
    tile_expert = jnp.minimum(_count_le(run_end, tile_start), n_experts - 1)
    tile_valid = (tile_start < run_end[-1]).astype(jnp.int32)
    n_sub = MOE_SLOT_TILE // MOE_SUB
    sub_expert = jnp.repeat(tile_expert, n_sub)
    sub_rank0 = jnp.arange(n_tiles * n_sub, dtype=jnp.int32) * MOE_SUB - run_start[sub_expert]
    cum_sub = cum[:, sub_expert]
    sub_valid = jnp.repeat(tile_valid, n_sub)
    c_lo = jnp.sum(cum_sub[1:] <= sub_rank0[None, :], axis=0).astype(jnp.int32) * sub_valid
    c_hi = jnp.sum(cum_sub[:-1] < sub_rank0[None, :] + MOE_SUB, axis=0).astype(jnp.int32) * sub_valid

    x_sorted, w_sorted = _gather_slots(c_lo, c_hi, h, srow, wrow, n_tiles)
    out_sorted = _expert_ffn(tile_expert, tile_valid, x_sorted, w_sorted, lw_)

    slots_col = jnp.stack([slot1, slot2], axis=-1)
    outs = []
    chunk0 = 0
    for x, mod, (nb, tt) in zip(xs_in, mods, tilings):
        nc = x.shape[0] * x.shape[1] // MOE_CHUNK
        lo = run_start[None, :] + cum[chunk0:chunk0 + nc]
        hi = run_start[None, :] + cum[chunk0 + 1:chunk0 + nc + 1]
        n_items = nc * n_experts + (TOP_K * nc * MOE_CHUNK) // MOE_COMBINE_BLOCK + 2 * n_experts
        items = _combine_items(lo, hi, n_items)
        sc = slots_col[chunk0 * MOE_CHUNK:(chunk0 + nc) * MOE_CHUNK]
        outs.append(_combine(items, x, mod, sc, out_sorted, lw_, nb=nb, tt=tt, alpha=alpha))
        chunk0 += nc
    return outs


def _pad_lanes(v, width=LANES):
    return jnp.pad(v, ((0, 0), (0, width - v.shape[-1])))


def _block_diag_halves(w):
    nblk, c, _ = w.shape
    half = nblk // 2
    out = jnp.zeros((2, half * c, half * c), w.dtype)
    for i in range(nblk):
        j = i % half
        out = out.at[i // half, j * c:(j + 1) * c, j * c:(j + 1) * c].set(w[i])
    return out


def _layer_weights(l, p, d, lw):
    w_in = p["w_in"][l]
    a_off = DN_QKV
    z_off = a_off + 2 * DN_HEADS
    x_off = z_off + DN_WIDTH
    y_off = x_off + lw
    w_ab = _pad_lanes(w_in[:, a_off:z_off])
    w_cat = jnp.concatenate([w_in[:, :DN_QKV], w_in[:, z_off:y_off + lw], w_ab], axis=1).astype(BF16)
    out = {
        "w_cat": w_cat,
        "dn_conv_w": p["dn_conv_w"][l],
        "lru_conv_w": p["lru_conv_w"][l],
        "lru_conv_b": p["lru_conv_b"][l][None],
        "a_log": _pad_lanes(p["dn_a_log"][l][None]),
        "dt_bias": _pad_lanes(p["dn_dt_bias"][l][None]),
        "w_r": _block_diag_halves(p["lru_w_r"][l]).astype(BF16),
        "w_i": _block_diag_halves(p["lru_w_i"][l]).astype(BF16),
        "b_r": p["lru_b_r"][l][None],
        "b_i": p["lru_b_i"][l][None],
        "lam": p["lru_lambda"][l][None],
        "w_out": p["w_out"][l].astype(BF16),
        "dn_norm_w": p["dn_norm_w"][l][None],
        "ln1_g": p["ln1_g"][l][None],
        "ln1_b": p["ln1_b"][l][None],
        "ln2_g": p["ln2_g"][l][None],
        "ln2_b": p["ln2_b"][l][None],
    }
    j = l // 2
    if l % 2 == 0:
        out.update(w_gate=p["ffn_w_gate"][j].astype(BF16), w_up=p["ffn_w_up"][j].astype(BF16),
                   w_down=p["ffn_w_down"][j].astype(BF16))
    else:
        out.update(w_router=_pad_lanes(p["moe_w_router"][j]),
                   w_gate=p["moe_w_gate"][j].astype(BF16), w_up=p["moe_w_up"][j].astype(BF16),
                   w_down=p["moe_w_down"][j].astype(BF16))
    return out


def _tiling(b, t):
    if t >= MXU_DIM:
        tt = MXU_DIM
        return dict(inproj=(1, tt), mixer=(1, tt, min(PROMPT_CHUNK, tt)), ffn=(1, min(t, 2 * MXU_DIM)))
    assert t == SUBLANES, "short sequences must be exactly one sublane tile long"
    return dict(inproj=(min(b, 32), t), mixer=(min(b, 16), t, t), ffn=(min(b, 64), t))


def _token_mix(x, mod, conv_dn, s_dn, conv_lru, s_lru, lw_, til, alpha):
    pad_hist = lambda c: jnp.pad(c, ((0, 0), (SUBLANES - (CONV_W - 1), 0), (0, 0)))
    nb, tt = til["inproj"]
    qkv, gz, lru, gb, tail_dn, tail_lru = _inproj(x, mod, pad_hist(conv_dn), pad_hist(conv_lru), lw_, nb=nb, tt=tt)
    nb, tt, chunk = til["mixer"]
    x, s_new, h_new = _mixer(qkv, gz, lru, gb, x, mod, s_dn, s_lru[:, None, :], lw_,
                             nb=nb, tt=tt, chunk=chunk, alpha=alpha)
    return (x, tail_dn[:, SUBLANES - (CONV_W - 1):, :], s_new, tail_lru[:, SUBLANES - (CONV_W - 1):, :],
            h_new[:, 0, :])


def kernel(x_prompt, x_sample, cache_dn_conv, state_dn, cache_lru_conv, state_lru, c_prompt, c_sample,
           w_ada, b_ada, w_in, dn_conv_w, dn_a_log, dn_dt_bias, dn_norm_w,
           lru_conv_w, lru_conv_b, lru_w_r, lru_b_r, lru_w_i, lru_b_i, lru_lambda, w_out,
           ln1_g, ln1_b, ln2_g, ln2_b, ffn_w_gate, ffn_w_up, ffn_w_down,
           moe_w_router, moe_w_gate, moe_w_up, moe_w_down):
    p = dict(w_in=w_in, dn_conv_w=dn_conv_w, dn_a_log=dn_a_log, dn_dt_bias=dn_dt_bias, dn_norm_w=dn_norm_w,
             lru_conv_w=lru_conv_w, lru_conv_b=lru_conv_b, lru_w_r=lru_w_r, lru_b_r=lru_b_r,
             lru_w_i=lru_w_i, lru_b_i=lru_b_i, lru_lambda=lru_lambda, w_out=w_out,
             ln1_g=ln1_g, ln1_b=ln1_b, ln2_g=ln2_g, ln2_b=ln2_b,
             ffn_w_gate=ffn_w_gate, ffn_w_up=ffn_w_up, ffn_w_down=ffn_w_down,
             moe_w_router=moe_w_router, moe_w_gate=moe_w_gate, moe_w_up=moe_w_up, moe_w_down=moe_w_down)
    depth, d, _ = w_ada.shape
    bp = x_prompt.shape[0]
    bs = x_sample.shape[0]
    lw = cache_lru_conv.shape[-1]
    alpha = (2 * depth) ** 0.25
    weights = [_layer_weights(l, p, d, lw) for l in range(depth)]

    c_all = jnp.concatenate([c_prompt, c_sample], axis=0)
    mod_all = _modulation(c_all, w_ada, b_ada)
    groups = [
        dict(x=x_prompt, rows=slice(0, bp), conv_dn=jnp.zeros((depth, bp, CONV_W - 1, DN_QKV), F32),
             s_dn=jnp.zeros((depth, bp, DN_HEADS, DN_DK, DN_DV), F32),
             conv_lru=jnp.zeros((depth, bp, CONV_W - 1, lw), F32), s_lru=jnp.zeros((depth, bp, lw), F32)),
        dict(x=x_sample, rows=slice(bp, bp + bs), conv_dn=cache_dn_conv, s_dn=state_dn,
             conv_lru=cache_lru_conv, s_lru=state_lru),
    ]
    for g in groups:
        g["til"] = _tiling(g["x"].shape[0], g["x"].shape[1])
        g["new"] = [[], [], [], []]
    for l in range(depth):
        lw_ = weights[l]
        mods = [mod_all[l, g["rows"]][:, None, :] for g in groups]
        for g, mod in zip(groups, mods):
            res = _token_mix(g["x"], mod, g["conv_dn"][l], g["s_dn"][l], g["conv_lru"][l], g["s_lru"][l],
                             lw_, g["til"], alpha)
            g["x"] = res[0]
            for acc, new in zip(g["new"], res[1:]):
                acc.append(new)
        if l % 2 == 0:
            for g, mod in zip(groups, mods):
                nb, tt = g["til"]["ffn"]
                g["x"] = _dense_ffn(g["x"], mod, lw_, nb=nb, tt=tt, alpha=alpha)
        else:
            xs = _moe_layer([g["x"] for g in groups], mods, lw_, [g["til"]["ffn"] for g in groups], alpha)
            for g, x in zip(groups, xs):
                g["x"] = x
    states = [jnp.stack(acc) for g in groups for acc in g["new"]]
    return (groups[0]["x"], groups[1]["x"]) + tuple(states)
```

```python
import functools
import math

import jax
import jax.numpy as jnp
from jax import lax
from jax.experimental import pallas as pl
from jax.experimental.pallas import tpu as pltpu

F32 = jnp.float32
BF16 = jnp.bfloat16

DN_HEADS = 4
DN_DK = 128
DN_DV = 128
DN_WIDTH = DN_HEADS * DN_DV
DN_QKV = 3 * DN_WIDTH
LRU_BLOCKS = 8
LRU_C = 8.0
CONV_W = 4
N_MOD = 6
TOP_K = 2
LN_EPS = 1e-5
NORM_EPS = 1e-6

SUBLANES = 8
LANES = 128
MXU_DIM = 256
VMEM_LIMIT_BYTES = 56 * 1024 * 1024

PROMPT_CHUNK = 64
INV_BASE_BLOCK = 16


def _sigmoid(x):
    return 1.0 / (1.0 + jnp.exp(-x))


def _silu(x):
    return x * _sigmoid(x)


def _softplus(x):
    return jnp.maximum(x, 0.0) + jnp.log1p(jnp.exp(-jnp.abs(x)))


def _gelu_tanh(x):
    return 0.5 * x * (1.0 + jnp.tanh(math.sqrt(2.0 / math.pi) * (x + 0.044715 * (x * x * x))))


def _mm(a, b):
    return jnp.dot(a.astype(BF16), b.astype(BF16), preferred_element_type=F32)


def _layer_norm(x, g, b):
    mu = jnp.mean(x, axis=-1, keepdims=True)
    xc = x - mu
    var = jnp.mean(xc * xc, axis=-1, keepdims=True)
    return xc * lax.rsqrt(var + LN_EPS) * g + b


def _params(sem):
    return pltpu.CompilerParams(dimension_semantics=sem, vmem_limit_bytes=VMEM_LIMIT_BYTES)


def _mod_kernel(c_ref, w_ref, b_ref, o_ref):
    sc = _silu(c_ref[...])
    o_ref[0] = _mm(sc, w_ref[0]) + b_ref[0]


def _modulation(c_all, w_ada, b_ada):
    depth, d, n = w_ada.shape
    rows = c_all.shape[0]
    tn = 1536 if n % 1536 == 0 else n
    return pl.pallas_call(
        _mod_kernel,
        out_shape=jax.ShapeDtypeStruct((depth, rows, n), F32),
        grid=(depth, n // tn),
        in_specs=[
            pl.BlockSpec((rows, d), lambda l, j: (0, 0)),
            pl.BlockSpec((1, d, tn), lambda l, j: (l, 0, j)),
            pl.BlockSpec((1, 1, tn), lambda l, j: (l, 0, j)),
        ],
        out_specs=pl.BlockSpec((1, rows, tn), lambda l, j: (l, 0, j)),
        compiler_params=_params(("arbitrary", "arbitrary")),
        name="adaln_modulation",
    )(c_all, w_ada, b_ada.reshape(depth, 1, n))


def _causal_conv(u, win_ref, w, nb, tt):
    c = u.shape[-1]
    win_ref[:, SUBLANES:, :] = u.reshape(nb, tt, c)
    out = u * w[CONV_W - 1:CONV_W, :]
    for j in range(1, CONV_W):
        prev = win_ref[:, SUBLANES - j:SUBLANES - j + tt, :].reshape(nb * tt, c)
        out = out + prev * w[CONV_W - 1 - j:CONV_W - j, :]
    tail = win_ref[:, tt:tt + SUBLANES, :]
    win_ref[:, 0:SUBLANES, :] = tail
    return out, tail


def _l2norm_heads(x, scale):
    outs = []
    for h in range(DN_HEADS):
        xh = x[:, h * DN_DK:(h + 1) * DN_DK]
        ss = jnp.sum(xh * xh, axis=-1, keepdims=True)
        outs.append(xh * (lax.rsqrt(ss + NORM_EPS) * scale))
    return jnp.concatenate(outs, axis=-1)


def _inproj_stage(x_ref, mod_ref, hdn_ref, hlru_ref, w_ref, cwdn_ref, cwlru_ref, cblru_ref,
                  alog_ref, dtb_ref, wr_ref, wi_ref, br_ref, bi_ref, lam_ref,
                  tdn_ref, tlru_ref, wdn_sc, wlru_sc, *, nb, tt):
    d = x_ref.shape[-1]
    lw = hlru_ref.shape[-1]
    rows = nb * tt

    @pl.when(pl.program_id(1) == 0)
    def _():
        wdn_sc[:, 0:SUBLANES, :] = hdn_ref[...]
        wlru_sc[:, 0:SUBLANES, :] = hlru_ref[...]

    m = mod_ref[...]
    shift = m[:, :, 0:d]
    scale = m[:, :, d:2 * d]
    h = (x_ref[...] * (1.0 + scale) + shift).reshape(rows, d)
    proj = _mm(h, w_ref[...])
    u_dn = proj[:, 0:DN_QKV]
    z = proj[:, DN_QKV:DN_QKV + DN_WIDTH]
    u_lru = proj[:, DN_QKV + DN_WIDTH:DN_QKV + DN_WIDTH + lw]
    y = proj[:, DN_QKV + DN_WIDTH + lw:DN_QKV + DN_WIDTH + 2 * lw]
    ab = proj[:, DN_QKV + DN_WIDTH + 2 * lw:]

    conv_dn, tail_dn = _causal_conv(u_dn, wdn_sc, cwdn_ref[...], nb, tt)
    qkv_c = _silu(conv_dn)
    q = _l2norm_heads(qkv_c[:, 0:DN_WIDTH], DN_DK ** -0.5)
    k = _l2norm_heads(qkv_c[:, DN_WIDTH:2 * DN_WIDTH], 1.0)
    v = qkv_c[:, 2 * DN_WIDTH:DN_QKV]
    qkv = jnp.concatenate([q, k, v], axis=-1)
    gz = _silu(z)
    lane = lax.broadcasted_iota(jnp.int32, ab.shape, 1)
    g_full = -jnp.exp(alog_ref[...]) * _softplus(ab + dtb_ref[...])
    gb = jnp.where(lane < DN_HEADS, g_full, _sigmoid(ab))

    conv_lru, tail_lru = _causal_conv(u_lru, wlru_sc, cwlru_ref[...], nb, tt)
    xc = conv_lru + cblru_ref[...]
    half = lw // 2
    r_pre = jnp.concatenate([_mm(xc[:, :half], wr_ref[0]), _mm(xc[:, half:], wr_ref[1])], axis=-1)
    i_pre = jnp.concatenate([_mm(xc[:, :half], wi_ref[0]), _mm(xc[:, half:], wi_ref[1])], axis=-1)
    r = _sigmoid(r_pre + br_ref[...])
    i = _sigmoid(i_pre + bi_ref[...])
    log_a = -LRU_C * r * _softplus(-lam_ref[...])
    a = jnp.exp(log_a)
    th = jnp.tanh(log_a)
    inp = jnp.sqrt(-2.0 * th / (1.0 - th)) * (i * xc)
    lru = jnp.concatenate([a, inp, _gelu_tanh(y)], axis=-1)

    tdn_ref[...] = tail_dn
    tlru_ref[...] = tail_lru
    return qkv, gz, lru, gb


def _unit_lower_inverses(ls, row, col, chunk, base):
    def same_block(s):
        k = s.bit_length() - 1
        return (row >> k) == (col >> k)

    eye = jnp.where(row == col, 1.0, 0.0)
    base_mask = same_block(base)
    powers = [jnp.where(base_mask, l, 0.0) for l in ls]
    invs = [eye - d for d in powers]
    p = 2
    while p < base:
        powers = [_mm(d, d) for d in powers]
        invs = [t + _mm(t, d) for t, d in zip(invs, powers)]
        p *= 2
    s = base
    while s < chunk:
        off_mask = same_block(2 * s) & jnp.logical_not(same_block(s))
        tmp = [_mm(jnp.where(off_mask, l, 0.0), t) for l, t in zip(ls, invs)]
        invs = [t - _mm(t, x) for t, x in zip(invs, tmp)]
        s *= 2
    return invs


def _mixer_stage(qkv, gz, lru, gb, x_ref, mod_ref, wout_ref, nw_ref, lng_ref, lnb_ref,
                 y_ref, sout_ref, hout_ref, *, nb, tt, chunk, alpha):
    d = x_ref.shape[-1]
    lw = hout_ref.shape[-1]
    rows = nb * tt
    n_chunks = rows // chunk
    chunks_per_seq = tt // chunk
    log_chunk = chunk.bit_length() - 1

    row = lax.broadcasted_iota(jnp.int32, (rows, rows), 0)
    col = lax.broadcasted_iota(jnp.int32, (rows, rows), 1)
    incl = ((row >> log_chunk) == (col >> log_chunk)) & (col <= row)
    strict = incl & (col < row)
    gc = jnp.dot(jnp.where(incl, 1.0, 0.0), gb, precision=lax.Precision.HIGHEST, preferred_element_type=F32)
    gc_t = gc.T

    heads = range(DN_HEADS)
    qs = [qkv[:, h * DN_DK:(h + 1) * DN_DK] for h in heads]
    ks = [qkv[:, DN_WIDTH + h * DN_DK:DN_WIDTH + (h + 1) * DN_DK] for h in heads]
    vs = [qkv[:, 2 * DN_WIDTH + h * DN_DV:2 * DN_WIDTH + (h + 1) * DN_DV] for h in heads]
    gcols = [gc[:, h:h + 1] for h in heads]
    betas = [gb[:, DN_HEADS + h:DN_HEADS + h + 1] for h in heads]
    decays = [jnp.where(incl, jnp.exp(jnp.where(incl, gcols[h] - gc_t[h:h + 1, :], 0.0)), 0.0) for h in heads]
    kbs = [ks[h] * betas[h] for h in heads]
    qk_kks = [lax.dot_general(jnp.concatenate([qs[h], kbs[h]], axis=0).astype(BF16), ks[h].astype(BF16),
                              (((1,), (1,)), ((), ())), preferred_element_type=F32) for h in heads]
    qks = [qk_kks[h][:rows] * decays[h] for h in heads]
    lmats = [jnp.where(strict, qk_kks[h][rows:] * decays[h], 0.0) for h in heads]
    tmats = _unit_lower_inverses(lmats, row, col, chunk, min(INV_BASE_BLOCK, chunk))
    egcs = [jnp.exp(g) for g in gcols]
    uws = [_mm(tmats[h], jnp.concatenate([vs[h] * betas[h], kbs[h] * egcs[h]], axis=-1)) for h in heads]
    us = [uw[:, :DN_DV] for uw in uws]
    ws = [uw[:, DN_DV:] for uw in uws]
    qes = [qs[h] * egcs[h] for h in heads]

    a = lru[:, 0:lw]
    bacc = lru[:, lw:2 * lw]
    gy = lru[:, 2 * lw:3 * lw]
    t = lax.broadcasted_iota(jnp.int32, (rows, lw), 0) & (tt - 1)
    s = 1
    while s < tt:
        keep = t >= s
        a_prev = jnp.where(keep, pltpu.roll(a, s, 0), 1.0)
        b_prev = jnp.where(keep, pltpu.roll(bacc, s, 0), 0.0)
        bacc = a * b_prev + bacc
        a = a * a_prev
        s *= 2
    hs = (a.reshape(nb, tt, lw) * hout_ref[...] + bacc.reshape(nb, tt, lw))
    hout_ref[...] = hs[:, tt - 1:tt, :]
    o_b = hs.reshape(rows, lw) * gy

    v_new_parts = [[] for _ in heads]
    o_inter_parts = [[] for _ in heads]
    states = [None for _ in heads]
    for c in range(n_chunks):
        seq = c // chunks_per_seq
        lo, hi = c * chunk, (c + 1) * chunk
        if c % chunks_per_seq == 0:
            states = [sout_ref[seq, h] for h in heads]
        wqs = [_mm(jnp.concatenate([ws[h][lo:hi], qes[h][lo:hi]], axis=0), states[h]) for h in heads]
        v_news = [us[h][lo:hi] - wqs[h][:chunk] for h in heads]
        new_states = []
        for h in heads:
            v_new_parts[h].append(v_news[h])
            o_inter_parts[h].append(wqs[h][chunk:])
            g_last = gcols[h][hi - 1:hi, :]
            k_dec = ks[h][lo:hi] * jnp.exp(g_last - gcols[h][lo:hi])
            new_states.append(states[h] * jnp.exp(g_last) + lax.dot_general(
                k_dec.astype(BF16), v_news[h].astype(BF16), (((0,), (0,)), ((), ())),
                preferred_element_type=F32))
        states = new_states
        if (c + 1) % chunks_per_seq == 0:
            for h in heads:
                sout_ref[seq, h] = states[h]
    o_heads = []
    for h in heads:
        o = jnp.concatenate(o_inter_parts[h], axis=0) + _mm(qks[h], jnp.concatenate(v_new_parts[h], axis=0))
        ms = jnp.mean(o * o, axis=-1, keepdims=True)
        o_heads.append(o * lax.rsqrt(ms + NORM_EPS) * nw_ref[...] * gz[:, h * DN_DV:(h + 1) * DN_DV])
    o_a = jnp.concatenate(o_heads, axis=-1)

    mixed = _mm(jnp.concatenate([o_a, o_b], axis=-1), wout_ref[...]).reshape(nb, tt, d)
    gate = mod_ref[...][:, :, 2 * d:3 * d]
    y_ref[...] = _layer_norm(alpha * x_ref[...] + (1.0 + gate) * mixed, lng_ref[...], lnb_ref[...])


def _token_mix_kernel(x_ref, mod_ref, hdn_ref, hlru_ref, s0_ref, h0_ref,
                      w_ref, cwdn_ref, cwlru_ref, cblru_ref, alog_ref, dtb_ref, wr_ref, wi_ref, br_ref, bi_ref,
                      lam_ref, wout_ref, nw_ref, lng_ref, lnb_ref,
                      y_ref, tdn_ref, tlru_ref, sout_ref, hout_ref, wdn_sc, wlru_sc, *, nb, tt, chunk, alpha):
    @pl.when(pl.program_id(1) == 0)
    def _():
        sout_ref[...] = s0_ref[0]
        hout_ref[...] = h0_ref[...]

    qkv, gz, lru, gb = _inproj_stage(
        x_ref, mod_ref, hdn_ref, hlru_ref, w_ref, cwdn_ref, cwlru_ref, cblru_ref, alog_ref, dtb_ref,
        wr_ref, wi_ref, br_ref, bi_ref, lam_ref, tdn_ref, tlru_ref, wdn_sc, wlru_sc, nb=nb, tt=tt)
    _mixer_stage(qkv, gz, lru, gb, x_ref, mod_ref, wout_ref, nw_ref, lng_ref, lnb_ref,
                 y_ref, sout_ref, hout_ref, nb=nb, tt=tt, chunk=chunk, alpha=alpha)


def _token_mix_call(x, mod, hist_dn, hist_lru, s_dn_all, layer, h0, lw_, *, nb, tt, chunk, alpha):
    b, t, d = x.shape
    lw = hist_lru.shape[-1]
    grid = (b // nb, t // tt)
    seq_blk = lambda c: pl.BlockSpec((nb, tt, c), lambda i, j: (i, j, 0))
    per_seq = lambda r, c: pl.BlockSpec((nb, r, c), lambda i, j: (i, 0, 0))
    whole = lambda a: pl.BlockSpec(a.shape, lambda i, j: (0,) * a.ndim)
    state_in = pl.BlockSpec((1, nb, DN_HEADS, DN_DK, DN_DV), lambda i, j: (layer, i, 0, 0, 0))
    state_out = pl.BlockSpec((nb, DN_HEADS, DN_DK, DN_DV), lambda i, j: (i, 0, 0, 0))
    weights = [lw_[n] for n in ("w_cat", "dn_conv_w", "lru_conv_w", "lru_conv_b", "a_log", "dt_bias",
                                "w_r", "w_i", "b_r", "b_i", "lam", "w_out", "dn_norm_w", "ln1_g", "ln1_b")]
    return pl.pallas_call(
        functools.partial(_token_mix_kernel, nb=nb, tt=tt, chunk=chunk, alpha=alpha),
        out_shape=(
            jax.ShapeDtypeStruct((b, t, d), F32),
            jax.ShapeDtypeStruct((b, SUBLANES, DN_QKV), F32),
            jax.ShapeDtypeStruct((b, SUBLANES, lw), F32),
            jax.ShapeDtypeStruct((b, DN_HEADS, DN_DK, DN_DV), F32),
            jax.ShapeDtypeStruct((b, 1, lw), F32),
        ),
        grid=grid,
        in_specs=[seq_blk(d), per_seq(1, N_MOD * d), per_seq(SUBLANES, DN_QKV), per_seq(SUBLANES, lw),
                  state_in, per_seq(1, lw)] + [whole(a) for a in weights],
        out_specs=(seq_blk(d), per_seq(SUBLANES, DN_QKV), per_seq(SUBLANES, lw), state_out, per_seq(1, lw)),
        scratch_shapes=[pltpu.VMEM((nb, SUBLANES + tt, DN_QKV), F32), pltpu.VMEM((nb, SUBLANES + tt, lw), F32)],
        compiler_params=_params(("arbitrary", "arbitrary")),
        name="token_mixer",
    )(x, mod, hist_dn, hist_lru, s_dn_all, h0, *weights)


def _ffn_kernel(x_ref, mod_ref, wg_ref, wu_ref, wd_ref, lng_ref, lnb_ref, y_ref, h_sc, acc_sc, *, alpha):
    nb, tt, d = x_ref.shape
    f = pl.program_id(1)

    @pl.when(f == 0)
    def _():
        m = mod_ref[...]
        h = x_ref[...] * (1.0 + m[:, :, 4 * d:5 * d]) + m[:, :, 3 * d:4 * d]
        h_sc[...] = h.reshape(nb * tt, d).astype(BF16)
        acc_sc[...] = jnp.zeros_like(acc_sc)

    h = h_sc[...]
    act = _silu(jnp.dot(h, wg_ref[...], preferred_element_type=F32)) * jnp.dot(
        h, wu_ref[...], preferred_element_type=F32)
    acc_sc[...] += _mm(act, wd_ref[...])

    @pl.when(f == pl.num_programs(1) - 1)
    def _():
        gate = mod_ref[...][:, :, 5 * d:6 * d]
        ff = acc_sc[...].reshape(nb, tt, d)
        y_ref[...] = _layer_norm(alpha * x_ref[...] + (1.0 + gate) * ff, lng_ref[...], lnb_ref[...])


def _ff_tile(ff):
    for n in (2, 4, 7, 8, 11, 14, 16, 22, 28):
        if ff % n == 0 and (ff // n) % LANES == 0 and ff // n <= 2048:
            return ff // n
    return ff


def _dense_ffn(x, mod, lw_, *, nb, tt, alpha):
    b, t, d = x.shape
    ff = lw_["w_gate"].shape[-1]
    tf = _ff_tile(ff)
    grid = ((b // nb) * (t // tt), ff // tf)
    nt = t // tt
    seq_blk = pl.BlockSpec((nb, tt, d), lambda i, f: (i // nt, i % nt, 0))
    whole = lambda a: pl.BlockSpec(a.shape, lambda i, f: (0,) * a.ndim)
    return pl.pallas_call(
        functools.partial(_ffn_kernel, alpha=alpha),
        out_shape=jax.ShapeDtypeStruct((b, t, d), F32),
        grid=grid,
        in_specs=[seq_blk, pl.BlockSpec((nb, 1, N_MOD * d), lambda i, f: (i // nt, 0, 0)),
                  pl.BlockSpec((d, tf), lambda i, f: (0, f)), pl.BlockSpec((d, tf), lambda i, f: (0, f)),
                  pl.BlockSpec((tf, d), lambda i, f: (f, 0)), whole(lw_["ln2_g"]), whole(lw_["ln2_b"])],
        out_specs=seq_blk,
        scratch_shapes=[pltpu.VMEM((nb * tt, d), BF16), pltpu.VMEM((nb * tt, d), F32)],
        compiler_params=_params(("arbitrary", "arbitrary")),
        name="dense_ffn",
    )(x, mod, lw_["w_gate"], lw_["w_up"], lw_["w_down"], lw_["ln2_g"], lw_["ln2_b"])


MOE_CHUNK = 512
MOE_SLOT_TILE = 512
MOE_SUB = 128
MOE_COMBINE_BLOCK = 256
ROUTE_I1, ROUTE_I2, ROUTE_R1, ROUTE_R2, ROUTE_W1, ROUTE_W2 = range(6)


def _router_kernel(x_ref, mod_ref, wr_ref, base_ref, h_ref, meta_ref, blkbase_ref, cnt_ref, run_sc,
                   *, n_experts):
    nb, tt, d = x_ref.shape
    rows = nb * tt

    @pl.when(pl.program_id(0) == 0)
    def _():
        run_sc[...] = base_ref[...]

    m = mod_ref[...]
    h = (x_ref[...] * (1.0 + m[:, :, 4 * d:5 * d]) + m[:, :, 3 * d:4 * d]).reshape(rows, d)
    h_ref[...] = h.astype(BF16)
    logits = jnp.dot(h, wr_ref[...], precision=lax.Precision.HIGHEST, preferred_element_type=F32)
    lane = lax.broadcasted_iota(jnp.int32, logits.shape, 1)
    neg = jnp.float32(-jnp.inf)
    lg = jnp.where(lane < n_experts, logits, neg)
    m1 = jnp.max(lg, axis=-1, keepdims=True)
    i1 = jnp.min(jnp.where(lg == m1, lane, LANES), axis=-1, keepdims=True)
    lg2 = jnp.where(lane == i1, neg, lg)
    m2 = jnp.max(lg2, axis=-1, keepdims=True)
    i2 = jnp.min(jnp.where(lg2 == m2, lane, LANES), axis=-1, keepdims=True)
    e2 = jnp.exp(m2 - m1)
    w1 = 1.0 / (1.0 + e2)
    w2 = e2 / (1.0 + e2)
    sel = jnp.where(lane == i1, 1.0, jnp.where(lane == i2, 1.0, 0.0))
    r = lax.broadcasted_iota(jnp.int32, (rows, rows), 0)
    c = lax.broadcasted_iota(jnp.int32, (rows, rows), 1)
    rank = _mm(jnp.where(c < r, 1.0, 0.0), sel) + run_sc[0:1, :]
    r1 = jnp.sum(jnp.where(lane == i1, rank, 0.0), axis=-1, keepdims=True)
    r2 = jnp.sum(jnp.where(lane == i2, rank, 0.0), axis=-1, keepdims=True)
    fields = (i1.astype(F32), i2.astype(F32), r1, r2, w1, w2)
    meta = jnp.zeros_like(logits)
    for k, v in enumerate(fields):
        meta = jnp.where(lane == k, v, meta)
    meta_ref[...] = meta
    blkbase_ref[0] = run_sc[...]
    run_sc[...] = run_sc[...] + jnp.sum(sel, axis=0, keepdims=True)
    cnt_ref[...] = run_sc[...]


def _router(x, mod, w_router, base, *, nb, tt, n_experts):
    b, t, d = x.shape
    rows = nb * tt
    nt = t // tt
    nblk = (b // nb) * nt
    whole = lambda a: pl.BlockSpec(a.shape, lambda i: (0,) * a.ndim)
    return pl.pallas_call(
        functools.partial(_router_kernel, n_experts=n_experts),
        out_shape=(jax.ShapeDtypeStruct((nblk * rows, d), BF16),
                   jax.ShapeDtypeStruct((nblk * rows, LANES), F32),
                   jax.ShapeDtypeStruct((nblk, SUBLANES, LANES), F32),
                   jax.ShapeDtypeStruct((SUBLANES, LANES), F32)),
        grid=(nblk,),
        in_specs=[pl.BlockSpec((nb, tt, d), lambda i: (i // nt, i % nt, 0)),
                  pl.BlockSpec((nb, 1, N_MOD * d), lambda i: (i // nt, 0, 0)),
                  whole(w_router), whole(base)],
        out_specs=(pl.BlockSpec((rows, d), lambda i: (i, 0)),
                   pl.BlockSpec((rows, LANES), lambda i: (i, 0)),
                   pl.BlockSpec((1, SUBLANES, LANES), lambda i: (i, 0, 0)),
                   pl.BlockSpec((SUBLANES, LANES), lambda i: (0, 0))),
        scratch_shapes=[pltpu.VMEM((SUBLANES, LANES), F32)],
        compiler_params=_params(("arbitrary",)),
        name="moe_router",
    )(x, mod, w_router, base)


def _select(s_ref, off, width, ids):
    hit1 = s_ref[0:1, pl.ds(off, width)] == ids
    hit2 = s_ref[1:2, pl.ds(off, width)] == ids
    return hit1, hit2, jnp.where(hit1, 1.0, jnp.where(hit2, 1.0, 0.0)).astype(BF16)


def _gather_kernel(clo_ref, chi_ref, srow_ref, wrow_ref, *refs, group_chunks):
    n_groups = len(group_chunks)
    h_refs = refs[:n_groups]
    xs_ref, ws_ref, acc_sc, wacc_sc = refs[n_groups:]
    g = pl.program_id(0)
    n_sub = MOE_SLOT_TILE // MOE_SUB
    for j in range(n_sub):
        q = g * n_sub + j
        ids = g * MOE_SLOT_TILE + j * MOE_SUB + lax.broadcasted_iota(jnp.int32, (MOE_SUB, MOE_CHUNK), 0)
        acc_sc[...] = jnp.zeros_like(acc_sc)
        wacc_sc[...] = jnp.zeros_like(wacc_sc)
        first = 0
        for h_ref, n_chunks in zip(h_refs, group_chunks):
            def body(c, carry, h_ref=h_ref, first=first):
                off = pl.multiple_of(c * MOE_CHUNK, MOE_CHUNK)
                local = pl.multiple_of((c - first) * MOE_CHUNK, MOE_CHUNK)
                hit1, hit2, p = _select(srow_ref, off, MOE_CHUNK, ids)
                acc_sc[...] += jnp.dot(p, h_ref[pl.ds(local, MOE_CHUNK), :], preferred_element_type=F32)
                w = (jnp.where(hit1, wrow_ref[0:1, pl.ds(off, MOE_CHUNK)], 0.0)
                     + jnp.where(hit2, wrow_ref[1:2, pl.ds(off, MOE_CHUNK)], 0.0))
                wacc_sc[...] += jnp.broadcast_to(jnp.sum(w, axis=-1, keepdims=True), wacc_sc.shape)
                return carry

            lax.fori_loop(jnp.maximum(clo_ref[q], first), jnp.minimum(chi_ref[q], first + n_chunks), body, 0)
            first += n_chunks
        xs_ref[j * MOE_SUB:(j + 1) * MOE_SUB, :] = acc_sc[...].astype(BF16)
        ws_ref[j * MOE_SUB:(j + 1) * MOE_SUB, :] = wacc_sc[...]


def _gather_slots(c_lo, c_hi, hs, srow, wrow, n_tiles):
    d = hs[0].shape[-1]
    vmem = pl.BlockSpec(memory_space=pltpu.VMEM)
    return pl.pallas_call(
        functools.partial(_gather_kernel, group_chunks=tuple(h.shape[0] // MOE_CHUNK for h in hs)),
        out_shape=(jax.ShapeDtypeStruct((n_tiles * MOE_SLOT_TILE, d), BF16),
                   jax.ShapeDtypeStruct((n_tiles * MOE_SLOT_TILE, LANES), F32)),
        grid_spec=pltpu.PrefetchScalarGridSpec(
            num_scalar_prefetch=2, grid=(n_tiles,),
            in_specs=[vmem, vmem] + [vmem] * len(hs),
            out_specs=(pl.BlockSpec((MOE_SLOT_TILE, d), lambda g, lo, hi: (g, 0)),
                       pl.BlockSpec((MOE_SLOT_TILE, LANES), lambda g, lo, hi: (g, 0))),
            scratch_shapes=[pltpu.VMEM((MOE_SUB, d), F32), pltpu.VMEM((MOE_SUB, LANES), F32)]),
        compiler_params=_params(("arbitrary",)),
        name="moe_gather",
    )(c_lo, c_hi, srow, wrow, *hs)


def _expert_kernel(te_ref, tv_ref, xs_ref, ws_ref, wg_ref, wu_ref, wd_ref, o_ref, *, ff_chunk):
    g = pl.program_id(0)

    @pl.when(tv_ref[g] != 0)
    def _():
        x = xs_ref[...]
        ff = wg_ref.shape[-1]
        acc = jnp.zeros(o_ref.shape, F32)
        for f0 in range(0, ff, ff_chunk):
            gate = jnp.dot(x, wg_ref[0, :, f0:f0 + ff_chunk], preferred_element_type=F32)
            up = jnp.dot(x, wu_ref[0, :, f0:f0 + ff_chunk], preferred_element_type=F32)
            acc = acc + _mm(_silu(gate) * up, wd_ref[0, f0:f0 + ff_chunk, :])
        o_ref[...] = (ws_ref[:, 0:1] * acc).astype(BF16)

    @pl.when(tv_ref[g] == 0)
    def _():
        o_ref[...] = jnp.zeros_like(o_ref)


def _expert_ffn(tile_expert, tile_valid, xs, ws, lw_):
    s_total, d = xs.shape
    _, _, ff = lw_["w_gate"].shape
    n_tiles = s_total // MOE_SLOT_TILE
    once = pl.Buffered(1)
    return pl.pallas_call(
        functools.partial(_expert_kernel, ff_chunk=_ff_tile(ff)),
        out_shape=jax.ShapeDtypeStruct((s_total, d), BF16),
        grid_spec=pltpu.PrefetchScalarGridSpec(
            num_scalar_prefetch=2, grid=(n_tiles,),
            in_specs=[pl.BlockSpec((MOE_SLOT_TILE, d), lambda g, te, tv: (g, 0)),
                      pl.BlockSpec((MOE_SLOT_TILE, LANES), lambda g, te, tv: (g, 0)),
                      pl.BlockSpec((1, d, ff), lambda g, te, tv: (te[g], 0, 0), pipeline_mode=once),
                      pl.BlockSpec((1, d, ff), lambda g, te, tv: (te[g], 0, 0), pipeline_mode=once),
                      pl.BlockSpec((1, ff, d), lambda g, te, tv: (te[g], 0, 0), pipeline_mode=once)],
            out_specs=pl.BlockSpec((MOE_SLOT_TILE, d), lambda g, te, tv: (g, 0))),
        compiler_params=_params(("arbitrary",)),
        name="expert_ffn",
    )(tile_expert, tile_valid, xs, ws, lw_["w_gate"], lw_["w_up"], lw_["w_down"])


def _combine_kernel(ic_ref, ik_ref, if_ref, x_ref, mod_ref, sc_ref, os_ref, lng_ref, lnb_ref, y_ref, acc_sc,
                    *, alpha):
    nb, tt, d = x_ref.shape
    rows = nb * tt
    w = pl.program_id(0)
    flags = if_ref[w]

    @pl.when((flags & 1) != 0)
    def _():
        acc_sc[...] = jnp.zeros_like(acc_sc)

    @pl.when((flags & 4) != 0)
    def _():
        ids = ik_ref[w] * MOE_COMBINE_BLOCK + lax.broadcasted_iota(jnp.int32, (rows, MOE_COMBINE_BLOCK), 1)
        q = jnp.where(sc_ref[:, 0:1] == ids, 1.0, jnp.where(sc_ref[:, 1:2] == ids, 1.0, 0.0)).astype(BF16)
        acc_sc[...] += jnp.dot(q, os_ref[...], preferred_element_type=F32)

    @pl.when((flags & 2) != 0)
    def _():
        gate = mod_ref[...][:, :, 5 * d:6 * d]
        ff = acc_sc[...].reshape(nb, tt, d)
        y_ref[...] = _layer_norm(alpha * x_ref[...] + (1.0 + gate) * ff, lng_ref[...], lnb_ref[...])


def _combine(items, x, mod, slots_col, out_sorted, lw_, *, nb, tt, alpha):
    item_chunk, item_block, item_flags = items
    b, t, d = x.shape
    rows = nb * tt
    nt = t // tt
    whole = lambda a: pl.BlockSpec(a.shape, lambda w, ic, ik, fl: (0,) * a.ndim)
    seq_blk = pl.BlockSpec((nb, tt, d), lambda w, ic, ik, fl: (ic[w] // nt, ic[w] % nt, 0))
    return pl.pallas_call(
        functools.partial(_combine_kernel, alpha=alpha),
        out_shape=jax.ShapeDtypeStruct((b, t, d), F32),
        grid_spec=pltpu.PrefetchScalarGridSpec(
            num_scalar_prefetch=3, grid=(item_chunk.shape[0],),
            in_specs=[seq_blk,
                      pl.BlockSpec((nb, 1, N_MOD * d), lambda w, ic, ik, fl: (ic[w] // nt, 0, 0)),
                      pl.BlockSpec((rows, TOP_K), lambda w, ic, ik, fl: (ic[w], 0)),
                      pl.BlockSpec((MOE_COMBINE_BLOCK, d), lambda w, ic, ik, fl: (ik[w], 0)),
                      whole(lw_["ln2_g"]), whole(lw_["ln2_b"])],
            out_specs=seq_blk,
            scratch_shapes=[pltpu.VMEM((rows, d), F32)]),
        compiler_params=_params(("arbitrary",)),
        name="moe_combine",
    )(item_chunk, item_block, item_flags, x, mod, slots_col, out_sorted, lw_["ln2_g"], lw_["ln2_b"])


def _count_le(sorted_vals, queries):
    return jnp.sum(sorted_vals[None, :] <= queries[:, None], axis=1).astype(jnp.int32)


def _combine_items(lo, hi, n_items):
    n_chunks, n_experts = lo.shape
    first = lo // MOE_COMBINE_BLOCK
    count = jnp.where(hi > lo, (hi - 1) // MOE_COMBINE_BLOCK - first + 1, 0).reshape(-1)
    ends = jnp.cumsum(count)
    total = ends[-1]
    w = jnp.arange(n_items, dtype=jnp.int32)
    pair = jnp.minimum(_count_le(ends, w), n_chunks * n_experts - 1)
    within = w - (ends[pair] - count[pair])
    valid = w < total
    chunk = pair // n_experts
    block = first.reshape(-1)[pair] + within
    last_valid = jnp.maximum(total - 1, 0)
    chunk = jnp.where(valid, chunk, chunk[last_valid])
    block = jnp.where(valid, block, block[last_valid])
    prev_chunk = jnp.concatenate([jnp.full((1,), -1, jnp.int32), chunk[:-1]])
    next_chunk = jnp.concatenate([chunk[1:], jnp.full((1,), -1, jnp.int32)])
    next_valid = jnp.concatenate([valid[1:], jnp.zeros((1,), bool)])
    is_first = valid & (chunk != prev_chunk)
    is_last = valid & ((chunk != next_chunk) | jnp.logical_not(next_valid))
    flags = is_first.astype(jnp.int32) + 2 * is_last.astype(jnp.int32) + 4 * valid.astype(jnp.int32)
    return chunk.astype(jnp.int32), block.astype(jnp.int32), flags


def _moe_layer(xs_in, mods, lw_, tilings, alpha):
    n_experts = lw_["w_gate"].shape[0]
    base = jnp.zeros((SUBLANES, LANES), F32)
    hs, metas, bases = [], [], []
    for x, mod, (nb, tt) in zip(xs_in, mods, tilings):
        assert nb * tt == MOE_CHUNK and (x.shape[0] * x.shape[1]) % MOE_CHUNK == 0
        h, meta, blkbase, base = _router(x, mod, lw_["w_router"], base, nb=nb, tt=tt, n_experts=n_experts)
        hs.append(h)
        metas.append(meta)
        bases.append(blkbase[:, 0, :n_experts])
    meta = jnp.concatenate(metas, axis=0)
    n = meta.shape[0]
    cum = jnp.concatenate(bases + [base[0:1, :n_experts]], axis=0).astype(jnp.int32)
    counts = cum[-1]
    sizes = ((counts + MOE_SLOT_TILE - 1) // MOE_SLOT_TILE) * MOE_SLOT_TILE
    run_end = jnp.cumsum(sizes)
    run_start = run_end - sizes
    col = lambda k: meta[:, k].astype(jnp.int32)
    slot1 = run_start[col(ROUTE_I1)] + col(ROUTE_R1)
    slot2 = run_start[col(ROUTE_I2)] + col(ROUTE_R2)
    pad_rows = lambda a, fill: jnp.concatenate(
        [a, jnp.full((SUBLANES - a.shape[0], n), fill, a.dtype)], axis=0)
    srow = pad_rows(jnp.stack([slot1, slot2]), -1)
    wrow = pad_rows(jnp.stack([meta[:, ROUTE_W1], meta[:, ROUTE_W2]]), 0.0)

    n_tiles = (TOP_K * n + MOE_SLOT_TILE - 1) // MOE_SLOT_TILE + n_experts
    tile_start = jnp.arange(n_tiles, dtype=jnp.int32) * MOE_SLOT_TILE
    tile_expert = jnp.minimum(_count_le(run_end, tile_start), n_experts - 1)
    tile_valid = (tile_start < run_end[-1]).astype(jnp.int32)
    n_sub = MOE_SLOT_TILE // MOE_SUB
    sub_expert = jnp.repeat(tile_expert, n_sub)
    sub_rank0 = jnp.arange(n_tiles * n_sub, dtype=jnp.int32) * MOE_SUB - run_start[sub_expert]
    cum_sub = cum[:, sub_expert]
    sub_valid = jnp.repeat(tile_valid, n_sub)
    c_lo = jnp.sum(cum_sub[1:] <= sub_rank0[None, :], axis=0).astype(jnp.int32) * sub_valid
    c_hi = jnp.sum(cum_sub[:-1] < sub_rank0[None, :] + MOE_SUB, axis=0).astype(jnp.int32) * sub_valid

    x_sorted, w_sorted = _gather_slots(c_lo, c_hi, hs, srow, wrow, n_tiles)
    out_sorted = _expert_ffn(tile_expert, tile_valid, x_sorted, w_sorted, lw_)

    slots_col = jnp.stack([slot1, slot2], axis=-1)
    outs = []
    chunk0 = 0
    for x, mod, (nb, tt) in zip(xs_in, mods, tilings):
        nc = x.shape[0] * x.shape[1] // MOE_CHUNK
        lo = run_start[None, :] + cum[chunk0:chunk0 + nc]
        hi = run_start[None, :] + cum[chunk0 + 1:chunk0 + nc + 1]
        n_items = nc * n_experts + (TOP_K * nc * MOE_CHUNK) // MOE_COMBINE_BLOCK + 2 * n_experts
        items = _combine_items(lo, hi, n_items)
        sc = slots_col[chunk0 * MOE_CHUNK:(chunk0 + nc) * MOE_CHUNK]
        outs.append(_combine(items, x, mod, sc, out_sorted, lw_, nb=nb, tt=tt, alpha=alpha))
        chunk0 += nc
    return outs


def _pad_lanes(v, width=LANES):
    return jnp.pad(v, ((0, 0), (0, width - v.shape[-1])))


def _block_diag_halves(w):
    nblk, c, _ = w.shape
    half = nblk // 2
    out = jnp.zeros((2, half * c, half * c), w.dtype)
    for i in range(nblk):
        j = i % half
        out = out.at[i // half, j * c:(j + 1) * c, j * c:(j + 1) * c].set(w[i])
    return out


def _layer_weights(l, p, d, lw):
    w_in = p["w_in"][l]
    a_off = DN_QKV
    z_off = a_off + 2 * DN_HEADS
    x_off = z_off + DN_WIDTH
    y_off = x_off + lw
    w_ab = _pad_lanes(w_in[:, a_off:z_off])
    w_cat = jnp.concatenate([w_in[:, :DN_QKV], w_in[:, z_off:y_off + lw], w_ab], axis=1).astype(BF16)
    out = {
        "w_cat": w_cat,
        "dn_conv_w": p["dn_conv_w"][l],
        "lru_conv_w": p["lru_conv_w"][l],
        "lru_conv_b": p["lru_conv_b"][l][None],
        "a_log": _pad_lanes(p["dn_a_log"][l][None]),
        "dt_bias": _pad_lanes(p["dn_dt_bias"][l][None]),
        "w_r": _block_diag_halves(p["lru_w_r"][l]).astype(BF16),
        "w_i": _block_diag_halves(p["lru_w_i"][l]).astype(BF16),
        "b_r": p["lru_b_r"][l][None],
        "b_i": p["lru_b_i"][l][None],
        "lam": p["lru_lambda"][l][None],
        "w_out": p["w_out"][l].astype(BF16),
        "dn_norm_w": p["dn_norm_w"][l][None],
        "ln1_g": p["ln1_g"][l][None],
        "ln1_b": p["ln1_b"][l][None],
        "ln2_g": p["ln2_g"][l][None],
        "ln2_b": p["ln2_b"][l][None],
    }
    j = l // 2
    if l % 2 == 0:
        out.update(w_gate=p["ffn_w_gate"][j].astype(BF16), w_up=p["ffn_w_up"][j].astype(BF16),
                   w_down=p["ffn_w_down"][j].astype(BF16))
    else:
        out.update(w_router=_pad_lanes(p["moe_w_router"][j]),
                   w_gate=p["moe_w_gate"][j].astype(BF16), w_up=p["moe_w_up"][j].astype(BF16),
                   w_down=p["moe_w_down"][j].astype(BF16))
    return out


def _tiling(b, t):
    if t >= MXU_DIM:
        tt = MXU_DIM
        return dict(mixer=(1, tt, min(PROMPT_CHUNK, tt)), ffn=(1, min(t, 2 * MXU_DIM)))
    assert t == SUBLANES, "short sequences must be exactly one sublane tile long"
    return dict(mixer=(min(b, 16), t, t), ffn=(min(b, 64), t))


def _token_mix(x, mod, conv_dn, s_dn_all, layer, conv_lru, s_lru, lw_, til, alpha):
    pad_hist = lambda c: jnp.pad(c, ((0, 0), (SUBLANES - (CONV_W - 1), 0), (0, 0)))
    nb, tt, chunk = til["mixer"]
    x, tail_dn, tail_lru, s_new, h_new = _token_mix_call(
        x, mod, pad_hist(conv_dn), pad_hist(conv_lru), s_dn_all, layer, s_lru[:, None, :], lw_,
        nb=nb, tt=tt, chunk=chunk, alpha=alpha)
    return (x, tail_dn[:, SUBLANES - (CONV_W - 1):, :], s_new, tail_lru[:, SUBLANES - (CONV_W - 1):, :],
            h_new[:, 0, :])


def kernel(x_prompt, x_sample, cache_dn_conv, state_dn, cache_lru_conv, state_lru, c_prompt, c_sample,
           w_ada, b_ada, w_in, dn_conv_w, dn_a_log, dn_dt_bias, dn_norm_w,
           lru_conv_w, lru_conv_b, lru_w_r, lru_b_r, lru_w_i, lru_b_i, lru_lambda, w_out,
           ln1_g, ln1_b, ln2_g, ln2_b, ffn_w_gate, ffn_w_up, ffn_w_down,
           moe_w_router, moe_w_gate, moe_w_up, moe_w_down):
    p = dict(w_in=w_in, dn_conv_w=dn_conv_w, dn_a_log=dn_a_log, dn_dt_bias=dn_dt_bias, dn_norm_w=dn_norm_w,
             lru_conv_w=lru_conv_w, lru_conv_b=lru_conv_b, lru_w_r=lru_w_r, lru_b_r=lru_b_r,
             lru_w_i=lru_w_i, lru_b_i=lru_b_i, lru_lambda=lru_lambda, w_out=w_out,
             ln1_g=ln1_g, ln1_b=ln1_b, ln2_g=ln2_g, ln2_b=ln2_b,
             ffn_w_gate=ffn_w_gate, ffn_w_up=ffn_w_up, ffn_w_down=ffn_w_down,
             moe_w_router=moe_w_router, moe_w_gate=moe_w_gate, moe_w_up=moe_w_up, moe_w_down=moe_w_down)
    depth, d, _ = w_ada.shape
    bp = x_prompt.shape[0]
    bs = x_sample.shape[0]
    lw = cache_lru_conv.shape[-1]
    alpha = (2 * depth) ** 0.25
    weights = [_layer_weights(l, p, d, lw) for l in range(depth)]

    c_all = jnp.concatenate([c_prompt, c_sample], axis=0)
    mod_all = _modulation(c_all, w_ada, b_ada)
    groups = [
        dict(x=x_prompt, rows=slice(0, bp), conv_dn=jnp.zeros((depth, bp, CONV_W - 1, DN_QKV), F32),
             s_dn=jnp.zeros((depth, bp, DN_HEADS, DN_DK, DN_DV), F32),
             conv_lru=jnp.zeros((depth, bp, CONV_W - 1, lw), F32), s_lru=jnp.zeros((depth, bp, lw), F32)),
        dict(x=x_sample, rows=slice(bp, bp + bs), conv_dn=cache_dn_conv, s_dn=state_dn,
             conv_lru=cache_lru_conv, s_lru=state_lru),
    ]
    for g in groups:
        g["til"] = _tiling(g["x"].shape[0], g["x"].shape[1])
        g["new"] = [[], [], [], []]
    for l in range(depth):
        lw_ = weights[l]
        mods = [mod_all[l, g["rows"]][:, None, :] for g in groups]
        for g, mod in zip(groups, mods):
            res = _token_mix(g["x"], mod, g["conv_dn"][l], g["s_dn"], l, g["conv_lru"][l], g["s_lru"][l],
                             lw_, g["til"], alpha)
            g["x"] = res[0]
            for acc, new in zip(g["new"], res[1:]):
                acc.append(new)
        if l % 2 == 0:
            for g, mod in zip(groups, mods):
                nb, tt = g["til"]["ffn"]
                g["x"] = _dense_ffn(g["x"], mod, lw_, nb=nb, tt=tt, alpha=alpha)
        else:
            xs = _moe_layer([g["x"] for g in groups], mods, lw_, [g["til"]["ffn"] for g in groups], alpha)
            for g, x in zip(groups, xs):
                g["x"] = x
    states = [jnp.stack(acc) for g in groups for acc in g["new"]]
    return (groups[0]["x"], groups[1]["x"]) + tuple(states)
```

```python
import functools
import math

import jax
import jax.numpy as jnp
from jax import lax
from jax.experimental import pallas as pl
from jax.experimental.pallas import tpu as pltpu

F32 = jnp.float32
BF16 = jnp.bfloat16

DN_HEADS = 4
DN_DK = 128
DN_DV = 128
DN_WIDTH = DN_HEADS * DN_DV
DN_QKV = 3 * DN_WIDTH
LRU_BLOCKS = 8
LRU_C = 8.0
CONV_W = 4
N_MOD = 6
TOP_K = 2
LN_EPS = 1e-5
NORM_EPS = 1e-6

SUBLANES = 8
LANES = 128
MXU_DIM = 256
VMEM_LIMIT_BYTES = 56 * 1024 * 1024

PROMPT_CHUNK = 64
INV_BASE_BLOCK = 16


def _sigmoid(x):
    return 1.0 / (1.0 + jnp.exp(-x))


def _silu(x):
    return x * _sigmoid(x)


def _softplus(x):
    return jnp.maximum(x, 0.0) + jnp.log1p(jnp.exp(-jnp.abs(x)))


def _gelu_tanh(x):
    return 0.5 * x * (1.0 + jnp.tanh(math.sqrt(2.0 / math.pi) * (x + 0.044715 * (x * x * x))))


def _mm(a, b):
    return jnp.dot(a.astype(BF16), b.astype(BF16), preferred_element_type=F32)


def _layer_norm(x, g, b):
    mu = jnp.mean(x, axis=-1, keepdims=True)
    xc = x - mu
    var = jnp.mean(xc * xc, axis=-1, keepdims=True)
    return xc * lax.rsqrt(var + LN_EPS) * g + b


def _params(sem):
    return pltpu.CompilerParams(dimension_semantics=sem, vmem_limit_bytes=VMEM_LIMIT_BYTES)


def _mod_kernel(c_ref, w_ref, b_ref, o_ref):
    sc = _silu(c_ref[...])
    o_ref[0] = _mm(sc, w_ref[0]) + b_ref[0]


def _modulation(c_all, w_ada, b_ada):
    depth, d, n = w_ada.shape
    rows = c_all.shape[0]
    tn = 1536 if n % 1536 == 0 else n
    return pl.pallas_call(
        _mod_kernel,
        out_shape=jax.ShapeDtypeStruct((depth, rows, n), F32),
        grid=(depth, n // tn),
        in_specs=[
            pl.BlockSpec((rows, d), lambda l, j: (0, 0)),
            pl.BlockSpec((1, d, tn), lambda l, j: (l, 0, j)),
            pl.BlockSpec((1, 1, tn), lambda l, j: (l, 0, j)),
        ],
        out_specs=pl.BlockSpec((1, rows, tn), lambda l, j: (l, 0, j)),
        compiler_params=_params(("arbitrary", "arbitrary")),
        name="adaln_modulation",
    )(c_all, w_ada, b_ada.reshape(depth, 1, n))


def _causal_conv(u, win_ref, w, nb, tt):
    c = u.shape[-1]
    win_ref[:, SUBLANES:, :] = u.reshape(nb, tt, c)
    out = u * w[CONV_W - 1:CONV_W, :]
    for j in range(1, CONV_W):
        prev = win_ref[:, SUBLANES - j:SUBLANES - j + tt, :].reshape(nb * tt, c)
        out = out + prev * w[CONV_W - 1 - j:CONV_W - j, :]
    tail = win_ref[:, tt:tt + SUBLANES, :]
    win_ref[:, 0:SUBLANES, :] = tail
    return out, tail


def _l2norm_heads(x, scale):
    outs = []
    for h in range(DN_HEADS):
        xh = x[:, h * DN_DK:(h + 1) * DN_DK]
        ss = jnp.sum(xh * xh, axis=-1, keepdims=True)
        outs.append(xh * (lax.rsqrt(ss + NORM_EPS) * scale))
    return jnp.concatenate(outs, axis=-1)


def _inproj_stage(x_ref, mod_ref, hdn_ref, hlru_ref, w_ref, cwdn_ref, cwlru_ref, cblru_ref,
                  alog_ref, dtb_ref, wr_ref, wi_ref, br_ref, bi_ref, lam_ref,
                  tdn_ref, tlru_ref, wdn_sc, wlru_sc, *, nb, tt):
    d = x_ref.shape[-1]
    lw = hlru_ref.shape[-1]
    rows = nb * tt

    @pl.when(pl.program_id(1) == 0)
    def _():
        wdn_sc[:, 0:SUBLANES, :] = hdn_ref[...]
        wlru_sc[:, 0:SUBLANES, :] = hlru_ref[...]

    m = mod_ref[...]
    shift = m[:, :, 0:d]
    scale = m[:, :, d:2 * d]
    h = (x_ref[...] * (1.0 + scale) + shift).reshape(rows, d)
    proj = _mm(h, w_ref[...])
    u_dn = proj[:, 0:DN_QKV]
    z = proj[:, DN_QKV:DN_QKV + DN_WIDTH]
    u_lru = proj[:, DN_QKV + DN_WIDTH:DN_QKV + DN_WIDTH + lw]
    y = proj[:, DN_QKV + DN_WIDTH + lw:DN_QKV + DN_WIDTH + 2 * lw]
    ab = proj[:, DN_QKV + DN_WIDTH + 2 * lw:]

    conv_dn, tail_dn = _causal_conv(u_dn, wdn_sc, cwdn_ref[...], nb, tt)
    qkv_c = _silu(conv_dn)
    q = _l2norm_heads(qkv_c[:, 0:DN_WIDTH], DN_DK ** -0.5)
    k = _l2norm_heads(qkv_c[:, DN_WIDTH:2 * DN_WIDTH], 1.0)
    v = qkv_c[:, 2 * DN_WIDTH:DN_QKV]
    qkv = jnp.concatenate([q, k, v], axis=-1)
    gz = _silu(z)
    lane = lax.broadcasted_iota(jnp.int32, ab.shape, 1)
    g_full = -jnp.exp(alog_ref[...]) * _softplus(ab + dtb_ref[...])
    gb = jnp.where(lane < DN_HEADS, g_full, _sigmoid(ab))

    conv_lru, tail_lru = _causal_conv(u_lru, wlru_sc, cwlru_ref[...], nb, tt)
    xc = conv_lru + cblru_ref[...]
    half = lw // 2
    r_pre = jnp.concatenate([_mm(xc[:, :half], wr_ref[0]), _mm(xc[:, half:], wr_ref[1])], axis=-1)
    i_pre = jnp.concatenate([_mm(xc[:, :half], wi_ref[0]), _mm(xc[:, half:], wi_ref[1])], axis=-1)
    r = _sigmoid(r_pre + br_ref[...])
    i = _sigmoid(i_pre + bi_ref[...])
    log_a = -LRU_C * r * _softplus(-lam_ref[...])
    a = jnp.exp(log_a)
    th = jnp.tanh(log_a)
    inp = jnp.sqrt(-2.0 * th / (1.0 - th)) * (i * xc)
    lru = jnp.concatenate([a, inp, _gelu_tanh(y)], axis=-1)

    tdn_ref[...] = tail_dn
    tlru_ref[...] = tail_lru
    return qkv, gz, lru, gb


def _unit_lower_inverses(ls, row, col, chunk, base):
    def same_block(s):
        k = s.bit_length() - 1
        return (row >> k) == (col >> k)

    eye = jnp.where(row == col, 1.0, 0.0)
    base_mask = same_block(base)
    powers = [jnp.where(base_mask, l, 0.0) for l in ls]
    invs = [eye - d for d in powers]
    p = 2
    while p < base:
        powers = [_mm(d, d) for d in powers]
        invs = [t + _mm(t, d) for t, d in zip(invs, powers)]
        p *= 2
    s = base
    while s < chunk:
        off_mask = same_block(2 * s) & jnp.logical_not(same_block(s))
        tmp = [_mm(jnp.where(off_mask, l, 0.0), t) for l, t in zip(ls, invs)]
        invs = [t - _mm(t, x) for t, x in zip(invs, tmp)]
        s *= 2
    return invs


def _mixer_stage(qkv, gz, lru, gb, x_ref, mod_ref, wout_ref, nw_ref, lng_ref, lnb_ref,
                 y_ref, sout_ref, hout_ref, *, nb, tt, chunk, alpha):
    d = x_ref.shape[-1]
    lw = hout_ref.shape[-1]
    rows = nb * tt
    n_chunks = rows // chunk
    chunks_per_seq = tt // chunk
    log_chunk = chunk.bit_length() - 1

    row = lax.broadcasted_iota(jnp.int32, (rows, rows), 0)
    col = lax.broadcasted_iota(jnp.int32, (rows, rows), 1)
    incl = ((row >> log_chunk) == (col >> log_chunk)) & (col <= row)
    strict = incl & (col < row)
    gc = jnp.dot(jnp.where(incl, 1.0, 0.0), gb, precision=lax.Precision.HIGHEST, preferred_element_type=F32)
    gc_t = gc.T

    heads = range(DN_HEADS)
    qs = [qkv[:, h * DN_DK:(h + 1) * DN_DK] for h in heads]
    ks = [qkv[:, DN_WIDTH + h * DN_DK:DN_WIDTH + (h + 1) * DN_DK] for h in heads]
    vs = [qkv[:, 2 * DN_WIDTH + h * DN_DV:2 * DN_WIDTH + (h + 1) * DN_DV] for h in heads]
    gcols = [gc[:, h:h + 1] for h in heads]
    betas = [gb[:, DN_HEADS + h:DN_HEADS + h + 1] for h in heads]
    decays = [jnp.where(incl, jnp.exp(jnp.where(incl, gcols[h] - gc_t[h:h + 1, :], 0.0)), 0.0) for h in heads]
    kbs = [ks[h] * betas[h] for h in heads]
    qk_kks = [lax.dot_general(jnp.concatenate([qs[h], kbs[h]], axis=0).astype(BF16), ks[h].astype(BF16),
                              (((1,), (1,)), ((), ())), preferred_element_type=F32) for h in heads]
    qks = [qk_kks[h][:rows] * decays[h] for h in heads]
    lmats = [jnp.where(strict, qk_kks[h][rows:] * decays[h], 0.0) for h in heads]
    tmats = _unit_lower_inverses(lmats, row, col, chunk, min(INV_BASE_BLOCK, chunk))
    egcs = [jnp.exp(g) for g in gcols]
    uws = [_mm(tmats[h], jnp.concatenate([vs[h] * betas[h], kbs[h] * egcs[h]], axis=-1)) for h in heads]
    us = [uw[:, :DN_DV] for uw in uws]
    ws = [uw[:, DN_DV:] for uw in uws]
    qes = [qs[h] * egcs[h] for h in heads]

    a = lru[:, 0:lw]
    bacc = lru[:, lw:2 * lw]
    gy = lru[:, 2 * lw:3 * lw]
    t = lax.broadcasted_iota(jnp.int32, (rows, lw), 0) & (tt - 1)
    s = 1
    while s < tt:
        keep = t >= s
        a_prev = jnp.where(keep, pltpu.roll(a, s, 0), 1.0)
        b_prev = jnp.where(keep, pltpu.roll(bacc, s, 0), 0.0)
        bacc = a * b_prev + bacc
        a = a * a_prev
        s *= 2
    hs = (a.reshape(nb, tt, lw) * hout_ref[...] + bacc.reshape(nb, tt, lw))
    hout_ref[...] = hs[:, tt - 1:tt, :]
    o_b = hs.reshape(rows, lw) * gy

    v_new_parts = [[] for _ in heads]
    o_inter_parts = [[] for _ in heads]
    states = [None for _ in heads]
    for c in range(n_chunks):
        seq = c // chunks_per_seq
        lo, hi = c * chunk, (c + 1) * chunk
        if c % chunks_per_seq == 0:
            states = [sout_ref[seq, h] for h in heads]
        wqs = [_mm(jnp.concatenate([ws[h][lo:hi], qes[h][lo:hi]], axis=0), states[h]) for h in heads]
        v_news = [us[h][lo:hi] - wqs[h][:chunk] for h in heads]
        new_states = []
        for h in heads:
            v_new_parts[h].append(v_news[h])
            o_inter_parts[h].append(wqs[h][chunk:])
            g_last = gcols[h][hi - 1:hi, :]
            k_dec = ks[h][lo:hi] * jnp.exp(g_last - gcols[h][lo:hi])
            new_states.append(states[h] * jnp.exp(g_last) + lax.dot_general(
                k_dec.astype(BF16), v_news[h].astype(BF16), (((0,), (0,)), ((), ())),
                preferred_element_type=F32))
        states = new_states
        if (c + 1) % chunks_per_seq == 0:
            for h in heads:
                sout_ref[seq, h] = states[h]
    o_heads = []
    for h in heads:
        o = jnp.concatenate(o_inter_parts[h], axis=0) + _mm(qks[h], jnp.concatenate(v_new_parts[h], axis=0))
        ms = jnp.mean(o * o, axis=-1, keepdims=True)
        o_heads.append(o * lax.rsqrt(ms + NORM_EPS) * nw_ref[...] * gz[:, h * DN_DV:(h + 1) * DN_DV])
    o_a = jnp.concatenate(o_heads, axis=-1)

    mixed = _mm(jnp.concatenate([o_a, o_b], axis=-1), wout_ref[...]).reshape(nb, tt, d)
    gate = mod_ref[...][:, :, 2 * d:3 * d]
    y_ref[...] = _layer_norm(alpha * x_ref[...] + (1.0 + gate) * mixed, lng_ref[...], lnb_ref[...])


N_TOKEN_MIX_INPUTS = 21
N_TOKEN_MIX_OUTPUTS = 5


def _token_mix_kernel(*refs, n_cast, nb, tt, chunk, alpha):
    (x_ref, mod_ref, hdn_ref, hlru_ref, s0_ref, h0_ref, w_ref, cwdn_ref, cwlru_ref, cblru_ref, alog_ref, dtb_ref,
     wr_ref, wi_ref, br_ref, bi_ref, lam_ref, wout_ref, nw_ref, lng_ref, lnb_ref) = refs[:N_TOKEN_MIX_INPUTS]
    cast_in = refs[N_TOKEN_MIX_INPUTS:N_TOKEN_MIX_INPUTS + n_cast]
    outs = refs[N_TOKEN_MIX_INPUTS + n_cast:]
    y_ref, tdn_ref, tlru_ref, sout_ref, hout_ref = outs[:N_TOKEN_MIX_OUTPUTS]
    cast_out = outs[N_TOKEN_MIX_OUTPUTS:N_TOKEN_MIX_OUTPUTS + n_cast]
    wdn_sc, wlru_sc = outs[N_TOKEN_MIX_OUTPUTS + n_cast:]

    for src, dst in zip(cast_in, cast_out):
        dst[...] = src[...].astype(BF16)

    @pl.when(pl.program_id(1) == 0)
    def _():
        sout_ref[...] = s0_ref[0]
        hout_ref[...] = h0_ref[...]

    qkv, gz, lru, gb = _inproj_stage(
        x_ref, mod_ref, hdn_ref, hlru_ref, w_ref, cwdn_ref, cwlru_ref, cblru_ref, alog_ref, dtb_ref,
        wr_ref, wi_ref, br_ref, bi_ref, lam_ref, tdn_ref, tlru_ref, wdn_sc, wlru_sc, nb=nb, tt=tt)
    _mixer_stage(qkv, gz, lru, gb, x_ref, mod_ref, wout_ref, nw_ref, lng_ref, lnb_ref,
                 y_ref, sout_ref, hout_ref, nb=nb, tt=tt, chunk=chunk, alpha=alpha)


def _token_mix_call(x, mod, hist_dn, hist_lru, s_dn_all, layer, h0, lw_, cast=(), *, nb, tt, chunk, alpha):
    b, t, d = x.shape
    lw = hist_lru.shape[-1]
    grid = (b // nb, t // tt)
    steps = grid[0] * grid[1]
    slab = lambda a: pl.BlockSpec((a.shape[0] // steps, a.shape[1]), lambda i, j: (i * grid[1] + j, 0))
    seq_blk = lambda c: pl.BlockSpec((nb, tt, c), lambda i, j: (i, j, 0))
    per_seq = lambda r, c: pl.BlockSpec((nb, r, c), lambda i, j: (i, 0, 0))
    whole = lambda a: pl.BlockSpec(a.shape, lambda i, j: (0,) * a.ndim)
    state_in = pl.BlockSpec((1, nb, DN_HEADS, DN_DK, DN_DV), lambda i, j: (layer, i, 0, 0, 0))
    state_out = pl.BlockSpec((nb, DN_HEADS, DN_DK, DN_DV), lambda i, j: (i, 0, 0, 0))
    weights = [lw_[n] for n in ("w_cat", "dn_conv_w", "lru_conv_w", "lru_conv_b", "a_log", "dt_bias",
                                "w_r", "w_i", "b_r", "b_i", "lam", "w_out", "dn_norm_w", "ln1_g", "ln1_b")]
    return pl.pallas_call(
        functools.partial(_token_mix_kernel, n_cast=len(cast), nb=nb, tt=tt, chunk=chunk, alpha=alpha),
        out_shape=(
            jax.ShapeDtypeStruct((b, t, d), F32),
            jax.ShapeDtypeStruct((b, SUBLANES, DN_QKV), F32),
            jax.ShapeDtypeStruct((b, SUBLANES, lw), F32),
            jax.ShapeDtypeStruct((b, DN_HEADS, DN_DK, DN_DV), F32),
            jax.ShapeDtypeStruct((b, 1, lw), F32),
        ) + tuple(jax.ShapeDtypeStruct(a.shape, BF16) for a in cast),
        grid=grid,
        in_specs=[seq_blk(d), per_seq(1, N_MOD * d), per_seq(SUBLANES, DN_QKV), per_seq(SUBLANES, lw),
                  state_in, per_seq(1, lw)] + [whole(a) for a in weights] + [slab(a) for a in cast],
        out_specs=(seq_blk(d), per_seq(SUBLANES, DN_QKV), per_seq(SUBLANES, lw), state_out, per_seq(1, lw))
        + tuple(slab(a) for a in cast),
        scratch_shapes=[pltpu.VMEM((nb, SUBLANES + tt, DN_QKV), F32), pltpu.VMEM((nb, SUBLANES + tt, lw), F32)],
        compiler_params=_params(("arbitrary", "arbitrary")),
        name="token_mixer",
    )(x, mod, hist_dn, hist_lru, s_dn_all, h0, *weights, *cast)


def _ffn_kernel(x_ref, mod_ref, wg_ref, wu_ref, wd_ref, lng_ref, lnb_ref, y_ref, *, alpha):
    nb, tt, d = x_ref.shape
    m = mod_ref[...]
    h = (x_ref[...] * (1.0 + m[:, :, 4 * d:5 * d]) + m[:, :, 3 * d:4 * d]).reshape(nb * tt, d).astype(BF16)
    act = _silu(jnp.dot(h, wg_ref[...], preferred_element_type=F32)) * jnp.dot(
        h, wu_ref[...], preferred_element_type=F32)
    ff = _mm(act, wd_ref[...]).reshape(nb, tt, d)
    y_ref[...] = _layer_norm(alpha * x_ref[...] + (1.0 + m[:, :, 5 * d:6 * d]) * ff, lng_ref[...], lnb_ref[...])


def _ff_tile(ff):
    for n in (2, 4, 7, 8, 11, 14, 16, 22, 28):
        if ff % n == 0 and (ff // n) % LANES == 0 and ff // n <= 2048:
            return ff // n
    return ff


def _dense_ffn(x, mod, lw_, *, nb, tt, alpha):
    b, t, d = x.shape
    nt = t // tt
    seq_blk = pl.BlockSpec((nb, tt, d), lambda i: (i // nt, i % nt, 0))
    whole = lambda a: pl.BlockSpec(a.shape, lambda i: (0,) * a.ndim)
    resident = lambda a: pl.BlockSpec(a.shape, lambda i: (0,) * a.ndim, pipeline_mode=pl.Buffered(1))
    return pl.pallas_call(
        functools.partial(_ffn_kernel, alpha=alpha),
        out_shape=jax.ShapeDtypeStruct((b, t, d), F32),
        grid=((b // nb) * nt,),
        in_specs=[seq_blk, pl.BlockSpec((nb, 1, N_MOD * d), lambda i: (i // nt, 0, 0)),
                  resident(lw_["w_gate"]), resident(lw_["w_up"]), resident(lw_["w_down"]),
                  whole(lw_["ln2_g"]), whole(lw_["ln2_b"])],
        out_specs=seq_blk,
        compiler_params=_params(("arbitrary",)),
        name="dense_ffn",
    )(x, mod, lw_["w_gate"], lw_["w_up"], lw_["w_down"], lw_["ln2_g"], lw_["ln2_b"])


MOE_CHUNK = 512
MOE_SLOT_TILE = 512
MOE_SUB = 128
MOE_COMBINE_BLOCK = 256
ROUTE_I1, ROUTE_I2, ROUTE_R1, ROUTE_R2, ROUTE_W1, ROUTE_W2 = range(6)


def _router_kernel(x_ref, mod_ref, wr_ref, base_ref, h_ref, meta_ref, blkbase_ref, cnt_ref, run_sc,
                   *, n_experts):
    nb, tt, d = x_ref.shape
    rows = nb * tt

    @pl.when(pl.program_id(0) == 0)
    def _():
        run_sc[...] = base_ref[...]

    m = mod_ref[...]
    h = (x_ref[...] * (1.0 + m[:, :, 4 * d:5 * d]) + m[:, :, 3 * d:4 * d]).reshape(rows, d)
    h_ref[...] = h.astype(BF16)
    logits = jnp.dot(h, wr_ref[...], precision=lax.Precision.HIGHEST, preferred_element_type=F32)
    lane = lax.broadcasted_iota(jnp.int32, logits.shape, 1)
    neg = jnp.float32(-jnp.inf)
    lg = jnp.where(lane < n_experts, logits, neg)
    m1 = jnp.max(lg, axis=-1, keepdims=True)
    i1 = jnp.min(jnp.where(lg == m1, lane, LANES), axis=-1, keepdims=True)
    lg2 = jnp.where(lane == i1, neg, lg)
    m2 = jnp.max(lg2, axis=-1, keepdims=True)
    i2 = jnp.min(jnp.where(lg2 == m2, lane, LANES), axis=-1, keepdims=True)
    e2 = jnp.exp(m2 - m1)
    w1 = 1.0 / (1.0 + e2)
    w2 = e2 / (1.0 + e2)
    sel = jnp.where(lane == i1, 1.0, jnp.where(lane == i2, 1.0, 0.0))
    r = lax.broadcasted_iota(jnp.int32, (rows, rows), 0)
    c = lax.broadcasted_iota(jnp.int32, (rows, rows), 1)
    rank = _mm(jnp.where(c < r, 1.0, 0.0), sel) + run_sc[0:1, :]
    r1 = jnp.sum(jnp.where(lane == i1, rank, 0.0), axis=-1, keepdims=True)
    r2 = jnp.sum(jnp.where(lane == i2, rank, 0.0), axis=-1, keepdims=True)
    fields = (i1.astype(F32), i2.astype(F32), r1, r2, w1, w2)
    meta = jnp.zeros_like(logits)
    for k, v in enumerate(fields):
        meta = jnp.where(lane == k, v, meta)
    meta_ref[...] = meta
    blkbase_ref[0] = run_sc[...]
    run_sc[...] = run_sc[...] + jnp.sum(sel, axis=0, keepdims=True)
    cnt_ref[...] = run_sc[...]


def _router(x, mod, w_router, base, *, nb, tt, n_experts):
    b, t, d = x.shape
    rows = nb * tt
    nt = t // tt
    nblk = (b // nb) * nt
    whole = lambda a: pl.BlockSpec(a.shape, lambda i: (0,) * a.ndim)
    return pl.pallas_call(
        functools.partial(_router_kernel, n_experts=n_experts),
        out_shape=(jax.ShapeDtypeStruct((nblk * rows, d), BF16),
                   jax.ShapeDtypeStruct((nblk * rows, LANES), F32),
                   jax.ShapeDtypeStruct((nblk, SUBLANES, LANES), F32),
                   jax.ShapeDtypeStruct((SUBLANES, LANES), F32)),
        grid=(nblk,),
        in_specs=[pl.BlockSpec((nb, tt, d), lambda i: (i // nt, i % nt, 0)),
                  pl.BlockSpec((nb, 1, N_MOD * d), lambda i: (i // nt, 0, 0)),
                  whole(w_router), whole(base)],
        out_specs=(pl.BlockSpec((rows, d), lambda i: (i, 0)),
                   pl.BlockSpec((rows, LANES), lambda i: (i, 0)),
                   pl.BlockSpec((1, SUBLANES, LANES), lambda i: (i, 0, 0)),
                   pl.BlockSpec((SUBLANES, LANES), lambda i: (0, 0))),
        scratch_shapes=[pltpu.VMEM((SUBLANES, LANES), F32)],
        compiler_params=_params(("arbitrary",)),
        name="moe_router",
    )(x, mod, w_router, base)


def _select(s_ref, off, width, ids):
    hit1 = s_ref[0:1, pl.ds(off, width)] == ids
    hit2 = s_ref[1:2, pl.ds(off, width)] == ids
    return hit1, hit2, jnp.where(hit1, 1.0, jnp.where(hit2, 1.0, 0.0)).astype(BF16)


def _gather_kernel(clo_ref, chi_ref, srow_ref, wrow_ref, *refs, group_chunks):
    n_groups = len(group_chunks)
    h_refs = refs[:n_groups]
    xs_ref, ws_ref, acc_sc, wacc_sc = refs[n_groups:]
    g = pl.program_id(0)
    n_sub = MOE_SLOT_TILE // MOE_SUB
    for j in range(n_sub):
        q = g * n_sub + j
        ids = g * MOE_SLOT_TILE + j * MOE_SUB + lax.broadcasted_iota(jnp.int32, (MOE_SUB, MOE_CHUNK), 0)
        acc_sc[...] = jnp.zeros_like(acc_sc)
        wacc_sc[...] = jnp.zeros_like(wacc_sc)
        first = 0
        for h_ref, n_chunks in zip(h_refs, group_chunks):
            def body(c, carry, h_ref=h_ref, first=first):
                off = pl.multiple_of(c * MOE_CHUNK, MOE_CHUNK)
                local = pl.multiple_of((c - first) * MOE_CHUNK, MOE_CHUNK)
                hit1, hit2, p = _select(srow_ref, off, MOE_CHUNK, ids)
                acc_sc[...] += jnp.dot(p, h_ref[pl.ds(local, MOE_CHUNK), :], preferred_element_type=F32)
                w = (jnp.where(hit1, wrow_ref[0:1, pl.ds(off, MOE_CHUNK)], 0.0)
                     + jnp.where(hit2, wrow_ref[1:2, pl.ds(off, MOE_CHUNK)], 0.0))
                wacc_sc[...] += jnp.broadcast_to(jnp.sum(w, axis=-1, keepdims=True), wacc_sc.shape)
                return carry

            lax.fori_loop(jnp.maximum(clo_ref[q], first), jnp.minimum(chi_ref[q], first + n_chunks), body, 0)
            first += n_chunks
        xs_ref[j * MOE_SUB:(j + 1) * MOE_SUB, :] = acc_sc[...].astype(BF16)
        ws_ref[j * MOE_SUB:(j + 1) * MOE_SUB, :] = wacc_sc[...]


def _gather_slots(c_lo, c_hi, hs, srow, wrow, n_tiles):
    d = hs[0].shape[-1]
    vmem = pl.BlockSpec(memory_space=pltpu.VMEM)
    return pl.pallas_call(
        functools.partial(_gather_kernel, group_chunks=tuple(h.shape[0] // MOE_CHUNK for h in hs)),
        out_shape=(jax.ShapeDtypeStruct((n_tiles * MOE_SLOT_TILE, d), BF16),
                   jax.ShapeDtypeStruct((n_tiles * MOE_SLOT_TILE, LANES), F32)),
        grid_spec=pltpu.PrefetchScalarGridSpec(
            num_scalar_prefetch=2, grid=(n_tiles,),
            in_specs=[vmem, vmem] + [vmem] * len(hs),
            out_specs=(pl.BlockSpec((MOE_SLOT_TILE, d), lambda g, lo, hi: (g, 0)),
                       pl.BlockSpec((MOE_SLOT_TILE, LANES), lambda g, lo, hi: (g, 0))),
            scratch_shapes=[pltpu.VMEM((MOE_SUB, d), F32), pltpu.VMEM((MOE_SUB, LANES), F32)]),
        compiler_params=_params(("arbitrary",)),
        name="moe_gather",
    )(c_lo, c_hi, srow, wrow, *hs)


def _expert_kernel(te_ref, tv_ref, xs_ref, ws_ref, wg_ref, wu_ref, wd_ref, o_ref, *, ff_chunk):
    g = pl.program_id(0)

    @pl.when(tv_ref[g] != 0)
    def _():
        x = xs_ref[...]
        ff = wg_ref.shape[-1]
        acc = jnp.zeros(o_ref.shape, F32)
        for f0 in range(0, ff, ff_chunk):
            gate = jnp.dot(x, wg_ref[0, :, f0:f0 + ff_chunk], preferred_element_type=F32)
            up = jnp.dot(x, wu_ref[0, :, f0:f0 + ff_chunk], preferred_element_type=F32)
            acc = acc + _mm(_silu(gate) * up, wd_ref[0, f0:f0 + ff_chunk, :])
        o_ref[...] = (ws_ref[:, 0:1] * acc).astype(BF16)

    @pl.when(tv_ref[g] == 0)
    def _():
        o_ref[...] = jnp.zeros_like(o_ref)


def _expert_ffn(tile_expert, tile_valid, xs, ws, lw_):
    s_total, d = xs.shape
    _, _, ff = lw_["w_gate"].shape
    n_tiles = s_total // MOE_SLOT_TILE
    once = pl.Buffered(1)
    return pl.pallas_call(
        functools.partial(_expert_kernel, ff_chunk=_ff_tile(ff)),
        out_shape=jax.ShapeDtypeStruct((s_total, d), BF16),
        grid_spec=pltpu.PrefetchScalarGridSpec(
            num_scalar_prefetch=2, grid=(n_tiles,),
            in_specs=[pl.BlockSpec((MOE_SLOT_TILE, d), lambda g, te, tv: (g, 0)),
                      pl.BlockSpec((MOE_SLOT_TILE, LANES), lambda g, te, tv: (g, 0)),
                      pl.BlockSpec((1, d, ff), lambda g, te, tv: (te[g], 0, 0), pipeline_mode=once),
                      pl.BlockSpec((1, d, ff), lambda g, te, tv: (te[g], 0, 0), pipeline_mode=once),
                      pl.BlockSpec((1, ff, d), lambda g, te, tv: (te[g], 0, 0), pipeline_mode=once)],
            out_specs=pl.BlockSpec((MOE_SLOT_TILE, d), lambda g, te, tv: (g, 0))),
        compiler_params=_params(("arbitrary",)),
        name="expert_ffn",
    )(tile_expert, tile_valid, xs, ws, lw_["w_gate"], lw_["w_up"], lw_["w_down"])


def _combine_kernel(ic_ref, ik_ref, if_ref, x_ref, mod_ref, sc_ref, os_ref, lng_ref, lnb_ref, y_ref, acc_sc,
                    *, alpha):
    nb, tt, d = x_ref.shape
    rows = nb * tt
    w = pl.program_id(0)
    flags = if_ref[w]

    @pl.when((flags & 1) != 0)
    def _():
        acc_sc[...] = jnp.zeros_like(acc_sc)

    @pl.when((flags & 4) != 0)
    def _():
        ids = ik_ref[w] * MOE_COMBINE_BLOCK + lax.broadcasted_iota(jnp.int32, (rows, MOE_COMBINE_BLOCK), 1)
        q = jnp.where(sc_ref[:, 0:1] == ids, 1.0, jnp.where(sc_ref[:, 1:2] == ids, 1.0, 0.0)).astype(BF16)
        acc_sc[...] += jnp.dot(q, os_ref[...], preferred_element_type=F32)

    @pl.when((flags & 2) != 0)
    def _():
        gate = mod_ref[...][:, :, 5 * d:6 * d]
        ff = acc_sc[...].reshape(nb, tt, d)
        y_ref[...] = _layer_norm(alpha * x_ref[...] + (1.0 + gate) * ff, lng_ref[...], lnb_ref[...])


def _combine(items, x, mod, slots_col, out_sorted, lw_, *, nb, tt, alpha):
    item_chunk, item_block, item_flags = items
    b, t, d = x.shape
    rows = nb * tt
    nt = t // tt
    whole = lambda a: pl.BlockSpec(a.shape, lambda w, ic, ik, fl: (0,) * a.ndim)
    seq_blk = pl.BlockSpec((nb, tt, d), lambda w, ic, ik, fl: (ic[w] // nt, ic[w] % nt, 0))
    return pl.pallas_call(
        functools.partial(_combine_kernel, alpha=alpha),
        out_shape=jax.ShapeDtypeStruct((b, t, d), F32),
        grid_spec=pltpu.PrefetchScalarGridSpec(
            num_scalar_prefetch=3, grid=(item_chunk.shape[0],),
            in_specs=[seq_blk,
                      pl.BlockSpec((nb, 1, N_MOD * d), lambda w, ic, ik, fl: (ic[w] // nt, 0, 0)),
                      pl.BlockSpec((rows, TOP_K), lambda w, ic, ik, fl: (ic[w], 0)),
                      pl.BlockSpec((MOE_COMBINE_BLOCK, d), lambda w, ic, ik, fl: (ik[w], 0)),
                      whole(lw_["ln2_g"]), whole(lw_["ln2_b"])],
            out_specs=seq_blk,
            scratch_shapes=[pltpu.VMEM((rows, d), F32)]),
        compiler_params=_params(("arbitrary",)),
        name="moe_combine",
    )(item_chunk, item_block, item_flags, x, mod, slots_col, out_sorted, lw_["ln2_g"], lw_["ln2_b"])


def _count_le(sorted_vals, queries):
    return jnp.sum(sorted_vals[None, :] <= queries[:, None], axis=1).astype(jnp.int32)


def _combine_items(lo, hi, n_items):
    n_chunks, n_experts = lo.shape
    first = lo // MOE_COMBINE_BLOCK
    count = jnp.where(hi > lo, (hi - 1) // MOE_COMBINE_BLOCK - first + 1, 0).reshape(-1)
    ends = jnp.cumsum(count)
    total = ends[-1]
    w = jnp.arange(n_items, dtype=jnp.int32)
    pair = jnp.minimum(_count_le(ends, w), n_chunks * n_experts - 1)
    within = w - (ends[pair] - count[pair])
    valid = w < total
    chunk = pair // n_experts
    block = first.reshape(-1)[pair] + within
    last_valid = jnp.maximum(total - 1, 0)
    chunk = jnp.where(valid, chunk, chunk[last_valid])
    block = jnp.where(valid, block, block[last_valid])
    prev_chunk = jnp.concatenate([jnp.full((1,), -1, jnp.int32), chunk[:-1]])
    next_chunk = jnp.concatenate([chunk[1:], jnp.full((1,), -1, jnp.int32)])
    next_valid = jnp.concatenate([valid[1:], jnp.zeros((1,), bool)])
    is_first = valid & (chunk != prev_chunk)
    is_last = valid & ((chunk != next_chunk) | jnp.logical_not(next_valid))
    flags = is_first.astype(jnp.int32) + 2 * is_last.astype(jnp.int32) + 4 * valid.astype(jnp.int32)
    return chunk.astype(jnp.int32), block.astype(jnp.int32), flags


def _moe_layer(xs_in, mods, lw_, tilings, alpha):
    n_experts = lw_["w_gate"].shape[0]
    base = jnp.zeros((SUBLANES, LANES), F32)
    hs, metas, bases = [], [], []
    for x, mod, (nb, tt) in zip(xs_in, mods, tilings):
        assert nb * tt == MOE_CHUNK and (x.shape[0] * x.shape[1]) % MOE_CHUNK == 0
        h, meta, blkbase, base = _router(x, mod, lw_["w_router"], base, nb=nb, tt=tt, n_experts=n_experts)
        hs.append(h)
        metas.append(meta)
        bases.append(blkbase[:, 0, :n_experts])
    meta = jnp.concatenate(metas, axis=0)
    n = meta.shape[0]
    cum = jnp.concatenate(bases + [base[0:1, :n_experts]], axis=0).astype(jnp.int32)
    counts = cum[-1]
    sizes = ((counts + MOE_SLOT_TILE - 1) // MOE_SLOT_TILE) * MOE_SLOT_TILE
    run_end = jnp.cumsum(sizes)
    run_start = run_end - sizes
    col = lambda k: meta[:, k].astype(jnp.int32)
    slot1 = run_start[col(ROUTE_I1)] + col(ROUTE_R1)
    slot2 = run_start[col(ROUTE_I2)] + col(ROUTE_R2)
    pad_rows = lambda a, fill: jnp.concatenate(
        [a, jnp.full((SUBLANES - a.shape[0], n), fill, a.dtype)], axis=0)
    srow = pad_rows(jnp.stack([slot1, slot2]), -1)
    wrow = pad_rows(jnp.stack([meta[:, ROUTE_W1], meta[:, ROUTE_W2]]), 0.0)

    n_tiles = (TOP_K * n + MOE_SLOT_TILE - 1) // MOE_SLOT_TILE + n_experts
    tile_start = jnp.arange(n_tiles, dtype=jnp.int32) * MOE_SLOT_TILE
    tile_expert = jnp.minimum(_count_le(run_end, tile_start), n_experts - 1)
    tile_valid = (tile_start < run_end[-1]).astype(jnp.int32)
    n_sub = MOE_SLOT_TILE // MOE_SUB
    sub_expert = jnp.repeat(tile_expert, n_sub)
    sub_rank0 = jnp.arange(n_tiles * n_sub, dtype=jnp.int32) * MOE_SUB - run_start[sub_expert]
    cum_sub = cum[:, sub_expert]
    sub_valid = jnp.repeat(tile_valid, n_sub)
    c_lo = jnp.sum(cum_sub[1:] <= sub_rank0[None, :], axis=0).astype(jnp.int32) * sub_valid
    c_hi = jnp.sum(cum_sub[:-1] < sub_rank0[None, :] + MOE_SUB, axis=0).astype(jnp.int32) * sub_valid

    x_sorted, w_sorted = _gather_slots(c_lo, c_hi, hs, srow, wrow, n_tiles)
    out_sorted = _expert_ffn(tile_expert, tile_valid, x_sorted, w_sorted, lw_)

    slots_col = jnp.stack([slot1, slot2], axis=-1)
    outs = []
    chunk0 = 0
    for x, mod, (nb, tt) in zip(xs_in, mods, tilings):
        nc = x.shape[0] * x.shape[1] // MOE_CHUNK
        lo = run_start[None, :] + cum[chunk0:chunk0 + nc]
        hi = run_start[None, :] + cum[chunk0 + 1:chunk0 + nc + 1]
        n_items = nc * n_experts + (TOP_K * nc * MOE_CHUNK) // MOE_COMBINE_BLOCK + 2 * n_experts
        items = _combine_items(lo, hi, n_items)
        sc = slots_col[chunk0 * MOE_CHUNK:(chunk0 + nc) * MOE_CHUNK]
        outs.append(_combine(items, x, mod, sc, out_sorted, lw_, nb=nb, tt=tt, alpha=alpha))
        chunk0 += nc
    return outs


def _pad_lanes(v, width=LANES):
    return jnp.pad(v, ((0, 0), (0, width - v.shape[-1])))


def _block_diag_halves(w):
    nblk, c, _ = w.shape
    half = nblk // 2
    out = jnp.zeros((2, half * c, half * c), w.dtype)
    for i in range(nblk):
        j = i % half
        out = out.at[i // half, j * c:(j + 1) * c, j * c:(j + 1) * c].set(w[i])
    return out


def _layer_weights(l, p, d, lw):
    w_in = p["w_in"][l]
    a_off = DN_QKV
    z_off = a_off + 2 * DN_HEADS
    x_off = z_off + DN_WIDTH
    y_off = x_off + lw
    w_ab = _pad_lanes(w_in[:, a_off:z_off])
    w_cat = jnp.concatenate([w_in[:, :DN_QKV], w_in[:, z_off:y_off + lw], w_ab], axis=1).astype(BF16)
    out = {
        "w_cat": w_cat,
        "dn_conv_w": p["dn_conv_w"][l],
        "lru_conv_w": p["lru_conv_w"][l],
        "lru_conv_b": p["lru_conv_b"][l][None],
        "a_log": _pad_lanes(p["dn_a_log"][l][None]),
        "dt_bias": _pad_lanes(p["dn_dt_bias"][l][None]),
        "w_r": _block_diag_halves(p["lru_w_r"][l]).astype(BF16),
        "w_i": _block_diag_halves(p["lru_w_i"][l]).astype(BF16),
        "b_r": p["lru_b_r"][l][None],
        "b_i": p["lru_b_i"][l][None],
        "lam": p["lru_lambda"][l][None],
        "w_out": p["w_out"][l].astype(BF16),
        "dn_norm_w": p["dn_norm_w"][l][None],
        "ln1_g": p["ln1_g"][l][None],
        "ln1_b": p["ln1_b"][l][None],
        "ln2_g": p["ln2_g"][l][None],
        "ln2_b": p["ln2_b"][l][None],
    }
    j = l // 2
    if l % 2 == 0:
        out.update(w_gate=p["ffn_w_gate"][j], w_up=p["ffn_w_up"][j], w_down=p["ffn_w_down"][j])
    else:
        out.update(w_router=_pad_lanes(p["moe_w_router"][j]),
                   w_gate=p["moe_w_gate"][j], w_up=p["moe_w_up"][j], w_down=p["moe_w_down"][j])
    return out


CHANNEL_MIX_WEIGHTS = ("w_gate", "w_up", "w_down")


def _bf16_slabs(weights, steps):
    slabs = [w[n].reshape(-1, w[n].shape[-1]) for w in weights for n in CHANNEL_MIX_WEIGHTS]
    ok = all(a.shape[0] % (steps * 2 * SUBLANES) == 0 for a in slabs)
    return slabs if ok else None


def _tiling(b, t):
    if t >= MXU_DIM:
        tt = MXU_DIM
        return dict(mixer=(1, tt, min(PROMPT_CHUNK, tt)), ffn=(1, min(t, 2 * MXU_DIM)))
    assert t == SUBLANES, "short sequences must be exactly one sublane tile long"
    return dict(mixer=(min(b, 16), t, t), ffn=(min(b, 64), t))


def _token_mix(x, mod, conv_dn, s_dn_all, layer, conv_lru, s_lru, lw_, til, alpha, cast=()):
    pad_hist = lambda c: jnp.pad(c, ((0, 0), (SUBLANES - (CONV_W - 1), 0), (0, 0)))
    nb, tt, chunk = til["mixer"]
    x, tail_dn, tail_lru, s_new, h_new, *converted = _token_mix_call(
        x, mod, pad_hist(conv_dn), pad_hist(conv_lru), s_dn_all, layer, s_lru[:, None, :], lw_, cast,
        nb=nb, tt=tt, chunk=chunk, alpha=alpha)
    return (x, tail_dn[:, SUBLANES - (CONV_W - 1):, :], s_new, tail_lru[:, SUBLANES - (CONV_W - 1):, :],
            h_new[:, 0, :]), converted


def kernel(x_prompt, x_sample, cache_dn_conv, state_dn, cache_lru_conv, state_lru, c_prompt, c_sample,
           w_ada, b_ada, w_in, dn_conv_w, dn_a_log, dn_dt_bias, dn_norm_w,
           lru_conv_w, lru_conv_b, lru_w_r, lru_b_r, lru_w_i, lru_b_i, lru_lambda, w_out,
           ln1_g, ln1_b, ln2_g, ln2_b, ffn_w_gate, ffn_w_up, ffn_w_down,
           moe_w_router, moe_w_gate, moe_w_up, moe_w_down):
    p = dict(w_in=w_in, dn_conv_w=dn_conv_w, dn_a_log=dn_a_log, dn_dt_bias=dn_dt_bias, dn_norm_w=dn_norm_w,
             lru_conv_w=lru_conv_w, lru_conv_b=lru_conv_b, lru_w_r=lru_w_r, lru_b_r=lru_b_r,
             lru_w_i=lru_w_i, lru_b_i=lru_b_i, lru_lambda=lru_lambda, w_out=w_out,
             ln1_g=ln1_g, ln1_b=ln1_b, ln2_g=ln2_g, ln2_b=ln2_b,
             ffn_w_gate=ffn_w_gate, ffn_w_up=ffn_w_up, ffn_w_down=ffn_w_down,
             moe_w_router=moe_w_router, moe_w_gate=moe_w_gate, moe_w_up=moe_w_up, moe_w_down=moe_w_down)
    depth, d, _ = w_ada.shape
    bp = x_prompt.shape[0]
    bs = x_sample.shape[0]
    lw = cache_lru_conv.shape[-1]
    alpha = (2 * depth) ** 0.25
    weights = [_layer_weights(l, p, d, lw) for l in range(depth)]

    c_all = jnp.concatenate([c_prompt, c_sample], axis=0)
    mod_all = _modulation(c_all, w_ada, b_ada)
    groups = [
        dict(x=x_prompt, rows=slice(0, bp), conv_dn=jnp.zeros((depth, bp, CONV_W - 1, DN_QKV), F32),
             s_dn=jnp.zeros((depth, bp, DN_HEADS, DN_DK, DN_DV), F32),
             conv_lru=jnp.zeros((depth, bp, CONV_W - 1, lw), F32), s_lru=jnp.zeros((depth, bp, lw), F32)),
        dict(x=x_sample, rows=slice(bp, bp + bs), conv_dn=cache_dn_conv, s_dn=state_dn,
             conv_lru=cache_lru_conv, s_lru=state_lru),
    ]
    for g in groups:
        g["til"] = _tiling(g["x"].shape[0], g["x"].shape[1])
        g["new"] = [[], [], [], []]
    nb0, tt0, _ = groups[0]["til"]["mixer"]
    slabs = _bf16_slabs(weights, (bp // nb0) * (x_prompt.shape[1] // tt0))
    for l in range(depth):
        lw_ = weights[l]
        mods = [mod_all[l, g["rows"]][:, None, :] for g in groups]
        for gi, (g, mod) in enumerate(zip(groups, mods)):
            first_call = l == 0 and gi == 0
            res, converted = _token_mix(g["x"], mod, g["conv_dn"][l], g["s_dn"], l, g["conv_lru"][l],
                                        g["s_lru"][l], lw_, g["til"], alpha,
                                        cast=tuple(slabs) if first_call and slabs else ())
            if first_call:
                if not slabs:
                    converted = [w[n].astype(BF16) for w in weights for n in CHANNEL_MIX_WEIGHTS]
                for k, w in enumerate(weights):
                    for m, n in enumerate(CHANNEL_MIX_WEIGHTS):
                        w[n] = converted[k * len(CHANNEL_MIX_WEIGHTS) + m].reshape(w[n].shape)
            g["x"] = res[0]
            for acc, new in zip(g["new"], res[1:]):
                acc.append(new)
        if l % 2 == 0:
            for g, mod in zip(groups, mods):
                nb, tt = g["til"]["ffn"]
                g["x"] = _dense_ffn(g["x"], mod, lw_, nb=nb, tt=tt, alpha=alpha)
        else:
            xs = _moe_layer([g["x"] for g in groups], mods, lw_, [g["til"]["ffn"] for g in groups], alpha)
            for g, x in zip(groups, xs):
                g["x"] = x
    states = [jnp.stack(acc) for g in groups for acc in g["new"]]
    return (groups[0]["x"], groups[1]["x"]) + tuple(states)
```

```python
import functools
import math

import jax
import jax.numpy as jnp
from jax import lax
from jax.experimental import pallas as pl
from jax.experimental.pallas import tpu as pltpu

F32 = jnp.float32
BF16 = jnp.bfloat16

DN_HEADS = 4
DN_DK = 128
DN_DV = 128
DN_WIDTH = DN_HEADS * DN_DV
DN_QKV = 3 * DN_WIDTH
LRU_BLOCKS = 8
LRU_C = 8.0
CONV_W = 4
N_MOD = 6
TOP_K = 2
LN_EPS = 1e-5
NORM_EPS = 1e-6

SUBLANES = 8
LANES = 128
MXU_DIM = 256
VMEM_LIMIT_BYTES = 56 * 1024 * 1024

PROMPT_CHUNK = 64
INV_BASE_BLOCK = 16


def _sigmoid(x):
    return 1.0 / (1.0 + jnp.exp(-x))


def _silu(x):
    return x * _sigmoid(x)


def _softplus(x):
    return jnp.maximum(x, 0.0) + jnp.log1p(jnp.exp(-jnp.abs(x)))


def _gelu_tanh(x):
    return 0.5 * x * (1.0 + jnp.tanh(math.sqrt(2.0 / math.pi) * (x + 0.044715 * (x * x * x))))


def _mm(a, b):
    return jnp.dot(a.astype(BF16), b.astype(BF16), preferred_element_type=F32)


def _layer_norm(x, g, b):
    mu = jnp.mean(x, axis=-1, keepdims=True)
    xc = x - mu
    var = jnp.mean(xc * xc, axis=-1, keepdims=True)
    return xc * lax.rsqrt(var + LN_EPS) * g + b


def _params(sem):
    return pltpu.CompilerParams(dimension_semantics=sem, vmem_limit_bytes=VMEM_LIMIT_BYTES)


def _mod_kernel(c_ref, w_ref, b_ref, o_ref):
    sc = _silu(c_ref[...])
    o_ref[0] = _mm(sc, w_ref[0]) + b_ref[0]


def _modulation(c_all, w_ada, b_ada):
    depth, d, n = w_ada.shape
    rows = c_all.shape[0]
    tn = 1536 if n % 1536 == 0 else n
    return pl.pallas_call(
        _mod_kernel,
        out_shape=jax.ShapeDtypeStruct((depth, rows, n), F32),
        grid=(depth, n // tn),
        in_specs=[
            pl.BlockSpec((rows, d), lambda l, j: (0, 0)),
            pl.BlockSpec((1, d, tn), lambda l, j: (l, 0, j)),
            pl.BlockSpec((1, 1, tn), lambda l, j: (l, 0, j)),
        ],
        out_specs=pl.BlockSpec((1, rows, tn), lambda l, j: (l, 0, j)),
        compiler_params=_params(("arbitrary", "arbitrary")),
        name="adaln_modulation",
    )(c_all, w_ada, b_ada.reshape(depth, 1, n))


def _causal_conv(u, win_ref, w, nb, tt):
    c = u.shape[-1]
    win_ref[:, SUBLANES:, :] = u.reshape(nb, tt, c)
    out = u * w[CONV_W - 1:CONV_W, :]
    for j in range(1, CONV_W):
        prev = win_ref[:, SUBLANES - j:SUBLANES - j + tt, :].reshape(nb * tt, c)
        out = out + prev * w[CONV_W - 1 - j:CONV_W - j, :]
    tail = win_ref[:, tt:tt + SUBLANES, :]
    win_ref[:, 0:SUBLANES, :] = tail
    return out, tail


def _l2norm_heads(x, scale):
    outs = []
    for h in range(DN_HEADS):
        xh = x[:, h * DN_DK:(h + 1) * DN_DK]
        ss = jnp.sum(xh * xh, axis=-1, keepdims=True)
        outs.append(xh * (lax.rsqrt(ss + NORM_EPS) * scale))
    return jnp.concatenate(outs, axis=-1)


def _inproj_stage(x_ref, mod_ref, hdn_ref, hlru_ref, w_ref, cwdn_ref, cwlru_ref, cblru_ref,
                  alog_ref, dtb_ref, wr_ref, wi_ref, br_ref, bi_ref, lam_ref,
                  tdn_ref, tlru_ref, wdn_sc, wlru_sc, *, nb, tt):
    d = x_ref.shape[-1]
    lw = hlru_ref.shape[-1]
    rows = nb * tt

    @pl.when(pl.program_id(1) == 0)
    def _():
        wdn_sc[:, 0:SUBLANES, :] = hdn_ref[...]
        wlru_sc[:, 0:SUBLANES, :] = hlru_ref[...]

    m = mod_ref[...]
    shift = m[:, :, 0:d]
    scale = m[:, :, d:2 * d]
    h = (x_ref[...] * (1.0 + scale) + shift).reshape(rows, d)
    proj = _mm(h, w_ref[...])
    u_dn = proj[:, 0:DN_QKV]
    z = proj[:, DN_QKV:DN_QKV + DN_WIDTH]
    u_lru = proj[:, DN_QKV + DN_WIDTH:DN_QKV + DN_WIDTH + lw]
    y = proj[:, DN_QKV + DN_WIDTH + lw:DN_QKV + DN_WIDTH + 2 * lw]
    ab = proj[:, DN_QKV + DN_WIDTH + 2 * lw:]

    conv_dn, tail_dn = _causal_conv(u_dn, wdn_sc, cwdn_ref[...], nb, tt)
    qkv_c = _silu(conv_dn)
    q = _l2norm_heads(qkv_c[:, 0:DN_WIDTH], DN_DK ** -0.5)
    k = _l2norm_heads(qkv_c[:, DN_WIDTH:2 * DN_WIDTH], 1.0)
    v = qkv_c[:, 2 * DN_WIDTH:DN_QKV]
    qkv = jnp.concatenate([q, k, v], axis=-1)
    gz = _silu(z)
    lane = lax.broadcasted_iota(jnp.int32, ab.shape, 1)
    g_full = -jnp.exp(alog_ref[...]) * _softplus(ab + dtb_ref[...])
    gb = jnp.where(lane < DN_HEADS, g_full, _sigmoid(ab))

    conv_lru, tail_lru = _causal_conv(u_lru, wlru_sc, cwlru_ref[...], nb, tt)
    xc = conv_lru + cblru_ref[...]
    half = lw // 2
    r_pre = jnp.concatenate([_mm(xc[:, :half], wr_ref[0]), _mm(xc[:, half:], wr_ref[1])], axis=-1)
    i_pre = jnp.concatenate([_mm(xc[:, :half], wi_ref[0]), _mm(xc[:, half:], wi_ref[1])], axis=-1)
    r = _sigmoid(r_pre + br_ref[...])
    i = _sigmoid(i_pre + bi_ref[...])
    log_a = -LRU_C * r * _softplus(-lam_ref[...])
    a = jnp.exp(log_a)
    th = jnp.tanh(log_a)
    inp = jnp.sqrt(-2.0 * th / (1.0 - th)) * (i * xc)
    lru = jnp.concatenate([a, inp, _gelu_tanh(y)], axis=-1)

    tdn_ref[...] = tail_dn
    tlru_ref[...] = tail_lru
    return qkv, gz, lru, gb


def _unit_lower_inverses(ls, row, col, chunk, base):
    def same_block(s):
        k = s.bit_length() - 1
        return (row >> k) == (col >> k)

    eye = jnp.where(row == col, 1.0, 0.0)
    base_mask = same_block(base)
    powers = [jnp.where(base_mask, l, 0.0) for l in ls]
    invs = [eye - d for d in powers]
    p = 2
    while p < base:
        powers = [_mm(d, d) for d in powers]
        invs = [t + _mm(t, d) for t, d in zip(invs, powers)]
        p *= 2
    s = base
    while s < chunk:
        off_mask = same_block(2 * s) & jnp.logical_not(same_block(s))
        tmp = [_mm(jnp.where(off_mask, l, 0.0), t) for l, t in zip(ls, invs)]
        invs = [t - _mm(t, x) for t, x in zip(invs, tmp)]
        s *= 2
    return invs


def _mixer_stage(qkv, gz, lru, gb, x_ref, mod_ref, wout_ref, nw_ref, lng_ref, lnb_ref,
                 y_ref, sout_ref, hout_ref, *, nb, tt, chunk, alpha):
    d = x_ref.shape[-1]
    lw = hout_ref.shape[-1]
    rows = nb * tt
    n_chunks = rows // chunk
    chunks_per_seq = tt // chunk
    log_chunk = chunk.bit_length() - 1

    row = lax.broadcasted_iota(jnp.int32, (rows, rows), 0)
    col = lax.broadcasted_iota(jnp.int32, (rows, rows), 1)
    incl = ((row >> log_chunk) == (col >> log_chunk)) & (col <= row)
    strict = incl & (col < row)
    gc = jnp.dot(jnp.where(incl, 1.0, 0.0), gb, precision=lax.Precision.HIGHEST, preferred_element_type=F32)
    gc_t = gc.T

    heads = range(DN_HEADS)
    qs = [qkv[:, h * DN_DK:(h + 1) * DN_DK] for h in heads]
    ks = [qkv[:, DN_WIDTH + h * DN_DK:DN_WIDTH + (h + 1) * DN_DK] for h in heads]
    vs = [qkv[:, 2 * DN_WIDTH + h * DN_DV:2 * DN_WIDTH + (h + 1) * DN_DV] for h in heads]
    gcols = [gc[:, h:h + 1] for h in heads]
    betas = [gb[:, DN_HEADS + h:DN_HEADS + h + 1] for h in heads]
    decays = [jnp.where(incl, jnp.exp(jnp.where(incl, gcols[h] - gc_t[h:h + 1, :], 0.0)), 0.0) for h in heads]
    kbs = [ks[h] * betas[h] for h in heads]
    qk_kks = [lax.dot_general(jnp.concatenate([qs[h], kbs[h]], axis=0).astype(BF16), ks[h].astype(BF16),
                              (((1,), (1,)), ((), ())), preferred_element_type=F32) for h in heads]
    qks = [qk_kks[h][:rows] * decays[h] for h in heads]
    lmats = [jnp.where(strict, qk_kks[h][rows:] * decays[h], 0.0) for h in heads]
    tmats = _unit_lower_inverses(lmats, row, col, chunk, min(INV_BASE_BLOCK, chunk))
    egcs = [jnp.exp(g) for g in gcols]
    uws = [_mm(tmats[h], jnp.concatenate([vs[h] * betas[h], kbs[h] * egcs[h]], axis=-1)) for h in heads]
    us = [uw[:, :DN_DV] for uw in uws]
    ws = [uw[:, DN_DV:] for uw in uws]
    qes = [qs[h] * egcs[h] for h in heads]

    a = lru[:, 0:lw]
    bacc = lru[:, lw:2 * lw]
    gy = lru[:, 2 * lw:3 * lw]
    t = lax.broadcasted_iota(jnp.int32, (rows, lw), 0) & (tt - 1)
    s = 1
    while s < tt:
        keep = t >= s
        a_prev = jnp.where(keep, pltpu.roll(a, s, 0), 1.0)
        b_prev = jnp.where(keep, pltpu.roll(bacc, s, 0), 0.0)
        bacc = a * b_prev + bacc
        a = a * a_prev
        s *= 2
    hs = (a.reshape(nb, tt, lw) * hout_ref[...] + bacc.reshape(nb, tt, lw))
    hout_ref[...] = hs[:, tt - 1:tt, :]
    o_b = hs.reshape(rows, lw) * gy

    v_new_parts = [[] for _ in heads]
    o_inter_parts = [[] for _ in heads]
    states = [None for _ in heads]
    for c in range(n_chunks):
        seq = c // chunks_per_seq
        lo, hi = c * chunk, (c + 1) * chunk
        if c % chunks_per_seq == 0:
            states = [sout_ref[seq, h] for h in heads]
        wqs = [_mm(jnp.concatenate([ws[h][lo:hi], qes[h][lo:hi]], axis=0), states[h]) for h in heads]
        v_news = [us[h][lo:hi] - wqs[h][:chunk] for h in heads]
        new_states = []
        for h in heads:
            v_new_parts[h].append(v_news[h])
            o_inter_parts[h].append(wqs[h][chunk:])
            g_last = gcols[h][hi - 1:hi, :]
            k_dec = ks[h][lo:hi] * jnp.exp(g_last - gcols[h][lo:hi])
            new_states.append(states[h] * jnp.exp(g_last) + lax.dot_general(
                k_dec.astype(BF16), v_news[h].astype(BF16), (((0,), (0,)), ((), ())),
                preferred_element_type=F32))
        states = new_states
        if (c + 1) % chunks_per_seq == 0:
            for h in heads:
                sout_ref[seq, h] = states[h]
    o_heads = []
    for h in heads:
        o = jnp.concatenate(o_inter_parts[h], axis=0) + _mm(qks[h], jnp.concatenate(v_new_parts[h], axis=0))
        ms = jnp.mean(o * o, axis=-1, keepdims=True)
        o_heads.append(o * lax.rsqrt(ms + NORM_EPS) * nw_ref[...] * gz[:, h * DN_DV:(h + 1) * DN_DV])
    o_a = jnp.concatenate(o_heads, axis=-1)

    mixed = _mm(jnp.concatenate([o_a, o_b], axis=-1), wout_ref[...]).reshape(nb, tt, d)
    gate = mod_ref[...][:, :, 2 * d:3 * d]
    y_ref[...] = _layer_norm(alpha * x_ref[...] + (1.0 + gate) * mixed, lng_ref[...], lnb_ref[...])


N_TOKEN_MIX_INPUTS = 21
N_TOKEN_MIX_OUTPUTS = 5


def _token_mix_kernel(*refs, n_cast, nb, tt, chunk, alpha):
    (x_ref, mod_ref, hdn_ref, hlru_ref, s0_ref, h0_ref, w_ref, cwdn_ref, cwlru_ref, cblru_ref, alog_ref, dtb_ref,
     wr_ref, wi_ref, br_ref, bi_ref, lam_ref, wout_ref, nw_ref, lng_ref, lnb_ref) = refs[:N_TOKEN_MIX_INPUTS]
    cast_in = refs[N_TOKEN_MIX_INPUTS:N_TOKEN_MIX_INPUTS + n_cast]
    outs = refs[N_TOKEN_MIX_INPUTS + n_cast:]
    y_ref, tdn_ref, tlru_ref, sout_ref, hout_ref = outs[:N_TOKEN_MIX_OUTPUTS]
    cast_out = outs[N_TOKEN_MIX_OUTPUTS:N_TOKEN_MIX_OUTPUTS + n_cast]
    wdn_sc, wlru_sc = outs[N_TOKEN_MIX_OUTPUTS + n_cast:]

    for src, dst in zip(cast_in, cast_out):
        dst[...] = src[...].astype(BF16)

    @pl.when(pl.program_id(1) == 0)
    def _():
        sout_ref[...] = s0_ref[0]
        hout_ref[...] = h0_ref[...]

    qkv, gz, lru, gb = _inproj_stage(
        x_ref, mod_ref, hdn_ref, hlru_ref, w_ref, cwdn_ref, cwlru_ref, cblru_ref, alog_ref, dtb_ref,
        wr_ref, wi_ref, br_ref, bi_ref, lam_ref, tdn_ref, tlru_ref, wdn_sc, wlru_sc, nb=nb, tt=tt)
    _mixer_stage(qkv, gz, lru, gb, x_ref, mod_ref, wout_ref, nw_ref, lng_ref, lnb_ref,
                 y_ref, sout_ref, hout_ref, nb=nb, tt=tt, chunk=chunk, alpha=alpha)


def _token_mix_call(x, mod, hist_dn, hist_lru, s_dn_all, layer, h0, lw_, cast=(), *, nb, tt, chunk, alpha):
    b, t, d = x.shape
    lw = hist_lru.shape[-1]
    grid = (b // nb, t // tt)
    steps = grid[0] * grid[1]
    slab = lambda a: pl.BlockSpec((a.shape[0] // steps, a.shape[1]), lambda i, j: (i * grid[1] + j, 0))
    seq_blk = lambda c: pl.BlockSpec((nb, tt, c), lambda i, j: (i, j, 0))
    per_seq = lambda r, c: pl.BlockSpec((nb, r, c), lambda i, j: (i, 0, 0))
    whole = lambda a: pl.BlockSpec(a.shape, lambda i, j: (0,) * a.ndim)
    state_in = pl.BlockSpec((1, nb, DN_HEADS, DN_DK, DN_DV), lambda i, j: (layer, i, 0, 0, 0))
    state_out = pl.BlockSpec((nb, DN_HEADS, DN_DK, DN_DV), lambda i, j: (i, 0, 0, 0))
    weights = [lw_[n] for n in ("w_cat", "dn_conv_w", "lru_conv_w", "lru_conv_b", "a_log", "dt_bias",
                                "w_r", "w_i", "b_r", "b_i", "lam", "w_out", "dn_norm_w", "ln1_g", "ln1_b")]
    return pl.pallas_call(
        functools.partial(_token_mix_kernel, n_cast=len(cast), nb=nb, tt=tt, chunk=chunk, alpha=alpha),
        out_shape=(
            jax.ShapeDtypeStruct((b, t, d), F32),
            jax.ShapeDtypeStruct((b, SUBLANES, DN_QKV), F32),
            jax.ShapeDtypeStruct((b, SUBLANES, lw), F32),
            jax.ShapeDtypeStruct((b, DN_HEADS, DN_DK, DN_DV), F32),
            jax.ShapeDtypeStruct((b, 1, lw), F32),
        ) + tuple(jax.ShapeDtypeStruct(a.shape, BF16) for a in cast),
        grid=grid,
        in_specs=[seq_blk(d), per_seq(1, N_MOD * d), per_seq(SUBLANES, DN_QKV), per_seq(SUBLANES, lw),
                  state_in, per_seq(1, lw)] + [whole(a) for a in weights] + [slab(a) for a in cast],
        out_specs=(seq_blk(d), per_seq(SUBLANES, DN_QKV), per_seq(SUBLANES, lw), state_out, per_seq(1, lw))
        + tuple(slab(a) for a in cast),
        scratch_shapes=[pltpu.VMEM((nb, SUBLANES + tt, DN_QKV), F32), pltpu.VMEM((nb, SUBLANES + tt, lw), F32)],
        compiler_params=_params(("arbitrary", "arbitrary")),
        name="token_mixer",
    )(x, mod, hist_dn, hist_lru, s_dn_all, h0, *weights, *cast)


def _ffn_kernel(x_ref, mod_ref, wg_ref, wu_ref, wd_ref, lng_ref, lnb_ref, y_ref, *, alpha):
    nb, tt, d = x_ref.shape
    m = mod_ref[...]
    h = (x_ref[...] * (1.0 + m[:, :, 4 * d:5 * d]) + m[:, :, 3 * d:4 * d]).reshape(nb * tt, d).astype(BF16)
    act = _silu(jnp.dot(h, wg_ref[...], preferred_element_type=F32)) * jnp.dot(
        h, wu_ref[...], preferred_element_type=F32)
    ff = _mm(act, wd_ref[...]).reshape(nb, tt, d)
    y_ref[...] = _layer_norm(alpha * x_ref[...] + (1.0 + m[:, :, 5 * d:6 * d]) * ff, lng_ref[...], lnb_ref[...])


def _ff_tile(ff):
    for n in (2, 4, 7, 8, 11, 14, 16, 22, 28):
        if ff % n == 0 and (ff // n) % LANES == 0 and ff // n <= 2048:
            return ff // n
    return ff


def _dense_ffn(x, mod, lw_, *, nb, tt, alpha):
    b, t, d = x.shape
    nt = t // tt
    seq_blk = pl.BlockSpec((nb, tt, d), lambda i: (i // nt, i % nt, 0))
    whole = lambda a: pl.BlockSpec(a.shape, lambda i: (0,) * a.ndim)
    resident = lambda a: pl.BlockSpec(a.shape, lambda i: (0,) * a.ndim, pipeline_mode=pl.Buffered(1))
    return pl.pallas_call(
        functools.partial(_ffn_kernel, alpha=alpha),
        out_shape=jax.ShapeDtypeStruct((b, t, d), F32),
        grid=((b // nb) * nt,),
        in_specs=[seq_blk, pl.BlockSpec((nb, 1, N_MOD * d), lambda i: (i // nt, 0, 0)),
                  resident(lw_["w_gate"]), resident(lw_["w_up"]), resident(lw_["w_down"]),
                  whole(lw_["ln2_g"]), whole(lw_["ln2_b"])],
        out_specs=seq_blk,
        compiler_params=_params(("arbitrary",)),
        name="dense_ffn",
    )(x, mod, lw_["w_gate"], lw_["w_up"], lw_["w_down"], lw_["ln2_g"], lw_["ln2_b"])


MOE_CHUNK = 512
MOE_SLOT_TILE = 512
MOE_SUB = 128
MOE_COMBINE_BLOCK = 256
MOE_COMBINE_FANIN = 4
ROUTE_I1, ROUTE_I2, ROUTE_R1, ROUTE_R2, ROUTE_W1, ROUTE_W2 = range(6)


def _router_kernel(x_ref, mod_ref, wr_ref, base_ref, h_ref, meta_ref, blkbase_ref, cnt_ref, run_sc,
                   *, n_experts):
    nb, tt, d = x_ref.shape
    rows = nb * tt

    @pl.when(pl.program_id(0) == 0)
    def _():
        run_sc[...] = base_ref[...]

    m = mod_ref[...]
    h = (x_ref[...] * (1.0 + m[:, :, 4 * d:5 * d]) + m[:, :, 3 * d:4 * d]).reshape(rows, d)
    h_ref[...] = h.astype(BF16)
    logits = jnp.dot(h, wr_ref[...], precision=lax.Precision.HIGHEST, preferred_element_type=F32)
    lane = lax.broadcasted_iota(jnp.int32, logits.shape, 1)
    neg = jnp.float32(-jnp.inf)
    lg = jnp.where(lane < n_experts, logits, neg)
    m1 = jnp.max(lg, axis=-1, keepdims=True)
    i1 = jnp.min(jnp.where(lg == m1, lane, LANES), axis=-1, keepdims=True)
    lg2 = jnp.where(lane == i1, neg, lg)
    m2 = jnp.max(lg2, axis=-1, keepdims=True)
    i2 = jnp.min(jnp.where(lg2 == m2, lane, LANES), axis=-1, keepdims=True)
    e2 = jnp.exp(m2 - m1)
    w1 = 1.0 / (1.0 + e2)
    w2 = e2 / (1.0 + e2)
    sel = jnp.where(lane == i1, 1.0, jnp.where(lane == i2, 1.0, 0.0))
    r = lax.broadcasted_iota(jnp.int32, (rows, rows), 0)
    c = lax.broadcasted_iota(jnp.int32, (rows, rows), 1)
    rank = _mm(jnp.where(c < r, 1.0, 0.0), sel) + run_sc[0:1, :]
    r1 = jnp.sum(jnp.where(lane == i1, rank, 0.0), axis=-1, keepdims=True)
    r2 = jnp.sum(jnp.where(lane == i2, rank, 0.0), axis=-1, keepdims=True)
    fields = (i1.astype(F32), i2.astype(F32), r1, r2, w1, w2)
    meta = jnp.zeros_like(logits)
    for k, v in enumerate(fields):
        meta = jnp.where(lane == k, v, meta)
    meta_ref[...] = meta
    blkbase_ref[0] = run_sc[...]
    run_sc[...] = run_sc[...] + jnp.sum(sel, axis=0, keepdims=True)
    cnt_ref[...] = run_sc[...]


def _router(x, mod, w_router, base, *, nb, tt, n_experts):
    b, t, d = x.shape
    rows = nb * tt
    nt = t // tt
    nblk = (b // nb) * nt
    whole = lambda a: pl.BlockSpec(a.shape, lambda i: (0,) * a.ndim)
    return pl.pallas_call(
        functools.partial(_router_kernel, n_experts=n_experts),
        out_shape=(jax.ShapeDtypeStruct((nblk * rows, d), BF16),
                   jax.ShapeDtypeStruct((nblk * rows, LANES), F32),
                   jax.ShapeDtypeStruct((nblk, SUBLANES, LANES), F32),
                   jax.ShapeDtypeStruct((SUBLANES, LANES), F32)),
        grid=(nblk,),
        in_specs=[pl.BlockSpec((nb, tt, d), lambda i: (i // nt, i % nt, 0)),
                  pl.BlockSpec((nb, 1, N_MOD * d), lambda i: (i // nt, 0, 0)),
                  whole(w_router), whole(base)],
        out_specs=(pl.BlockSpec((rows, d), lambda i: (i, 0)),
                   pl.BlockSpec((rows, LANES), lambda i: (i, 0)),
                   pl.BlockSpec((1, SUBLANES, LANES), lambda i: (i, 0, 0)),
                   pl.BlockSpec((SUBLANES, LANES), lambda i: (0, 0))),
        scratch_shapes=[pltpu.VMEM((SUBLANES, LANES), F32)],
        compiler_params=_params(("arbitrary",)),
        name="moe_router",
    )(x, mod, w_router, base)


def _select(s_ref, off, width, ids):
    hit1 = s_ref[0:1, pl.ds(off, width)] == ids
    hit2 = s_ref[1:2, pl.ds(off, width)] == ids
    return hit1, hit2, jnp.where(hit1, 1.0, jnp.where(hit2, 1.0, 0.0)).astype(BF16)


def _gather_kernel(clo_ref, chi_ref, srow_ref, wrow_ref, *refs, group_chunks):
    n_groups = len(group_chunks)
    h_refs = refs[:n_groups]
    xs_ref, ws_ref, acc_sc, wacc_sc = refs[n_groups:]
    g = pl.program_id(0)
    n_sub = MOE_SLOT_TILE // MOE_SUB
    for j in range(n_sub):
        q = g * n_sub + j
        ids = g * MOE_SLOT_TILE + j * MOE_SUB + lax.broadcasted_iota(jnp.int32, (MOE_SUB, MOE_CHUNK), 0)
        acc_sc[...] = jnp.zeros_like(acc_sc)
        wacc_sc[...] = jnp.zeros_like(wacc_sc)
        first = 0
        for h_ref, n_chunks in zip(h_refs, group_chunks):
            def body(c, carry, h_ref=h_ref, first=first):
                off = pl.multiple_of(c * MOE_CHUNK, MOE_CHUNK)
                local = pl.multiple_of((c - first) * MOE_CHUNK, MOE_CHUNK)
                hit1, hit2, p = _select(srow_ref, off, MOE_CHUNK, ids)
                acc_sc[...] += jnp.dot(p, h_ref[pl.ds(local, MOE_CHUNK), :], preferred_element_type=F32)
                w = (jnp.where(hit1, wrow_ref[0:1, pl.ds(off, MOE_CHUNK)], 0.0)
                     + jnp.where(hit2, wrow_ref[1:2, pl.ds(off, MOE_CHUNK)], 0.0))
                wacc_sc[...] += jnp.broadcast_to(jnp.sum(w, axis=-1, keepdims=True), wacc_sc.shape)
                return carry

            lax.fori_loop(jnp.maximum(clo_ref[q], first), jnp.minimum(chi_ref[q], first + n_chunks), body, 0)
            first += n_chunks
        xs_ref[j * MOE_SUB:(j + 1) * MOE_SUB, :] = acc_sc[...].astype(BF16)
        ws_ref[j * MOE_SUB:(j + 1) * MOE_SUB, :] = wacc_sc[...]


def _gather_slots(c_lo, c_hi, hs, srow, wrow, n_tiles):
    d = hs[0].shape[-1]
    vmem = pl.BlockSpec(memory_space=pltpu.VMEM)
    return pl.pallas_call(
        functools.partial(_gather_kernel, group_chunks=tuple(h.shape[0] // MOE_CHUNK for h in hs)),
        out_shape=(jax.ShapeDtypeStruct((n_tiles * MOE_SLOT_TILE, d), BF16),
                   jax.ShapeDtypeStruct((n_tiles * MOE_SLOT_TILE, LANES), F32)),
        grid_spec=pltpu.PrefetchScalarGridSpec(
            num_scalar_prefetch=2, grid=(n_tiles,),
            in_specs=[vmem, vmem] + [vmem] * len(hs),
            out_specs=(pl.BlockSpec((MOE_SLOT_TILE, d), lambda g, lo, hi: (g, 0)),
                       pl.BlockSpec((MOE_SLOT_TILE, LANES), lambda g, lo, hi: (g, 0))),
            scratch_shapes=[pltpu.VMEM((MOE_SUB, d), F32), pltpu.VMEM((MOE_SUB, LANES), F32)]),
        compiler_params=_params(("arbitrary",)),
        name="moe_gather",
    )(c_lo, c_hi, srow, wrow, *hs)


def _expert_kernel(te_ref, tv_ref, xs_ref, ws_ref, wg_ref, wu_ref, wd_ref, o_ref, *, ff_chunk):
    g = pl.program_id(0)

    @pl.when(tv_ref[g] != 0)
    def _():
        x = xs_ref[...]
        ff = wg_ref.shape[-1]
        acc = jnp.zeros(o_ref.shape, F32)
        for f0 in range(0, ff, ff_chunk):
            gate = jnp.dot(x, wg_ref[0, :, f0:f0 + ff_chunk], preferred_element_type=F32)
            up = jnp.dot(x, wu_ref[0, :, f0:f0 + ff_chunk], preferred_element_type=F32)
            acc = acc + _mm(_silu(gate) * up, wd_ref[0, f0:f0 + ff_chunk, :])
        o_ref[...] = (ws_ref[:, 0:1] * acc).astype(BF16)

    @pl.when(tv_ref[g] == 0)
    def _():
        o_ref[...] = jnp.zeros_like(o_ref)


def _expert_ffn(tile_expert, tile_valid, xs, ws, lw_):
    s_total, d = xs.shape
    _, _, ff = lw_["w_gate"].shape
    n_tiles = s_total // MOE_SLOT_TILE
    once = pl.Buffered(1)
    return pl.pallas_call(
        functools.partial(_expert_kernel, ff_chunk=_ff_tile(ff)),
        out_shape=jax.ShapeDtypeStruct((s_total, d), BF16),
        grid_spec=pltpu.PrefetchScalarGridSpec(
            num_scalar_prefetch=2, grid=(n_tiles,),
            in_specs=[pl.BlockSpec((MOE_SLOT_TILE, d), lambda g, te, tv: (g, 0)),
                      pl.BlockSpec((MOE_SLOT_TILE, LANES), lambda g, te, tv: (g, 0)),
                      pl.BlockSpec((1, d, ff), lambda g, te, tv: (te[g], 0, 0), pipeline_mode=once),
                      pl.BlockSpec((1, d, ff), lambda g, te, tv: (te[g], 0, 0), pipeline_mode=once),
                      pl.BlockSpec((1, ff, d), lambda g, te, tv: (te[g], 0, 0), pipeline_mode=once)],
            out_specs=pl.BlockSpec((MOE_SLOT_TILE, d), lambda g, te, tv: (g, 0))),
        compiler_params=_params(("arbitrary",)),
        name="expert_ffn",
    )(tile_expert, tile_valid, xs, ws, lw_["w_gate"], lw_["w_up"], lw_["w_down"])


def _combine_kernel(ic_ref, ik_ref, if_ref, bf_ref, x_ref, mod_ref, sc_ref, *refs, alpha):
    os_refs = refs[:MOE_COMBINE_FANIN]
    lng_ref, lnb_ref, y_ref, acc_sc = refs[MOE_COMBINE_FANIN:]
    nb, tt, d = x_ref.shape
    rows = nb * tt
    w = pl.program_id(0)
    flags = if_ref[w]
    lane = lax.broadcasted_iota(jnp.int32, (rows, MOE_COMBINE_BLOCK), 1)
    s1 = sc_ref[:, 0:1]
    s2 = sc_ref[:, 1:2]
    total = None
    for j, os_ref in enumerate(os_refs):
        ids = ik_ref[w * MOE_COMBINE_FANIN + j] * MOE_COMBINE_BLOCK + lane
        q = jnp.where(s1 == ids, 1.0, jnp.where(s2 == ids, 1.0, 0.0)).astype(BF16)
        part = jnp.dot(q, os_ref[...], preferred_element_type=F32)
        total = part if total is None else total + part

    @pl.when((flags & 1) != 0)
    def _():
        acc_sc[...] = total

    @pl.when((flags & 1) == 0)
    def _():
        acc_sc[...] += total

    @pl.when((flags & 2) != 0)
    def _():
        gate = mod_ref[...][:, :, 5 * d:6 * d]
        ff = acc_sc[...].reshape(nb, tt, d)
        y_ref[...] = _layer_norm(alpha * x_ref[...] + (1.0 + gate) * ff, lng_ref[...], lnb_ref[...])


def _combine(items, x, mod, slots_col, out_sorted, lw_, *, nb, tt, alpha):
    step_chunk, step_blocks, step_flags, block_fetch = items
    b, t, d = x.shape
    rows = nb * tt
    nt = t // tt
    whole = lambda a: pl.BlockSpec(a.shape, lambda w, ic, ik, fl, bf: (0,) * a.ndim)
    seq_blk = pl.BlockSpec((nb, tt, d), lambda w, ic, ik, fl, bf: (ic[w] // nt, ic[w] % nt, 0))
    slot_blk = lambda j: pl.BlockSpec((MOE_COMBINE_BLOCK, d),
                                      lambda w, ic, ik, fl, bf: (bf[w * MOE_COMBINE_FANIN + j], 0))
    return pl.pallas_call(
        functools.partial(_combine_kernel, alpha=alpha),
        out_shape=jax.ShapeDtypeStruct((b, t, d), F32),
        grid_spec=pltpu.PrefetchScalarGridSpec(
            num_scalar_prefetch=4, grid=(step_chunk.shape[0],),
            in_specs=[seq_blk,
                      pl.BlockSpec((nb, 1, N_MOD * d), lambda w, ic, ik, fl, bf: (ic[w] // nt, 0, 0)),
                      pl.BlockSpec((rows, TOP_K), lambda w, ic, ik, fl, bf: (ic[w], 0))]
            + [slot_blk(j) for j in range(MOE_COMBINE_FANIN)]
            + [whole(lw_["ln2_g"]), whole(lw_["ln2_b"])],
            out_specs=seq_blk,
            scratch_shapes=[pltpu.VMEM((rows, d), F32)]),
        compiler_params=_params(("arbitrary",)),
        name="moe_combine",
    )(step_chunk, step_blocks, step_flags, block_fetch, x, mod, slots_col,
      *([out_sorted] * MOE_COMBINE_FANIN), lw_["ln2_g"], lw_["ln2_b"])


def _count_le(sorted_vals, queries):
    return jnp.sum(sorted_vals[None, :] <= queries[:, None], axis=1).astype(jnp.int32)


def _combine_steps(lo, hi, n_steps):
    n_chunks, n_experts = lo.shape
    fan = MOE_COMBINE_FANIN
    first = (lo // MOE_COMBINE_BLOCK).reshape(-1)
    count = jnp.where(hi > lo, (hi - 1) // MOE_COMBINE_BLOCK - lo // MOE_COMBINE_BLOCK + 1, 0).reshape(-1)
    pair_end = jnp.cumsum(count)
    chunk_items = jnp.sum(count.reshape(n_chunks, n_experts), axis=1)
    chunk_item0 = jnp.cumsum(chunk_items) - chunk_items
    chunk_steps = (chunk_items + fan - 1) // fan
    step_end = jnp.cumsum(chunk_steps)
    total_steps = step_end[-1]
    s = jnp.arange(n_steps, dtype=jnp.int32)
    live = s < total_steps
    chunk = jnp.minimum(_count_le(step_end, s), n_chunks - 1)
    chunk = jnp.where(live, chunk, chunk[jnp.maximum(total_steps - 1, 0)])
    q = s - (step_end[chunk] - chunk_steps[chunk])
    j = q[:, None] * fan + jnp.arange(fan, dtype=jnp.int32)[None, :]
    used = live[:, None] & (j < chunk_items[chunk][:, None])
    item = jnp.where(used, chunk_item0[chunk][:, None] + j, 0).reshape(-1)
    pair = jnp.minimum(_count_le(pair_end, item), n_chunks * n_experts - 1)
    block = first[pair] + item - (pair_end[pair] - count[pair])
    block = jnp.where(used.reshape(-1), block, -1)
    fetch = jnp.where(block >= 0, block, jnp.repeat(jnp.maximum(block.reshape(-1, fan)[:, 0], 0), fan))
    flags = (jnp.where(live & (q == 0), 1, 0) + jnp.where(live & (q == chunk_steps[chunk] - 1), 2, 0))
    return chunk.astype(jnp.int32), block.astype(jnp.int32), flags.astype(jnp.int32), fetch.astype(jnp.int32)


def _moe_layer(xs_in, mods, lw_, tilings, alpha):
    n_experts = lw_["w_gate"].shape[0]
    base = jnp.zeros((SUBLANES, LANES), F32)
    hs, metas, bases = [], [], []
    for x, mod, (nb, tt) in zip(xs_in, mods, tilings):
        assert nb * tt == MOE_CHUNK and (x.shape[0] * x.shape[1]) % MOE_CHUNK == 0
        h, meta, blkbase, base = _router(x, mod, lw_["w_router"], base, nb=nb, tt=tt, n_experts=n_experts)
        hs.append(h)
        metas.append(meta)
        bases.append(blkbase[:, 0, :n_experts])
    meta = jnp.concatenate(metas, axis=0)
    n = meta.shape[0]
    cum = jnp.concatenate(bases + [base[0:1, :n_experts]], axis=0).astype(jnp.int32)
    counts = cum[-1]
    sizes = ((counts + MOE_SLOT_TILE - 1) // MOE_SLOT_TILE) * MOE_SLOT_TILE
    run_end = jnp.cumsum(sizes)
    run_start = run_end - sizes
    col = lambda k: meta[:, k].astype(jnp.int32)
    slot1 = run_start[col(ROUTE_I1)] + col(ROUTE_R1)
    slot2 = run_start[col(ROUTE_I2)] + col(ROUTE_R2)
    pad_rows = lambda a, fill: jnp.concatenate(
        [a, jnp.full((SUBLANES - a.shape[0], n), fill, a.dtype)], axis=0)
    srow = pad_rows(jnp.stack([slot1, slot2]), -1)
    wrow = pad_rows(jnp.stack([meta[:, ROUTE_W1], meta[:, ROUTE_W2]]), 0.0)

    n_tiles = (TOP_K * n + MOE_SLOT_TILE - 1) // MOE_SLOT_TILE + n_experts
    tile_start = jnp.arange(n_tiles, dtype=jnp.int32) * MOE_SLOT_TILE
    tile_expert = jnp.minimum(_count_le(run_end, tile_start), n_experts - 1)
    tile_valid = (tile_start < run_end[-1]).astype(jnp.int32)
    n_sub = MOE_SLOT_TILE // MOE_SUB
    sub_expert = jnp.repeat(tile_expert, n_sub)
    sub_rank0 = jnp.arange(n_tiles * n_sub, dtype=jnp.int32) * MOE_SUB - run_start[sub_expert]
    cum_sub = cum[:, sub_expert]
    sub_valid = jnp.repeat(tile_valid, n_sub)
    c_lo = jnp.sum(cum_sub[1:] <= sub_rank0[None, :], axis=0).astype(jnp.int32) * sub_valid
    c_hi = jnp.sum(cum_sub[:-1] < sub_rank0[None, :] + MOE_SUB, axis=0).astype(jnp.int32) * sub_valid

    x_sorted, w_sorted = _gather_slots(c_lo, c_hi, hs, srow, wrow, n_tiles)
    out_sorted = _expert_ffn(tile_expert, tile_valid, x_sorted, w_sorted, lw_)

    slots_col = jnp.stack([slot1, slot2], axis=-1)
    outs = []
    chunk0 = 0
    for x, mod, (nb, tt) in zip(xs_in, mods, tilings):
        nc = x.shape[0] * x.shape[1] // MOE_CHUNK
        lo = run_start[None, :] + cum[chunk0:chunk0 + nc]
        hi = run_start[None, :] + cum[chunk0 + 1:chunk0 + nc + 1]
        max_items = nc * n_experts + (TOP_K * nc * MOE_CHUNK) // MOE_COMBINE_BLOCK + 2 * n_experts
        items = _combine_steps(lo, hi, max_items // MOE_COMBINE_FANIN + nc)
        sc = slots_col[chunk0 * MOE_CHUNK:(chunk0 + nc) * MOE_CHUNK]
        outs.append(_combine(items, x, mod, sc, out_sorted, lw_, nb=nb, tt=tt, alpha=alpha))
        chunk0 += nc
    return outs


def _pad_lanes(v, width=LANES):
    return jnp.pad(v, ((0, 0), (0, width - v.shape[-1])))


def _block_diag_halves(w):
    nblk, c, _ = w.shape
    half = nblk // 2
    out = jnp.zeros((2, half * c, half * c), w.dtype)
    for i in range(nblk):
        j = i % half
        out = out.at[i // half, j * c:(j + 1) * c, j * c:(j + 1) * c].set(w[i])
    return out


def _layer_weights(l, p, d, lw):
    w_in = p["w_in"][l]
    a_off = DN_QKV
    z_off = a_off + 2 * DN_HEADS
    x_off = z_off + DN_WIDTH
    y_off = x_off + lw
    w_ab = _pad_lanes(w_in[:, a_off:z_off])
    w_cat = jnp.concatenate([w_in[:, :DN_QKV], w_in[:, z_off:y_off + lw], w_ab], axis=1).astype(BF16)
    out = {
        "w_cat": w_cat,
        "dn_conv_w": p["dn_conv_w"][l],
        "lru_conv_w": p["lru_conv_w"][l],
        "lru_conv_b": p["lru_conv_b"][l][None],
        "a_log": _pad_lanes(p["dn_a_log"][l][None]),
        "dt_bias": _pad_lanes(p["dn_dt_bias"][l][None]),
        "w_r": _block_diag_halves(p["lru_w_r"][l]).astype(BF16),
        "w_i": _block_diag_halves(p["lru_w_i"][l]).astype(BF16),
        "b_r": p["lru_b_r"][l][None],
        "b_i": p["lru_b_i"][l][None],
        "lam": p["lru_lambda"][l][None],
        "w_out": p["w_out"][l].astype(BF16),
        "dn_norm_w": p["dn_norm_w"][l][None],
        "ln1_g": p["ln1_g"][l][None],
        "ln1_b": p["ln1_b"][l][None],
        "ln2_g": p["ln2_g"][l][None],
        "ln2_b": p["ln2_b"][l][None],
    }
    j = l // 2
    if l % 2 == 0:
        out.update(w_gate=p["ffn_w_gate"][j], w_up=p["ffn_w_up"][j], w_down=p["ffn_w_down"][j])
    else:
        out.update(w_router=_pad_lanes(p["moe_w_router"][j]),
                   w_gate=p["moe_w_gate"][j], w_up=p["moe_w_up"][j], w_down=p["moe_w_down"][j])
    return out


CHANNEL_MIX_WEIGHTS = ("w_gate", "w_up", "w_down")


def _bf16_slabs(weights, steps):
    out = []
    for k, w in enumerate(weights):
        for n in CHANNEL_MIX_WEIGHTS:
            a = w[n].reshape(-1, w[n].shape[-1])
            if a.shape[0] % (steps * 2 * SUBLANES) == 0:
                out.append((k, n, a))
    return out


def _tiling(b, t):
    if t >= MXU_DIM:
        tt = MXU_DIM
        return dict(mixer=(1, tt, min(PROMPT_CHUNK, tt)), ffn=(1, min(t, 2 * MXU_DIM)))
    assert t == SUBLANES, "short sequences must be exactly one sublane tile long"
    return dict(mixer=(min(b, 16), t, t), ffn=(min(b, 64), t))


def _token_mix(x, mod, conv_dn, s_dn_all, layer, conv_lru, s_lru, lw_, til, alpha, cast=()):
    pad_hist = lambda c: jnp.pad(c, ((0, 0), (SUBLANES - (CONV_W - 1), 0), (0, 0)))
    nb, tt, chunk = til["mixer"]
    x, tail_dn, tail_lru, s_new, h_new, *converted = _token_mix_call(
        x, mod, pad_hist(conv_dn), pad_hist(conv_lru), s_dn_all, layer, s_lru[:, None, :], lw_, cast,
        nb=nb, tt=tt, chunk=chunk, alpha=alpha)
    return (x, tail_dn[:, SUBLANES - (CONV_W - 1):, :], s_new, tail_lru[:, SUBLANES - (CONV_W - 1):, :],
            h_new[:, 0, :]), converted


def kernel(x_prompt, x_sample, cache_dn_conv, state_dn, cache_lru_conv, state_lru, c_prompt, c_sample,
           w_ada, b_ada, w_in, dn_conv_w, dn_a_log, dn_dt_bias, dn_norm_w,
           lru_conv_w, lru_conv_b, lru_w_r, lru_b_r, lru_w_i, lru_b_i, lru_lambda, w_out,
           ln1_g, ln1_b, ln2_g, ln2_b, ffn_w_gate, ffn_w_up, ffn_w_down,
           moe_w_router, moe_w_gate, moe_w_up, moe_w_down):
    p = dict(w_in=w_in, dn_conv_w=dn_conv_w, dn_a_log=dn_a_log, dn_dt_bias=dn_dt_bias, dn_norm_w=dn_norm_w,
             lru_conv_w=lru_conv_w, lru_conv_b=lru_conv_b, lru_w_r=lru_w_r, lru_b_r=lru_b_r,
             lru_w_i=lru_w_i, lru_b_i=lru_b_i, lru_lambda=lru_lambda, w_out=w_out,
             ln1_g=ln1_g, ln1_b=ln1_b, ln2_g=ln2_g, ln2_b=ln2_b,
             ffn_w_gate=ffn_w_gate, ffn_w_up=ffn_w_up, ffn_w_down=ffn_w_down,
             moe_w_router=moe_w_router, moe_w_gate=moe_w_gate, moe_w_up=moe_w_up, moe_w_down=moe_w_down)
    depth, d, _ = w_ada.shape
    bp = x_prompt.shape[0]
    bs = x_sample.shape[0]
    lw = cache_lru_conv.shape[-1]
    alpha = (2 * depth) ** 0.25
    weights = [_layer_weights(l, p, d, lw) for l in range(depth)]

    c_all = jnp.concatenate([c_prompt, c_sample], axis=0)
    mod_all = _modulation(c_all, w_ada, b_ada)
    groups = [
        dict(x=x_prompt, rows=slice(0, bp), conv_dn=jnp.zeros((depth, bp, CONV_W - 1, DN_QKV), F32),
             s_dn=jnp.zeros((depth, bp, DN_HEADS, DN_DK, DN_DV), F32),
             conv_lru=jnp.zeros((depth, bp, CONV_W - 1, lw), F32), s_lru=jnp.zeros((depth, bp, lw), F32)),
        dict(x=x_sample, rows=slice(bp, bp + bs), conv_dn=cache_dn_conv, s_dn=state_dn,
             conv_lru=cache_lru_conv, s_lru=state_lru),
    ]
    for g in groups:
        g["til"] = _tiling(g["x"].shape[0], g["x"].shape[1])
        g["new"] = [[], [], [], []]
    nb0, tt0, _ = groups[0]["til"]["mixer"]
    slabs = _bf16_slabs(weights, (bp // nb0) * (x_prompt.shape[1] // tt0))
    for l in range(depth):
        lw_ = weights[l]
        mods = [mod_all[l, g["rows"]][:, None, :] for g in groups]
        for gi, (g, mod) in enumerate(zip(groups, mods)):
            first_call = l == 0 and gi == 0
            res, converted = _token_mix(g["x"], mod, g["conv_dn"][l], g["s_dn"], l, g["conv_lru"][l],
                                        g["s_lru"][l], lw_, g["til"], alpha,
                                        cast=tuple(a for _, _, a in slabs) if first_call else ())
            if first_call:
                for (k, n, _), c in zip(slabs, converted):
                    weights[k][n] = c.reshape(weights[k][n].shape)
                for w in weights:
                    for n in CHANNEL_MIX_WEIGHTS:
                        w[n] = w[n].astype(BF16)
            g["x"] = res[0]
            for acc, new in zip(g["new"], res[1:]):
                acc.append(new)
        if l % 2 == 0:
            for g, mod in zip(groups, mods):
                nb, tt = g["til"]["ffn"]
                g["x"] = _dense_ffn(g["x"], mod, lw_, nb=nb, tt=tt, alpha=alpha)
        else:
            xs = _moe_layer([g["x"] for g in groups], mods, lw_, [g["til"]["ffn"] for g in groups], alpha)
            for g, x in zip(groups, xs):
                g["x"] = x
    states = [jnp.stack(acc) for g in groups for acc in g["new"]]
    return (groups[0]["x"], groups[1]["x"]) + tuple(states)
```

```python
import functools
import math

import jax
import jax.numpy as jnp
from jax import lax
from jax.experimental import pallas as pl
from jax.experimental.pallas import tpu as pltpu

F32 = jnp.float32
BF16 = jnp.bfloat16

DN_HEADS = 4
DN_DK = 128
DN_DV = 128
DN_WIDTH = DN_HEADS * DN_DV
DN_QKV = 3 * DN_WIDTH
LRU_BLOCKS = 8
LRU_C = 8.0
CONV_W = 4
N_MOD = 6
TOP_K = 2
LN_EPS = 1e-5
NORM_EPS = 1e-6

SUBLANES = 8
LANES = 128
MXU_DIM = 256
VMEM_LIMIT_BYTES = 56 * 1024 * 1024

PROMPT_CHUNK = 64
INV_BASE_BLOCK = 16


def _sigmoid(x):
    return 0.5 + 0.5 * jnp.tanh(0.5 * x)


def _silu(x):
    half = 0.5 * x
    return half + half * jnp.tanh(half)


def _softplus(x):
    return jnp.maximum(x, 0.0) + jnp.log1p(jnp.exp(-jnp.abs(x)))


def _gelu_tanh(x):
    return 0.5 * x * (1.0 + jnp.tanh(math.sqrt(2.0 / math.pi) * (x + 0.044715 * (x * x * x))))


def _mm(a, b):
    return jnp.dot(a.astype(BF16), b.astype(BF16), preferred_element_type=F32)


def _layer_norm(x, g, b):
    mu = jnp.mean(x, axis=-1, keepdims=True)
    xc = x - mu
    var = jnp.mean(xc * xc, axis=-1, keepdims=True)
    return xc * lax.rsqrt(var + LN_EPS) * g + b


def _params(sem):
    return pltpu.CompilerParams(dimension_semantics=sem, vmem_limit_bytes=VMEM_LIMIT_BYTES)


def _mod_kernel(c_ref, w_ref, b_ref, o_ref):
    sc = _silu(c_ref[...])
    o_ref[0] = _mm(sc, w_ref[0]) + b_ref[0]


def _modulation(c_all, w_ada, b_ada):
    depth, d, n = w_ada.shape
    rows = c_all.shape[0]
    tn = 1536 if n % 1536 == 0 else n
    return pl.pallas_call(
        _mod_kernel,
        out_shape=jax.ShapeDtypeStruct((depth, rows, n), F32),
        grid=(depth, n // tn),
        in_specs=[
            pl.BlockSpec((rows, d), lambda l, j: (0, 0)),
            pl.BlockSpec((1, d, tn), lambda l, j: (l, 0, j)),
            pl.BlockSpec((1, 1, tn), lambda l, j: (l, 0, j)),
        ],
        out_specs=pl.BlockSpec((1, rows, tn), lambda l, j: (l, 0, j)),
        compiler_params=_params(("arbitrary", "arbitrary")),
        name="adaln_modulation",
    )(c_all, w_ada, b_ada.reshape(depth, 1, n))


def _causal_conv(u, win_ref, w, nb, tt, c0):
    c = u.shape[-1]
    cols = slice(c0, c0 + c)
    win_ref[:, SUBLANES:, cols] = u.reshape(nb, tt, c)
    out = u * w[CONV_W - 1:CONV_W, cols]
    for j in range(1, CONV_W):
        prev = win_ref[:, SUBLANES - j:SUBLANES - j + tt, cols].reshape(nb * tt, c)
        out = out + prev * w[CONV_W - 1 - j:CONV_W - j, cols]
    tail = win_ref[:, tt:tt + SUBLANES, cols]
    win_ref[:, 0:SUBLANES, cols] = tail
    return out, tail


def _l2norm_heads(x, scale):
    outs = []
    for h in range(DN_HEADS):
        xh = x[:, h * DN_DK:(h + 1) * DN_DK]
        ss = jnp.sum(xh * xh, axis=-1, keepdims=True)
        outs.append(xh * (lax.rsqrt(ss + NORM_EPS) * scale))
    return jnp.concatenate(outs, axis=-1)


def _inproj_stage(x_ref, mod_ref, hdn_ref, hlru_ref, w_ref, cwdn_ref, cwlru_ref, cblru_ref,
                  alog_ref, dtb_ref, wr_ref, wi_ref, br_ref, bi_ref, lam_ref,
                  tdn_ref, tlru_ref, wdn_sc, wlru_sc, out, *, nb, tt):
    d = x_ref.shape[-1]
    lw = hlru_ref.shape[-1]
    rows = nb * tt

    @pl.when(pl.program_id(1) == 0)
    def _():
        wdn_sc[:, 0:SUBLANES, :] = hdn_ref[...]
        wlru_sc[:, 0:SUBLANES, :] = hlru_ref[...]

    m = mod_ref[...]
    shift = m[:, :, 0:d]
    scale = m[:, :, d:2 * d]
    h = (x_ref[...] * (1.0 + scale) + shift).reshape(rows, d)
    proj = _mm(h, w_ref[...])
    z = proj[:, DN_QKV:DN_QKV + DN_WIDTH]
    u_lru = proj[:, DN_QKV + DN_WIDTH:DN_QKV + DN_WIDTH + lw]
    y = proj[:, DN_QKV + DN_WIDTH + lw:DN_QKV + DN_WIDTH + 2 * lw]
    ab = proj[:, DN_QKV + DN_WIDTH + 2 * lw:]

    cw_dn = cwdn_ref[...]

    def dn_part(part, norm_scale):
        c0 = part * DN_WIDTH
        conv, tail = _causal_conv(proj[:, c0:c0 + DN_WIDTH], wdn_sc, cw_dn, nb, tt, c0)
        tdn_ref[:, :, c0:c0 + DN_WIDTH] = tail
        act = _silu(conv)
        return act if norm_scale is None else _l2norm_heads(act, norm_scale)

    out["q"] = dn_part(0, DN_DK ** -0.5)
    yield
    out["k"] = dn_part(1, 1.0)
    lane = lax.broadcasted_iota(jnp.int32, ab.shape, 1)
    g_full = -jnp.exp(alog_ref[...]) * _softplus(ab + dtb_ref[...])
    out["gb"] = jnp.where(lane < DN_HEADS, g_full, _sigmoid(ab))
    yield
    out["v"] = dn_part(2, None)
    yield
    out["gz"] = _silu(z)
    out["gy"] = _gelu_tanh(y)
    yield

    conv_lru, tail_lru = _causal_conv(u_lru, wlru_sc, cwlru_ref[...], nb, tt, 0)
    tlru_ref[...] = tail_lru
    xc = conv_lru + cblru_ref[...]
    half = lw // 2
    r_pre = jnp.concatenate([_mm(xc[:, :half], wr_ref[0]), _mm(xc[:, half:], wr_ref[1])], axis=-1)
    i_pre = jnp.concatenate([_mm(xc[:, :half], wi_ref[0]), _mm(xc[:, half:], wi_ref[1])], axis=-1)
    r = _sigmoid(r_pre + br_ref[...])
    i = _sigmoid(i_pre + bi_ref[...])
    log_a = -LRU_C * r * _softplus(-lam_ref[...])
    out["a"] = jnp.exp(log_a)
    th = jnp.tanh(log_a)
    out["inp"] = jnp.sqrt(-2.0 * th / (1.0 - th)) * (i * xc)
    yield


def _unit_lower_inverses(ls, row, col, chunk, base):
    def same_block(s):
        k = s.bit_length() - 1
        return (row >> k) == (col >> k)

    eye = jnp.where(row == col, 1.0, 0.0)
    base_mask = same_block(base)
    powers = [jnp.where(base_mask, l, 0.0) for l in ls]
    invs = [eye - d for d in powers]
    p = 2
    while p < base:
        powers = [_mm(d, d) for d in powers]
        invs = [t + _mm(t, d) for t, d in zip(invs, powers)]
        p *= 2
        yield
    s = base
    while s < chunk:
        off_mask = same_block(2 * s) & jnp.logical_not(same_block(s))
        tmp = [_mm(jnp.where(off_mask, l, 0.0), t) for l, t in zip(ls, invs)]
        invs = [t - _mm(t, x) for t, x in zip(invs, tmp)]
        s *= 2
        yield
    return invs


def _mixer_stage(vals, x_ref, mod_ref, wout_ref, nw_ref, lng_ref, lnb_ref,
                 y_ref, sout_ref, hout_ref, *, nb, tt, chunk, alpha):
    d = x_ref.shape[-1]
    lw = hout_ref.shape[-1]
    rows = nb * tt
    n_chunks = rows // chunk
    chunks_per_seq = tt // chunk
    log_chunk = chunk.bit_length() - 1

    row = lax.broadcasted_iota(jnp.int32, (rows, rows), 0)
    col = lax.broadcasted_iota(jnp.int32, (rows, rows), 1)
    incl = ((row >> log_chunk) == (col >> log_chunk)) & (col <= row)
    strict = incl & (col < row)
    gb = vals["gb"]
    gc = jnp.dot(jnp.where(incl, 1.0, 0.0), gb, precision=lax.Precision.HIGHEST, preferred_element_type=F32)
    gc_t = gc.T

    heads = range(DN_HEADS)
    qs = [vals["q"][:, h * DN_DK:(h + 1) * DN_DK] for h in heads]
    ks = [vals["k"][:, h * DN_DK:(h + 1) * DN_DK] for h in heads]
    gcols = [gc[:, h:h + 1] for h in heads]
    betas = [gb[:, DN_HEADS + h:DN_HEADS + h + 1] for h in heads]
    decays = [jnp.where(incl, jnp.exp(jnp.where(incl, gcols[h] - gc_t[h:h + 1, :], 0.0)), 0.0) for h in heads]
    kbs = [ks[h] * betas[h] for h in heads]
    qk_kks = [lax.dot_general(jnp.concatenate([qs[h], kbs[h]], axis=0).astype(BF16), ks[h].astype(BF16),
                              (((1,), (1,)), ((), ())), preferred_element_type=F32) for h in heads]
    qks = [qk_kks[h][:rows] * decays[h] for h in heads]
    lmats = [jnp.where(strict, qk_kks[h][rows:] * decays[h], 0.0) for h in heads]
    yield
    tmats = yield from _unit_lower_inverses(lmats, row, col, chunk, min(INV_BASE_BLOCK, chunk))
    egcs = [jnp.exp(g) for g in gcols]
    vs = [vals["v"][:, h * DN_DV:(h + 1) * DN_DV] for h in heads]
    uws = [_mm(tmats[h], jnp.concatenate([vs[h] * betas[h], kbs[h] * egcs[h]], axis=-1)) for h in heads]
    us = [uw[:, :DN_DV] for uw in uws]
    ws = [uw[:, DN_DV:] for uw in uws]
    qes = [qs[h] * egcs[h] for h in heads]
    yield

    a = vals["a"]
    bacc = vals["inp"]
    gy = vals["gy"]
    t = lax.broadcasted_iota(jnp.int32, (rows, lw), 0) & (SUBLANES - 1)
    s = 1
    while s < SUBLANES:
        keep = t >= s
        a_prev = jnp.where(keep, pltpu.roll(a, s, 0), 1.0)
        b_prev = jnp.where(keep, pltpu.roll(bacc, s, 0), 0.0)
        bacc = a * b_prev + bacc
        a = a * a_prev
        s *= 2
    tiles = tt // SUBLANES
    a4 = a.reshape(nb, tiles, SUBLANES, lw)
    b4 = bacc.reshape(nb, tiles, SUBLANES, lw)
    carry = hout_ref[...]
    h_tiles = []
    for k in range(tiles):
        hk = a4[:, k] * carry + b4[:, k]
        carry = hk[:, SUBLANES - 1:SUBLANES, :]
        h_tiles.append(hk)
    hout_ref[...] = carry
    hs = h_tiles[0] if tiles == 1 else jnp.stack(h_tiles, axis=1)
    o_b = hs.reshape(rows, lw) * gy
    yield

    v_new_parts = [[] for _ in heads]
    o_inter_parts = [[] for _ in heads]
    states = [None for _ in heads]
    for c in range(n_chunks):
        seq = c // chunks_per_seq
        lo, hi = c * chunk, (c + 1) * chunk
        if c % chunks_per_seq == 0:
            states = [sout_ref[seq, h] for h in heads]
        wqs = [_mm(jnp.concatenate([ws[h][lo:hi], qes[h][lo:hi]], axis=0), states[h]) for h in heads]
        v_news = [us[h][lo:hi] - wqs[h][:chunk] for h in heads]
        new_states = []
        for h in heads:
            v_new_parts[h].append(v_news[h])
            o_inter_parts[h].append(wqs[h][chunk:])
            g_last = gcols[h][hi - 1:hi, :]
            k_dec = ks[h][lo:hi] * jnp.exp(g_last - gcols[h][lo:hi])
            new_states.append(states[h] * jnp.exp(g_last) + lax.dot_general(
                k_dec.astype(BF16), v_news[h].astype(BF16), (((0,), (0,)), ((), ())),
                preferred_element_type=F32))
        states = new_states
        if (c + 1) % chunks_per_seq == 0:
            for h in heads:
                sout_ref[seq, h] = states[h]
        yield
    o_heads = []
    for h in heads:
        o = jnp.concatenate(o_inter_parts[h], axis=0) + _mm(qks[h], jnp.concatenate(v_new_parts[h], axis=0))
        ms = jnp.mean(o * o, axis=-1, keepdims=True)
        o_heads.append(o * lax.rsqrt(ms + NORM_EPS) * nw_ref[...] * vals["gz"][:, h * DN_DV:(h + 1) * DN_DV])
    o_a = jnp.concatenate(o_heads, axis=-1)

    mixed = _mm(jnp.concatenate([o_a, o_b], axis=-1), wout_ref[...]).reshape(nb, tt, d)
    gate = mod_ref[...][:, :, 2 * d:3 * d]
    y_ref[...] = _layer_norm(alpha * x_ref[...] + (1.0 + gate) * mixed, lng_ref[...], lnb_ref[...])


N_TOKEN_MIX_INPUTS = 21
N_TOKEN_MIX_OUTPUTS = 5


def _token_mix_kernel(*refs, n_cast, nb, tt, chunk, alpha):
    (x_ref, mod_ref, hdn_ref, hlru_ref, s0_ref, h0_ref, w_ref, cwdn_ref, cwlru_ref, cblru_ref, alog_ref, dtb_ref,
     wr_ref, wi_ref, br_ref, bi_ref, lam_ref, wout_ref, nw_ref, lng_ref, lnb_ref) = refs[:N_TOKEN_MIX_INPUTS]
    cast_in = refs[N_TOKEN_MIX_INPUTS:N_TOKEN_MIX_INPUTS + n_cast]
    outs = refs[N_TOKEN_MIX_INPUTS + n_cast:]
    y_ref, tdn_ref, tlru_ref, sout_ref, hout_ref = outs[:N_TOKEN_MIX_OUTPUTS]
    cast_out = outs[N_TOKEN_MIX_OUTPUTS:N_TOKEN_MIX_OUTPUTS + n_cast]
    wdn_sc, wlru_sc = outs[N_TOKEN_MIX_OUTPUTS + n_cast:]

    for src, dst in zip(cast_in, cast_out):
        dst[...] = src[...].astype(BF16)

    @pl.when(pl.program_id(1) == 0)
    def _():
        sout_ref[...] = s0_ref[0]
        hout_ref[...] = h0_ref[...]

    vals = {}
    stage1 = _inproj_stage(
        x_ref, mod_ref, hdn_ref, hlru_ref, w_ref, cwdn_ref, cwlru_ref, cblru_ref, alog_ref, dtb_ref,
        wr_ref, wi_ref, br_ref, bi_ref, lam_ref, tdn_ref, tlru_ref, wdn_sc, wlru_sc, vals, nb=nb, tt=tt)
    stage2 = _mixer_stage(vals, x_ref, mod_ref, wout_ref, nw_ref, lng_ref, lnb_ref,
                          y_ref, sout_ref, hout_ref, nb=nb, tt=tt, chunk=chunk, alpha=alpha)
    while "gb" not in vals:
        next(stage1)
    pending = [stage2, stage1]
    while pending:
        pending = [g for g in pending if next(g, StopIteration) is not StopIteration]


def _token_mix_call(x, mod, hist_dn, hist_lru, s_dn_all, layer, h0, lw_, cast=(), *, nb, tt, chunk, alpha):
    b, t, d = x.shape
    lw = hist_lru.shape[-1]
    grid = (b // nb, t // tt)
    steps = grid[0] * grid[1]
    slab = lambda a: pl.BlockSpec((a.shape[0] // steps, a.shape[1]), lambda i, j: (i * grid[1] + j, 0))
    seq_blk = lambda c: pl.BlockSpec((nb, tt, c), lambda i, j: (i, j, 0))
    per_seq = lambda r, c: pl.BlockSpec((nb, r, c), lambda i, j: (i, 0, 0))
    whole = lambda a: pl.BlockSpec(a.shape, lambda i, j: (0,) * a.ndim)
    state_in = pl.BlockSpec((1, nb, DN_HEADS, DN_DK, DN_DV), lambda i, j: (layer, i, 0, 0, 0))
    state_out = pl.BlockSpec((nb, DN_HEADS, DN_DK, DN_DV), lambda i, j: (i, 0, 0, 0))
    weights = [lw_[n] for n in ("w_cat", "dn_conv_w", "lru_conv_w", "lru_conv_b", "a_log", "dt_bias",
                                "w_r", "w_i", "b_r", "b_i", "lam", "w_out", "dn_norm_w", "ln1_g", "ln1_b")]
    return pl.pallas_call(
        functools.partial(_token_mix_kernel, n_cast=len(cast), nb=nb, tt=tt, chunk=chunk, alpha=alpha),
        out_shape=(
            jax.ShapeDtypeStruct((b, t, d), F32),
            jax.ShapeDtypeStruct((b, SUBLANES, DN_QKV), F32),
            jax.ShapeDtypeStruct((b, SUBLANES, lw), F32),
            jax.ShapeDtypeStruct((b, DN_HEADS, DN_DK, DN_DV), F32),
            jax.ShapeDtypeStruct((b, 1, lw), F32),
        ) + tuple(jax.ShapeDtypeStruct(a.shape, BF16) for a in cast),
        grid=grid,
        in_specs=[seq_blk(d), per_seq(1, N_MOD * d), per_seq(SUBLANES, DN_QKV), per_seq(SUBLANES, lw),
                  state_in, per_seq(1, lw)] + [whole(a) for a in weights] + [slab(a) for a in cast],
        out_specs=(seq_blk(d), per_seq(SUBLANES, DN_QKV), per_seq(SUBLANES, lw), state_out, per_seq(1, lw))
        + tuple(slab(a) for a in cast),
        scratch_shapes=[pltpu.VMEM((nb, SUBLANES + tt, DN_QKV), F32), pltpu.VMEM((nb, SUBLANES + tt, lw), F32)],
        compiler_params=_params(("arbitrary", "arbitrary")),
        name="token_mixer",
    )(x, mod, hist_dn, hist_lru, s_dn_all, h0, *weights, *cast)


def _ffn_kernel(x_ref, mod_ref, wg_ref, wu_ref, wd_ref, lng_ref, lnb_ref, y_ref, *, alpha):
    nb, tt, d = x_ref.shape
    m = mod_ref[...]
    h = (x_ref[...] * (1.0 + m[:, :, 4 * d:5 * d]) + m[:, :, 3 * d:4 * d]).reshape(nb * tt, d).astype(BF16)
    act = _silu(jnp.dot(h, wg_ref[...], preferred_element_type=F32)) * jnp.dot(
        h, wu_ref[...], preferred_element_type=F32)
    ff = _mm(act, wd_ref[...]).reshape(nb, tt, d)
    y_ref[...] = _layer_norm(alpha * x_ref[...] + (1.0 + m[:, :, 5 * d:6 * d]) * ff, lng_ref[...], lnb_ref[...])


def _ff_tile(ff):
    for n in (2, 4, 7, 8, 11, 14, 16, 22, 28):
        if ff % n == 0 and (ff // n) % LANES == 0 and ff // n <= 2048:
            return ff // n
    return ff


def _dense_ffn(x, mod, lw_, *, nb, tt, alpha):
    b, t, d = x.shape
    nt = t // tt
    seq_blk = pl.BlockSpec((nb, tt, d), lambda i: (i // nt, i % nt, 0))
    whole = lambda a: pl.BlockSpec(a.shape, lambda i: (0,) * a.ndim)
    resident = lambda a: pl.BlockSpec(a.shape, lambda i: (0,) * a.ndim, pipeline_mode=pl.Buffered(1))
    return pl.pallas_call(
        functools.partial(_ffn_kernel, alpha=alpha),
        out_shape=jax.ShapeDtypeStruct((b, t, d), F32),
        grid=((b // nb) * nt,),
        in_specs=[seq_blk, pl.BlockSpec((nb, 1, N_MOD * d), lambda i: (i // nt, 0, 0)),
                  resident(lw_["w_gate"]), resident(lw_["w_up"]), resident(lw_["w_down"]),
                  whole(lw_["ln2_g"]), whole(lw_["ln2_b"])],
        out_specs=seq_blk,
        compiler_params=_params(("arbitrary",)),
        name="dense_ffn",
    )(x, mod, lw_["w_gate"], lw_["w_up"], lw_["w_down"], lw_["ln2_g"], lw_["ln2_b"])


MOE_CHUNK = 512
MOE_SLOT_TILE = 512
MOE_SUB = 128
MOE_COMBINE_BLOCK = 256
MOE_COMBINE_FANIN = 4
ROUTE_I1, ROUTE_I2, ROUTE_R1, ROUTE_R2, ROUTE_W1, ROUTE_W2 = range(6)


def _router_kernel(x_ref, mod_ref, wr_ref, base_ref, h_ref, meta_ref, blkbase_ref, cnt_ref, run_sc,
                   *, n_experts):
    nb, tt, d = x_ref.shape
    rows = nb * tt

    @pl.when(pl.program_id(0) == 0)
    def _():
        run_sc[...] = base_ref[...]

    m = mod_ref[...]
    h = (x_ref[...] * (1.0 + m[:, :, 4 * d:5 * d]) + m[:, :, 3 * d:4 * d]).reshape(rows, d)
    h_ref[...] = h.astype(BF16)
    logits = jnp.dot(h, wr_ref[...], precision=lax.Precision.HIGHEST, preferred_element_type=F32)
    lane = lax.broadcasted_iota(jnp.int32, logits.shape, 1)
    neg = jnp.float32(-jnp.inf)
    lg = jnp.where(lane < n_experts, logits, neg)
    m1 = jnp.max(lg, axis=-1, keepdims=True)
    i1 = jnp.min(jnp.where(lg == m1, lane, LANES), axis=-1, keepdims=True)
    lg2 = jnp.where(lane == i1, neg, lg)
    m2 = jnp.max(lg2, axis=-1, keepdims=True)
    i2 = jnp.min(jnp.where(lg2 == m2, lane, LANES), axis=-1, keepdims=True)
    e2 = jnp.exp(m2 - m1)
    w1 = 1.0 / (1.0 + e2)
    w2 = e2 / (1.0 + e2)
    sel = jnp.where(lane == i1, 1.0, jnp.where(lane == i2, 1.0, 0.0))
    r = lax.broadcasted_iota(jnp.int32, (rows, rows), 0)
    c = lax.broadcasted_iota(jnp.int32, (rows, rows), 1)
    rank = _mm(jnp.where(c < r, 1.0, 0.0), sel) + run_sc[0:1, :]
    r1 = jnp.sum(jnp.where(lane == i1, rank, 0.0), axis=-1, keepdims=True)
    r2 = jnp.sum(jnp.where(lane == i2, rank, 0.0), axis=-1, keepdims=True)
    fields = (i1.astype(F32), i2.astype(F32), r1, r2, w1, w2)
    meta = jnp.zeros_like(logits)
    for k, v in enumerate(fields):
        meta = jnp.where(lane == k, v, meta)
    meta_ref[...] = meta
    blkbase_ref[0] = run_sc[...]
    run_sc[...] = run_sc[...] + jnp.sum(sel, axis=0, keepdims=True)
    cnt_ref[...] = run_sc[...]


def _router(x, mod, w_router, base, *, nb, tt, n_experts):
    b, t, d = x.shape
    rows = nb * tt
    nt = t // tt
    nblk = (b // nb) * nt
    whole = lambda a: pl.BlockSpec(a.shape, lambda i: (0,) * a.ndim)
    return pl.pallas_call(
        functools.partial(_router_kernel, n_experts=n_experts),
        out_shape=(jax.ShapeDtypeStruct((nblk * rows, d), BF16),
                   jax.ShapeDtypeStruct((nblk * rows, LANES), F32),
                   jax.ShapeDtypeStruct((nblk, SUBLANES, LANES), F32),
                   jax.ShapeDtypeStruct((SUBLANES, LANES), F32)),
        grid=(nblk,),
        in_specs=[pl.BlockSpec((nb, tt, d), lambda i: (i // nt, i % nt, 0)),
                  pl.BlockSpec((nb, 1, N_MOD * d), lambda i: (i // nt, 0, 0)),
                  whole(w_router), whole(base)],
        out_specs=(pl.BlockSpec((rows, d), lambda i: (i, 0)),
                   pl.BlockSpec((rows, LANES), lambda i: (i, 0)),
                   pl.BlockSpec((1, SUBLANES, LANES), lambda i: (i, 0, 0)),
                   pl.BlockSpec((SUBLANES, LANES), lambda i: (0, 0))),
        scratch_shapes=[pltpu.VMEM((SUBLANES, LANES), F32)],
        compiler_params=_params(("arbitrary",)),
        name="moe_router",
    )(x, mod, w_router, base)


def _select(s_ref, off, width, ids):
    hit1 = s_ref[0:1, pl.ds(off, width)] == ids
    hit2 = s_ref[1:2, pl.ds(off, width)] == ids
    return hit1, hit2, jnp.where(hit1, 1.0, jnp.where(hit2, 1.0, 0.0)).astype(BF16)


def _gather_kernel(clo_ref, chi_ref, srow_ref, wrow_ref, *refs, group_chunks):
    n_groups = len(group_chunks)
    h_refs = refs[:n_groups]
    xs_ref, ws_ref, acc_sc, wacc_sc = refs[n_groups:]
    g = pl.program_id(0)
    n_sub = MOE_SLOT_TILE // MOE_SUB
    for j in range(n_sub):
        q = g * n_sub + j
        ids = g * MOE_SLOT_TILE + j * MOE_SUB + lax.broadcasted_iota(jnp.int32, (MOE_SUB, MOE_CHUNK), 0)
        acc_sc[...] = jnp.zeros_like(acc_sc)
        wacc_sc[...] = jnp.zeros_like(wacc_sc)
        first = 0
        for h_ref, n_chunks in zip(h_refs, group_chunks):
            def body(c, carry, h_ref=h_ref, first=first):
                off = pl.multiple_of(c * MOE_CHUNK, MOE_CHUNK)
                local = pl.multiple_of((c - first) * MOE_CHUNK, MOE_CHUNK)
                hit1, hit2, p = _select(srow_ref, off, MOE_CHUNK, ids)
                acc_sc[...] += jnp.dot(p, h_ref[pl.ds(local, MOE_CHUNK), :], preferred_element_type=F32)
                w = (jnp.where(hit1, wrow_ref[0:1, pl.ds(off, MOE_CHUNK)], 0.0)
                     + jnp.where(hit2, wrow_ref[1:2, pl.ds(off, MOE_CHUNK)], 0.0))
                wacc_sc[...] += jnp.broadcast_to(jnp.sum(w, axis=-1, keepdims=True), wacc_sc.shape)
                return carry

            lax.fori_loop(jnp.maximum(clo_ref[q], first), jnp.minimum(chi_ref[q], first + n_chunks), body, 0)
            first += n_chunks
        xs_ref[j * MOE_SUB:(j + 1) * MOE_SUB, :] = acc_sc[...].astype(BF16)
        ws_ref[j * MOE_SUB:(j + 1) * MOE_SUB, :] = wacc_sc[...]


def _gather_slots(c_lo, c_hi, hs, srow, wrow, n_tiles):
    d = hs[0].shape[-1]
    vmem = pl.BlockSpec(memory_space=pltpu.VMEM)
    return pl.pallas_call(
        functools.partial(_gather_kernel, group_chunks=tuple(h.shape[0] // MOE_CHUNK for h in hs)),
        out_shape=(jax.ShapeDtypeStruct((n_tiles * MOE_SLOT_TILE, d), BF16),
                   jax.ShapeDtypeStruct((n_tiles * MOE_SLOT_TILE, LANES), F32)),
        grid_spec=pltpu.PrefetchScalarGridSpec(
            num_scalar_prefetch=2, grid=(n_tiles,),
            in_specs=[vmem, vmem] + [vmem] * len(hs),
            out_specs=(pl.BlockSpec((MOE_SLOT_TILE, d), lambda g, lo, hi: (g, 0)),
                       pl.BlockSpec((MOE_SLOT_TILE, LANES), lambda g, lo, hi: (g, 0))),
            scratch_shapes=[pltpu.VMEM((MOE_SUB, d), F32), pltpu.VMEM((MOE_SUB, LANES), F32)]),
        compiler_params=_params(("arbitrary",)),
        name="moe_gather",
    )(c_lo, c_hi, srow, wrow, *hs)


def _expert_kernel(te_ref, tv_ref, xs_ref, ws_ref, wg_ref, wu_ref, wd_ref, o_ref, *, ff_chunk):
    g = pl.program_id(0)

    @pl.when(tv_ref[g] != 0)
    def _():
        x = xs_ref[...]
        ff = wg_ref.shape[-1]
        acc = jnp.zeros(o_ref.shape, F32)
        for f0 in range(0, ff, ff_chunk):
            gate = jnp.dot(x, wg_ref[0, :, f0:f0 + ff_chunk], preferred_element_type=F32)
            up = jnp.dot(x, wu_ref[0, :, f0:f0 + ff_chunk], preferred_element_type=F32)
            acc = acc + _mm(_silu(gate) * up, wd_ref[0, f0:f0 + ff_chunk, :])
        o_ref[...] = (ws_ref[:, 0:1] * acc).astype(BF16)

    @pl.when(tv_ref[g] == 0)
    def _():
        o_ref[...] = jnp.zeros_like(o_ref)


def _expert_ffn(tile_expert, tile_valid, xs, ws, lw_):
    s_total, d = xs.shape
    _, _, ff = lw_["w_gate"].shape
    n_tiles = s_total // MOE_SLOT_TILE
    once = pl.Buffered(1)
    return pl.pallas_call(
        functools.partial(_expert_kernel, ff_chunk=_ff_tile(ff)),
        out_shape=jax.ShapeDtypeStruct((s_total, d), BF16),
        grid_spec=pltpu.PrefetchScalarGridSpec(
            num_scalar_prefetch=2, grid=(n_tiles,),
            in_specs=[pl.BlockSpec((MOE_SLOT_TILE, d), lambda g, te, tv: (g, 0)),
                      pl.BlockSpec((MOE_SLOT_TILE, LANES), lambda g, te, tv: (g, 0)),
                      pl.BlockSpec((1, d, ff), lambda g, te, tv: (te[g], 0, 0), pipeline_mode=once),
                      pl.BlockSpec((1, d, ff), lambda g, te, tv: (te[g], 0, 0), pipeline_mode=once),
                      pl.BlockSpec((1, ff, d), lambda g, te, tv: (te[g], 0, 0), pipeline_mode=once)],
            out_specs=pl.BlockSpec((MOE_SLOT_TILE, d), lambda g, te, tv: (g, 0))),
        compiler_params=_params(("arbitrary",)),
        name="expert_ffn",
    )(tile_expert, tile_valid, xs, ws, lw_["w_gate"], lw_["w_up"], lw_["w_down"])


def _combine_kernel(ic_ref, ik_ref, if_ref, bf_ref, x_ref, mod_ref, sc_ref, *refs, alpha):
    os_refs = refs[:MOE_COMBINE_FANIN]
    lng_ref, lnb_ref, y_ref, acc_sc = refs[MOE_COMBINE_FANIN:]
    nb, tt, d = x_ref.shape
    rows = nb * tt
    w = pl.program_id(0)
    flags = if_ref[w]
    lane = lax.broadcasted_iota(jnp.int32, (rows, MOE_COMBINE_BLOCK), 1)
    s1 = sc_ref[:, 0:1]
    s2 = sc_ref[:, 1:2]
    total = None
    for j, os_ref in enumerate(os_refs):
        ids = ik_ref[w * MOE_COMBINE_FANIN + j] * MOE_COMBINE_BLOCK + lane
        q = jnp.where(s1 == ids, 1.0, jnp.where(s2 == ids, 1.0, 0.0)).astype(BF16)
        part = jnp.dot(q, os_ref[...], preferred_element_type=F32)
        total = part if total is None else total + part

    @pl.when((flags & 1) != 0)
    def _():
        acc_sc[...] = total

    @pl.when((flags & 1) == 0)
    def _():
        acc_sc[...] += total

    @pl.when((flags & 2) != 0)
    def _():
        gate = mod_ref[...][:, :, 5 * d:6 * d]
        ff = acc_sc[...].reshape(nb, tt, d)
        y_ref[...] = _layer_norm(alpha * x_ref[...] + (1.0 + gate) * ff, lng_ref[...], lnb_ref[...])


def _combine(items, x, mod, slots_col, out_sorted, lw_, *, nb, tt, alpha):
    step_chunk, step_blocks, step_flags, block_fetch = items
    b, t, d = x.shape
    rows = nb * tt
    nt = t // tt
    whole = lambda a: pl.BlockSpec(a.shape, lambda w, ic, ik, fl, bf: (0,) * a.ndim)
    seq_blk = pl.BlockSpec((nb, tt, d), lambda w, ic, ik, fl, bf: (ic[w] // nt, ic[w] % nt, 0))
    slot_blk = lambda j: pl.BlockSpec((MOE_COMBINE_BLOCK, d),
                                      lambda w, ic, ik, fl, bf: (bf[w * MOE_COMBINE_FANIN + j], 0))
    return pl.pallas_call(
        functools.partial(_combine_kernel, alpha=alpha),
        out_shape=jax.ShapeDtypeStruct((b, t, d), F32),
        grid_spec=pltpu.PrefetchScalarGridSpec(
            num_scalar_prefetch=4, grid=(step_chunk.shape[0],),
            in_specs=[seq_blk,
                      pl.BlockSpec((nb, 1, N_MOD * d), lambda w, ic, ik, fl, bf: (ic[w] // nt, 0, 0)),
                      pl.BlockSpec((rows, TOP_K), lambda w, ic, ik, fl, bf: (ic[w], 0))]
            + [slot_blk(j) for j in range(MOE_COMBINE_FANIN)]
            + [whole(lw_["ln2_g"]), whole(lw_["ln2_b"])],
            out_specs=seq_blk,
            scratch_shapes=[pltpu.VMEM((rows, d), F32)]),
        compiler_params=_params(("arbitrary",)),
        name="moe_combine",
    )(step_chunk, step_blocks, step_flags, block_fetch, x, mod, slots_col,
      *([out_sorted] * MOE_COMBINE_FANIN), lw_["ln2_g"], lw_["ln2_b"])


def _count_le(sorted_vals, queries):
    return jnp.sum(sorted_vals[None, :] <= queries[:, None], axis=1).astype(jnp.int32)


def _combine_steps(lo, hi, n_steps):
    n_chunks, n_experts = lo.shape
    fan = MOE_COMBINE_FANIN
    first = (lo // MOE_COMBINE_BLOCK).reshape(-1)
    count = jnp.where(hi > lo, (hi - 1) // MOE_COMBINE_BLOCK - lo // MOE_COMBINE_BLOCK + 1, 0).reshape(-1)
    pair_end = jnp.cumsum(count)
    chunk_items = jnp.sum(count.reshape(n_chunks, n_experts), axis=1)
    chunk_item0 = jnp.cumsum(chunk_items) - chunk_items
    chunk_steps = (chunk_items + fan - 1) // fan
    step_end = jnp.cumsum(chunk_steps)
    total_steps = step_end[-1]
    s = jnp.arange(n_steps, dtype=jnp.int32)
    live = s < total_steps
    chunk = jnp.minimum(_count_le(step_end, s), n_chunks - 1)
    chunk = jnp.where(live, chunk, chunk[jnp.maximum(total_steps - 1, 0)])
    q = s - (step_end[chunk] - chunk_steps[chunk])
    j = q[:, None] * fan + jnp.arange(fan, dtype=jnp.int32)[None, :]
    used = live[:, None] & (j < chunk_items[chunk][:, None])
    item = jnp.where(used, chunk_item0[chunk][:, None] + j, 0).reshape(-1)
    pair = jnp.minimum(_count_le(pair_end, item), n_chunks * n_experts - 1)
    block = first[pair] + item - (pair_end[pair] - count[pair])
    block = jnp.where(used.reshape(-1), block, -1)
    fetch = jnp.where(block >= 0, block, jnp.repeat(jnp.maximum(block.reshape(-1, fan)[:, 0], 0), fan))
    flags = (jnp.where(live & (q == 0), 1, 0) + jnp.where(live & (q == chunk_steps[chunk] - 1), 2, 0))
    return chunk.astype(jnp.int32), block.astype(jnp.int32), flags.astype(jnp.int32), fetch.astype(jnp.int32)


def _moe_layer(xs_in, mods, lw_, tilings, alpha):
    n_experts = lw_["w_gate"].shape[0]
    base = jnp.zeros((SUBLANES, LANES), F32)
    hs, metas, bases = [], [], []
    for x, mod, (nb, tt) in zip(xs_in, mods, tilings):
        assert nb * tt == MOE_CHUNK and (x.shape[0] * x.shape[1]) % MOE_CHUNK == 0
        h, meta, blkbase, base = _router(x, mod, lw_["w_router"], base, nb=nb, tt=tt, n_experts=n_experts)
        hs.append(h)
        metas.append(meta)
        bases.append(blkbase[:, 0, :n_experts])
    meta = jnp.concatenate(metas, axis=0)
    n = meta.shape[0]
    cum = jnp.concatenate(bases + [base[0:1, :n_experts]], axis=0).astype(jnp.int32)
    counts = cum[-1]
    sizes = ((counts + MOE_SLOT_TILE - 1) // MOE_SLOT_TILE) * MOE_SLOT_TILE
    run_end = jnp.cumsum(sizes)
    run_start = run_end - sizes
    col = lambda k: meta[:, k].astype(jnp.int32)
    slot1 = run_start[col(ROUTE_I1)] + col(ROUTE_R1)
    slot2 = run_start[col(ROUTE_I2)] + col(ROUTE_R2)
    pad_rows = lambda a, fill: jnp.concatenate(
        [a, jnp.full((SUBLANES - a.shape[0], n), fill, a.dtype)], axis=0)
    srow = pad_rows(jnp.stack([slot1, slot2]), -1)
    wrow = pad_rows(jnp.stack([meta[:, ROUTE_W1], meta[:, ROUTE_W2]]), 0.0)

    n_tiles = (TOP_K * n + MOE_SLOT_TILE - 1) // MOE_SLOT_TILE + n_experts
    tile_start = jnp.arange(n_tiles, dtype=jnp.int32) * MOE_SLOT_TILE
    tile_expert = jnp.minimum(_count_le(run_end, tile_start), n_experts - 1)
    tile_valid = (tile_start < run_end[-1]).astype(jnp.int32)
    n_sub = MOE_SLOT_TILE // MOE_SUB
    sub_expert = jnp.repeat(tile_expert, n_sub)
    sub_rank0 = jnp.arange(n_tiles * n_sub, dtype=jnp.int32) * MOE_SUB - run_start[sub_expert]
    cum_sub = cum[:, sub_expert]
    sub_valid = jnp.repeat(tile_valid, n_sub)
    c_lo = jnp.sum(cum_sub[1:] <= sub_rank0[None, :], axis=0).astype(jnp.int32) * sub_valid
    c_hi = jnp.sum(cum_sub[:-1] < sub_rank0[None, :] + MOE_SUB, axis=0).astype(jnp.int32) * sub_valid

    x_sorted, w_sorted = _gather_slots(c_lo, c_hi, hs, srow, wrow, n_tiles)
    out_sorted = _expert_ffn(tile_expert, tile_valid, x_sorted, w_sorted, lw_)

    slots_col = jnp.stack([slot1, slot2], axis=-1)
    outs = []
    chunk0 = 0
    for x, mod, (nb, tt) in zip(xs_in, mods, tilings):
        nc = x.shape[0] * x.shape[1] // MOE_CHUNK
        lo = run_start[None, :] + cum[chunk0:chunk0 + nc]
        hi = run_start[None, :] + cum[chunk0 + 1:chunk0 + nc + 1]
        max_items = nc * n_experts + (TOP_K * nc * MOE_CHUNK) // MOE_COMBINE_BLOCK + 2 * n_experts
        items = _combine_steps(lo, hi, max_items // MOE_COMBINE_FANIN + nc)
        sc = slots_col[chunk0 * MOE_CHUNK:(chunk0 + nc) * MOE_CHUNK]
        outs.append(_combine(items, x, mod, sc, out_sorted, lw_, nb=nb, tt=tt, alpha=alpha))
        chunk0 += nc
    return outs


def _pad_lanes(v, width=LANES):
    return jnp.pad(v, ((0, 0), (0, width - v.shape[-1])))


def _block_diag_halves(w):
    nblk, c, _ = w.shape
    half = nblk // 2
    out = jnp.zeros((2, half * c, half * c), w.dtype)
    for i in range(nblk):
        j = i % half
        out = out.at[i // half, j * c:(j + 1) * c, j * c:(j + 1) * c].set(w[i])
    return out


def _layer_weights(l, p, d, lw):
    w_in = p["w_in"][l]
    a_off = DN_QKV
    z_off = a_off + 2 * DN_HEADS
    x_off = z_off + DN_WIDTH
    y_off = x_off + lw
    w_ab = _pad_lanes(w_in[:, a_off:z_off])
    w_cat = jnp.concatenate([w_in[:, :DN_QKV], w_in[:, z_off:y_off + lw], w_ab], axis=1).astype(BF16)
    out = {
        "w_cat": w_cat,
        "dn_conv_w": p["dn_conv_w"][l],
        "lru_conv_w": p["lru_conv_w"][l],
        "lru_conv_b": p["lru_conv_b"][l][None],
        "a_log": _pad_lanes(p["dn_a_log"][l][None]),
        "dt_bias": _pad_lanes(p["dn_dt_bias"][l][None]),
        "w_r": _block_diag_halves(p["lru_w_r"][l]).astype(BF16),
        "w_i": _block_diag_halves(p["lru_w_i"][l]).astype(BF16),
        "b_r": p["lru_b_r"][l][None],
        "b_i": p["lru_b_i"][l][None],
        "lam": p["lru_lambda"][l][None],
        "w_out": p["w_out"][l].astype(BF16),
        "dn_norm_w": p["dn_norm_w"][l][None],
        "ln1_g": p["ln1_g"][l][None],
        "ln1_b": p["ln1_b"][l][None],
        "ln2_g": p["ln2_g"][l][None],
        "ln2_b": p["ln2_b"][l][None],
    }
    j = l // 2
    if l % 2 == 0:
        out.update(w_gate=p["ffn_w_gate"][j], w_up=p["ffn_w_up"][j], w_down=p["ffn_w_down"][j])
    else:
        out.update(w_router=_pad_lanes(p["moe_w_router"][j]),
                   w_gate=p["moe_w_gate"][j], w_up=p["moe_w_up"][j], w_down=p["moe_w_down"][j])
    return out


CHANNEL_MIX_WEIGHTS = ("w_gate", "w_up", "w_down")


def _bf16_slabs(weights, steps):
    out = []
    for k, w in enumerate(weights):
        for n in CHANNEL_MIX_WEIGHTS:
            a = w[n].reshape(-1, w[n].shape[-1])
            if a.shape[0] % (steps * 2 * SUBLANES) == 0:
                out.append((k, n, a))
    return out


def _tiling(b, t):
    if t >= MXU_DIM:
        tt = MXU_DIM
        return dict(mixer=(1, tt, min(PROMPT_CHUNK, tt)), ffn=(1, min(t, 2 * MXU_DIM)))
    assert t == SUBLANES, "short sequences must be exactly one sublane tile long"
    return dict(mixer=(min(b, 16), t, t), ffn=(min(b, 64), t))


def _token_mix(x, mod, conv_dn, s_dn_all, layer, conv_lru, s_lru, lw_, til, alpha, cast=()):
    pad_hist = lambda c: jnp.pad(c, ((0, 0), (SUBLANES - (CONV_W - 1), 0), (0, 0)))
    nb, tt, chunk = til["mixer"]
    x, tail_dn, tail_lru, s_new, h_new, *converted = _token_mix_call(
        x, mod, pad_hist(conv_dn), pad_hist(conv_lru), s_dn_all, layer, s_lru[:, None, :], lw_, cast,
        nb=nb, tt=tt, chunk=chunk, alpha=alpha)
    return (x, tail_dn[:, SUBLANES - (CONV_W - 1):, :], s_new, tail_lru[:, SUBLANES - (CONV_W - 1):, :],
            h_new[:, 0, :]), converted


def kernel(x_prompt, x_sample, cache_dn_conv, state_dn, cache_lru_conv, state_lru, c_prompt, c_sample,
           w_ada, b_ada, w_in, dn_conv_w, dn_a_log, dn_dt_bias, dn_norm_w,
           lru_conv_w, lru_conv_b, lru_w_r, lru_b_r, lru_w_i, lru_b_i, lru_lambda, w_out,
           ln1_g, ln1_b, ln2_g, ln2_b, ffn_w_gate, ffn_w_up, ffn_w_down,
           moe_w_router, moe_w_gate, moe_w_up, moe_w_down):
    p = dict(w_in=w_in, dn_conv_w=dn_conv_w, dn_a_log=dn_a_log, dn_dt_bias=dn_dt_bias, dn_norm_w=dn_norm_w,
             lru_conv_w=lru_conv_w, lru_conv_b=lru_conv_b, lru_w_r=lru_w_r, lru_b_r=lru_b_r,
             lru_w_i=lru_w_i, lru_b_i=lru_b_i, lru_lambda=lru_lambda, w_out=w_out,
             ln1_g=ln1_g, ln1_b=ln1_b, ln2_g=ln2_g, ln2_b=ln2_b,
             ffn_w_gate=ffn_w_gate, ffn_w_up=ffn_w_up, ffn_w_down=ffn_w_down,
             moe_w_router=moe_w_router, moe_w_gate=moe_w_gate, moe_w_up=moe_w_up, moe_w_down=moe_w_down)
    depth, d, _ = w_ada.shape
    bp = x_prompt.shape[0]
    bs = x_sample.shape[0]
    lw = cache_lru_conv.shape[-1]
    alpha = (2 * depth) ** 0.25
    weights = [_layer_weights(l, p, d, lw) for l in range(depth)]

    c_all = jnp.concatenate([c_prompt, c_sample], axis=0)
    mod_all = _modulation(c_all, w_ada, b_ada)
    groups = [
        dict(x=x_prompt, rows=slice(0, bp), conv_dn=jnp.zeros((depth, bp, CONV_W - 1, DN_QKV), F32),
             s_dn=jnp.zeros((depth, bp, DN_HEADS, DN_DK, DN_DV), F32),
             conv_lru=jnp.zeros((depth, bp, CONV_W - 1, lw), F32), s_lru=jnp.zeros((depth, bp, lw), F32)),
        dict(x=x_sample, rows=slice(bp, bp + bs), conv_dn=cache_dn_conv, s_dn=state_dn,
             conv_lru=cache_lru_conv, s_lru=state_lru),
    ]
    for g in groups:
        g["til"] = _tiling(g["x"].shape[0], g["x"].shape[1])
        g["new"] = [[], [], [], []]
    nb0, tt0, _ = groups[0]["til"]["mixer"]
    slabs = _bf16_slabs(weights, (bp // nb0) * (x_prompt.shape[1] // tt0))
    for l in range(depth):
        lw_ = weights[l]
        mods = [mod_all[l, g["rows"]][:, None, :] for g in groups]
        for gi, (g, mod) in enumerate(zip(groups, mods)):
            first_call = l == 0 and gi == 0
            res, converted = _token_mix(g["x"], mod, g["conv_dn"][l], g["s_dn"], l, g["conv_lru"][l],
                                        g["s_lru"][l], lw_, g["til"], alpha,
                                        cast=tuple(a for _, _, a in slabs) if first_call else ())
            if first_call:
                for (k, n, _), c in zip(slabs, converted):
                    weights[k][n] = c.reshape(weights[k][n].shape)
                for w in weights:
                    for n in CHANNEL_MIX_WEIGHTS:
                        w[n] = w[n].astype(BF16)
            g["x"] = res[0]
            for acc, new in zip(g["new"], res[1:]):
                acc.append(new)
        if l % 2 == 0:
            for g, mod in zip(groups, mods):
                nb, tt = g["til"]["ffn"]
                g["x"] = _dense_ffn(g["x"], mod, lw_, nb=nb, tt=tt, alpha=alpha)
        else:
            xs = _moe_layer([g["x"] for g in groups], mods, lw_, [g["til"]["ffn"] for g in groups], alpha)
            for g, x in zip(groups, xs):
                g["x"] = x
    states = [jnp.stack(acc) for g in groups for acc in g["new"]]
    return (groups[0]["x"], groups[1]["x"]) + tuple(states)
```

```python
import functools
import math

import jax
import jax.numpy as jnp
from jax import lax
from jax.experimental import pallas as pl
from jax.experimental.pallas import tpu as pltpu

F32 = jnp.float32
BF16 = jnp.bfloat16

DN_HEADS = 4
DN_DK = 128
DN_DV = 128
DN_WIDTH = DN_HEADS * DN_DV
DN_QKV = 3 * DN_WIDTH
LRU_BLOCKS = 8
LRU_C = 8.0
CONV_W = 4
N_MOD = 6
TOP_K = 2
LN_EPS = 1e-5
NORM_EPS = 1e-6

SUBLANES = 8
LANES = 128
MXU_DIM = 256
VMEM_LIMIT_BYTES = 56 * 1024 * 1024

PROMPT_CHUNK = 64
INV_BASE_BLOCK = 16


def _sigmoid(x):
    return 0.5 + 0.5 * jnp.tanh(0.5 * x)


def _silu(x):
    half = 0.5 * x
    return half + half * jnp.tanh(half)


def _softplus(x):
    return jnp.maximum(x, 0.0) + jnp.log1p(jnp.exp(-jnp.abs(x)))


def _gelu_tanh(x):
    return 0.5 * x * (1.0 + jnp.tanh(math.sqrt(2.0 / math.pi) * (x + 0.044715 * (x * x * x))))


def _mm(a, b):
    return jnp.dot(a.astype(BF16), b.astype(BF16), preferred_element_type=F32)


def _layer_norm(x, g, b):
    mu = jnp.mean(x, axis=-1, keepdims=True)
    xc = x - mu
    var = jnp.mean(xc * xc, axis=-1, keepdims=True)
    return xc * lax.rsqrt(var + LN_EPS) * g + b


def _params(sem):
    return pltpu.CompilerParams(dimension_semantics=sem, vmem_limit_bytes=VMEM_LIMIT_BYTES)


def _mod_kernel(c_ref, w_ref, b_ref, o_ref):
    sc = _silu(c_ref[...])
    o_ref[0] = _mm(sc, w_ref[0]) + b_ref[0]


def _modulation(c_all, w_ada, b_ada):
    depth, d, n = w_ada.shape
    rows = c_all.shape[0]
    tn = 1536 if n % 1536 == 0 else n
    return pl.pallas_call(
        _mod_kernel,
        out_shape=jax.ShapeDtypeStruct((depth, rows, n), F32),
        grid=(depth, n // tn),
        in_specs=[
            pl.BlockSpec((rows, d), lambda l, j: (0, 0)),
            pl.BlockSpec((1, d, tn), lambda l, j: (l, 0, j)),
            pl.BlockSpec((1, 1, tn), lambda l, j: (l, 0, j)),
        ],
        out_specs=pl.BlockSpec((1, rows, tn), lambda l, j: (l, 0, j)),
        compiler_params=_params(("arbitrary", "arbitrary")),
        name="adaln_modulation",
    )(c_all, w_ada, b_ada.reshape(depth, 1, n))


def _causal_conv(u, win_ref, w, nb, tt, c0):
    c = u.shape[-1]
    cols = slice(c0, c0 + c)
    win_ref[:, SUBLANES:, cols] = u.reshape(nb, tt, c)
    out = u * w[CONV_W - 1:CONV_W, cols]
    for j in range(1, CONV_W):
        prev = win_ref[:, SUBLANES - j:SUBLANES - j + tt, cols].reshape(nb * tt, c)
        out = out + prev * w[CONV_W - 1 - j:CONV_W - j, cols]
    tail = win_ref[:, tt:tt + SUBLANES, cols]
    win_ref[:, 0:SUBLANES, cols] = tail
    return out, tail


def _l2norm_heads(x, scale):
    outs = []
    for h in range(DN_HEADS):
        xh = x[:, h * DN_DK:(h + 1) * DN_DK]
        ss = jnp.sum(xh * xh, axis=-1, keepdims=True)
        outs.append(xh * (lax.rsqrt(ss + NORM_EPS) * scale))
    return jnp.concatenate(outs, axis=-1)


def _inproj_stage(x_ref, mod_ref, hdn_ref, hlru_ref, w_ref, cwdn_ref, cwlru_ref, cblru_ref,
                  alog_ref, dtb_ref, wr_ref, wi_ref, br_ref, bi_ref, lam_ref,
                  tdn_ref, tlru_ref, wdn_sc, wlru_sc, out, *, nb, tt):
    d = x_ref.shape[-1]
    lw = hlru_ref.shape[-1]
    rows = nb * tt

    @pl.when(pl.program_id(1) == 0)
    def _():
        wdn_sc[:, 0:SUBLANES, :] = hdn_ref[...]
        wlru_sc[:, 0:SUBLANES, :] = hlru_ref[...]

    m = mod_ref[...]
    shift = m[:, :, 0:d]
    scale = m[:, :, d:2 * d]
    h = (x_ref[...] * (1.0 + scale) + shift).reshape(rows, d)
    proj = _mm(h, w_ref[...])
    z = proj[:, DN_QKV:DN_QKV + DN_WIDTH]
    u_lru = proj[:, DN_QKV + DN_WIDTH:DN_QKV + DN_WIDTH + lw]
    y = proj[:, DN_QKV + DN_WIDTH + lw:DN_QKV + DN_WIDTH + 2 * lw]
    ab = proj[:, DN_QKV + DN_WIDTH + 2 * lw:]

    cw_dn = cwdn_ref[...]

    def dn_part(part, norm_scale):
        c0 = part * DN_WIDTH
        conv, tail = _causal_conv(proj[:, c0:c0 + DN_WIDTH], wdn_sc, cw_dn, nb, tt, c0)
        tdn_ref[:, :, c0:c0 + DN_WIDTH] = tail
        act = _silu(conv)
        return act if norm_scale is None else _l2norm_heads(act, norm_scale)

    out["q"] = dn_part(0, DN_DK ** -0.5)
    yield
    out["k"] = dn_part(1, 1.0)
    lane = lax.broadcasted_iota(jnp.int32, ab.shape, 1)
    g_full = -jnp.exp(alog_ref[...]) * _softplus(ab + dtb_ref[...])
    out["gb"] = jnp.where(lane < DN_HEADS, g_full, _sigmoid(ab))
    yield
    out["v"] = dn_part(2, None)
    yield
    out["gz"] = _silu(z)
    out["gy"] = _gelu_tanh(y)
    yield

    conv_lru, tail_lru = _causal_conv(u_lru, wlru_sc, cwlru_ref[...], nb, tt, 0)
    tlru_ref[...] = tail_lru
    xc = conv_lru + cblru_ref[...]
    half = lw // 2
    r_pre = jnp.concatenate([_mm(xc[:, :half], wr_ref[0]), _mm(xc[:, half:], wr_ref[1])], axis=-1)
    i_pre = jnp.concatenate([_mm(xc[:, :half], wi_ref[0]), _mm(xc[:, half:], wi_ref[1])], axis=-1)
    r = _sigmoid(r_pre + br_ref[...])
    i = _sigmoid(i_pre + bi_ref[...])
    log_a = -LRU_C * r * _softplus(-lam_ref[...])
    out["a"] = jnp.exp(log_a)
    th = jnp.tanh(log_a)
    out["inp"] = jnp.sqrt(-2.0 * th / (1.0 - th)) * (i * xc)
    yield


def _unit_lower_inverses(ls, row, col, chunk, base):
    def same_block(s):
        k = s.bit_length() - 1
        return (row >> k) == (col >> k)

    eye = jnp.where(row == col, 1.0, 0.0)
    base_mask = same_block(base)
    powers = [jnp.where(base_mask, l, 0.0) for l in ls]
    invs = [eye - d for d in powers]
    p = 2
    while p < base:
        powers = [_mm(d, d) for d in powers]
        invs = [t + _mm(t, d) for t, d in zip(invs, powers)]
        p *= 2
        yield
    s = base
    while s < chunk:
        off_mask = same_block(2 * s) & jnp.logical_not(same_block(s))
        tmp = [_mm(jnp.where(off_mask, l, 0.0), t) for l, t in zip(ls, invs)]
        invs = [t - _mm(t, x) for t, x in zip(invs, tmp)]
        s *= 2
        yield
    return invs


def _mixer_stage(vals, x_ref, mod_ref, wout_ref, nw_ref, lng_ref, lnb_ref,
                 y_ref, sout_ref, hout_ref, *, nb, tt, chunk, alpha):
    d = x_ref.shape[-1]
    lw = hout_ref.shape[-1]
    rows = nb * tt
    n_chunks = rows // chunk
    chunks_per_seq = tt // chunk
    log_chunk = chunk.bit_length() - 1

    row = lax.broadcasted_iota(jnp.int32, (rows, rows), 0)
    col = lax.broadcasted_iota(jnp.int32, (rows, rows), 1)
    incl = ((row >> log_chunk) == (col >> log_chunk)) & (col <= row)
    strict = incl & (col < row)
    gb = vals["gb"]
    gc = jnp.dot(jnp.where(incl, 1.0, 0.0), gb, precision=lax.Precision.HIGHEST, preferred_element_type=F32)
    gc_t = gc.T

    heads = range(DN_HEADS)
    qs = [vals["q"][:, h * DN_DK:(h + 1) * DN_DK] for h in heads]
    ks = [vals["k"][:, h * DN_DK:(h + 1) * DN_DK] for h in heads]
    gcols = [gc[:, h:h + 1] for h in heads]
    betas = [gb[:, DN_HEADS + h:DN_HEADS + h + 1] for h in heads]
    decays = [jnp.where(incl, jnp.exp(jnp.where(incl, gcols[h] - gc_t[h:h + 1, :], 0.0)), 0.0) for h in heads]
    kbs = [ks[h] * betas[h] for h in heads]
    qk_kks = [lax.dot_general(jnp.concatenate([qs[h], kbs[h]], axis=0).astype(BF16), ks[h].astype(BF16),
                              (((1,), (1,)), ((), ())), preferred_element_type=F32) for h in heads]
    qks = [qk_kks[h][:rows] * decays[h] for h in heads]
    lmats = [jnp.where(strict, qk_kks[h][rows:] * decays[h], 0.0) for h in heads]
    yield
    tmats = yield from _unit_lower_inverses(lmats, row, col, chunk, min(INV_BASE_BLOCK, chunk))
    egcs = [jnp.exp(g) for g in gcols]
    vs = [vals["v"][:, h * DN_DV:(h + 1) * DN_DV] for h in heads]
    uws = [_mm(tmats[h], jnp.concatenate([vs[h] * betas[h], kbs[h] * egcs[h]], axis=-1)) for h in heads]
    us = [uw[:, :DN_DV] for uw in uws]
    ws = [uw[:, DN_DV:] for uw in uws]
    qes = [qs[h] * egcs[h] for h in heads]
    yield

    a = vals["a"]
    bacc = vals["inp"]
    gy = vals["gy"]
    t = lax.broadcasted_iota(jnp.int32, (rows, lw), 0) & (SUBLANES - 1)
    s = 1
    while s < SUBLANES:
        keep = t >= s
        a_prev = jnp.where(keep, pltpu.roll(a, s, 0), 1.0)
        b_prev = jnp.where(keep, pltpu.roll(bacc, s, 0), 0.0)
        bacc = a * b_prev + bacc
        a = a * a_prev
        s *= 2
    tiles = tt // SUBLANES
    a4 = a.reshape(nb, tiles, SUBLANES, lw)
    b4 = bacc.reshape(nb, tiles, SUBLANES, lw)
    carry = hout_ref[...]
    h_tiles = []
    for k in range(tiles):
        hk = a4[:, k] * carry + b4[:, k]
        carry = hk[:, SUBLANES - 1:SUBLANES, :]
        h_tiles.append(hk)
    hout_ref[...] = carry
    hs = h_tiles[0] if tiles == 1 else jnp.stack(h_tiles, axis=1)
    o_b = hs.reshape(rows, lw) * gy
    yield

    v_new_parts = [[] for _ in heads]
    o_inter_parts = [[] for _ in heads]
    states = [None for _ in heads]
    for c in range(n_chunks):
        seq = c // chunks_per_seq
        lo, hi = c * chunk, (c + 1) * chunk
        if c % chunks_per_seq == 0:
            states = [sout_ref[seq, h] for h in heads]
        wqs = [_mm(jnp.concatenate([ws[h][lo:hi], qes[h][lo:hi]], axis=0), states[h]) for h in heads]
        v_news = [us[h][lo:hi] - wqs[h][:chunk] for h in heads]
        new_states = []
        for h in heads:
            v_new_parts[h].append(v_news[h])
            o_inter_parts[h].append(wqs[h][chunk:])
            g_last = gcols[h][hi - 1:hi, :]
            k_dec = ks[h][lo:hi] * jnp.exp(g_last - gcols[h][lo:hi])
            new_states.append(states[h] * jnp.exp(g_last) + lax.dot_general(
                k_dec.astype(BF16), v_news[h].astype(BF16), (((0,), (0,)), ((), ())),
                preferred_element_type=F32))
        states = new_states
        if (c + 1) % chunks_per_seq == 0:
            for h in heads:
                sout_ref[seq, h] = states[h]
        yield
    o_heads = []
    for h in heads:
        o = jnp.concatenate(o_inter_parts[h], axis=0) + _mm(qks[h], jnp.concatenate(v_new_parts[h], axis=0))
        ms = jnp.mean(o * o, axis=-1, keepdims=True)
        o_heads.append(o * lax.rsqrt(ms + NORM_EPS) * nw_ref[...] * vals["gz"][:, h * DN_DV:(h + 1) * DN_DV])
    o_a = jnp.concatenate(o_heads, axis=-1)

    mixed = _mm(jnp.concatenate([o_a, o_b], axis=-1), wout_ref[...]).reshape(nb, tt, d)
    gate = mod_ref[...][:, :, 2 * d:3 * d]
    y_ref[...] = _layer_norm(alpha * x_ref[...] + (1.0 + gate) * mixed, lng_ref[...], lnb_ref[...])


N_TOKEN_MIX_INPUTS = 21
N_TOKEN_MIX_OUTPUTS = 5


def _token_mix_kernel(*refs, n_cast, nb, tt, chunk, alpha):
    (x_ref, mod_ref, hdn_ref, hlru_ref, s0_ref, h0_ref, w_ref, cwdn_ref, cwlru_ref, cblru_ref, alog_ref, dtb_ref,
     wr_ref, wi_ref, br_ref, bi_ref, lam_ref, wout_ref, nw_ref, lng_ref, lnb_ref) = refs[:N_TOKEN_MIX_INPUTS]
    cast_in = refs[N_TOKEN_MIX_INPUTS:N_TOKEN_MIX_INPUTS + n_cast]
    outs = refs[N_TOKEN_MIX_INPUTS + n_cast:]
    y_ref, tdn_ref, tlru_ref, sout_ref, hout_ref = outs[:N_TOKEN_MIX_OUTPUTS]
    cast_out = outs[N_TOKEN_MIX_OUTPUTS:N_TOKEN_MIX_OUTPUTS + n_cast]
    wdn_sc, wlru_sc = outs[N_TOKEN_MIX_OUTPUTS + n_cast:]

    for src, dst in zip(cast_in, cast_out):
        dst[...] = src[...].astype(BF16)

    @pl.when(pl.program_id(1) == 0)
    def _():
        sout_ref[...] = s0_ref[0]
        hout_ref[...] = h0_ref[...]

    vals = {}
    stage1 = _inproj_stage(
        x_ref, mod_ref, hdn_ref, hlru_ref, w_ref, cwdn_ref, cwlru_ref, cblru_ref, alog_ref, dtb_ref,
        wr_ref, wi_ref, br_ref, bi_ref, lam_ref, tdn_ref, tlru_ref, wdn_sc, wlru_sc, vals, nb=nb, tt=tt)
    stage2 = _mixer_stage(vals, x_ref, mod_ref, wout_ref, nw_ref, lng_ref, lnb_ref,
                          y_ref, sout_ref, hout_ref, nb=nb, tt=tt, chunk=chunk, alpha=alpha)
    while "gb" not in vals:
        next(stage1)
    pending = [stage2, stage1]
    while pending:
        pending = [g for g in pending if next(g, StopIteration) is not StopIteration]


def _token_mix_call(x, mod, hist_dn, hist_lru, s_dn_all, layer, h0, lw_, cast=(), *, nb, tt, chunk, alpha):
    b, t, d = x.shape
    lw = hist_lru.shape[-1]
    grid = (b // nb, t // tt)
    steps = grid[0] * grid[1]
    slab = lambda a: pl.BlockSpec((a.shape[0] // steps, a.shape[1]), lambda i, j: (i * grid[1] + j, 0))
    seq_blk = lambda c: pl.BlockSpec((nb, tt, c), lambda i, j: (i, j, 0))
    per_seq = lambda r, c: pl.BlockSpec((nb, r, c), lambda i, j: (i, 0, 0))
    whole = lambda a: pl.BlockSpec(a.shape, lambda i, j: (0,) * a.ndim)
    state_in = pl.BlockSpec((1, nb, DN_HEADS, DN_DK, DN_DV), lambda i, j: (layer, i, 0, 0, 0))
    state_out = pl.BlockSpec((nb, DN_HEADS, DN_DK, DN_DV), lambda i, j: (i, 0, 0, 0))
    weights = [lw_[n] for n in ("w_cat", "dn_conv_w", "lru_conv_w", "lru_conv_b", "a_log", "dt_bias",
                                "w_r", "w_i", "b_r", "b_i", "lam", "w_out", "dn_norm_w", "ln1_g", "ln1_b")]
    return pl.pallas_call(
        functools.partial(_token_mix_kernel, n_cast=len(cast), nb=nb, tt=tt, chunk=chunk, alpha=alpha),
        out_shape=(
            jax.ShapeDtypeStruct((b, t, d), F32),
            jax.ShapeDtypeStruct((b, SUBLANES, DN_QKV), F32),
            jax.ShapeDtypeStruct((b, SUBLANES, lw), F32),
            jax.ShapeDtypeStruct((b, DN_HEADS, DN_DK, DN_DV), F32),
            jax.ShapeDtypeStruct((b, 1, lw), F32),
        ) + tuple(jax.ShapeDtypeStruct(a.shape, BF16) for a in cast),
        grid=grid,
        in_specs=[seq_blk(d), per_seq(1, N_MOD * d), per_seq(SUBLANES, DN_QKV), per_seq(SUBLANES, lw),
                  state_in, per_seq(1, lw)] + [whole(a) for a in weights] + [slab(a) for a in cast],
        out_specs=(seq_blk(d), per_seq(SUBLANES, DN_QKV), per_seq(SUBLANES, lw), state_out, per_seq(1, lw))
        + tuple(slab(a) for a in cast),
        scratch_shapes=[pltpu.VMEM((nb, SUBLANES + tt, DN_QKV), F32), pltpu.VMEM((nb, SUBLANES + tt, lw), F32)],
        compiler_params=_params(("arbitrary", "arbitrary")),
        name="token_mixer",
    )(x, mod, hist_dn, hist_lru, s_dn_all, h0, *weights, *cast)


def _ffn_kernel(x_ref, mod_ref, wg_ref, wu_ref, wd_ref, lng_ref, lnb_ref, y_ref, *, alpha):
    nb, tt, d = x_ref.shape
    m = mod_ref[...]
    h = (x_ref[...] * (1.0 + m[:, :, 4 * d:5 * d]) + m[:, :, 3 * d:4 * d]).reshape(nb * tt, d).astype(BF16)
    act = _silu(jnp.dot(h, wg_ref[...], preferred_element_type=F32)) * jnp.dot(
        h, wu_ref[...], preferred_element_type=F32)
    ff = _mm(act, wd_ref[...]).reshape(nb, tt, d)
    y_ref[...] = _layer_norm(alpha * x_ref[...] + (1.0 + m[:, :, 5 * d:6 * d]) * ff, lng_ref[...], lnb_ref[...])


def _ff_tile(ff):
    for n in (2, 4, 7, 8, 11, 14, 16, 22, 28):
        if ff % n == 0 and (ff // n) % LANES == 0 and ff // n <= 2048:
            return ff // n
    return ff


def _dense_ffn(x, mod, lw_, *, nb, tt, alpha):
    b, t, d = x.shape
    nt = t // tt
    seq_blk = pl.BlockSpec((nb, tt, d), lambda i: (i // nt, i % nt, 0))
    whole = lambda a: pl.BlockSpec(a.shape, lambda i: (0,) * a.ndim)
    resident = lambda a: pl.BlockSpec(a.shape, lambda i: (0,) * a.ndim, pipeline_mode=pl.Buffered(1))
    return pl.pallas_call(
        functools.partial(_ffn_kernel, alpha=alpha),
        out_shape=jax.ShapeDtypeStruct((b, t, d), F32),
        grid=((b // nb) * nt,),
        in_specs=[seq_blk, pl.BlockSpec((nb, 1, N_MOD * d), lambda i: (i // nt, 0, 0)),
                  resident(lw_["w_gate"]), resident(lw_["w_up"]), resident(lw_["w_down"]),
                  whole(lw_["ln2_g"]), whole(lw_["ln2_b"])],
        out_specs=seq_blk,
        compiler_params=_params(("arbitrary",)),
        name="dense_ffn",
    )(x, mod, lw_["w_gate"], lw_["w_up"], lw_["w_down"], lw_["ln2_g"], lw_["ln2_b"])


MOE_CHUNK = 512
MOE_SLOT_TILE = 512
MOE_SUB = 128
MOE_COMBINE_BLOCK = 256
MOE_COMBINE_FANIN = 4
ROUTE_I1, ROUTE_I2, ROUTE_R1, ROUTE_R2, ROUTE_W1, ROUTE_W2 = range(6)


def _router_kernel(x_ref, mod_ref, wr_ref, base_ref, h_ref, meta_ref, meta_t_ref, blkbase_ref, cnt_ref, run_sc,
                   *, n_experts):
    nb, tt, d = x_ref.shape
    rows = nb * tt

    @pl.when(pl.program_id(0) == 0)
    def _():
        run_sc[...] = base_ref[...]

    m = mod_ref[...]
    h = (x_ref[...] * (1.0 + m[:, :, 4 * d:5 * d]) + m[:, :, 3 * d:4 * d]).reshape(rows, d)
    h_ref[...] = h.astype(BF16)
    logits = jnp.dot(h, wr_ref[...], precision=lax.Precision.HIGHEST, preferred_element_type=F32)
    lane = lax.broadcasted_iota(jnp.int32, logits.shape, 1)
    neg = jnp.float32(-jnp.inf)
    lg = jnp.where(lane < n_experts, logits, neg)
    m1 = jnp.max(lg, axis=-1, keepdims=True)
    i1 = jnp.min(jnp.where(lg == m1, lane, LANES), axis=-1, keepdims=True)
    lg2 = jnp.where(lane == i1, neg, lg)
    m2 = jnp.max(lg2, axis=-1, keepdims=True)
    i2 = jnp.min(jnp.where(lg2 == m2, lane, LANES), axis=-1, keepdims=True)
    e2 = jnp.exp(m2 - m1)
    w1 = 1.0 / (1.0 + e2)
    w2 = e2 / (1.0 + e2)
    sel = jnp.where(lane == i1, 1.0, jnp.where(lane == i2, 1.0, 0.0))
    r = lax.broadcasted_iota(jnp.int32, (rows, rows), 0)
    c = lax.broadcasted_iota(jnp.int32, (rows, rows), 1)
    rank = _mm(jnp.where(c < r, 1.0, 0.0), sel) + run_sc[0:1, :]
    r1 = jnp.sum(jnp.where(lane == i1, rank, 0.0), axis=-1, keepdims=True)
    r2 = jnp.sum(jnp.where(lane == i2, rank, 0.0), axis=-1, keepdims=True)
    fields = (i1.astype(F32), i2.astype(F32), r1, r2, w1, w2)
    meta = jnp.zeros_like(logits)
    for k, v in enumerate(fields):
        meta = jnp.where(lane == k, v, meta)
    meta_ref[...] = meta
    meta_t_ref[...] = meta.T[0:SUBLANES, :]
    blkbase_ref[0] = run_sc[...]
    run_sc[...] = run_sc[...] + jnp.sum(sel, axis=0, keepdims=True)
    cnt_ref[...] = run_sc[...]


def _router(x, mod, w_router, base, *, nb, tt, n_experts):
    b, t, d = x.shape
    rows = nb * tt
    nt = t // tt
    nblk = (b // nb) * nt
    whole = lambda a: pl.BlockSpec(a.shape, lambda i: (0,) * a.ndim)
    return pl.pallas_call(
        functools.partial(_router_kernel, n_experts=n_experts),
        out_shape=(jax.ShapeDtypeStruct((nblk * rows, d), BF16),
                   jax.ShapeDtypeStruct((nblk * rows, LANES), F32),
                   jax.ShapeDtypeStruct((SUBLANES, nblk * rows), F32),
                   jax.ShapeDtypeStruct((nblk, SUBLANES, LANES), F32),
                   jax.ShapeDtypeStruct((SUBLANES, LANES), F32)),
        grid=(nblk,),
        in_specs=[pl.BlockSpec((nb, tt, d), lambda i: (i // nt, i % nt, 0)),
                  pl.BlockSpec((nb, 1, N_MOD * d), lambda i: (i // nt, 0, 0)),
                  whole(w_router), whole(base)],
        out_specs=(pl.BlockSpec((rows, d), lambda i: (i, 0)),
                   pl.BlockSpec((rows, LANES), lambda i: (i, 0)),
                   pl.BlockSpec((SUBLANES, rows), lambda i: (0, i)),
                   pl.BlockSpec((1, SUBLANES, LANES), lambda i: (i, 0, 0)),
                   pl.BlockSpec((SUBLANES, LANES), lambda i: (0, 0))),
        scratch_shapes=[pltpu.VMEM((SUBLANES, LANES), F32)],
        compiler_params=_params(("arbitrary",)),
        name="moe_router",
    )(x, mod, w_router, base)


def _slot_of(expert, rank, start_ref, n_experts):
    start = jnp.zeros_like(rank)
    for e in range(n_experts):
        start = jnp.where(expert == e, start_ref[e].astype(F32), start)
    return (start + rank).astype(jnp.int32)


def _gather_kernel(clo_ref, chi_ref, start_ref, *refs, group_chunks, n_experts):
    n_groups = len(group_chunks)
    m_refs = refs[:n_groups]
    h_refs = refs[n_groups:2 * n_groups]
    xs_ref, ws_ref, acc_sc, wacc_sc = refs[2 * n_groups:]
    g = pl.program_id(0)
    n_sub = MOE_SLOT_TILE // MOE_SUB
    for j in range(n_sub):
        q = g * n_sub + j
        ids = g * MOE_SLOT_TILE + j * MOE_SUB + lax.broadcasted_iota(jnp.int32, (MOE_SUB, MOE_CHUNK), 0)
        acc_sc[...] = jnp.zeros_like(acc_sc)
        wacc_sc[...] = jnp.zeros_like(wacc_sc)
        first = 0
        for m_ref, h_ref, n_chunks in zip(m_refs, h_refs, group_chunks):
            def body(c, carry, m_ref=m_ref, h_ref=h_ref):
                off = pl.multiple_of(c * MOE_CHUNK, MOE_CHUNK)
                rec = m_ref[:, pl.ds(off, MOE_CHUNK)]
                row = lambda k: rec[k:k + 1, :]
                hit1 = _slot_of(row(ROUTE_I1), row(ROUTE_R1), start_ref, n_experts) == ids
                hit2 = _slot_of(row(ROUTE_I2), row(ROUTE_R2), start_ref, n_experts) == ids
                p = jnp.where(hit1, 1.0, jnp.where(hit2, 1.0, 0.0)).astype(BF16)
                acc_sc[...] += jnp.dot(p, h_ref[pl.ds(off, MOE_CHUNK), :], preferred_element_type=F32)
                w = jnp.where(hit1, row(ROUTE_W1), 0.0) + jnp.where(hit2, row(ROUTE_W2), 0.0)
                wacc_sc[...] += jnp.broadcast_to(jnp.sum(w, axis=-1, keepdims=True), wacc_sc.shape)
                return carry

            lo = jnp.clip(clo_ref[q] - first, 0, n_chunks)
            hi = jnp.clip(chi_ref[q] - first, 0, n_chunks)
            lax.fori_loop(lo, hi, body, 0)
            first += n_chunks
        xs_ref[j * MOE_SUB:(j + 1) * MOE_SUB, :] = acc_sc[...].astype(BF16)
        ws_ref[j * MOE_SUB:(j + 1) * MOE_SUB, :] = wacc_sc[...]


def _gather_slots(c_lo, c_hi, run_start, metas_t, hs, n_tiles, n_experts):
    d = hs[0].shape[-1]
    vmem = pl.BlockSpec(memory_space=pltpu.VMEM)
    return pl.pallas_call(
        functools.partial(_gather_kernel, group_chunks=tuple(h.shape[0] // MOE_CHUNK for h in hs),
                          n_experts=n_experts),
        out_shape=(jax.ShapeDtypeStruct((n_tiles * MOE_SLOT_TILE, d), BF16),
                   jax.ShapeDtypeStruct((n_tiles * MOE_SLOT_TILE, LANES), F32)),
        grid_spec=pltpu.PrefetchScalarGridSpec(
            num_scalar_prefetch=3, grid=(n_tiles,),
            in_specs=[vmem] * (2 * len(hs)),
            out_specs=(pl.BlockSpec((MOE_SLOT_TILE, d), lambda g, lo, hi, st: (g, 0)),
                       pl.BlockSpec((MOE_SLOT_TILE, LANES), lambda g, lo, hi, st: (g, 0))),
            scratch_shapes=[pltpu.VMEM((MOE_SUB, d), F32), pltpu.VMEM((MOE_SUB, LANES), F32)]),
        compiler_params=_params(("arbitrary",)),
        name="moe_gather",
    )(c_lo, c_hi, run_start, *metas_t, *hs)


def _expert_kernel(te_ref, tv_ref, xs_ref, ws_ref, wg_ref, wu_ref, wd_ref, o_ref, *, ff_chunk):
    g = pl.program_id(0)

    @pl.when(tv_ref[g] != 0)
    def _():
        x = xs_ref[...]
        ff = wg_ref.shape[-1]
        acc = jnp.zeros(o_ref.shape, F32)
        for f0 in range(0, ff, ff_chunk):
            gate = jnp.dot(x, wg_ref[0, :, f0:f0 + ff_chunk], preferred_element_type=F32)
            up = jnp.dot(x, wu_ref[0, :, f0:f0 + ff_chunk], preferred_element_type=F32)
            acc = acc + _mm(_silu(gate) * up, wd_ref[0, f0:f0 + ff_chunk, :])
        o_ref[...] = (ws_ref[:, 0:1] * acc).astype(BF16)

    @pl.when(tv_ref[g] == 0)
    def _():
        o_ref[...] = jnp.zeros_like(o_ref)


def _expert_ffn(tile_expert, tile_valid, xs, ws, lw_):
    s_total, d = xs.shape
    _, _, ff = lw_["w_gate"].shape
    n_tiles = s_total // MOE_SLOT_TILE
    once = pl.Buffered(1)
    return pl.pallas_call(
        functools.partial(_expert_kernel, ff_chunk=_ff_tile(ff)),
        out_shape=jax.ShapeDtypeStruct((s_total, d), BF16),
        grid_spec=pltpu.PrefetchScalarGridSpec(
            num_scalar_prefetch=2, grid=(n_tiles,),
            in_specs=[pl.BlockSpec((MOE_SLOT_TILE, d), lambda g, te, tv: (g, 0)),
                      pl.BlockSpec((MOE_SLOT_TILE, LANES), lambda g, te, tv: (g, 0)),
                      pl.BlockSpec((1, d, ff), lambda g, te, tv: (te[g], 0, 0), pipeline_mode=once),
                      pl.BlockSpec((1, d, ff), lambda g, te, tv: (te[g], 0, 0), pipeline_mode=once),
                      pl.BlockSpec((1, ff, d), lambda g, te, tv: (te[g], 0, 0), pipeline_mode=once)],
            out_specs=pl.BlockSpec((MOE_SLOT_TILE, d), lambda g, te, tv: (g, 0))),
        compiler_params=_params(("arbitrary",)),
        name="expert_ffn",
    )(tile_expert, tile_valid, xs, ws, lw_["w_gate"], lw_["w_up"], lw_["w_down"])


def _combine_kernel(ic_ref, ik_ref, if_ref, bf_ref, start_ref, x_ref, mod_ref, meta_ref, *refs, alpha,
                    n_experts):
    os_refs = refs[:MOE_COMBINE_FANIN]
    lng_ref, lnb_ref, y_ref, acc_sc, slot_sc = refs[MOE_COMBINE_FANIN:]
    nb, tt, d = x_ref.shape
    rows = nb * tt
    w = pl.program_id(0)
    flags = if_ref[w]

    @pl.when((flags & 1) != 0)
    def _():
        rec = meta_ref[...]
        col = lambda k: rec[:, k:k + 1]
        slot_sc[:, 0:1] = _slot_of(col(ROUTE_I1), col(ROUTE_R1), start_ref, n_experts)
        slot_sc[:, 1:2] = _slot_of(col(ROUTE_I2), col(ROUTE_R2), start_ref, n_experts)

    lane = lax.broadcasted_iota(jnp.int32, (rows, MOE_COMBINE_BLOCK), 1)
    s1 = slot_sc[:, 0:1]
    s2 = slot_sc[:, 1:2]
    total = None
    for j, os_ref in enumerate(os_refs):
        ids = ik_ref[w * MOE_COMBINE_FANIN + j] * MOE_COMBINE_BLOCK + lane
        q = jnp.where(s1 == ids, 1.0, jnp.where(s2 == ids, 1.0, 0.0)).astype(BF16)
        part = jnp.dot(q, os_ref[...], preferred_element_type=F32)
        total = part if total is None else total + part

    @pl.when((flags & 1) != 0)
    def _():
        acc_sc[...] = total

    @pl.when((flags & 1) == 0)
    def _():
        acc_sc[...] += total

    @pl.when((flags & 2) != 0)
    def _():
        gate = mod_ref[...][:, :, 5 * d:6 * d]
        ff = acc_sc[...].reshape(nb, tt, d)
        y_ref[...] = _layer_norm(alpha * x_ref[...] + (1.0 + gate) * ff, lng_ref[...], lnb_ref[...])


def _combine(items, run_start, x, mod, meta, out_sorted, lw_, *, nb, tt, alpha, n_experts):
    step_chunk, step_blocks, step_flags, block_fetch = items
    b, t, d = x.shape
    rows = nb * tt
    nt = t // tt
    whole = lambda a: pl.BlockSpec(a.shape, lambda w, ic, ik, fl, bf, st: (0,) * a.ndim)
    seq_blk = pl.BlockSpec((nb, tt, d), lambda w, ic, ik, fl, bf, st: (ic[w] // nt, ic[w] % nt, 0))
    slot_blk = lambda j: pl.BlockSpec((MOE_COMBINE_BLOCK, d),
                                      lambda w, ic, ik, fl, bf, st: (bf[w * MOE_COMBINE_FANIN + j], 0))
    return pl.pallas_call(
        functools.partial(_combine_kernel, alpha=alpha, n_experts=n_experts),
        out_shape=jax.ShapeDtypeStruct((b, t, d), F32),
        grid_spec=pltpu.PrefetchScalarGridSpec(
            num_scalar_prefetch=5, grid=(step_chunk.shape[0],),
            in_specs=[seq_blk,
                      pl.BlockSpec((nb, 1, N_MOD * d), lambda w, ic, ik, fl, bf, st: (ic[w] // nt, 0, 0)),
                      pl.BlockSpec((rows, LANES), lambda w, ic, ik, fl, bf, st: (ic[w], 0))]
            + [slot_blk(j) for j in range(MOE_COMBINE_FANIN)]
            + [whole(lw_["ln2_g"]), whole(lw_["ln2_b"])],
            out_specs=seq_blk,
            scratch_shapes=[pltpu.VMEM((rows, d), F32), pltpu.VMEM((rows, LANES), jnp.int32)]),
        compiler_params=_params(("arbitrary",)),
        name="moe_combine",
    )(step_chunk, step_blocks, step_flags, block_fetch, run_start, x, mod, meta,
      *([out_sorted] * MOE_COMBINE_FANIN), lw_["ln2_g"], lw_["ln2_b"])


def _count_le(sorted_vals, queries):
    return jnp.sum(sorted_vals[None, :] <= queries[:, None], axis=1).astype(jnp.int32)


def _combine_steps(lo, hi, n_steps):
    n_chunks, n_experts = lo.shape
    fan = MOE_COMBINE_FANIN
    first = (lo // MOE_COMBINE_BLOCK).reshape(-1)
    count = jnp.where(hi > lo, (hi - 1) // MOE_COMBINE_BLOCK - lo // MOE_COMBINE_BLOCK + 1, 0).reshape(-1)
    pair_end = jnp.cumsum(count)
    chunk_items = jnp.sum(count.reshape(n_chunks, n_experts), axis=1)
    chunk_item0 = jnp.cumsum(chunk_items) - chunk_items
    chunk_steps = (chunk_items + fan - 1) // fan
    step_end = jnp.cumsum(chunk_steps)
    total_steps = step_end[-1]
    s = jnp.arange(n_steps, dtype=jnp.int32)
    live = s < total_steps
    chunk = jnp.minimum(_count_le(step_end, s), n_chunks - 1)
    chunk = jnp.where(live, chunk, chunk[jnp.maximum(total_steps - 1, 0)])
    q = s - (step_end[chunk] - chunk_steps[chunk])
    j = q[:, None] * fan + jnp.arange(fan, dtype=jnp.int32)[None, :]
    used = live[:, None] & (j < chunk_items[chunk][:, None])
    item = jnp.where(used, chunk_item0[chunk][:, None] + j, 0).reshape(-1)
    pair = jnp.minimum(_count_le(pair_end, item), n_chunks * n_experts - 1)
    block = first[pair] + item - (pair_end[pair] - count[pair])
    block = jnp.where(used.reshape(-1), block, -1)
    fetch = jnp.where(block >= 0, block, jnp.repeat(jnp.maximum(block.reshape(-1, fan)[:, 0], 0), fan))
    flags = (jnp.where(live & (q == 0), 1, 0) + jnp.where(live & (q == chunk_steps[chunk] - 1), 2, 0))
    return chunk.astype(jnp.int32), block.astype(jnp.int32), flags.astype(jnp.int32), fetch.astype(jnp.int32)


def _moe_layer(xs_in, mods, lw_, tilings, alpha):
    n_experts = lw_["w_gate"].shape[0]
    base = jnp.zeros((SUBLANES, LANES), F32)
    hs, metas, metas_t, bases = [], [], [], []
    for x, mod, (nb, tt) in zip(xs_in, mods, tilings):
        assert nb * tt == MOE_CHUNK and (x.shape[0] * x.shape[1]) % MOE_CHUNK == 0
        h, meta, meta_t, blkbase, base = _router(x, mod, lw_["w_router"], base, nb=nb, tt=tt,
                                                 n_experts=n_experts)
        hs.append(h)
        metas.append(meta)
        metas_t.append(meta_t)
        bases.append(blkbase[:, 0, :n_experts])
    n = sum(m.shape[0] for m in metas)
    cum = jnp.concatenate(bases + [base[0:1, :n_experts]], axis=0).astype(jnp.int32)
    counts = cum[-1]
    sizes = ((counts + MOE_SLOT_TILE - 1) // MOE_SLOT_TILE) * MOE_SLOT_TILE
    run_end = jnp.cumsum(sizes)
    run_start = (run_end - sizes).astype(jnp.int32)

    n_tiles = (TOP_K * n + MOE_SLOT_TILE - 1) // MOE_SLOT_TILE + n_experts
    tile_start = jnp.arange(n_tiles, dtype=jnp.int32) * MOE_SLOT_TILE
    tile_expert = jnp.minimum(_count_le(run_end, tile_start), n_experts - 1)
    tile_valid = (tile_start < run_end[-1]).astype(jnp.int32)
    n_sub = MOE_SLOT_TILE // MOE_SUB
    sub_expert = jnp.repeat(tile_expert, n_sub)
    sub_rank0 = jnp.arange(n_tiles * n_sub, dtype=jnp.int32) * MOE_SUB - run_start[sub_expert]
    cum_sub = cum[:, sub_expert]
    sub_valid = jnp.repeat(tile_valid, n_sub)
    c_lo = jnp.sum(cum_sub[1:] <= sub_rank0[None, :], axis=0).astype(jnp.int32) * sub_valid
    c_hi = jnp.sum(cum_sub[:-1] < sub_rank0[None, :] + MOE_SUB, axis=0).astype(jnp.int32) * sub_valid

    x_sorted, w_sorted = _gather_slots(c_lo, c_hi, run_start, metas_t, hs, n_tiles, n_experts)
    out_sorted = _expert_ffn(tile_expert, tile_valid, x_sorted, w_sorted, lw_)

    outs = []
    chunk0 = 0
    for x, mod, meta, (nb, tt) in zip(xs_in, mods, metas, tilings):
        nc = x.shape[0] * x.shape[1] // MOE_CHUNK
        lo = run_start[None, :] + cum[chunk0:chunk0 + nc]
        hi = run_start[None, :] + cum[chunk0 + 1:chunk0 + nc + 1]
        max_items = nc * n_experts + (TOP_K * nc * MOE_CHUNK) // MOE_COMBINE_BLOCK + 2 * n_experts
        items = _combine_steps(lo, hi, max_items // MOE_COMBINE_FANIN + nc)
        outs.append(_combine(items, run_start, x, mod, meta, out_sorted, lw_, nb=nb, tt=tt, alpha=alpha,
                             n_experts=n_experts))
        chunk0 += nc
    return outs


def _pad_lanes(v, width=LANES):
    return jnp.pad(v, ((0, 0), (0, width - v.shape[-1])))


def _block_diag_halves(w):
    nblk, c, _ = w.shape
    half = nblk // 2
    out = jnp.zeros((2, half * c, half * c), w.dtype)
    for i in range(nblk):
        j = i % half
        out = out.at[i // half, j * c:(j + 1) * c, j * c:(j + 1) * c].set(w[i])
    return out


def _layer_weights(l, p, d, lw):
    w_in = p["w_in"][l]
    a_off = DN_QKV
    z_off = a_off + 2 * DN_HEADS
    x_off = z_off + DN_WIDTH
    y_off = x_off + lw
    w_ab = _pad_lanes(w_in[:, a_off:z_off])
    w_cat = jnp.concatenate([w_in[:, :DN_QKV], w_in[:, z_off:y_off + lw], w_ab], axis=1).astype(BF16)
    out = {
        "w_cat": w_cat,
        "dn_conv_w": p["dn_conv_w"][l],
        "lru_conv_w": p["lru_conv_w"][l],
        "lru_conv_b": p["lru_conv_b"][l][None],
        "a_log": _pad_lanes(p["dn_a_log"][l][None]),
        "dt_bias": _pad_lanes(p["dn_dt_bias"][l][None]),
        "w_r": _block_diag_halves(p["lru_w_r"][l]).astype(BF16),
        "w_i": _block_diag_halves(p["lru_w_i"][l]).astype(BF16),
        "b_r": p["lru_b_r"][l][None],
        "b_i": p["lru_b_i"][l][None],
        "lam": p["lru_lambda"][l][None],
        "w_out": p["w_out"][l].astype(BF16),
        "dn_norm_w": p["dn_norm_w"][l][None],
        "ln1_g": p["ln1_g"][l][None],
        "ln1_b": p["ln1_b"][l][None],
        "ln2_g": p["ln2_g"][l][None],
        "ln2_b": p["ln2_b"][l][None],
    }
    j = l // 2
    if l % 2 == 0:
        out.update(w_gate=p["ffn_w_gate"][j], w_up=p["ffn_w_up"][j], w_down=p["ffn_w_down"][j])
    else:
        out.update(w_router=_pad_lanes(p["moe_w_router"][j]),
                   w_gate=p["moe_w_gate"][j], w_up=p["moe_w_up"][j], w_down=p["moe_w_down"][j])
    return out


CHANNEL_MIX_WEIGHTS = ("w_gate", "w_up", "w_down")


def _bf16_slabs(weights, steps):
    out = []
    for k, w in enumerate(weights):
        for n in CHANNEL_MIX_WEIGHTS:
            a = w[n].reshape(-1, w[n].shape[-1])
            if a.shape[0] % (steps * 2 * SUBLANES) == 0:
                out.append((k, n, a))
    return out


def _tiling(b, t):
    if t >= MXU_DIM:
        tt = MXU_DIM
        return dict(mixer=(1, tt, min(PROMPT_CHUNK, tt)), ffn=(1, min(t, 2 * MXU_DIM)))
    assert t == SUBLANES, "short sequences must be exactly one sublane tile long"
    return dict(mixer=(min(b, 16), t, t), ffn=(min(b, 64), t))


def _token_mix(x, mod, conv_dn, s_dn_all, layer, conv_lru, s_lru, lw_, til, alpha, cast=()):
    pad_hist = lambda c: jnp.pad(c, ((0, 0), (SUBLANES - (CONV_W - 1), 0), (0, 0)))
    nb, tt, chunk = til["mixer"]
    x, tail_dn, tail_lru, s_new, h_new, *converted = _token_mix_call(
        x, mod, pad_hist(conv_dn), pad_hist(conv_lru), s_dn_all, layer, s_lru[:, None, :], lw_, cast,
        nb=nb, tt=tt, chunk=chunk, alpha=alpha)
    return (x, tail_dn[:, SUBLANES - (CONV_W - 1):, :], s_new, tail_lru[:, SUBLANES - (CONV_W - 1):, :],
            h_new[:, 0, :]), converted


def kernel(x_prompt, x_sample, cache_dn_conv, state_dn, cache_lru_conv, state_lru, c_prompt, c_sample,
           w_ada, b_ada, w_in, dn_conv_w, dn_a_log, dn_dt_bias, dn_norm_w,
           lru_conv_w, lru_conv_b, lru_w_r, lru_b_r, lru_w_i, lru_b_i, lru_lambda, w_out,
           ln1_g, ln1_b, ln2_g, ln2_b, ffn_w_gate, ffn_w_up, ffn_w_down,
           moe_w_router, moe_w_gate, moe_w_up, moe_w_down):
    p = dict(w_in=w_in, dn_conv_w=dn_conv_w, dn_a_log=dn_a_log, dn_dt_bias=dn_dt_bias, dn_norm_w=dn_norm_w,
             lru_conv_w=lru_conv_w, lru_conv_b=lru_conv_b, lru_w_r=lru_w_r, lru_b_r=lru_b_r,
             lru_w_i=lru_w_i, lru_b_i=lru_b_i, lru_lambda=lru_lambda, w_out=w_out,
             ln1_g=ln1_g, ln1_b=ln1_b, ln2_g=ln2_g, ln2_b=ln2_b,
             ffn_w_gate=ffn_w_gate, ffn_w_up=ffn_w_up, ffn_w_down=ffn_w_down,
             moe_w_router=moe_w_router, moe_w_gate=moe_w_gate, moe_w_up=moe_w_up, moe_w_down=moe_w_down)
    depth, d, _ = w_ada.shape
    bp = x_prompt.shape[0]
    bs = x_sample.shape[0]
    lw = cache_lru_conv.shape[-1]
    alpha = (2 * depth) ** 0.25
    weights = [_layer_weights(l, p, d, lw) for l in range(depth)]

    c_all = jnp.concatenate([c_prompt, c_sample], axis=0)
    mod_all = _modulation(c_all, w_ada, b_ada)
    groups = [
        dict(x=x_prompt, rows=slice(0, bp), conv_dn=jnp.zeros((depth, bp, CONV_W - 1, DN_QKV), F32),
             s_dn=jnp.zeros((depth, bp, DN_HEADS, DN_DK, DN_DV), F32),
             conv_lru=jnp.zeros((depth, bp, CONV_W - 1, lw), F32), s_lru=jnp.zeros((depth, bp, lw), F32)),
        dict(x=x_sample, rows=slice(bp, bp + bs), conv_dn=cache_dn_conv, s_dn=state_dn,
             conv_lru=cache_lru_conv, s_lru=state_lru),
    ]
    for g in groups:
        g["til"] = _tiling(g["x"].shape[0], g["x"].shape[1])
        g["new"] = [[], [], [], []]
    nb0, tt0, _ = groups[0]["til"]["mixer"]
    slabs = _bf16_slabs(weights, (bp // nb0) * (x_prompt.shape[1] // tt0))
    for l in range(depth):
        lw_ = weights[l]
        mods = [mod_all[l, g["rows"]][:, None, :] for g in groups]
        for gi, (g, mod) in enumerate(zip(groups, mods)):
            first_call = l == 0 and gi == 0
            res, converted = _token_mix(g["x"], mod, g["conv_dn"][l], g["s_dn"], l, g["conv_lru"][l],
                                        g["s_lru"][l], lw_, g["til"], alpha,
                                        cast=tuple(a for _, _, a in slabs) if first_call else ())
            if first_call:
                for (k, n, _), c in zip(slabs, converted):
                    weights[k][n] = c.reshape(weights[k][n].shape)
                for w in weights:
                    for n in CHANNEL_MIX_WEIGHTS:
                        w[n] = w[n].astype(BF16)
            g["x"] = res[0]
            for acc, new in zip(g["new"], res[1:]):
                acc.append(new)
        if l % 2 == 0:
            for g, mod in zip(groups, mods):
                nb, tt = g["til"]["ffn"]
                g["x"] = _dense_ffn(g["x"], mod, lw_, nb=nb, tt=tt, alpha=alpha)
        else:
            xs = _moe_layer([g["x"] for g in groups], mods, lw_, [g["til"]["ffn"] for g in groups], alpha)
            for g, x in zip(groups, xs):
                g["x"] = x
    states = [jnp.stack(acc) for g in groups for acc in g["new"]]
    return (groups[0]["x"], groups[1]["x"]) + tuple(states)
```

```python
import functools
import math

import jax
import jax.numpy as jnp
from jax import lax
from jax.experimental import pallas as pl
from jax.experimental.pallas import tpu as pltpu

F32 = jnp.float32
BF16 = jnp.bfloat16

DN_HEADS = 4
DN_DK = 128
DN_DV = 128
DN_WIDTH = DN_HEADS * DN_DV
DN_QKV = 3 * DN_WIDTH
LRU_BLOCKS = 8
LRU_C = 8.0
CONV_W = 4
N_MOD = 6
TOP_K = 2
LN_EPS = 1e-5
NORM_EPS = 1e-6

SUBLANES = 8
LANES = 128
MXU_DIM = 256
VMEM_LIMIT_BYTES = 56 * 1024 * 1024

PROMPT_CHUNK = 64
INV_BASE_BLOCK = 16


def _sigmoid(x):
    return 0.5 + 0.5 * jnp.tanh(0.5 * x)


def _silu(x):
    half = 0.5 * x
    return half + half * jnp.tanh(half)


def _softplus(x):
    return jnp.maximum(x, 0.0) + jnp.log1p(jnp.exp(-jnp.abs(x)))


def _gelu_tanh(x):
    return 0.5 * x * (1.0 + jnp.tanh(math.sqrt(2.0 / math.pi) * (x + 0.044715 * (x * x * x))))


def _mm(a, b):
    return jnp.dot(a.astype(BF16), b.astype(BF16), preferred_element_type=F32)


def _layer_norm(x, g, b):
    mu = jnp.mean(x, axis=-1, keepdims=True)
    xc = x - mu
    var = jnp.mean(xc * xc, axis=-1, keepdims=True)
    return xc * lax.rsqrt(var + LN_EPS) * g + b


def _params(sem):
    return pltpu.CompilerParams(dimension_semantics=sem, vmem_limit_bytes=VMEM_LIMIT_BYTES)


def _mod_kernel(c_ref, w_ref, b_ref, o_ref):
    sc = _silu(c_ref[...])
    o_ref[0] = _mm(sc, w_ref[0]) + b_ref[0]


def _modulation(c_all, w_ada, b_ada):
    depth, d, n = w_ada.shape
    rows = c_all.shape[0]
    tn = 1536 if n % 1536 == 0 else n
    return pl.pallas_call(
        _mod_kernel,
        out_shape=jax.ShapeDtypeStruct((depth, rows, n), F32),
        grid=(depth, n // tn),
        in_specs=[
            pl.BlockSpec((rows, d), lambda l, j: (0, 0)),
            pl.BlockSpec((1, d, tn), lambda l, j: (l, 0, j)),
            pl.BlockSpec((1, 1, tn), lambda l, j: (l, 0, j)),
        ],
        out_specs=pl.BlockSpec((1, rows, tn), lambda l, j: (l, 0, j)),
        compiler_params=_params(("arbitrary", "arbitrary")),
        name="adaln_modulation",
    )(c_all, w_ada, b_ada.reshape(depth, 1, n))


def _causal_conv(u, win_ref, w, nb, tt, c0):
    c = u.shape[-1]
    cols = slice(c0, c0 + c)
    win_ref[:, SUBLANES:, cols] = u.reshape(nb, tt, c)
    out = u * w[CONV_W - 1:CONV_W, cols]
    for j in range(1, CONV_W):
        prev = win_ref[:, SUBLANES - j:SUBLANES - j + tt, cols].reshape(nb * tt, c)
        out = out + prev * w[CONV_W - 1 - j:CONV_W - j, cols]
    tail = win_ref[:, tt:tt + SUBLANES, cols]
    win_ref[:, 0:SUBLANES, cols] = tail
    return out, tail


def _l2norm_heads(x, scale):
    outs = []
    for h in range(DN_HEADS):
        xh = x[:, h * DN_DK:(h + 1) * DN_DK]
        ss = jnp.sum(xh * xh, axis=-1, keepdims=True)
        outs.append(xh * (lax.rsqrt(ss + NORM_EPS) * scale))
    return jnp.concatenate(outs, axis=-1)


def _inproj_stage(x_ref, mod_ref, hdn_ref, hlru_ref, w_ref, cwdn_ref, cwlru_ref, cblru_ref,
                  alog_ref, dtb_ref, wr_ref, wi_ref, br_ref, bi_ref, lam_ref,
                  tdn_ref, tlru_ref, wdn_sc, wlru_sc, out, *, nb, tt):
    d = x_ref.shape[-1]
    lw = hlru_ref.shape[-1]
    rows = nb * tt

    @pl.when(pl.program_id(1) == 0)
    def _():
        wdn_sc[:, 0:SUBLANES, :] = hdn_ref[...]
        wlru_sc[:, 0:SUBLANES, :] = hlru_ref[...]

    m = mod_ref[...]
    shift = m[:, :, 0:d]
    scale = m[:, :, d:2 * d]
    h = (x_ref[...] * (1.0 + scale) + shift).reshape(rows, d)
    proj = _mm(h, w_ref[...])
    z = proj[:, DN_QKV:DN_QKV + DN_WIDTH]
    u_lru = proj[:, DN_QKV + DN_WIDTH:DN_QKV + DN_WIDTH + lw]
    y = proj[:, DN_QKV + DN_WIDTH + lw:DN_QKV + DN_WIDTH + 2 * lw]
    ab = proj[:, DN_QKV + DN_WIDTH + 2 * lw:]

    cw_dn = cwdn_ref[...]

    def dn_part(part, norm_scale):
        c0 = part * DN_WIDTH
        conv, tail = _causal_conv(proj[:, c0:c0 + DN_WIDTH], wdn_sc, cw_dn, nb, tt, c0)
        tdn_ref[:, :, c0:c0 + DN_WIDTH] = tail
        act = _silu(conv)
        return act if norm_scale is None else _l2norm_heads(act, norm_scale)

    out["q"] = dn_part(0, DN_DK ** -0.5)
    yield
    out["k"] = dn_part(1, 1.0)
    lane = lax.broadcasted_iota(jnp.int32, ab.shape, 1)
    g_full = -jnp.exp(alog_ref[...]) * _softplus(ab + dtb_ref[...])
    out["gb"] = jnp.where(lane < DN_HEADS, g_full, _sigmoid(ab))
    yield
    out["v"] = dn_part(2, None)
    yield
    out["gz"] = _silu(z)
    out["gy"] = _gelu_tanh(y)
    yield

    conv_lru, tail_lru = _causal_conv(u_lru, wlru_sc, cwlru_ref[...], nb, tt, 0)
    tlru_ref[...] = tail_lru
    xc = conv_lru + cblru_ref[...]
    half = lw // 2
    r_pre = jnp.concatenate([_mm(xc[:, :half], wr_ref[0]), _mm(xc[:, half:], wr_ref[1])], axis=-1)
    i_pre = jnp.concatenate([_mm(xc[:, :half], wi_ref[0]), _mm(xc[:, half:], wi_ref[1])], axis=-1)
    r = _sigmoid(r_pre + br_ref[...])
    i = _sigmoid(i_pre + bi_ref[...])
    log_a = -LRU_C * r * _softplus(-lam_ref[...])
    out["a"] = jnp.exp(log_a)
    th = jnp.tanh(log_a)
    out["inp"] = jnp.sqrt(-2.0 * th / (1.0 - th)) * (i * xc)
    yield


def _unit_lower_inverses(ls, row, col, chunk, base):
    def same_block(s):
        k = s.bit_length() - 1
        return (row >> k) == (col >> k)

    eye = jnp.where(row == col, 1.0, 0.0)
    base_mask = same_block(base)
    powers = [jnp.where(base_mask, l, 0.0) for l in ls]
    invs = [eye - d for d in powers]
    p = 2
    while p < base:
        powers = [_mm(d, d) for d in powers]
        invs = [t + _mm(t, d) for t, d in zip(invs, powers)]
        p *= 2
        yield
    s = base
    while s < chunk:
        off_mask = same_block(2 * s) & jnp.logical_not(same_block(s))
        tmp = [_mm(jnp.where(off_mask, l, 0.0), t) for l, t in zip(ls, invs)]
        invs = [t - _mm(t, x) for t, x in zip(invs, tmp)]
        s *= 2
        yield
    return invs


def _mixer_stage(vals, x_ref, mod_ref, wout_ref, nw_ref, lng_ref, lnb_ref,
                 y_ref, sout_ref, hout_ref, *, nb, tt, chunk, alpha):
    d = x_ref.shape[-1]
    lw = hout_ref.shape[-1]
    rows = nb * tt
    n_chunks = rows // chunk
    chunks_per_seq = tt // chunk
    log_chunk = chunk.bit_length() - 1

    row = lax.broadcasted_iota(jnp.int32, (rows, rows), 0)
    col = lax.broadcasted_iota(jnp.int32, (rows, rows), 1)
    incl = ((row >> log_chunk) == (col >> log_chunk)) & (col <= row)
    strict = incl & (col < row)
    gb = vals["gb"]
    pieces = []
    rest = gb
    for _ in range(3):
        piece = rest.astype(BF16)
        pieces.append(piece)
        rest = rest - piece.astype(F32)
    sums = jnp.dot(jnp.where(incl, 1.0, 0.0).astype(BF16), jnp.concatenate(pieces, axis=-1),
                   preferred_element_type=F32)
    gc = sums[:, 0:LANES] + sums[:, LANES:2 * LANES] + sums[:, 2 * LANES:3 * LANES]
    gc_t = gc.T

    heads = range(DN_HEADS)
    qs = [vals["q"][:, h * DN_DK:(h + 1) * DN_DK] for h in heads]
    ks = [vals["k"][:, h * DN_DK:(h + 1) * DN_DK] for h in heads]
    gcols = [gc[:, h:h + 1] for h in heads]
    betas = [gb[:, DN_HEADS + h:DN_HEADS + h + 1] for h in heads]
    decays = [jnp.where(incl, jnp.exp(jnp.where(incl, gcols[h] - gc_t[h:h + 1, :], 0.0)), 0.0) for h in heads]
    kbs = [ks[h] * betas[h] for h in heads]
    qk_kks = [lax.dot_general(jnp.concatenate([qs[h], kbs[h]], axis=0).astype(BF16), ks[h].astype(BF16),
                              (((1,), (1,)), ((), ())), preferred_element_type=F32) for h in heads]
    qks = [qk_kks[h][:rows] * decays[h] for h in heads]
    lmats = [jnp.where(strict, qk_kks[h][rows:] * decays[h], 0.0) for h in heads]
    yield
    tmats = yield from _unit_lower_inverses(lmats, row, col, chunk, min(INV_BASE_BLOCK, chunk))
    egcs = [jnp.exp(g) for g in gcols]
    vs = [vals["v"][:, h * DN_DV:(h + 1) * DN_DV] for h in heads]
    uws = [_mm(tmats[h], jnp.concatenate([vs[h] * betas[h], kbs[h] * egcs[h]], axis=-1)) for h in heads]
    us = [uw[:, :DN_DV] for uw in uws]
    ws = [uw[:, DN_DV:] for uw in uws]
    qes = [qs[h] * egcs[h] for h in heads]
    yield

    a = vals["a"]
    bacc = vals["inp"]
    gy = vals["gy"]
    t = lax.broadcasted_iota(jnp.int32, (rows, lw), 0) & (SUBLANES - 1)
    s = 1
    while s < SUBLANES:
        keep = t >= s
        a_prev = jnp.where(keep, pltpu.roll(a, s, 0), 1.0)
        b_prev = jnp.where(keep, pltpu.roll(bacc, s, 0), 0.0)
        bacc = a * b_prev + bacc
        a = a * a_prev
        s *= 2
    tiles = tt // SUBLANES
    a4 = a.reshape(nb, tiles, SUBLANES, lw)
    b4 = bacc.reshape(nb, tiles, SUBLANES, lw)
    carry = hout_ref[...]
    h_tiles = []
    for k in range(tiles):
        hk = a4[:, k] * carry + b4[:, k]
        carry = hk[:, SUBLANES - 1:SUBLANES, :]
        h_tiles.append(hk)
    hout_ref[...] = carry
    hs = h_tiles[0] if tiles == 1 else jnp.stack(h_tiles, axis=1)
    o_b = hs.reshape(rows, lw) * gy
    yield

    v_new_parts = [[] for _ in heads]
    o_inter_parts = [[] for _ in heads]
    states = [None for _ in heads]
    for c in range(n_chunks):
        seq = c // chunks_per_seq
        lo, hi = c * chunk, (c + 1) * chunk
        if c % chunks_per_seq == 0:
            states = [sout_ref[seq, h] for h in heads]
        wqs = [_mm(jnp.concatenate([ws[h][lo:hi], qes[h][lo:hi]], axis=0), states[h]) for h in heads]
        v_news = [us[h][lo:hi] - wqs[h][:chunk] for h in heads]
        new_states = []
        for h in heads:
            v_new_parts[h].append(v_news[h])
            o_inter_parts[h].append(wqs[h][chunk:])
            g_last = gcols[h][hi - 1:hi, :]
            k_dec = ks[h][lo:hi] * jnp.exp(g_last - gcols[h][lo:hi])
            new_states.append(states[h] * jnp.exp(g_last) + lax.dot_general(
                k_dec.astype(BF16), v_news[h].astype(BF16), (((0,), (0,)), ((), ())),
                preferred_element_type=F32))
        states = new_states
        if (c + 1) % chunks_per_seq == 0:
            for h in heads:
                sout_ref[seq, h] = states[h]
        yield
    o_heads = []
    for h in heads:
        o = jnp.concatenate(o_inter_parts[h], axis=0) + _mm(qks[h], jnp.concatenate(v_new_parts[h], axis=0))
        ms = jnp.mean(o * o, axis=-1, keepdims=True)
        o_heads.append(o * lax.rsqrt(ms + NORM_EPS) * nw_ref[...] * vals["gz"][:, h * DN_DV:(h + 1) * DN_DV])
    o_a = jnp.concatenate(o_heads, axis=-1)

    mixed = _mm(jnp.concatenate([o_a, o_b], axis=-1), wout_ref[...]).reshape(nb, tt, d)
    gate = mod_ref[...][:, :, 2 * d:3 * d]
    y_ref[...] = _layer_norm(alpha * x_ref[...] + (1.0 + gate) * mixed, lng_ref[...], lnb_ref[...])


N_TOKEN_MIX_INPUTS = 21
N_TOKEN_MIX_OUTPUTS = 5


def _token_mix_kernel(*refs, n_cast, nb, tt, chunk, alpha):
    (x_ref, mod_ref, hdn_ref, hlru_ref, s0_ref, h0_ref, w_ref, cwdn_ref, cwlru_ref, cblru_ref, alog_ref, dtb_ref,
     wr_ref, wi_ref, br_ref, bi_ref, lam_ref, wout_ref, nw_ref, lng_ref, lnb_ref) = refs[:N_TOKEN_MIX_INPUTS]
    cast_in = refs[N_TOKEN_MIX_INPUTS:N_TOKEN_MIX_INPUTS + n_cast]
    outs = refs[N_TOKEN_MIX_INPUTS + n_cast:]
    y_ref, tdn_ref, tlru_ref, sout_ref, hout_ref = outs[:N_TOKEN_MIX_OUTPUTS]
    cast_out = outs[N_TOKEN_MIX_OUTPUTS:N_TOKEN_MIX_OUTPUTS + n_cast]
    wdn_sc, wlru_sc = outs[N_TOKEN_MIX_OUTPUTS + n_cast:]

    for src, dst in zip(cast_in, cast_out):
        dst[...] = src[...].astype(BF16)

    @pl.when(pl.program_id(1) == 0)
    def _():
        sout_ref[...] = s0_ref[0]
        hout_ref[...] = h0_ref[...]

    vals = {}
    stage1 = _inproj_stage(
        x_ref, mod_ref, hdn_ref, hlru_ref, w_ref, cwdn_ref, cwlru_ref, cblru_ref, alog_ref, dtb_ref,
        wr_ref, wi_ref, br_ref, bi_ref, lam_ref, tdn_ref, tlru_ref, wdn_sc, wlru_sc, vals, nb=nb, tt=tt)
    stage2 = _mixer_stage(vals, x_ref, mod_ref, wout_ref, nw_ref, lng_ref, lnb_ref,
                          y_ref, sout_ref, hout_ref, nb=nb, tt=tt, chunk=chunk, alpha=alpha)
    while "gb" not in vals:
        next(stage1)
    pending = [stage2, stage1]
    while pending:
        pending = [g for g in pending if next(g, StopIteration) is not StopIteration]


def _token_mix_call(x, mod, hist_dn, hist_lru, s_dn_all, layer, h0, lw_, cast=(), *, nb, tt, chunk, alpha):
    b, t, d = x.shape
    lw = hist_lru.shape[-1]
    grid = (b // nb, t // tt)
    steps = grid[0] * grid[1]
    slab = lambda a: pl.BlockSpec((a.shape[0] // steps, a.shape[1]), lambda i, j: (i * grid[1] + j, 0))
    seq_blk = lambda c: pl.BlockSpec((nb, tt, c), lambda i, j: (i, j, 0))
    per_seq = lambda r, c: pl.BlockSpec((nb, r, c), lambda i, j: (i, 0, 0))
    whole = lambda a: pl.BlockSpec(a.shape, lambda i, j: (0,) * a.ndim)
    state_in = pl.BlockSpec((1, nb, DN_HEADS, DN_DK, DN_DV), lambda i, j: (layer, i, 0, 0, 0))
    state_out = pl.BlockSpec((nb, DN_HEADS, DN_DK, DN_DV), lambda i, j: (i, 0, 0, 0))
    weights = [lw_[n] for n in ("w_cat", "dn_conv_w", "lru_conv_w", "lru_conv_b", "a_log", "dt_bias",
                                "w_r", "w_i", "b_r", "b_i", "lam", "w_out", "dn_norm_w", "ln1_g", "ln1_b")]
    return pl.pallas_call(
        functools.partial(_token_mix_kernel, n_cast=len(cast), nb=nb, tt=tt, chunk=chunk, alpha=alpha),
        out_shape=(
            jax.ShapeDtypeStruct((b, t, d), F32),
            jax.ShapeDtypeStruct((b, SUBLANES, DN_QKV), F32),
            jax.ShapeDtypeStruct((b, SUBLANES, lw), F32),
            jax.ShapeDtypeStruct((b, DN_HEADS, DN_DK, DN_DV), F32),
            jax.ShapeDtypeStruct((b, 1, lw), F32),
        ) + tuple(jax.ShapeDtypeStruct(a.shape, BF16) for a in cast),
        grid=grid,
        in_specs=[seq_blk(d), per_seq(1, N_MOD * d), per_seq(SUBLANES, DN_QKV), per_seq(SUBLANES, lw),
                  state_in, per_seq(1, lw)] + [whole(a) for a in weights] + [slab(a) for a in cast],
        out_specs=(seq_blk(d), per_seq(SUBLANES, DN_QKV), per_seq(SUBLANES, lw), state_out, per_seq(1, lw))
        + tuple(slab(a) for a in cast),
        scratch_shapes=[pltpu.VMEM((nb, SUBLANES + tt, DN_QKV), F32), pltpu.VMEM((nb, SUBLANES + tt, lw), F32)],
        compiler_params=_params(("arbitrary", "arbitrary")),
        name="token_mixer",
    )(x, mod, hist_dn, hist_lru, s_dn_all, h0, *weights, *cast)


def _ffn_kernel(x_ref, mod_ref, wg_ref, wu_ref, wd_ref, lng_ref, lnb_ref, y_ref, *, alpha):
    nb, tt, d = x_ref.shape
    m = mod_ref[...]
    h = (x_ref[...] * (1.0 + m[:, :, 4 * d:5 * d]) + m[:, :, 3 * d:4 * d]).reshape(nb * tt, d).astype(BF16)
    act = _silu(jnp.dot(h, wg_ref[...], preferred_element_type=F32)) * jnp.dot(
        h, wu_ref[...], preferred_element_type=F32)
    ff = _mm(act, wd_ref[...]).reshape(nb, tt, d)
    y_ref[...] = _layer_norm(alpha * x_ref[...] + (1.0 + m[:, :, 5 * d:6 * d]) * ff, lng_ref[...], lnb_ref[...])


def _ff_tile(ff):
    for n in (2, 4, 7, 8, 11, 14, 16, 22, 28):
        if ff % n == 0 and (ff // n) % LANES == 0 and ff // n <= 2048:
            return ff // n
    return ff


def _dense_ffn(x, mod, lw_, *, nb, tt, alpha):
    b, t, d = x.shape
    nt = t // tt
    seq_blk = pl.BlockSpec((nb, tt, d), lambda i: (i // nt, i % nt, 0))
    whole = lambda a: pl.BlockSpec(a.shape, lambda i: (0,) * a.ndim)
    resident = lambda a: pl.BlockSpec(a.shape, lambda i: (0,) * a.ndim, pipeline_mode=pl.Buffered(1))
    return pl.pallas_call(
        functools.partial(_ffn_kernel, alpha=alpha),
        out_shape=jax.ShapeDtypeStruct((b, t, d), F32),
        grid=((b // nb) * nt,),
        in_specs=[seq_blk, pl.BlockSpec((nb, 1, N_MOD * d), lambda i: (i // nt, 0, 0)),
                  resident(lw_["w_gate"]), resident(lw_["w_up"]), resident(lw_["w_down"]),
                  whole(lw_["ln2_g"]), whole(lw_["ln2_b"])],
        out_specs=seq_blk,
        compiler_params=_params(("arbitrary",)),
        name="dense_ffn",
    )(x, mod, lw_["w_gate"], lw_["w_up"], lw_["w_down"], lw_["ln2_g"], lw_["ln2_b"])


MOE_CHUNK = 512
MOE_SLOT_TILE = 512
MOE_SUB = 128
MOE_COMBINE_BLOCK = 256
MOE_COMBINE_FANIN = 4
ROUTE_I1, ROUTE_I2, ROUTE_R1, ROUTE_R2, ROUTE_W1, ROUTE_W2 = range(6)


def _router_kernel(x_ref, mod_ref, wr_ref, base_ref, h_ref, meta_ref, meta_t_ref, blkbase_ref, cnt_ref, run_sc,
                   *, n_experts):
    nb, tt, d = x_ref.shape
    rows = nb * tt

    @pl.when(pl.program_id(0) == 0)
    def _():
        run_sc[...] = base_ref[...]

    m = mod_ref[...]
    h = (x_ref[...] * (1.0 + m[:, :, 4 * d:5 * d]) + m[:, :, 3 * d:4 * d]).reshape(rows, d)
    h_hi = h.astype(BF16)
    h_ref[...] = h_hi
    w = wr_ref[...]
    w_hi = w.astype(BF16)
    h_lo = (h - h_hi.astype(F32)).astype(BF16)
    w_lo = (w - w_hi.astype(F32)).astype(BF16)
    logits = (jnp.dot(h_hi, w_hi, preferred_element_type=F32) + jnp.dot(h_lo, w_hi, preferred_element_type=F32)
              + jnp.dot(h_hi, w_lo, preferred_element_type=F32))
    lane = lax.broadcasted_iota(jnp.int32, logits.shape, 1)
    neg = jnp.float32(-jnp.inf)
    lg = jnp.where(lane < n_experts, logits, neg)
    m1 = jnp.max(lg, axis=-1, keepdims=True)
    i1 = jnp.min(jnp.where(lg == m1, lane, LANES), axis=-1, keepdims=True)
    lg2 = jnp.where(lane == i1, neg, lg)
    m2 = jnp.max(lg2, axis=-1, keepdims=True)
    i2 = jnp.min(jnp.where(lg2 == m2, lane, LANES), axis=-1, keepdims=True)
    e2 = jnp.exp(m2 - m1)
    w1 = 1.0 / (1.0 + e2)
    w2 = e2 / (1.0 + e2)
    sel = jnp.where(lane == i1, 1.0, jnp.where(lane == i2, 1.0, 0.0))
    r = lax.broadcasted_iota(jnp.int32, (rows, rows), 0)
    c = lax.broadcasted_iota(jnp.int32, (rows, rows), 1)
    rank = _mm(jnp.where(c < r, 1.0, 0.0), sel) + run_sc[0:1, :]
    r1 = jnp.sum(jnp.where(lane == i1, rank, 0.0), axis=-1, keepdims=True)
    r2 = jnp.sum(jnp.where(lane == i2, rank, 0.0), axis=-1, keepdims=True)
    fields = (i1.astype(F32), i2.astype(F32), r1, r2, w1, w2)
    meta = jnp.zeros_like(logits)
    for k, v in enumerate(fields):
        meta = jnp.where(lane == k, v, meta)
    meta_ref[...] = meta
    meta_t_ref[...] = meta.T[0:SUBLANES, :]
    blkbase_ref[0] = run_sc[...]
    run_sc[...] = run_sc[...] + jnp.sum(sel, axis=0, keepdims=True)
    cnt_ref[...] = run_sc[...]


def _router(x, mod, w_router, base, *, nb, tt, n_experts):
    b, t, d = x.shape
    rows = nb * tt
    nt = t // tt
    nblk = (b // nb) * nt
    whole = lambda a: pl.BlockSpec(a.shape, lambda i: (0,) * a.ndim)
    return pl.pallas_call(
        functools.partial(_router_kernel, n_experts=n_experts),
        out_shape=(jax.ShapeDtypeStruct((nblk * rows, d), BF16),
                   jax.ShapeDtypeStruct((nblk * rows, LANES), F32),
                   jax.ShapeDtypeStruct((SUBLANES, nblk * rows), F32),
                   jax.ShapeDtypeStruct((nblk, SUBLANES, LANES), F32),
                   jax.ShapeDtypeStruct((SUBLANES, LANES), F32)),
        grid=(nblk,),
        in_specs=[pl.BlockSpec((nb, tt, d), lambda i: (i // nt, i % nt, 0)),
                  pl.BlockSpec((nb, 1, N_MOD * d), lambda i: (i // nt, 0, 0)),
                  whole(w_router), whole(base)],
        out_specs=(pl.BlockSpec((rows, d), lambda i: (i, 0)),
                   pl.BlockSpec((rows, LANES), lambda i: (i, 0)),
                   pl.BlockSpec((SUBLANES, rows), lambda i: (0, i)),
                   pl.BlockSpec((1, SUBLANES, LANES), lambda i: (i, 0, 0)),
                   pl.BlockSpec((SUBLANES, LANES), lambda i: (0, 0))),
        scratch_shapes=[pltpu.VMEM((SUBLANES, LANES), F32)],
        compiler_params=_params(("arbitrary",)),
        name="moe_router",
    )(x, mod, w_router, base)


def _slot_of(expert, rank, start_ref, n_experts):
    start = jnp.zeros_like(rank)
    for e in range(n_experts):
        start = jnp.where(expert == e, start_ref[e].astype(F32), start)
    return (start + rank).astype(jnp.int32)


def _gather_kernel(clo_ref, chi_ref, start_ref, *refs, group_chunks, n_experts):
    n_groups = len(group_chunks)
    m_refs = refs[:n_groups]
    h_refs = refs[n_groups:2 * n_groups]
    xs_ref, ws_ref, acc_sc, wacc_sc = refs[2 * n_groups:]
    g = pl.program_id(0)
    n_sub = MOE_SLOT_TILE // MOE_SUB
    for j in range(n_sub):
        q = g * n_sub + j
        ids = g * MOE_SLOT_TILE + j * MOE_SUB + lax.broadcasted_iota(jnp.int32, (MOE_SUB, MOE_CHUNK), 0)
        acc_sc[...] = jnp.zeros_like(acc_sc)
        wacc_sc[...] = jnp.zeros_like(wacc_sc)
        first = 0
        for m_ref, h_ref, n_chunks in zip(m_refs, h_refs, group_chunks):
            def body(c, carry, m_ref=m_ref, h_ref=h_ref):
                off = pl.multiple_of(c * MOE_CHUNK, MOE_CHUNK)
                rec = m_ref[:, pl.ds(off, MOE_CHUNK)]
                row = lambda k: rec[k:k + 1, :]
                hit1 = _slot_of(row(ROUTE_I1), row(ROUTE_R1), start_ref, n_experts) == ids
                hit2 = _slot_of(row(ROUTE_I2), row(ROUTE_R2), start_ref, n_experts) == ids
                p = jnp.where(hit1, 1.0, jnp.where(hit2, 1.0, 0.0)).astype(BF16)
                acc_sc[...] += jnp.dot(p, h_ref[pl.ds(off, MOE_CHUNK), :], preferred_element_type=F32)
                w = jnp.where(hit1, row(ROUTE_W1), 0.0) + jnp.where(hit2, row(ROUTE_W2), 0.0)
                wacc_sc[...] += jnp.broadcast_to(jnp.sum(w, axis=-1, keepdims=True), wacc_sc.shape)
                return carry

            lo = jnp.clip(clo_ref[q] - first, 0, n_chunks)
            hi = jnp.clip(chi_ref[q] - first, 0, n_chunks)
            lax.fori_loop(lo, hi, body, 0)
            first += n_chunks
        xs_ref[j * MOE_SUB:(j + 1) * MOE_SUB, :] = acc_sc[...].astype(BF16)
        ws_ref[j * MOE_SUB:(j + 1) * MOE_SUB, :] = wacc_sc[...]


def _gather_slots(c_lo, c_hi, run_start, metas_t, hs, n_tiles, n_experts):
    d = hs[0].shape[-1]
    vmem = pl.BlockSpec(memory_space=pltpu.VMEM)
    return pl.pallas_call(
        functools.partial(_gather_kernel, group_chunks=tuple(h.shape[0] // MOE_CHUNK for h in hs),
                          n_experts=n_experts),
        out_shape=(jax.ShapeDtypeStruct((n_tiles * MOE_SLOT_TILE, d), BF16),
                   jax.ShapeDtypeStruct((n_tiles * MOE_SLOT_TILE, LANES), F32)),
        grid_spec=pltpu.PrefetchScalarGridSpec(
            num_scalar_prefetch=3, grid=(n_tiles,),
            in_specs=[vmem] * (2 * len(hs)),
            out_specs=(pl.BlockSpec((MOE_SLOT_TILE, d), lambda g, lo, hi, st: (g, 0)),
                       pl.BlockSpec((MOE_SLOT_TILE, LANES), lambda g, lo, hi, st: (g, 0))),
            scratch_shapes=[pltpu.VMEM((MOE_SUB, d), F32), pltpu.VMEM((MOE_SUB, LANES), F32)]),
        compiler_params=_params(("arbitrary",)),
        name="moe_gather",
    )(c_lo, c_hi, run_start, *metas_t, *hs)


def _expert_kernel(te_ref, tv_ref, xs_ref, ws_ref, wg_ref, wu_ref, wd_ref, o_ref, *, ff_chunk):
    g = pl.program_id(0)

    @pl.when(tv_ref[g] != 0)
    def _():
        x = xs_ref[...]
        ff = wg_ref.shape[-1]
        acc = jnp.zeros(o_ref.shape, F32)
        for f0 in range(0, ff, ff_chunk):
            gate = jnp.dot(x, wg_ref[0, :, f0:f0 + ff_chunk], preferred_element_type=F32)
            up = jnp.dot(x, wu_ref[0, :, f0:f0 + ff_chunk], preferred_element_type=F32)
            acc = acc + _mm(_silu(gate) * up, wd_ref[0, f0:f0 + ff_chunk, :])
        o_ref[...] = (ws_ref[:, 0:1] * acc).astype(BF16)

    @pl.when(tv_ref[g] == 0)
    def _():
        o_ref[...] = jnp.zeros_like(o_ref)


def _expert_ffn(tile_expert, tile_valid, xs, ws, lw_):
    s_total, d = xs.shape
    _, _, ff = lw_["w_gate"].shape
    n_tiles = s_total // MOE_SLOT_TILE
    once = pl.Buffered(1)
    twice = pl.Buffered(2)
    return pl.pallas_call(
        functools.partial(_expert_kernel, ff_chunk=_ff_tile(ff)),
        out_shape=jax.ShapeDtypeStruct((s_total, d), BF16),
        grid_spec=pltpu.PrefetchScalarGridSpec(
            num_scalar_prefetch=2, grid=(n_tiles,),
            in_specs=[pl.BlockSpec((MOE_SLOT_TILE, d), lambda g, te, tv: (g, 0)),
                      pl.BlockSpec((MOE_SLOT_TILE, LANES), lambda g, te, tv: (g, 0)),
                      pl.BlockSpec((1, d, ff), lambda g, te, tv: (te[g], 0, 0), pipeline_mode=twice),
                      pl.BlockSpec((1, d, ff), lambda g, te, tv: (te[g], 0, 0), pipeline_mode=twice),
                      pl.BlockSpec((1, ff, d), lambda g, te, tv: (te[g], 0, 0), pipeline_mode=once)],
            out_specs=pl.BlockSpec((MOE_SLOT_TILE, d), lambda g, te, tv: (g, 0))),
        compiler_params=_params(("arbitrary",)),
        name="expert_ffn",
    )(tile_expert, tile_valid, xs, ws, lw_["w_gate"], lw_["w_up"], lw_["w_down"])


def _combine_kernel(ic_ref, ik_ref, if_ref, bf_ref, start_ref, x_ref, mod_ref, meta_ref, *refs, alpha,
                    n_experts):
    os_refs = refs[:MOE_COMBINE_FANIN]
    lng_ref, lnb_ref, y_ref, acc_sc, slot_sc = refs[MOE_COMBINE_FANIN:]
    nb, tt, d = x_ref.shape
    rows = nb * tt
    w = pl.program_id(0)
    flags = if_ref[w]

    @pl.when((flags & 1) != 0)
    def _():
        rec = meta_ref[...]
        col = lambda k: rec[:, k:k + 1]
        slot_sc[:, 0:1] = _slot_of(col(ROUTE_I1), col(ROUTE_R1), start_ref, n_experts)
        slot_sc[:, 1:2] = _slot_of(col(ROUTE_I2), col(ROUTE_R2), start_ref, n_experts)

    lane = lax.broadcasted_iota(jnp.int32, (rows, MOE_COMBINE_BLOCK), 1)
    s1 = slot_sc[:, 0:1]
    s2 = slot_sc[:, 1:2]
    total = None
    for j, os_ref in enumerate(os_refs):
        ids = ik_ref[w * MOE_COMBINE_FANIN + j] * MOE_COMBINE_BLOCK + lane
        q = jnp.where(s1 == ids, 1.0, jnp.where(s2 == ids, 1.0, 0.0)).astype(BF16)
        part = jnp.dot(q, os_ref[...], preferred_element_type=F32)
        total = part if total is None else total + part

    @pl.when((flags & 1) != 0)
    def _():
        acc_sc[...] = total

    @pl.when((flags & 1) == 0)
    def _():
        acc_sc[...] += total

    @pl.when((flags & 2) != 0)
    def _():
        gate = mod_ref[...][:, :, 5 * d:6 * d]
        ff = acc_sc[...].reshape(nb, tt, d)
        y_ref[...] = _layer_norm(alpha * x_ref[...] + (1.0 + gate) * ff, lng_ref[...], lnb_ref[...])


def _combine(items, run_start, x, mod, meta, out_sorted, lw_, *, nb, tt, alpha, n_experts):
    step_chunk, step_blocks, step_flags, block_fetch = items
    b, t, d = x.shape
    rows = nb * tt
    nt = t // tt
    whole = lambda a: pl.BlockSpec(a.shape, lambda w, ic, ik, fl, bf, st: (0,) * a.ndim)
    seq_blk = pl.BlockSpec((nb, tt, d), lambda w, ic, ik, fl, bf, st: (ic[w] // nt, ic[w] % nt, 0))
    slot_blk = lambda j: pl.BlockSpec((MOE_COMBINE_BLOCK, d),
                                      lambda w, ic, ik, fl, bf, st: (bf[w * MOE_COMBINE_FANIN + j], 0))
    return pl.pallas_call(
        functools.partial(_combine_kernel, alpha=alpha, n_experts=n_experts),
        out_shape=jax.ShapeDtypeStruct((b, t, d), F32),
        grid_spec=pltpu.PrefetchScalarGridSpec(
            num_scalar_prefetch=5, grid=(step_chunk.shape[0],),
            in_specs=[seq_blk,
                      pl.BlockSpec((nb, 1, N_MOD * d), lambda w, ic, ik, fl, bf, st: (ic[w] // nt, 0, 0)),
                      pl.BlockSpec((rows, LANES), lambda w, ic, ik, fl, bf, st: (ic[w], 0))]
            + [slot_blk(j) for j in range(MOE_COMBINE_FANIN)]
            + [whole(lw_["ln2_g"]), whole(lw_["ln2_b"])],
            out_specs=seq_blk,
            scratch_shapes=[pltpu.VMEM((rows, d), F32), pltpu.VMEM((rows, LANES), jnp.int32)]),
        compiler_params=_params(("arbitrary",)),
        name="moe_combine",
    )(step_chunk, step_blocks, step_flags, block_fetch, run_start, x, mod, meta,
      *([out_sorted] * MOE_COMBINE_FANIN), lw_["ln2_g"], lw_["ln2_b"])


def _count_le(sorted_vals, queries):
    return jnp.sum(sorted_vals[None, :] <= queries[:, None], axis=1).astype(jnp.int32)


def _combine_steps(lo, hi, n_steps):
    n_chunks, n_experts = lo.shape
    fan = MOE_COMBINE_FANIN
    first = (lo // MOE_COMBINE_BLOCK).reshape(-1)
    count = jnp.where(hi > lo, (hi - 1) // MOE_COMBINE_BLOCK - lo // MOE_COMBINE_BLOCK + 1, 0).reshape(-1)
    pair_end = jnp.cumsum(count)
    chunk_items = jnp.sum(count.reshape(n_chunks, n_experts), axis=1)
    chunk_item0 = jnp.cumsum(chunk_items) - chunk_items
    chunk_steps = (chunk_items + fan - 1) // fan
    step_end = jnp.cumsum(chunk_steps)
    total_steps = step_end[-1]
    s = jnp.arange(n_steps, dtype=jnp.int32)
    live = s < total_steps
    chunk = jnp.minimum(_count_le(step_end, s), n_chunks - 1)
    chunk = jnp.where(live, chunk, chunk[jnp.maximum(total_steps - 1, 0)])
    q = s - (step_end[chunk] - chunk_steps[chunk])
    j = q[:, None] * fan + jnp.arange(fan, dtype=jnp.int32)[None, :]
    used = live[:, None] & (j < chunk_items[chunk][:, None])
    item = jnp.where(used, chunk_item0[chunk][:, None] + j, 0).reshape(-1)
    pair = jnp.minimum(_count_le(pair_end, item), n_chunks * n_experts - 1)
    block = first[pair] + item - (pair_end[pair] - count[pair])
    block = jnp.where(used.reshape(-1), block, -1)
    fetch = jnp.where(block >= 0, block, jnp.repeat(jnp.maximum(block.reshape(-1, fan)[:, 0], 0), fan))
    flags = (jnp.where(live & (q == 0), 1, 0) + jnp.where(live & (q == chunk_steps[chunk] - 1), 2, 0))
    return chunk.astype(jnp.int32), block.astype(jnp.int32), flags.astype(jnp.int32), fetch.astype(jnp.int32)


def _moe_layer(xs_in, mods, lw_, tilings, alpha):
    n_experts = lw_["w_gate"].shape[0]
    base = jnp.zeros((SUBLANES, LANES), F32)
    hs, metas, metas_t, bases = [], [], [], []
    for x, mod, (nb, tt) in zip(xs_in, mods, tilings):
        assert nb * tt == MOE_CHUNK and (x.shape[0] * x.shape[1]) % MOE_CHUNK == 0
        h, meta, meta_t, blkbase, base = _router(x, mod, lw_["w_router"], base, nb=nb, tt=tt,
                                                 n_experts=n_experts)
        hs.append(h)
        metas.append(meta)
        metas_t.append(meta_t)
        bases.append(blkbase[:, 0, :n_experts])
    n = sum(m.shape[0] for m in metas)
    cum = jnp.concatenate(bases + [base[0:1, :n_experts]], axis=0).astype(jnp.int32)
    counts = cum[-1]
    sizes = ((counts + MOE_SLOT_TILE - 1) // MOE_SLOT_TILE) * MOE_SLOT_TILE
    run_end = jnp.cumsum(sizes)
    run_start = (run_end - sizes).astype(jnp.int32)

    n_tiles = (TOP_K * n + MOE_SLOT_TILE - 1) // MOE_SLOT_TILE + n_experts
    tile_start = jnp.arange(n_tiles, dtype=jnp.int32) * MOE_SLOT_TILE
    tile_expert = jnp.minimum(_count_le(run_end, tile_start), n_experts - 1)
    tile_valid = (tile_start < run_end[-1]).astype(jnp.int32)
    n_sub = MOE_SLOT_TILE // MOE_SUB
    sub_expert = jnp.repeat(tile_expert, n_sub)
    sub_rank0 = jnp.arange(n_tiles * n_sub, dtype=jnp.int32) * MOE_SUB - run_start[sub_expert]
    cum_sub = cum[:, sub_expert]
    sub_valid = jnp.repeat(tile_valid, n_sub)
    c_lo = jnp.sum(cum_sub[1:] <= sub_rank0[None, :], axis=0).astype(jnp.int32) * sub_valid
    c_hi = jnp.sum(cum_sub[:-1] < sub_rank0[None, :] + MOE_SUB, axis=0).astype(jnp.int32) * sub_valid

    x_sorted, w_sorted = _gather_slots(c_lo, c_hi, run_start, metas_t, hs, n_tiles, n_experts)
    out_sorted = _expert_ffn(tile_expert, tile_valid, x_sorted, w_sorted, lw_)

    outs = []
    chunk0 = 0
    for x, mod, meta, (nb, tt) in zip(xs_in, mods, metas, tilings):
        nc = x.shape[0] * x.shape[1] // MOE_CHUNK
        lo = run_start[None, :] + cum[chunk0:chunk0 + nc]
        hi = run_start[None, :] + cum[chunk0 + 1:chunk0 + nc + 1]
        max_items = nc * n_experts + (TOP_K * nc * MOE_CHUNK) // MOE_COMBINE_BLOCK + 2 * n_experts
        items = _combine_steps(lo, hi, max_items // MOE_COMBINE_FANIN + nc)
        outs.append(_combine(items, run_start, x, mod, meta, out_sorted, lw_, nb=nb, tt=tt, alpha=alpha,
                             n_experts=n_experts))
        chunk0 += nc
    return outs


def _pad_lanes(v, width=LANES):
    return jnp.pad(v, ((0, 0), (0, width - v.shape[-1])))


def _block_diag_halves(w):
    nblk, c, _ = w.shape
    half = nblk // 2
    out = jnp.zeros((2, half * c, half * c), w.dtype)
    for i in range(nblk):
        j = i % half
        out = out.at[i // half, j * c:(j + 1) * c, j * c:(j + 1) * c].set(w[i])
    return out


def _layer_weights(l, p, d, lw):
    w_in = p["w_in"][l]
    a_off = DN_QKV
    z_off = a_off + 2 * DN_HEADS
    x_off = z_off + DN_WIDTH
    y_off = x_off + lw
    w_ab = _pad_lanes(w_in[:, a_off:z_off])
    w_cat = jnp.concatenate([w_in[:, :DN_QKV], w_in[:, z_off:y_off + lw], w_ab], axis=1).astype(BF16)
    out = {
        "w_cat": w_cat,
        "dn_conv_w": p["dn_conv_w"][l],
        "lru_conv_w": p["lru_conv_w"][l],
        "lru_conv_b": p["lru_conv_b"][l][None],
        "a_log": _pad_lanes(p["dn_a_log"][l][None]),
        "dt_bias": _pad_lanes(p["dn_dt_bias"][l][None]),
        "w_r": _block_diag_halves(p["lru_w_r"][l]).astype(BF16),
        "w_i": _block_diag_halves(p["lru_w_i"][l]).astype(BF16),
        "b_r": p["lru_b_r"][l][None],
        "b_i": p["lru_b_i"][l][None],
        "lam": p["lru_lambda"][l][None],
        "w_out": p["w_out"][l].astype(BF16),
        "dn_norm_w": p["dn_norm_w"][l][None],
        "ln1_g": p["ln1_g"][l][None],
        "ln1_b": p["ln1_b"][l][None],
        "ln2_g": p["ln2_g"][l][None],
        "ln2_b": p["ln2_b"][l][None],
    }
    j = l // 2
    if l % 2 == 0:
        out.update(w_gate=p["ffn_w_gate"][j], w_up=p["ffn_w_up"][j], w_down=p["ffn_w_down"][j])
    else:
        out.update(w_router=_pad_lanes(p["moe_w_router"][j]),
                   w_gate=p["moe_w_gate"][j], w_up=p["moe_w_up"][j], w_down=p["moe_w_down"][j])
    return out


CHANNEL_MIX_WEIGHTS = ("w_gate", "w_up", "w_down")


def _bf16_slabs(weights, steps):
    out = []
    for k, w in enumerate(weights):
        for n in CHANNEL_MIX_WEIGHTS:
            a = w[n].reshape(-1, w[n].shape[-1])
            if a.shape[0] % (steps * 2 * SUBLANES) == 0:
                out.append((k, n, a))
    return out


def _tiling(b, t):
    if t >= MXU_DIM:
        tt = MXU_DIM
        return dict(mixer=(1, tt, min(PROMPT_CHUNK, tt)), ffn=(1, min(t, 2 * MXU_DIM)))
    assert t == SUBLANES, "short sequences must be exactly one sublane tile long"
    return dict(mixer=(min(b, 16), t, t), ffn=(min(b, 64), t))


def _token_mix(x, mod, conv_dn, s_dn_all, layer, conv_lru, s_lru, lw_, til, alpha, cast=()):
    pad_hist = lambda c: jnp.pad(c, ((0, 0), (SUBLANES - (CONV_W - 1), 0), (0, 0)))
    nb, tt, chunk = til["mixer"]
    x, tail_dn, tail_lru, s_new, h_new, *converted = _token_mix_call(
        x, mod, pad_hist(conv_dn), pad_hist(conv_lru), s_dn_all, layer, s_lru[:, None, :], lw_, cast,
        nb=nb, tt=tt, chunk=chunk, alpha=alpha)
    return (x, tail_dn[:, SUBLANES - (CONV_W - 1):, :], s_new, tail_lru[:, SUBLANES - (CONV_W - 1):, :],
            h_new[:, 0, :]), converted


def kernel(x_prompt, x_sample, cache_dn_conv, state_dn, cache_lru_conv, state_lru, c_prompt, c_sample,
           w_ada, b_ada, w_in, dn_conv_w, dn_a_log, dn_dt_bias, dn_norm_w,
           lru_conv_w, lru_conv_b, lru_w_r, lru_b_r, lru_w_i, lru_b_i, lru_lambda, w_out,
           ln1_g, ln1_b, ln2_g, ln2_b, ffn_w_gate, ffn_w_up, ffn_w_down,
           moe_w_router, moe_w_gate, moe_w_up, moe_w_down):
    p = dict(w_in=w_in, dn_conv_w=dn_conv_w, dn_a_log=dn_a_log, dn_dt_bias=dn_dt_bias, dn_norm_w=dn_norm_w,
             lru_conv_w=lru_conv_w, lru_conv_b=lru_conv_b, lru_w_r=lru_w_r, lru_b_r=lru_b_r,
             lru_w_i=lru_w_i, lru_b_i=lru_b_i, lru_lambda=lru_lambda, w_out=w_out,
             ln1_g=ln1_g, ln1_b=ln1_b, ln2_g=ln2_g, ln2_b=ln2_b,
             ffn_w_gate=ffn_w_gate, ffn_w_up=ffn_w_up, ffn_w_down=ffn_w_down,
             moe_w_router=moe_w_router, moe_w_gate=moe_w_gate, moe_w_up=moe_w_up, moe_w_down=moe_w_down)
    depth, d, _ = w_ada.shape
    bp = x_prompt.shape[0]
    bs = x_sample.shape[0]
    lw = cache_lru_conv.shape[-1]
    alpha = (2 * depth) ** 0.25
    weights = [_layer_weights(l, p, d, lw) for l in range(depth)]

    c_all = jnp.concatenate([c_prompt, c_sample], axis=0)
    mod_all = _modulation(c_all, w_ada, b_ada)
    groups = [
        dict(x=x_prompt, rows=slice(0, bp), conv_dn=jnp.zeros((depth, bp, CONV_W - 1, DN_QKV), F32),
             s_dn=jnp.zeros((depth, bp, DN_HEADS, DN_DK, DN_DV), F32),
             conv_lru=jnp.zeros((depth, bp, CONV_W - 1, lw), F32), s_lru=jnp.zeros((depth, bp, lw), F32)),
        dict(x=x_sample, rows=slice(bp, bp + bs), conv_dn=cache_dn_conv, s_dn=state_dn,
             conv_lru=cache_lru_conv, s_lru=state_lru),
    ]
    for g in groups:
        g["til"] = _tiling(g["x"].shape[0], g["x"].shape[1])
        g["new"] = [[], [], [], []]
    nb0, tt0, _ = groups[0]["til"]["mixer"]
    slabs = _bf16_slabs(weights, (bp // nb0) * (x_prompt.shape[1] // tt0))
    for l in range(depth):
        lw_ = weights[l]
        mods = [mod_all[l, g["rows"]][:, None, :] for g in groups]
        for gi, (g, mod) in enumerate(zip(groups, mods)):
            first_call = l == 0 and gi == 0
            res, converted = _token_mix(g["x"], mod, g["conv_dn"][l], g["s_dn"], l, g["conv_lru"][l],
                                        g["s_lru"][l], lw_, g["til"], alpha,
                                        cast=tuple(a for _, _, a in slabs) if first_call else ())
            if first_call:
                for (k, n, _), c in zip(slabs, converted):
                    weights[k][n] = c.reshape(weights[k][n].shape)
                for w in weights:
                    for n in CHANNEL_MIX_WEIGHTS:
                        w[n] = w[n].astype(BF16)
            g["x"] = res[0]
            for acc, new in zip(g["new"], res[1:]):
                acc.append(new)
        if l % 2 == 0:
            for g, mod in zip(groups, mods):
                nb, tt = g["til"]["ffn"]
                g["x"] = _dense_ffn(g["x"], mod, lw_, nb=nb, tt=tt, alpha=alpha)
        else:
            xs = _moe_layer([g["x"] for g in groups], mods, lw_, [g["til"]["ffn"] for g in groups], alpha)
            for g, x in zip(groups, xs):
                g["x"] = x
    states = [jnp.stack(acc) for g in groups for acc in g["new"]]
    return (groups[0]["x"], groups[1]["x"]) + tuple(states)
```

```python
import functools
import math

import jax
import jax.numpy as jnp
from jax import lax
from jax.experimental import pallas as pl
from jax.experimental.pallas import tpu as pltpu

F32 = jnp.float32
BF16 = jnp.bfloat16

DN_HEADS = 4
DN_DK = 128
DN_DV = 128
DN_WIDTH = DN_HEADS * DN_DV
DN_QKV = 3 * DN_WIDTH
LRU_BLOCKS = 8
LRU_C = 8.0
CONV_W = 4
N_MOD = 6
TOP_K = 2
LN_EPS = 1e-5
NORM_EPS = 1e-6

SUBLANES = 8
LANES = 128
MXU_DIM = 256
VMEM_LIMIT_BYTES = 56 * 1024 * 1024

PROMPT_CHUNK = 64
INV_BASE_BLOCK = 16


def _sigmoid(x):
    return 0.5 + 0.5 * jnp.tanh(0.5 * x)


def _silu(x):
    half = 0.5 * x
    return half + half * jnp.tanh(half)


def _softplus(x):
    return jnp.maximum(x, 0.0) + jnp.log1p(jnp.exp(-jnp.abs(x)))


def _gelu_tanh(x):
    return 0.5 * x * (1.0 + jnp.tanh(math.sqrt(2.0 / math.pi) * (x + 0.044715 * (x * x * x))))


def _mm(a, b):
    return jnp.dot(a.astype(BF16), b.astype(BF16), preferred_element_type=F32)


def _layer_norm(x, g, b):
    mu = jnp.mean(x, axis=-1, keepdims=True)
    xc = x - mu
    var = jnp.mean(xc * xc, axis=-1, keepdims=True)
    return xc * lax.rsqrt(var + LN_EPS) * g + b


def _params(sem):
    return pltpu.CompilerParams(dimension_semantics=sem, vmem_limit_bytes=VMEM_LIMIT_BYTES)


def _mod_kernel(c_ref, w_ref, b_ref, o_ref):
    sc = _silu(c_ref[...])
    o_ref[0] = _mm(sc, w_ref[0]) + b_ref[0]


def _modulation(c_all, w_ada, b_ada):
    depth, d, n = w_ada.shape
    rows = c_all.shape[0]
    tn = 1536 if n % 1536 == 0 else n
    return pl.pallas_call(
        _mod_kernel,
        out_shape=jax.ShapeDtypeStruct((depth, rows, n), F32),
        grid=(depth, n // tn),
        in_specs=[
            pl.BlockSpec((rows, d), lambda l, j: (0, 0)),
            pl.BlockSpec((1, d, tn), lambda l, j: (l, 0, j)),
            pl.BlockSpec((1, 1, tn), lambda l, j: (l, 0, j)),
        ],
        out_specs=pl.BlockSpec((1, rows, tn), lambda l, j: (l, 0, j)),
        compiler_params=_params(("arbitrary", "arbitrary")),
        name="adaln_modulation",
    )(c_all, w_ada, b_ada.reshape(depth, 1, n))


def _causal_conv(u, win_ref, w, nb, tt, c0):
    c = u.shape[-1]
    cols = slice(c0, c0 + c)
    win_ref[:, SUBLANES:, cols] = u.reshape(nb, tt, c)
    out = u * w[CONV_W - 1:CONV_W, cols]
    for j in range(1, CONV_W):
        prev = win_ref[:, SUBLANES - j:SUBLANES - j + tt, cols].reshape(nb * tt, c)
        out = out + prev * w[CONV_W - 1 - j:CONV_W - j, cols]
    tail = win_ref[:, tt:tt + SUBLANES, cols]
    win_ref[:, 0:SUBLANES, cols] = tail
    return out, tail


def _l2norm_heads(x, scale):
    outs = []
    for h in range(DN_HEADS):
        xh = x[:, h * DN_DK:(h + 1) * DN_DK]
        ss = jnp.sum(xh * xh, axis=-1, keepdims=True)
        outs.append(xh * (lax.rsqrt(ss + NORM_EPS) * scale))
    return jnp.concatenate(outs, axis=-1)


def _inproj_stage(x_ref, mod_ref, hdn_ref, hlru_ref, w_ref, cwdn_ref, cwlru_ref, cblru_ref,
                  alog_ref, dtb_ref, wr_ref, wi_ref, br_ref, bi_ref, lam_ref,
                  tdn_ref, tlru_ref, wdn_sc, wlru_sc, out, *, nb, tt):
    d = x_ref.shape[-1]
    lw = hlru_ref.shape[-1]
    rows = nb * tt

    @pl.when(pl.program_id(1) == 0)
    def _():
        wdn_sc[:, 0:SUBLANES, :] = hdn_ref[...]
        wlru_sc[:, 0:SUBLANES, :] = hlru_ref[...]

    m = mod_ref[...]
    shift = m[:, :, 0:d]
    scale = m[:, :, d:2 * d]
    h = (x_ref[...] * (1.0 + scale) + shift).reshape(rows, d)
    proj = _mm(h, w_ref[...])
    z = proj[:, DN_QKV:DN_QKV + DN_WIDTH]
    u_lru = proj[:, DN_QKV + DN_WIDTH:DN_QKV + DN_WIDTH + lw]
    y = proj[:, DN_QKV + DN_WIDTH + lw:DN_QKV + DN_WIDTH + 2 * lw]
    ab = proj[:, DN_QKV + DN_WIDTH + 2 * lw:]

    cw_dn = cwdn_ref[...]

    def dn_part(part, norm_scale):
        c0 = part * DN_WIDTH
        conv, tail = _causal_conv(proj[:, c0:c0 + DN_WIDTH], wdn_sc, cw_dn, nb, tt, c0)
        tdn_ref[:, :, c0:c0 + DN_WIDTH] = tail
        act = _silu(conv)
        return act if norm_scale is None else _l2norm_heads(act, norm_scale)

    out["q"] = dn_part(0, DN_DK ** -0.5)
    yield
    out["k"] = dn_part(1, 1.0)
    lane = lax.broadcasted_iota(jnp.int32, ab.shape, 1)
    g_full = -jnp.exp(alog_ref[...]) * _softplus(ab + dtb_ref[...])
    out["gb"] = jnp.where(lane < DN_HEADS, g_full, _sigmoid(ab))
    yield
    out["v"] = dn_part(2, None)
    yield
    out["gz"] = _silu(z)
    out["gy"] = _gelu_tanh(y)
    yield

    conv_lru, tail_lru = _causal_conv(u_lru, wlru_sc, cwlru_ref[...], nb, tt, 0)
    tlru_ref[...] = tail_lru
    xc = conv_lru + cblru_ref[...]
    half = lw // 2
    r_pre = jnp.concatenate([_mm(xc[:, :half], wr_ref[0]), _mm(xc[:, half:], wr_ref[1])], axis=-1)
    i_pre = jnp.concatenate([_mm(xc[:, :half], wi_ref[0]), _mm(xc[:, half:], wi_ref[1])], axis=-1)
    r = _sigmoid(r_pre + br_ref[...])
    i = _sigmoid(i_pre + bi_ref[...])
    log_a = -LRU_C * r * _softplus(-lam_ref[...])
    out["a"] = jnp.exp(log_a)
    th = jnp.tanh(log_a)
    out["inp"] = jnp.sqrt(-2.0 * th / (1.0 - th)) * (i * xc)
    yield


def _unit_lower_inverses(ls, row, col, chunk, base):
    def same_block(s):
        k = s.bit_length() - 1
        return (row >> k) == (col >> k)

    eye = jnp.where(row == col, 1.0, 0.0)
    base_mask = same_block(base)
    powers = [jnp.where(base_mask, l, 0.0) for l in ls]
    invs = [eye - d for d in powers]
    p = 2
    while p < base:
        powers = [_mm(d, d) for d in powers]
        invs = [t + _mm(t, d) for t, d in zip(invs, powers)]
        p *= 2
        yield
    s = base
    while s < chunk:
        off_mask = same_block(2 * s) & jnp.logical_not(same_block(s))
        tmp = [_mm(jnp.where(off_mask, l, 0.0), t) for l, t in zip(ls, invs)]
        invs = [t - _mm(t, x) for t, x in zip(invs, tmp)]
        s *= 2
        yield
    return invs


def _mixer_stage(vals, x_ref, mod_ref, wout_ref, nw_ref, lng_ref, lnb_ref,
                 y_ref, sout_ref, hout_ref, *, nb, tt, chunk, alpha):
    d = x_ref.shape[-1]
    lw = hout_ref.shape[-1]
    rows = nb * tt
    grows = min(rows, MXU_DIM)
    n_groups = rows // grows
    seqs_per_group = max(grows // tt, 1)
    n_chunks = grows // chunk
    chunks_per_seq = tt // chunk if tt >= chunk else 1
    log_chunk = chunk.bit_length() - 1

    row = lax.broadcasted_iota(jnp.int32, (grows, grows), 0)
    col = lax.broadcasted_iota(jnp.int32, (grows, grows), 1)
    incl = ((row >> log_chunk) == (col >> log_chunk)) & (col <= row)
    strict = incl & (col < row)
    gb = vals["gb"]
    pieces = []
    rest = gb
    for _ in range(3):
        piece = rest.astype(BF16)
        pieces.append(piece)
        rest = rest - piece.astype(F32)
    split = jnp.concatenate(pieces, axis=-1)
    incl_b = jnp.where(incl, 1.0, 0.0).astype(BF16)
    chains = [(g, h) for g in range(n_groups) for h in range(DN_HEADS)]
    gcs, gc_ts = [], []
    for g in range(n_groups):
        sums = jnp.dot(incl_b, split[g * grows:(g + 1) * grows], preferred_element_type=F32)
        gc = sums[:, 0:LANES] + sums[:, LANES:2 * LANES] + sums[:, 2 * LANES:3 * LANES]
        gcs.append(gc)
        gc_ts.append(gc.T)

    rsl = lambda g: slice(g * grows, (g + 1) * grows)
    qs = [vals["q"][rsl(g), h * DN_DK:(h + 1) * DN_DK] for g, h in chains]
    ks = [vals["k"][rsl(g), h * DN_DK:(h + 1) * DN_DK] for g, h in chains]
    gcols = [gcs[g][:, h:h + 1] for g, h in chains]
    betas = [gb[rsl(g), DN_HEADS + h:DN_HEADS + h + 1] for g, h in chains]
    decays = [jnp.where(incl, jnp.exp(jnp.where(incl, gcols[i] - gc_ts[g][h:h + 1, :], 0.0)), 0.0)
              for i, (g, h) in enumerate(chains)]
    idx = range(len(chains))
    kbs = [ks[i] * betas[i] for i in idx]
    qk_kks = [lax.dot_general(jnp.concatenate([qs[i], kbs[i]], axis=0).astype(BF16), ks[i].astype(BF16),
                              (((1,), (1,)), ((), ())), preferred_element_type=F32) for i in idx]
    qks = [qk_kks[i][:grows] * decays[i] for i in idx]
    lmats = [jnp.where(strict, qk_kks[i][grows:] * decays[i], 0.0) for i in idx]
    yield
    tmats = yield from _unit_lower_inverses(lmats, row, col, chunk, min(INV_BASE_BLOCK, chunk))
    egcs = [jnp.exp(gcol) for gcol in gcols]
    vs = [vals["v"][rsl(g), h * DN_DV:(h + 1) * DN_DV] for g, h in chains]
    uws = [_mm(tmats[i], jnp.concatenate([vs[i] * betas[i], kbs[i] * egcs[i]], axis=-1)) for i in idx]
    us = [uw[:, :DN_DV] for uw in uws]
    ws = [uw[:, DN_DV:] for uw in uws]
    qes = [qs[i] * egcs[i] for i in idx]
    yield

    a = vals["a"]
    bacc = vals["inp"]
    gy = vals["gy"]
    t = lax.broadcasted_iota(jnp.int32, (rows, lw), 0) & (SUBLANES - 1)
    s = 1
    while s < SUBLANES:
        keep = t >= s
        a_prev = jnp.where(keep, pltpu.roll(a, s, 0), 1.0)
        b_prev = jnp.where(keep, pltpu.roll(bacc, s, 0), 0.0)
        bacc = a * b_prev + bacc
        a = a * a_prev
        s *= 2
    tiles = tt // SUBLANES
    a4 = a.reshape(nb, tiles, SUBLANES, lw)
    b4 = bacc.reshape(nb, tiles, SUBLANES, lw)
    carry = hout_ref[...]
    h_tiles = []
    for k in range(tiles):
        hk = a4[:, k] * carry + b4[:, k]
        carry = hk[:, SUBLANES - 1:SUBLANES, :]
        h_tiles.append(hk)
    hout_ref[...] = carry
    hs = h_tiles[0] if tiles == 1 else jnp.stack(h_tiles, axis=1)
    o_b = hs.reshape(rows, lw) * gy
    yield

    v_new_parts = [[] for _ in idx]
    o_inter_parts = [[] for _ in idx]
    states = [None for _ in idx]
    for c in range(n_chunks):
        lo, hi = c * chunk, (c + 1) * chunk
        seq_of = lambda g: g * seqs_per_group + c // chunks_per_seq
        if c % chunks_per_seq == 0:
            states = [sout_ref[seq_of(g), h] for g, h in chains]
        wqs = [_mm(jnp.concatenate([ws[i][lo:hi], qes[i][lo:hi]], axis=0), states[i]) for i in idx]
        v_news = [us[i][lo:hi] - wqs[i][:chunk] for i in idx]
        new_states = []
        for i in idx:
            v_new_parts[i].append(v_news[i])
            o_inter_parts[i].append(wqs[i][chunk:])
            g_last = gcols[i][hi - 1:hi, :]
            k_dec = ks[i][lo:hi] * jnp.exp(g_last - gcols[i][lo:hi])
            new_states.append(states[i] * jnp.exp(g_last) + lax.dot_general(
                k_dec.astype(BF16), v_news[i].astype(BF16), (((0,), (0,)), ((), ())),
                preferred_element_type=F32))
        states = new_states
        if (c + 1) % chunks_per_seq == 0:
            for i, (g, h) in enumerate(chains):
                sout_ref[seq_of(g), h] = states[i]
        yield
    o_groups = []
    for g in range(n_groups):
        o_heads = []
        for h in range(DN_HEADS):
            i = g * DN_HEADS + h
            o = (jnp.concatenate(o_inter_parts[i], axis=0)
                 + _mm(qks[i], jnp.concatenate(v_new_parts[i], axis=0)))
            ms = jnp.mean(o * o, axis=-1, keepdims=True)
            o_heads.append(o * lax.rsqrt(ms + NORM_EPS) * nw_ref[...]
                           * vals["gz"][rsl(g), h * DN_DV:(h + 1) * DN_DV])
        o_groups.append(jnp.concatenate(o_heads, axis=-1))
    o_a = o_groups[0] if n_groups == 1 else jnp.concatenate(o_groups, axis=0)

    mixed = _mm(jnp.concatenate([o_a, o_b], axis=-1), wout_ref[...]).reshape(nb, tt, d)
    gate = mod_ref[...][:, :, 2 * d:3 * d]
    y_ref[...] = _layer_norm(alpha * x_ref[...] + (1.0 + gate) * mixed, lng_ref[...], lnb_ref[...])


N_TOKEN_MIX_INPUTS = 21
N_TOKEN_MIX_OUTPUTS = 5


def _token_mix_kernel(*refs, n_cast, nb, tt, chunk, alpha):
    (x_ref, mod_ref, hdn_ref, hlru_ref, s0_ref, h0_ref, w_ref, cwdn_ref, cwlru_ref, cblru_ref, alog_ref, dtb_ref,
     wr_ref, wi_ref, br_ref, bi_ref, lam_ref, wout_ref, nw_ref, lng_ref, lnb_ref) = refs[:N_TOKEN_MIX_INPUTS]
    cast_in = refs[N_TOKEN_MIX_INPUTS:N_TOKEN_MIX_INPUTS + n_cast]
    outs = refs[N_TOKEN_MIX_INPUTS + n_cast:]
    y_ref, tdn_ref, tlru_ref, sout_ref, hout_ref = outs[:N_TOKEN_MIX_OUTPUTS]
    cast_out = outs[N_TOKEN_MIX_OUTPUTS:N_TOKEN_MIX_OUTPUTS + n_cast]
    wdn_sc, wlru_sc = outs[N_TOKEN_MIX_OUTPUTS + n_cast:]

    for src, dst in zip(cast_in, cast_out):
        dst[...] = src[...].astype(BF16)

    @pl.when(pl.program_id(1) == 0)
    def _():
        sout_ref[...] = s0_ref[0]
        hout_ref[...] = h0_ref[...]

    vals = {}
    stage1 = _inproj_stage(
        x_ref, mod_ref, hdn_ref, hlru_ref, w_ref, cwdn_ref, cwlru_ref, cblru_ref, alog_ref, dtb_ref,
        wr_ref, wi_ref, br_ref, bi_ref, lam_ref, tdn_ref, tlru_ref, wdn_sc, wlru_sc, vals, nb=nb, tt=tt)
    stage2 = _mixer_stage(vals, x_ref, mod_ref, wout_ref, nw_ref, lng_ref, lnb_ref,
                          y_ref, sout_ref, hout_ref, nb=nb, tt=tt, chunk=chunk, alpha=alpha)
    while "gb" not in vals:
        next(stage1)
    pending = [stage2, stage1]
    while pending:
        pending = [g for g in pending if next(g, StopIteration) is not StopIteration]


def _token_mix_call(x, mod, hist_dn, hist_lru, s_dn_all, layer, h0, lw_, cast=(), *, nb, tt, chunk, alpha):
    b, t, d = x.shape
    lw = hist_lru.shape[-1]
    grid = (b // nb, t // tt)
    steps = grid[0] * grid[1]
    slab = lambda a: pl.BlockSpec((a.shape[0] // steps, a.shape[1]), lambda i, j: (i * grid[1] + j, 0))
    seq_blk = lambda c: pl.BlockSpec((nb, tt, c), lambda i, j: (i, j, 0))
    per_seq = lambda r, c: pl.BlockSpec((nb, r, c), lambda i, j: (i, 0, 0))
    whole = lambda a: pl.BlockSpec(a.shape, lambda i, j: (0,) * a.ndim, pipeline_mode=pl.Buffered(1))
    state_in = pl.BlockSpec((1, nb, DN_HEADS, DN_DK, DN_DV), lambda i, j: (layer, i, 0, 0, 0))
    state_out = pl.BlockSpec((nb, DN_HEADS, DN_DK, DN_DV), lambda i, j: (i, 0, 0, 0))
    weights = [lw_[n] for n in ("w_cat", "dn_conv_w", "lru_conv_w", "lru_conv_b", "a_log", "dt_bias",
                                "w_r", "w_i", "b_r", "b_i", "lam", "w_out", "dn_norm_w", "ln1_g", "ln1_b")]
    return pl.pallas_call(
        functools.partial(_token_mix_kernel, n_cast=len(cast), nb=nb, tt=tt, chunk=chunk, alpha=alpha),
        out_shape=(
            jax.ShapeDtypeStruct((b, t, d), F32),
            jax.ShapeDtypeStruct((b, SUBLANES, DN_QKV), F32),
            jax.ShapeDtypeStruct((b, SUBLANES, lw), F32),
            jax.ShapeDtypeStruct((b, DN_HEADS, DN_DK, DN_DV), F32),
            jax.ShapeDtypeStruct((b, 1, lw), F32),
        ) + tuple(jax.ShapeDtypeStruct(a.shape, BF16) for a in cast),
        grid=grid,
        in_specs=[seq_blk(d), per_seq(1, N_MOD * d), per_seq(SUBLANES, DN_QKV), per_seq(SUBLANES, lw),
                  state_in, per_seq(1, lw)] + [whole(a) for a in weights] + [slab(a) for a in cast],
        out_specs=(seq_blk(d), per_seq(SUBLANES, DN_QKV), per_seq(SUBLANES, lw), state_out, per_seq(1, lw))
        + tuple(slab(a) for a in cast),
        scratch_shapes=[pltpu.VMEM((nb, SUBLANES + tt, DN_QKV), F32), pltpu.VMEM((nb, SUBLANES + tt, lw), F32)],
        compiler_params=_params(("arbitrary", "arbitrary")),
        name="token_mixer",
    )(x, mod, hist_dn, hist_lru, s_dn_all, h0, *weights, *cast)


def _ffn_kernel(x_ref, mod_ref, wg_ref, wu_ref, wd_ref, lng_ref, lnb_ref, *refs, alpha):
    n_cast = (len(refs) - 1) // 2
    y_ref = refs[n_cast]
    for src, dst in zip(refs[:n_cast], refs[n_cast + 1:]):
        dst[...] = src[...].astype(BF16)
    nb, tt, d = x_ref.shape
    m = mod_ref[...]
    h = (x_ref[...] * (1.0 + m[:, :, 4 * d:5 * d]) + m[:, :, 3 * d:4 * d]).reshape(nb * tt, d).astype(BF16)
    act = _silu(jnp.dot(h, wg_ref[...], preferred_element_type=F32)) * jnp.dot(
        h, wu_ref[...], preferred_element_type=F32)
    ff = _mm(act, wd_ref[...]).reshape(nb, tt, d)
    y_ref[...] = _layer_norm(alpha * x_ref[...] + (1.0 + m[:, :, 5 * d:6 * d]) * ff, lng_ref[...], lnb_ref[...])


def _ff_tile(ff):
    for n in (2, 4, 7, 8, 11, 14, 16, 22, 28):
        if ff % n == 0 and (ff // n) % LANES == 0 and ff // n <= 2048:
            return ff // n
    return ff


def _dense_ffn(x, mod, lw_, cast=(), *, nb, tt, alpha):
    b, t, d = x.shape
    nt = t // tt
    steps = (b // nb) * nt
    slab = lambda a: pl.BlockSpec((a.shape[0] // steps, a.shape[1]), lambda i: (i, 0))
    seq_blk = pl.BlockSpec((nb, tt, d), lambda i: (i // nt, i % nt, 0))
    whole = lambda a: pl.BlockSpec(a.shape, lambda i: (0,) * a.ndim)
    resident = lambda a: pl.BlockSpec(a.shape, lambda i: (0,) * a.ndim, pipeline_mode=pl.Buffered(1))
    return pl.pallas_call(
        functools.partial(_ffn_kernel, alpha=alpha),
        out_shape=(jax.ShapeDtypeStruct((b, t, d), F32),) + tuple(jax.ShapeDtypeStruct(a.shape, BF16) for a in cast),
        grid=(steps,),
        in_specs=[seq_blk, pl.BlockSpec((nb, 1, N_MOD * d), lambda i: (i // nt, 0, 0)),
                  resident(lw_["w_gate"]), resident(lw_["w_up"]), resident(lw_["w_down"]),
                  whole(lw_["ln2_g"]), whole(lw_["ln2_b"])] + [slab(a) for a in cast],
        out_specs=(seq_blk,) + tuple(slab(a) for a in cast),
        compiler_params=_params(("arbitrary",)),
        name="dense_ffn",
    )(x, mod, lw_["w_gate"], lw_["w_up"], lw_["w_down"], lw_["ln2_g"], lw_["ln2_b"], *cast)


MOE_CHUNK = 512
MOE_SLOT_TILE = 512
MOE_SUB = 128
MOE_COMBINE_BLOCK = 256
MOE_COMBINE_FANIN = 4
ROUTE_I1, ROUTE_I2, ROUTE_R1, ROUTE_R2, ROUTE_W1, ROUTE_W2 = range(6)


def _router_kernel(x_ref, mod_ref, wr_ref, base_ref, h_ref, meta_ref, meta_t_ref, blkbase_ref, cnt_ref, run_sc,
                   *, n_experts):
    nb, tt, d = x_ref.shape
    rows = nb * tt

    @pl.when(pl.program_id(0) == 0)
    def _():
        run_sc[...] = base_ref[...]

    m = mod_ref[...]
    h = (x_ref[...] * (1.0 + m[:, :, 4 * d:5 * d]) + m[:, :, 3 * d:4 * d]).reshape(rows, d)
    h_hi = h.astype(BF16)
    h_ref[...] = h_hi
    w = wr_ref[...]
    w_hi = w.astype(BF16)
    h_lo = (h - h_hi.astype(F32)).astype(BF16)
    w_lo = (w - w_hi.astype(F32)).astype(BF16)
    logits = (jnp.dot(h_hi, w_hi, preferred_element_type=F32) + jnp.dot(h_lo, w_hi, preferred_element_type=F32)
              + jnp.dot(h_hi, w_lo, preferred_element_type=F32))
    lane = lax.broadcasted_iota(jnp.int32, logits.shape, 1)
    neg = jnp.float32(-jnp.inf)
    lg = jnp.where(lane < n_experts, logits, neg)
    m1 = jnp.max(lg, axis=-1, keepdims=True)
    i1 = jnp.min(jnp.where(lg == m1, lane, LANES), axis=-1, keepdims=True)
    lg2 = jnp.where(lane == i1, neg, lg)
    m2 = jnp.max(lg2, axis=-1, keepdims=True)
    i2 = jnp.min(jnp.where(lg2 == m2, lane, LANES), axis=-1, keepdims=True)
    e2 = jnp.exp(m2 - m1)
    w1 = 1.0 / (1.0 + e2)
    w2 = e2 / (1.0 + e2)
    sel = jnp.where(lane == i1, 1.0, jnp.where(lane == i2, 1.0, 0.0))
    r = lax.broadcasted_iota(jnp.int32, (rows, rows), 0)
    c = lax.broadcasted_iota(jnp.int32, (rows, rows), 1)
    rank = _mm(jnp.where(c < r, 1.0, 0.0), sel) + run_sc[0:1, :]
    r1 = jnp.sum(jnp.where(lane == i1, rank, 0.0), axis=-1, keepdims=True)
    r2 = jnp.sum(jnp.where(lane == i2, rank, 0.0), axis=-1, keepdims=True)
    fields = (i1.astype(F32), i2.astype(F32), r1, r2, w1, w2)
    meta = jnp.zeros_like(logits)
    for k, v in enumerate(fields):
        meta = jnp.where(lane == k, v, meta)
    meta_ref[...] = meta
    meta_t_ref[...] = meta.T[0:SUBLANES, :]
    blkbase_ref[0] = run_sc[...]
    run_sc[...] = run_sc[...] + jnp.sum(sel, axis=0, keepdims=True)
    cnt_ref[...] = run_sc[...]


def _router(x, mod, w_router, base, *, nb, tt, n_experts):
    b, t, d = x.shape
    rows = nb * tt
    nt = t // tt
    nblk = (b // nb) * nt
    whole = lambda a: pl.BlockSpec(a.shape, lambda i: (0,) * a.ndim)
    return pl.pallas_call(
        functools.partial(_router_kernel, n_experts=n_experts),
        out_shape=(jax.ShapeDtypeStruct((nblk * rows, d), BF16),
                   jax.ShapeDtypeStruct((nblk * rows, LANES), F32),
                   jax.ShapeDtypeStruct((SUBLANES, nblk * rows), F32),
                   jax.ShapeDtypeStruct((nblk, SUBLANES, LANES), F32),
                   jax.ShapeDtypeStruct((SUBLANES, LANES), F32)),
        grid=(nblk,),
        in_specs=[pl.BlockSpec((nb, tt, d), lambda i: (i // nt, i % nt, 0)),
                  pl.BlockSpec((nb, 1, N_MOD * d), lambda i: (i // nt, 0, 0)),
                  whole(w_router), whole(base)],
        out_specs=(pl.BlockSpec((rows, d), lambda i: (i, 0)),
                   pl.BlockSpec((rows, LANES), lambda i: (i, 0)),
                   pl.BlockSpec((SUBLANES, rows), lambda i: (0, i)),
                   pl.BlockSpec((1, SUBLANES, LANES), lambda i: (i, 0, 0)),
                   pl.BlockSpec((SUBLANES, LANES), lambda i: (0, 0))),
        scratch_shapes=[pltpu.VMEM((SUBLANES, LANES), F32)],
        compiler_params=_params(("arbitrary",)),
        name="moe_router",
    )(x, mod, w_router, base)


def _slot_of(expert, rank, start_ref, n_experts):
    start = jnp.zeros_like(rank)
    for e in range(n_experts):
        start = jnp.where(expert == e, start_ref[e].astype(F32), start)
    return (start + rank).astype(jnp.int32)


def _gather_kernel(clo_ref, chi_ref, start_ref, *refs, group_chunks, n_experts):
    n_groups = len(group_chunks)
    m_refs = refs[:n_groups]
    h_refs = refs[n_groups:2 * n_groups]
    xs_ref, ws_ref, acc_sc, wacc_sc = refs[2 * n_groups:]
    g = pl.program_id(0)
    n_sub = MOE_SLOT_TILE // MOE_SUB
    for j in range(n_sub):
        q = g * n_sub + j
        ids = g * MOE_SLOT_TILE + j * MOE_SUB + lax.broadcasted_iota(jnp.int32, (MOE_SUB, MOE_CHUNK), 0)
        acc_sc[...] = jnp.zeros_like(acc_sc)
        wacc_sc[...] = jnp.zeros_like(wacc_sc)
        first = 0
        for m_ref, h_ref, n_chunks in zip(m_refs, h_refs, group_chunks):
            def body(c, carry, m_ref=m_ref, h_ref=h_ref):
                off = pl.multiple_of(c * MOE_CHUNK, MOE_CHUNK)
                rec = m_ref[:, pl.ds(off, MOE_CHUNK)]
                row = lambda k: rec[k:k + 1, :]
                hit1 = _slot_of(row(ROUTE_I1), row(ROUTE_R1), start_ref, n_experts) == ids
                hit2 = _slot_of(row(ROUTE_I2), row(ROUTE_R2), start_ref, n_experts) == ids
                p = jnp.where(hit1, 1.0, jnp.where(hit2, 1.0, 0.0)).astype(BF16)
                acc_sc[...] += jnp.dot(p, h_ref[pl.ds(off, MOE_CHUNK), :], preferred_element_type=F32)
                w = jnp.where(hit1, row(ROUTE_W1), 0.0) + jnp.where(hit2, row(ROUTE_W2), 0.0)
                wacc_sc[...] += jnp.broadcast_to(jnp.sum(w, axis=-1, keepdims=True), wacc_sc.shape)
                return carry

            lo = jnp.clip(clo_ref[q] - first, 0, n_chunks)
            hi = jnp.clip(chi_ref[q] - first, 0, n_chunks)
            lax.fori_loop(lo, hi, body, 0)
            first += n_chunks
        xs_ref[j * MOE_SUB:(j + 1) * MOE_SUB, :] = acc_sc[...].astype(BF16)
        ws_ref[j * MOE_SUB:(j + 1) * MOE_SUB, :] = wacc_sc[...]


def _gather_slots(c_lo, c_hi, run_start, metas_t, hs, n_tiles, n_experts):
    d = hs[0].shape[-1]
    vmem = pl.BlockSpec(memory_space=pltpu.VMEM)
    return pl.pallas_call(
        functools.partial(_gather_kernel, group_chunks=tuple(h.shape[0] // MOE_CHUNK for h in hs),
                          n_experts=n_experts),
        out_shape=(jax.ShapeDtypeStruct((n_tiles * MOE_SLOT_TILE, d), BF16),
                   jax.ShapeDtypeStruct((n_tiles * MOE_SLOT_TILE, LANES), F32)),
        grid_spec=pltpu.PrefetchScalarGridSpec(
            num_scalar_prefetch=3, grid=(n_tiles,),
            in_specs=[vmem] * (2 * len(hs)),
            out_specs=(pl.BlockSpec((MOE_SLOT_TILE, d), lambda g, lo, hi, st: (g, 0)),
                       pl.BlockSpec((MOE_SLOT_TILE, LANES), lambda g, lo, hi, st: (g, 0))),
            scratch_shapes=[pltpu.VMEM((MOE_SUB, d), F32), pltpu.VMEM((MOE_SUB, LANES), F32)]),
        compiler_params=_params(("arbitrary",)),
        name="moe_gather",
    )(c_lo, c_hi, run_start, *metas_t, *hs)


def _expert_kernel(te_ref, tv_ref, xs_ref, ws_ref, wg_ref, wu_ref, wd_ref, o_ref, *, ff_chunk):
    g = pl.program_id(0)

    @pl.when(tv_ref[g] != 0)
    def _():
        x = xs_ref[...]
        ff = wg_ref.shape[-1]
        acc = jnp.zeros(o_ref.shape, F32)
        for f0 in range(0, ff, ff_chunk):
            gate = jnp.dot(x, wg_ref[0, :, f0:f0 + ff_chunk], preferred_element_type=F32)
            up = jnp.dot(x, wu_ref[0, :, f0:f0 + ff_chunk], preferred_element_type=F32)
            acc = acc + _mm(_silu(gate) * up, wd_ref[0, f0:f0 + ff_chunk, :])
        o_ref[...] = (ws_ref[:, 0:1] * acc).astype(BF16)

    @pl.when(tv_ref[g] == 0)
    def _():
        o_ref[...] = jnp.zeros_like(o_ref)


def _expert_ffn(tile_expert, tile_valid, xs, ws, lw_):
    s_total, d = xs.shape
    _, _, ff = lw_["w_gate"].shape
    n_tiles = s_total // MOE_SLOT_TILE
    once = pl.Buffered(1)
    twice = pl.Buffered(2)
    return pl.pallas_call(
        functools.partial(_expert_kernel, ff_chunk=_ff_tile(ff)),
        out_shape=jax.ShapeDtypeStruct((s_total, d), BF16),
        grid_spec=pltpu.PrefetchScalarGridSpec(
            num_scalar_prefetch=2, grid=(n_tiles,),
            in_specs=[pl.BlockSpec((MOE_SLOT_TILE, d), lambda g, te, tv: (g, 0)),
                      pl.BlockSpec((MOE_SLOT_TILE, LANES), lambda g, te, tv: (g, 0)),
                      pl.BlockSpec((1, d, ff), lambda g, te, tv: (te[g], 0, 0), pipeline_mode=twice),
                      pl.BlockSpec((1, d, ff), lambda g, te, tv: (te[g], 0, 0), pipeline_mode=twice),
                      pl.BlockSpec((1, ff, d), lambda g, te, tv: (te[g], 0, 0), pipeline_mode=once)],
            out_specs=pl.BlockSpec((MOE_SLOT_TILE, d), lambda g, te, tv: (g, 0))),
        compiler_params=_params(("arbitrary",)),
        name="expert_ffn",
    )(tile_expert, tile_valid, xs, ws, lw_["w_gate"], lw_["w_up"], lw_["w_down"])


def _combine_kernel(ic_ref, ik_ref, if_ref, bf_ref, start_ref, x_ref, mod_ref, meta_ref, *refs, alpha,
                    n_experts):
    os_refs = refs[:MOE_COMBINE_FANIN]
    lng_ref, lnb_ref, y_ref, acc_sc, slot_sc = refs[MOE_COMBINE_FANIN:]
    nb, tt, d = x_ref.shape
    rows = nb * tt
    w = pl.program_id(0)
    flags = if_ref[w]

    @pl.when((flags & 1) != 0)
    def _():
        rec = meta_ref[...]
        col = lambda k: rec[:, k:k + 1]
        slot_sc[:, 0:1] = _slot_of(col(ROUTE_I1), col(ROUTE_R1), start_ref, n_experts)
        slot_sc[:, 1:2] = _slot_of(col(ROUTE_I2), col(ROUTE_R2), start_ref, n_experts)

    lane = lax.broadcasted_iota(jnp.int32, (rows, MOE_COMBINE_BLOCK), 1)
    s1 = slot_sc[:, 0:1]
    s2 = slot_sc[:, 1:2]
    total = None
    for j, os_ref in enumerate(os_refs):
        ids = ik_ref[w * MOE_COMBINE_FANIN + j] * MOE_COMBINE_BLOCK + lane
        q = jnp.where(s1 == ids, 1.0, jnp.where(s2 == ids, 1.0, 0.0)).astype(BF16)
        part = jnp.dot(q, os_ref[...], preferred_element_type=F32)
        total = part if total is None else total + part

    @pl.when((flags & 1) != 0)
    def _():
        acc_sc[...] = total

    @pl.when((flags & 1) == 0)
    def _():
        acc_sc[...] += total

    @pl.when((flags & 2) != 0)
    def _():
        gate = mod_ref[...][:, :, 5 * d:6 * d]
        ff = acc_sc[...].reshape(nb, tt, d)
        y_ref[...] = _layer_norm(alpha * x_ref[...] + (1.0 + gate) * ff, lng_ref[...], lnb_ref[...])


def _combine(items, run_start, x, mod, meta, out_sorted, lw_, *, nb, tt, alpha, n_experts):
    step_chunk, step_blocks, step_flags, block_fetch = items
    b, t, d = x.shape
    rows = nb * tt
    nt = t // tt
    whole = lambda a: pl.BlockSpec(a.shape, lambda w, ic, ik, fl, bf, st: (0,) * a.ndim)
    seq_blk = pl.BlockSpec((nb, tt, d), lambda w, ic, ik, fl, bf, st: (ic[w] // nt, ic[w] % nt, 0))
    slot_blk = lambda j: pl.BlockSpec((MOE_COMBINE_BLOCK, d),
                                      lambda w, ic, ik, fl, bf, st: (bf[w * MOE_COMBINE_FANIN + j], 0))
    return pl.pallas_call(
        functools.partial(_combine_kernel, alpha=alpha, n_experts=n_experts),
        out_shape=jax.ShapeDtypeStruct((b, t, d), F32),
        grid_spec=pltpu.PrefetchScalarGridSpec(
            num_scalar_prefetch=5, grid=(step_chunk.shape[0],),
            in_specs=[seq_blk,
                      pl.BlockSpec((nb, 1, N_MOD * d), lambda w, ic, ik, fl, bf, st: (ic[w] // nt, 0, 0)),
                      pl.BlockSpec((rows, LANES), lambda w, ic, ik, fl, bf, st: (ic[w], 0))]
            + [slot_blk(j) for j in range(MOE_COMBINE_FANIN)]
            + [whole(lw_["ln2_g"]), whole(lw_["ln2_b"])],
            out_specs=seq_blk,
            scratch_shapes=[pltpu.VMEM((rows, d), F32), pltpu.VMEM((rows, LANES), jnp.int32)]),
        compiler_params=_params(("arbitrary",)),
        name="moe_combine",
    )(step_chunk, step_blocks, step_flags, block_fetch, run_start, x, mod, meta,
      *([out_sorted] * MOE_COMBINE_FANIN), lw_["ln2_g"], lw_["ln2_b"])


def _count_le(sorted_vals, queries):
    return jnp.sum(sorted_vals[None, :] <= queries[:, None], axis=1).astype(jnp.int32)


def _combine_steps(lo, hi, n_steps):
    n_chunks, n_experts = lo.shape
    fan = MOE_COMBINE_FANIN
    first = (lo // MOE_COMBINE_BLOCK).reshape(-1)
    count = jnp.where(hi > lo, (hi - 1) // MOE_COMBINE_BLOCK - lo // MOE_COMBINE_BLOCK + 1, 0).reshape(-1)
    pair_end = jnp.cumsum(count)
    chunk_items = jnp.sum(count.reshape(n_chunks, n_experts), axis=1)
    chunk_item0 = jnp.cumsum(chunk_items) - chunk_items
    chunk_steps = (chunk_items + fan - 1) // fan
    step_end = jnp.cumsum(chunk_steps)
    total_steps = step_end[-1]
    s = jnp.arange(n_steps, dtype=jnp.int32)
    live = s < total_steps
    chunk = jnp.minimum(_count_le(step_end, s), n_chunks - 1)
    chunk = jnp.where(live, chunk, chunk[jnp.maximum(total_steps - 1, 0)])
    q = s - (step_end[chunk] - chunk_steps[chunk])
    j = q[:, None] * fan + jnp.arange(fan, dtype=jnp.int32)[None, :]
    used = live[:, None] & (j < chunk_items[chunk][:, None])
    item = jnp.where(used, chunk_item0[chunk][:, None] + j, 0).reshape(-1)
    pair = jnp.minimum(_count_le(pair_end, item), n_chunks * n_experts - 1)
    block = first[pair] + item - (pair_end[pair] - count[pair])
    block = jnp.where(used.reshape(-1), block, -1)
    fetch = jnp.where(block >= 0, block, jnp.repeat(jnp.maximum(block.reshape(-1, fan)[:, 0], 0), fan))
    flags = (jnp.where(live & (q == 0), 1, 0) + jnp.where(live & (q == chunk_steps[chunk] - 1), 2, 0))
    return chunk.astype(jnp.int32), block.astype(jnp.int32), flags.astype(jnp.int32), fetch.astype(jnp.int32)


def _moe_layer(xs_in, mods, lw_, tilings, alpha):
    n_experts = lw_["w_gate"].shape[0]
    base = jnp.zeros((SUBLANES, LANES), F32)
    hs, metas, metas_t, bases = [], [], [], []
    for x, mod, (nb, tt) in zip(xs_in, mods, tilings):
        assert nb * tt == MOE_CHUNK and (x.shape[0] * x.shape[1]) % MOE_CHUNK == 0
        h, meta, meta_t, blkbase, base = _router(x, mod, lw_["w_router"], base, nb=nb, tt=tt,
                                                 n_experts=n_experts)
        hs.append(h)
        metas.append(meta)
        metas_t.append(meta_t)
        bases.append(blkbase[:, 0, :n_experts])
    n = sum(m.shape[0] for m in metas)
    cum = jnp.concatenate(bases + [base[0:1, :n_experts]], axis=0).astype(jnp.int32)
    counts = cum[-1]
    sizes = ((counts + MOE_SLOT_TILE - 1) // MOE_SLOT_TILE) * MOE_SLOT_TILE
    run_end = jnp.cumsum(sizes)
    run_start = (run_end - sizes).astype(jnp.int32)

    n_tiles = (TOP_K * n + MOE_SLOT_TILE - 1) // MOE_SLOT_TILE + n_experts
    tile_start = jnp.arange(n_tiles, dtype=jnp.int32) * MOE_SLOT_TILE
    tile_expert = jnp.minimum(_count_le(run_end, tile_start), n_experts - 1)
    tile_valid = (tile_start < run_end[-1]).astype(jnp.int32)
    n_sub = MOE_SLOT_TILE // MOE_SUB
    sub_expert = jnp.repeat(tile_expert, n_sub)
    sub_rank0 = jnp.arange(n_tiles * n_sub, dtype=jnp.int32) * MOE_SUB - run_start[sub_expert]
    cum_sub = cum[:, sub_expert]
    sub_valid = jnp.repeat(tile_valid, n_sub)
    c_lo = jnp.sum(cum_sub[1:] <= sub_rank0[None, :], axis=0).astype(jnp.int32) * sub_valid
    c_hi = jnp.sum(cum_sub[:-1] < sub_rank0[None, :] + MOE_SUB, axis=0).astype(jnp.int32) * sub_valid

    x_sorted, w_sorted = _gather_slots(c_lo, c_hi, run_start, metas_t, hs, n_tiles, n_experts)
    out_sorted = _expert_ffn(tile_expert, tile_valid, x_sorted, w_sorted, lw_)

    outs = []
    chunk0 = 0
    for x, mod, meta, (nb, tt) in zip(xs_in, mods, metas, tilings):
        nc = x.shape[0] * x.shape[1] // MOE_CHUNK
        lo = run_start[None, :] + cum[chunk0:chunk0 + nc]
        hi = run_start[None, :] + cum[chunk0 + 1:chunk0 + nc + 1]
        max_items = nc * n_experts + (TOP_K * nc * MOE_CHUNK) // MOE_COMBINE_BLOCK + 2 * n_experts
        items = _combine_steps(lo, hi, max_items // MOE_COMBINE_FANIN + nc)
        outs.append(_combine(items, run_start, x, mod, meta, out_sorted, lw_, nb=nb, tt=tt, alpha=alpha,
                             n_experts=n_experts))
        chunk0 += nc
    return outs


def _pad_lanes(v, width=LANES):
    return jnp.pad(v, ((0, 0), (0, width - v.shape[-1])))


def _block_diag_halves(w):
    nblk, c, _ = w.shape
    half = nblk // 2
    out = jnp.zeros((2, half * c, half * c), w.dtype)
    for i in range(nblk):
        j = i % half
        out = out.at[i // half, j * c:(j + 1) * c, j * c:(j + 1) * c].set(w[i])
    return out


def _layer_weights(l, p, d, lw):
    w_in = p["w_in"][l]
    a_off = DN_QKV
    z_off = a_off + 2 * DN_HEADS
    x_off = z_off + DN_WIDTH
    y_off = x_off + lw
    w_ab = _pad_lanes(w_in[:, a_off:z_off])
    w_cat = jnp.concatenate([w_in[:, :DN_QKV], w_in[:, z_off:y_off + lw], w_ab], axis=1).astype(BF16)
    out = {
        "w_cat": w_cat,
        "dn_conv_w": p["dn_conv_w"][l],
        "lru_conv_w": p["lru_conv_w"][l],
        "lru_conv_b": p["lru_conv_b"][l][None],
        "a_log": _pad_lanes(p["dn_a_log"][l][None]),
        "dt_bias": _pad_lanes(p["dn_dt_bias"][l][None]),
        "w_r": _block_diag_halves(p["lru_w_r"][l]).astype(BF16),
        "w_i": _block_diag_halves(p["lru_w_i"][l]).astype(BF16),
        "b_r": p["lru_b_r"][l][None],
        "b_i": p["lru_b_i"][l][None],
        "lam": p["lru_lambda"][l][None],
        "w_out": p["w_out"][l].astype(BF16),
        "dn_norm_w": p["dn_norm_w"][l][None],
        "ln1_g": p["ln1_g"][l][None],
        "ln1_b": p["ln1_b"][l][None],
        "ln2_g": p["ln2_g"][l][None],
        "ln2_b": p["ln2_b"][l][None],
    }
    j = l // 2
    if l % 2 == 0:
        out.update(w_gate=p["ffn_w_gate"][j], w_up=p["ffn_w_up"][j], w_down=p["ffn_w_down"][j])
    else:
        out.update(w_router=_pad_lanes(p["moe_w_router"][j]),
                   w_gate=p["moe_w_gate"][j], w_up=p["moe_w_up"][j], w_down=p["moe_w_down"][j])
    return out


CHANNEL_MIX_WEIGHTS = ("w_gate", "w_up", "w_down")


LARGE_SLAB_BYTES = 64 * 1024 * 1024


def _cast_plan(weights, host_steps):
    hosts = list(host_steps)
    plan = {h: [] for h in hosts}
    for k in reversed(range(len(weights))):
        dense = ("ffn", k) in host_steps
        last = hosts.index(("ffn", k)) if dense else hosts.index(("mixer", k)) + 1
        for n in CHANNEL_MIX_WEIGHTS:
            a = weights[k][n].reshape(-1, weights[k][n].shape[-1])
            large = a.size * 4 > LARGE_SLAB_BYTES
            for h in reversed(hosts[:last]):
                fits = a.shape[0] % (host_steps[h] * 2 * SUBLANES) == 0
                busy = large and any(s.size * 4 > LARGE_SLAB_BYTES for _, _, s in plan[h])
                if fits and not busy:
                    plan[h].append((k, n, a))
                    break
    return plan


def _tiling(b, t):
    if t >= MXU_DIM:
        tt = MXU_DIM
        return dict(mixer=(2 if b % 2 == 0 else 1, tt, min(PROMPT_CHUNK, tt)), ffn=(1, min(t, 2 * MXU_DIM)))
    assert t == SUBLANES, "short sequences must be exactly one sublane tile long"
    return dict(mixer=(min(b, 16), t, t), ffn=(min(b, 64), t))


def _token_mix(x, mod, conv_dn, s_dn_all, layer, conv_lru, s_lru, lw_, til, alpha, cast=()):
    pad_hist = lambda c: jnp.pad(c, ((0, 0), (SUBLANES - (CONV_W - 1), 0), (0, 0)))
    nb, tt, chunk = til["mixer"]
    x, tail_dn, tail_lru, s_new, h_new, *converted = _token_mix_call(
        x, mod, pad_hist(conv_dn), pad_hist(conv_lru), s_dn_all, layer, s_lru[:, None, :], lw_, cast,
        nb=nb, tt=tt, chunk=chunk, alpha=alpha)
    return (x, tail_dn[:, SUBLANES - (CONV_W - 1):, :], s_new, tail_lru[:, SUBLANES - (CONV_W - 1):, :],
            h_new[:, 0, :]), converted


def kernel(x_prompt, x_sample, cache_dn_conv, state_dn, cache_lru_conv, state_lru, c_prompt, c_sample,
           w_ada, b_ada, w_in, dn_conv_w, dn_a_log, dn_dt_bias, dn_norm_w,
           lru_conv_w, lru_conv_b, lru_w_r, lru_b_r, lru_w_i, lru_b_i, lru_lambda, w_out,
           ln1_g, ln1_b, ln2_g, ln2_b, ffn_w_gate, ffn_w_up, ffn_w_down,
           moe_w_router, moe_w_gate, moe_w_up, moe_w_down):
    p = dict(w_in=w_in, dn_conv_w=dn_conv_w, dn_a_log=dn_a_log, dn_dt_bias=dn_dt_bias, dn_norm_w=dn_norm_w,
             lru_conv_w=lru_conv_w, lru_conv_b=lru_conv_b, lru_w_r=lru_w_r, lru_b_r=lru_b_r,
             lru_w_i=lru_w_i, lru_b_i=lru_b_i, lru_lambda=lru_lambda, w_out=w_out,
             ln1_g=ln1_g, ln1_b=ln1_b, ln2_g=ln2_g, ln2_b=ln2_b,
             ffn_w_gate=ffn_w_gate, ffn_w_up=ffn_w_up, ffn_w_down=ffn_w_down,
             moe_w_router=moe_w_router, moe_w_gate=moe_w_gate, moe_w_up=moe_w_up, moe_w_down=moe_w_down)
    depth, d, _ = w_ada.shape
    bp = x_prompt.shape[0]
    bs = x_sample.shape[0]
    lw = cache_lru_conv.shape[-1]
    alpha = (2 * depth) ** 0.25
    weights = [_layer_weights(l, p, d, lw) for l in range(depth)]

    c_all = jnp.concatenate([c_prompt, c_sample], axis=0)
    mod_all = _modulation(c_all, w_ada, b_ada)
    groups = [
        dict(x=x_prompt, rows=slice(0, bp), conv_dn=jnp.zeros((depth, bp, CONV_W - 1, DN_QKV), F32),
             s_dn=jnp.zeros((depth, bp, DN_HEADS, DN_DK, DN_DV), F32),
             conv_lru=jnp.zeros((depth, bp, CONV_W - 1, lw), F32), s_lru=jnp.zeros((depth, bp, lw), F32)),
        dict(x=x_sample, rows=slice(bp, bp + bs), conv_dn=cache_dn_conv, s_dn=state_dn,
             conv_lru=cache_lru_conv, s_lru=state_lru),
    ]
    for g in groups:
        g["til"] = _tiling(g["x"].shape[0], g["x"].shape[1])
        g["new"] = [[], [], [], []]
    t_p = x_prompt.shape[1]
    nb_m, tt_m, _ = groups[0]["til"]["mixer"]
    nb_f, tt_f = groups[0]["til"]["ffn"]
    host_steps = {}
    for l in range(depth):
        host_steps[("mixer", l)] = (bp // nb_m) * (t_p // tt_m)
        if l % 2 == 0:
            host_steps[("ffn", l)] = (bp // nb_f) * (t_p // tt_f)
    plan = _cast_plan(weights, host_steps)
    planned = {(k, n) for slabs in plan.values() for k, n, _ in slabs}
    for k, w in enumerate(weights):
        for n in CHANNEL_MIX_WEIGHTS:
            if (k, n) not in planned:
                w[n] = w[n].astype(BF16)

    def adopt(slabs, converted):
        for (k, n, _), c in zip(slabs, converted):
            weights[k][n] = c.reshape(weights[k][n].shape)

    for l in range(depth):
        lw_ = weights[l]
        mods = [mod_all[l, g["rows"]][:, None, :] for g in groups]
        for gi, (g, mod) in enumerate(zip(groups, mods)):
            slabs = plan[("mixer", l)] if gi == 0 else []
            res, converted = _token_mix(g["x"], mod, g["conv_dn"][l], g["s_dn"], l, g["conv_lru"][l],
                                        g["s_lru"][l], lw_, g["til"], alpha,
                                        cast=tuple(a for _, _, a in slabs))
            adopt(slabs, converted)
            g["x"] = res[0]
            for acc, new in zip(g["new"], res[1:]):
                acc.append(new)
        if l % 2 == 0:
            for gi, (g, mod) in enumerate(zip(groups, mods)):
                nb, tt = g["til"]["ffn"]
                slabs = plan[("ffn", l)] if gi == 0 else []
                g["x"], *converted = _dense_ffn(g["x"], mod, lw_, tuple(a for _, _, a in slabs),
                                                nb=nb, tt=tt, alpha=alpha)
                adopt(slabs, converted)
        else:
            xs = _moe_layer([g["x"] for g in groups], mods, lw_, [g["til"]["ffn"] for g in groups], alpha)
            for g, x in zip(groups, xs):
                g["x"] = x
    states = [jnp.stack(acc) for g in groups for acc in g["new"]]
    return (groups[0]["x"], groups[1]["x"]) + tuple(states)
```

```python
import functools
import math

import jax
import jax.numpy as jnp
from jax import lax
from jax.experimental import pallas as pl
from jax.experimental.pallas import tpu as pltpu

F32 = jnp.float32
BF16 = jnp.bfloat16

DN_HEADS = 4
DN_DK = 128
DN_DV = 128
DN_WIDTH = DN_HEADS * DN_DV
DN_QKV = 3 * DN_WIDTH
LRU_BLOCKS = 8
LRU_C = 8.0
CONV_W = 4
N_MOD = 6
TOP_K = 2
LN_EPS = 1e-5
NORM_EPS = 1e-6

SUBLANES = 8
LANES = 128
MXU_DIM = 256
VMEM_LIMIT_BYTES = 56 * 1024 * 1024

PROMPT_CHUNK = 64
INV_BASE_BLOCK = 16


def _sigmoid(x):
    return 0.5 + 0.5 * jnp.tanh(0.5 * x)


def _silu(x):
    half = 0.5 * x
    return half + half * jnp.tanh(half)


def _softplus(x):
    return jnp.maximum(x, 0.0) + jnp.log1p(jnp.exp(-jnp.abs(x)))


def _gelu_tanh(x):
    return 0.5 * x * (1.0 + jnp.tanh(math.sqrt(2.0 / math.pi) * (x + 0.044715 * (x * x * x))))


def _mm(a, b):
    return jnp.dot(a.astype(BF16), b.astype(BF16), preferred_element_type=F32)


def _layer_norm(x, g, b):
    mu = jnp.mean(x, axis=-1, keepdims=True)
    xc = x - mu
    var = jnp.mean(xc * xc, axis=-1, keepdims=True)
    return xc * lax.rsqrt(var + LN_EPS) * g + b


def _params(sem):
    return pltpu.CompilerParams(dimension_semantics=sem, vmem_limit_bytes=VMEM_LIMIT_BYTES)


def _mod_kernel(c_ref, w_ref, b_ref, o_ref):
    sc = _silu(c_ref[...])
    o_ref[0] = _mm(sc, w_ref[0]) + b_ref[0]


def _modulation(c_all, w_ada, b_ada):
    depth, d, n = w_ada.shape
    rows = c_all.shape[0]
    tn = 1536 if n % 1536 == 0 else n
    return pl.pallas_call(
        _mod_kernel,
        out_shape=jax.ShapeDtypeStruct((depth, rows, n), F32),
        grid=(depth, n // tn),
        in_specs=[
            pl.BlockSpec((rows, d), lambda l, j: (0, 0)),
            pl.BlockSpec((1, d, tn), lambda l, j: (l, 0, j)),
            pl.BlockSpec((1, 1, tn), lambda l, j: (l, 0, j)),
        ],
        out_specs=pl.BlockSpec((1, rows, tn), lambda l, j: (l, 0, j)),
        compiler_params=_params(("arbitrary", "arbitrary")),
        name="adaln_modulation",
    )(c_all, w_ada, b_ada.reshape(depth, 1, n))


def _causal_conv(u, win_ref, w, nb, tt, c0):
    c = u.shape[-1]
    cols = slice(c0, c0 + c)
    win_ref[:, SUBLANES:, cols] = u.reshape(nb, tt, c)
    out = u * w[CONV_W - 1:CONV_W, cols]
    for j in range(1, CONV_W):
        prev = win_ref[:, SUBLANES - j:SUBLANES - j + tt, cols].reshape(nb * tt, c)
        out = out + prev * w[CONV_W - 1 - j:CONV_W - j, cols]
    tail = win_ref[:, tt:tt + SUBLANES, cols]
    win_ref[:, 0:SUBLANES, cols] = tail
    return out, tail


def _l2norm_heads(x, scale):
    outs = []
    for h in range(DN_HEADS):
        xh = x[:, h * DN_DK:(h + 1) * DN_DK]
        ss = jnp.sum(xh * xh, axis=-1, keepdims=True)
        outs.append(xh * (lax.rsqrt(ss + NORM_EPS) * scale))
    return jnp.concatenate(outs, axis=-1)


def _inproj_stage(x_ref, mod_ref, hdn_ref, hlru_ref, w_ref, cwdn_ref, cwlru_ref, cblru_ref,
                  alog_ref, dtb_ref, wr_ref, wi_ref, br_ref, bi_ref, lam_ref,
                  tdn_ref, tlru_ref, wdn_sc, wlru_sc, out, *, nb, tt):
    d = x_ref.shape[-1]
    lw = hlru_ref.shape[-1]
    rows = nb * tt

    @pl.when(pl.program_id(1) == 0)
    def _():
        wdn_sc[:, 0:SUBLANES, :] = hdn_ref[...]
        wlru_sc[:, 0:SUBLANES, :] = hlru_ref[...]

    m = mod_ref[...]
    shift = m[:, :, 0:d]
    scale = m[:, :, d:2 * d]
    h = (x_ref[...] * (1.0 + scale) + shift).reshape(rows, d)
    proj = _mm(h, w_ref[...])
    z = proj[:, DN_QKV:DN_QKV + DN_WIDTH]
    u_lru = proj[:, DN_QKV + DN_WIDTH:DN_QKV + DN_WIDTH + lw]
    y = proj[:, DN_QKV + DN_WIDTH + lw:DN_QKV + DN_WIDTH + 2 * lw]
    ab = proj[:, DN_QKV + DN_WIDTH + 2 * lw:]

    cw_dn = cwdn_ref[...]

    def dn_part(part, norm_scale):
        c0 = part * DN_WIDTH
        conv, tail = _causal_conv(proj[:, c0:c0 + DN_WIDTH], wdn_sc, cw_dn, nb, tt, c0)
        tdn_ref[:, :, c0:c0 + DN_WIDTH] = tail
        act = _silu(conv)
        return act if norm_scale is None else _l2norm_heads(act, norm_scale)

    out["q"] = dn_part(0, DN_DK ** -0.5)
    yield
    out["k"] = dn_part(1, 1.0)
    lane = lax.broadcasted_iota(jnp.int32, ab.shape, 1)
    g_full = -jnp.exp(alog_ref[...]) * _softplus(ab + dtb_ref[...])
    out["gb"] = jnp.where(lane < DN_HEADS, g_full, _sigmoid(ab))
    yield
    out["v"] = dn_part(2, None)
    yield
    out["gz"] = _silu(z)
    out["gy"] = _gelu_tanh(y)
    yield

    conv_lru, tail_lru = _causal_conv(u_lru, wlru_sc, cwlru_ref[...], nb, tt, 0)
    tlru_ref[...] = tail_lru
    xc = conv_lru + cblru_ref[...]
    half = lw // 2
    r_pre = jnp.concatenate([_mm(xc[:, :half], wr_ref[0]), _mm(xc[:, half:], wr_ref[1])], axis=-1)
    i_pre = jnp.concatenate([_mm(xc[:, :half], wi_ref[0]), _mm(xc[:, half:], wi_ref[1])], axis=-1)
    r = _sigmoid(r_pre + br_ref[...])
    i = _sigmoid(i_pre + bi_ref[...])
    log_a = -LRU_C * r * _softplus(-lam_ref[...])
    out["a"] = jnp.exp(log_a)
    th = jnp.tanh(log_a)
    out["inp"] = jnp.sqrt(-2.0 * th / (1.0 - th)) * (i * xc)
    yield


def _unit_lower_inverses(ls, row, col, chunk, base):
    def same_block(s):
        k = s.bit_length() - 1
        return (row >> k) == (col >> k)

    eye = jnp.where(row == col, 1.0, 0.0)
    base_mask = same_block(base)
    powers = [jnp.where(base_mask, l, 0.0) for l in ls]
    invs = [eye - d for d in powers]
    p = 2
    while p < base:
        powers = [_mm(d, d) for d in powers]
        invs = [t + _mm(t, d) for t, d in zip(invs, powers)]
        p *= 2
        yield
    s = base
    while s < chunk:
        off_mask = same_block(2 * s) & jnp.logical_not(same_block(s))
        tmp = [_mm(jnp.where(off_mask, l, 0.0), t) for l, t in zip(ls, invs)]
        invs = [t - _mm(t, x) for t, x in zip(invs, tmp)]
        s *= 2
        yield
    return invs


def _mixer_stage(vals, x_ref, mod_ref, wout_ref, nw_ref, lng_ref, lnb_ref,
                 y_ref, sout_ref, hout_ref, *, nb, tt, chunk, alpha):
    d = x_ref.shape[-1]
    lw = hout_ref.shape[-1]
    rows = nb * tt
    grows = min(rows, MXU_DIM)
    n_groups = rows // grows
    seqs_per_group = max(grows // tt, 1)
    n_chunks = grows // chunk
    chunks_per_seq = tt // chunk if tt >= chunk else 1
    log_chunk = chunk.bit_length() - 1

    row = lax.broadcasted_iota(jnp.int32, (grows, grows), 0)
    col = lax.broadcasted_iota(jnp.int32, (grows, grows), 1)
    incl = ((row >> log_chunk) == (col >> log_chunk)) & (col <= row)
    strict = incl & (col < row)
    gb = vals["gb"]
    pieces = []
    rest = gb
    for _ in range(3):
        piece = rest.astype(BF16)
        pieces.append(piece)
        rest = rest - piece.astype(F32)
    split = jnp.concatenate(pieces, axis=-1)
    incl_b = jnp.where(incl, 1.0, 0.0).astype(BF16)
    chains = [(g, h) for g in range(n_groups) for h in range(DN_HEADS)]
    gcs, gc_ts = [], []
    for g in range(n_groups):
        sums = jnp.dot(incl_b, split[g * grows:(g + 1) * grows], preferred_element_type=F32)
        gc = sums[:, 0:LANES] + sums[:, LANES:2 * LANES] + sums[:, 2 * LANES:3 * LANES]
        gcs.append(gc)
        gc_ts.append(gc.T)

    rsl = lambda g: slice(g * grows, (g + 1) * grows)
    qs = [vals["q"][rsl(g), h * DN_DK:(h + 1) * DN_DK] for g, h in chains]
    ks = [vals["k"][rsl(g), h * DN_DK:(h + 1) * DN_DK] for g, h in chains]
    gcols = [gcs[g][:, h:h + 1] for g, h in chains]
    betas = [gb[rsl(g), DN_HEADS + h:DN_HEADS + h + 1] for g, h in chains]
    decays = [jnp.where(incl, jnp.exp(jnp.where(incl, gcols[i] - gc_ts[g][h:h + 1, :], 0.0)), 0.0)
              for i, (g, h) in enumerate(chains)]
    idx = range(len(chains))
    kbs = [ks[i] * betas[i] for i in idx]
    qk_kks = [lax.dot_general(jnp.concatenate([qs[i], kbs[i]], axis=0).astype(BF16), ks[i].astype(BF16),
                              (((1,), (1,)), ((), ())), preferred_element_type=F32) for i in idx]
    qks = [qk_kks[i][:grows] * decays[i] for i in idx]
    lmats = [jnp.where(strict, qk_kks[i][grows:] * decays[i], 0.0) for i in idx]
    yield
    tmats = yield from _unit_lower_inverses(lmats, row, col, chunk, min(INV_BASE_BLOCK, chunk))
    egcs = [jnp.exp(gcol) for gcol in gcols]
    vs = [vals["v"][rsl(g), h * DN_DV:(h + 1) * DN_DV] for g, h in chains]
    uws = [_mm(tmats[i], jnp.concatenate([vs[i] * betas[i], kbs[i] * egcs[i]], axis=-1)) for i in idx]
    us = [uw[:, :DN_DV] for uw in uws]
    ws = [uw[:, DN_DV:] for uw in uws]
    qes = [qs[i] * egcs[i] for i in idx]
    yield

    a = vals["a"]
    bacc = vals["inp"]
    gy = vals["gy"]
    t = lax.broadcasted_iota(jnp.int32, (rows, lw), 0) & (SUBLANES - 1)
    s = 1
    while s < SUBLANES:
        keep = t >= s
        a_prev = jnp.where(keep, pltpu.roll(a, s, 0), 1.0)
        b_prev = jnp.where(keep, pltpu.roll(bacc, s, 0), 0.0)
        bacc = a * b_prev + bacc
        a = a * a_prev
        s *= 2
    tiles = tt // SUBLANES
    a4 = a.reshape(nb, tiles, SUBLANES, lw)
    b4 = bacc.reshape(nb, tiles, SUBLANES, lw)
    carry = hout_ref[...]
    h_tiles = []
    for k in range(tiles):
        hk = a4[:, k] * carry + b4[:, k]
        carry = hk[:, SUBLANES - 1:SUBLANES, :]
        h_tiles.append(hk)
    hout_ref[...] = carry
    hs = h_tiles[0] if tiles == 1 else jnp.stack(h_tiles, axis=1)
    o_b = hs.reshape(rows, lw) * gy
    yield

    v_new_parts = [[] for _ in idx]
    o_inter_parts = [[] for _ in idx]
    states = [None for _ in idx]
    for c in range(n_chunks):
        lo, hi = c * chunk, (c + 1) * chunk
        seq_of = lambda g: g * seqs_per_group + c // chunks_per_seq
        if c % chunks_per_seq == 0:
            states = [sout_ref[seq_of(g), h] for g, h in chains]
        wqs = [_mm(jnp.concatenate([ws[i][lo:hi], qes[i][lo:hi]], axis=0), states[i]) for i in idx]
        v_news = [us[i][lo:hi] - wqs[i][:chunk] for i in idx]
        new_states = []
        for i in idx:
            v_new_parts[i].append(v_news[i])
            o_inter_parts[i].append(wqs[i][chunk:])
            g_last = gcols[i][hi - 1:hi, :]
            k_dec = ks[i][lo:hi] * jnp.exp(g_last - gcols[i][lo:hi])
            new_states.append(states[i] * jnp.exp(g_last) + lax.dot_general(
                k_dec.astype(BF16), v_news[i].astype(BF16), (((0,), (0,)), ((), ())),
                preferred_element_type=F32))
        states = new_states
        if (c + 1) % chunks_per_seq == 0:
            for i, (g, h) in enumerate(chains):
                sout_ref[seq_of(g), h] = states[i]
        yield
    o_groups = []
    for g in range(n_groups):
        o_heads = []
        for h in range(DN_HEADS):
            i = g * DN_HEADS + h
            o = (jnp.concatenate(o_inter_parts[i], axis=0)
                 + _mm(qks[i], jnp.concatenate(v_new_parts[i], axis=0)))
            ms = jnp.mean(o * o, axis=-1, keepdims=True)
            o_heads.append(o * lax.rsqrt(ms + NORM_EPS) * nw_ref[...]
                           * vals["gz"][rsl(g), h * DN_DV:(h + 1) * DN_DV])
        o_groups.append(jnp.concatenate(o_heads, axis=-1))
    o_a = o_groups[0] if n_groups == 1 else jnp.concatenate(o_groups, axis=0)

    mixed = _mm(jnp.concatenate([o_a, o_b], axis=-1), wout_ref[...]).reshape(nb, tt, d)
    gate = mod_ref[...][:, :, 2 * d:3 * d]
    y_ref[...] = _layer_norm(alpha * x_ref[...] + (1.0 + gate) * mixed, lng_ref[...], lnb_ref[...])


N_TOKEN_MIX_INPUTS = 21
N_TOKEN_MIX_OUTPUTS = 5


def _token_mix_kernel(*refs, n_cast, nb, tt, chunk, alpha):
    (x_ref, mod_ref, hdn_ref, hlru_ref, s0_ref, h0_ref, w_ref, cwdn_ref, cwlru_ref, cblru_ref, alog_ref, dtb_ref,
     wr_ref, wi_ref, br_ref, bi_ref, lam_ref, wout_ref, nw_ref, lng_ref, lnb_ref) = refs[:N_TOKEN_MIX_INPUTS]
    cast_in = refs[N_TOKEN_MIX_INPUTS:N_TOKEN_MIX_INPUTS + n_cast]
    outs = refs[N_TOKEN_MIX_INPUTS + n_cast:]
    y_ref, tdn_ref, tlru_ref, sout_ref, hout_ref = outs[:N_TOKEN_MIX_OUTPUTS]
    cast_out = outs[N_TOKEN_MIX_OUTPUTS:N_TOKEN_MIX_OUTPUTS + n_cast]
    wdn_sc, wlru_sc = outs[N_TOKEN_MIX_OUTPUTS + n_cast:]

    for src, dst in zip(cast_in, cast_out):
        dst[...] = src[...].astype(BF16)

    @pl.when(pl.program_id(1) == 0)
    def _():
        sout_ref[...] = s0_ref[0]
        hout_ref[...] = h0_ref[...]

    vals = {}
    stage1 = _inproj_stage(
        x_ref, mod_ref, hdn_ref, hlru_ref, w_ref, cwdn_ref, cwlru_ref, cblru_ref, alog_ref, dtb_ref,
        wr_ref, wi_ref, br_ref, bi_ref, lam_ref, tdn_ref, tlru_ref, wdn_sc, wlru_sc, vals, nb=nb, tt=tt)
    stage2 = _mixer_stage(vals, x_ref, mod_ref, wout_ref, nw_ref, lng_ref, lnb_ref,
                          y_ref, sout_ref, hout_ref, nb=nb, tt=tt, chunk=chunk, alpha=alpha)
    while "gb" not in vals:
        next(stage1)
    pending = [stage2, stage1]
    while pending:
        pending = [g for g in pending if next(g, StopIteration) is not StopIteration]


def _token_mix_call(x, mod, hist_dn, hist_lru, s_dn_all, layer, h0, lw_, cast=(), *, nb, tt, chunk, alpha):
    b, t, d = x.shape
    lw = hist_lru.shape[-1]
    grid = (b // nb, t // tt)
    steps = grid[0] * grid[1]
    slab = lambda a: pl.BlockSpec((a.shape[0] // steps, a.shape[1]), lambda i, j: (i * grid[1] + j, 0))
    seq_blk = lambda c: pl.BlockSpec((nb, tt, c), lambda i, j: (i, j, 0))
    per_seq = lambda r, c: pl.BlockSpec((nb, r, c), lambda i, j: (i, 0, 0))
    whole = lambda a: pl.BlockSpec(a.shape, lambda i, j: (0,) * a.ndim, pipeline_mode=pl.Buffered(1))
    state_in = pl.BlockSpec((1, nb, DN_HEADS, DN_DK, DN_DV), lambda i, j: (layer, i, 0, 0, 0))
    state_out = pl.BlockSpec((nb, DN_HEADS, DN_DK, DN_DV), lambda i, j: (i, 0, 0, 0))
    weights = [lw_[n] for n in ("w_cat", "dn_conv_w", "lru_conv_w", "lru_conv_b", "a_log", "dt_bias",
                                "w_r", "w_i", "b_r", "b_i", "lam", "w_out", "dn_norm_w", "ln1_g", "ln1_b")]
    return pl.pallas_call(
        functools.partial(_token_mix_kernel, n_cast=len(cast), nb=nb, tt=tt, chunk=chunk, alpha=alpha),
        out_shape=(
            jax.ShapeDtypeStruct((b, t, d), F32),
            jax.ShapeDtypeStruct((b, SUBLANES, DN_QKV), F32),
            jax.ShapeDtypeStruct((b, SUBLANES, lw), F32),
            jax.ShapeDtypeStruct((b, DN_HEADS, DN_DK, DN_DV), F32),
            jax.ShapeDtypeStruct((b, 1, lw), F32),
        ) + tuple(jax.ShapeDtypeStruct(a.shape, BF16) for a in cast),
        grid=grid,
        in_specs=[seq_blk(d), per_seq(1, N_MOD * d), per_seq(SUBLANES, DN_QKV), per_seq(SUBLANES, lw),
                  state_in, per_seq(1, lw)] + [whole(a) for a in weights] + [slab(a) for a in cast],
        out_specs=(seq_blk(d), per_seq(SUBLANES, DN_QKV), per_seq(SUBLANES, lw), state_out, per_seq(1, lw))
        + tuple(slab(a) for a in cast),
        scratch_shapes=[pltpu.VMEM((nb, SUBLANES + tt, DN_QKV), F32), pltpu.VMEM((nb, SUBLANES + tt, lw), F32)],
        compiler_params=_params(("arbitrary", "arbitrary")),
        name="token_mixer",
    )(x, mod, hist_dn, hist_lru, s_dn_all, h0, *weights, *cast)


def _ffn_kernel(x_ref, mod_ref, wg_ref, wu_ref, wd_ref, lng_ref, lnb_ref, *refs, alpha):
    n_cast = (len(refs) - 1) // 2
    y_ref = refs[n_cast]
    for src, dst in zip(refs[:n_cast], refs[n_cast + 1:]):
        dst[...] = src[...].astype(BF16)
    nb, tt, d = x_ref.shape
    m = mod_ref[...]
    h = (x_ref[...] * (1.0 + m[:, :, 4 * d:5 * d]) + m[:, :, 3 * d:4 * d]).reshape(nb * tt, d).astype(BF16)
    act = _silu(jnp.dot(h, wg_ref[...], preferred_element_type=F32)) * jnp.dot(
        h, wu_ref[...], preferred_element_type=F32)
    ff = _mm(act, wd_ref[...]).reshape(nb, tt, d)
    y_ref[...] = _layer_norm(alpha * x_ref[...] + (1.0 + m[:, :, 5 * d:6 * d]) * ff, lng_ref[...], lnb_ref[...])


def _ff_tile(ff):
    for n in (2, 4, 7, 8, 11, 14, 16, 22, 28):
        if ff % n == 0 and (ff // n) % LANES == 0 and ff // n <= 2048:
            return ff // n
    return ff


def _dense_ffn(x, mod, lw_, cast=(), *, nb, tt, alpha):
    b, t, d = x.shape
    nt = t // tt
    steps = (b // nb) * nt
    slab = lambda a: pl.BlockSpec((a.shape[0] // steps, a.shape[1]), lambda i: (i, 0))
    seq_blk = pl.BlockSpec((nb, tt, d), lambda i: (i // nt, i % nt, 0))
    whole = lambda a: pl.BlockSpec(a.shape, lambda i: (0,) * a.ndim)
    resident = lambda a: pl.BlockSpec(a.shape, lambda i: (0,) * a.ndim, pipeline_mode=pl.Buffered(1))
    return pl.pallas_call(
        functools.partial(_ffn_kernel, alpha=alpha),
        out_shape=(jax.ShapeDtypeStruct((b, t, d), F32),) + tuple(jax.ShapeDtypeStruct(a.shape, BF16) for a in cast),
        grid=(steps,),
        in_specs=[seq_blk, pl.BlockSpec((nb, 1, N_MOD * d), lambda i: (i // nt, 0, 0)),
                  resident(lw_["w_gate"]), resident(lw_["w_up"]), resident(lw_["w_down"]),
                  whole(lw_["ln2_g"]), whole(lw_["ln2_b"])] + [slab(a) for a in cast],
        out_specs=(seq_blk,) + tuple(slab(a) for a in cast),
        compiler_params=_params(("arbitrary",)),
        name="dense_ffn",
    )(x, mod, lw_["w_gate"], lw_["w_up"], lw_["w_down"], lw_["ln2_g"], lw_["ln2_b"], *cast)


MOE_CHUNK = 512
MOE_SLOT_TILE = 512
MOE_SUB = 128
MOE_COMBINE_BLOCK = 256
MOE_COMBINE_FANIN = 4
ROUTE_I1, ROUTE_I2, ROUTE_R1, ROUTE_R2, ROUTE_W1, ROUTE_W2 = range(6)


def _router_kernel(x_ref, mod_ref, wr_ref, base_ref, h_ref, meta_ref, meta_t_ref, blkbase_ref, cnt_ref, run_sc,
                   *, n_experts):
    nb, tt, d = x_ref.shape
    rows = nb * tt

    @pl.when(pl.program_id(0) == 0)
    def _():
        run_sc[...] = base_ref[...]

    m = mod_ref[...]
    h = (x_ref[...] * (1.0 + m[:, :, 4 * d:5 * d]) + m[:, :, 3 * d:4 * d]).reshape(rows, d)
    h_hi = h.astype(BF16)
    h_ref[...] = h_hi
    w = wr_ref[...]
    w_hi = w.astype(BF16)
    h_lo = (h - h_hi.astype(F32)).astype(BF16)
    w_lo = (w - w_hi.astype(F32)).astype(BF16)
    logits = (jnp.dot(h_hi, w_hi, preferred_element_type=F32) + jnp.dot(h_lo, w_hi, preferred_element_type=F32)
              + jnp.dot(h_hi, w_lo, preferred_element_type=F32))
    lane = lax.broadcasted_iota(jnp.int32, logits.shape, 1)
    neg = jnp.float32(-jnp.inf)
    lg = jnp.where(lane < n_experts, logits, neg)
    m1 = jnp.max(lg, axis=-1, keepdims=True)
    i1 = jnp.min(jnp.where(lg == m1, lane, LANES), axis=-1, keepdims=True)
    lg2 = jnp.where(lane == i1, neg, lg)
    m2 = jnp.max(lg2, axis=-1, keepdims=True)
    i2 = jnp.min(jnp.where(lg2 == m2, lane, LANES), axis=-1, keepdims=True)
    e2 = jnp.exp(m2 - m1)
    w1 = 1.0 / (1.0 + e2)
    w2 = e2 / (1.0 + e2)
    sel = jnp.where(lane == i1, 1.0, jnp.where(lane == i2, 1.0, 0.0))
    r = lax.broadcasted_iota(jnp.int32, (rows, rows), 0)
    c = lax.broadcasted_iota(jnp.int32, (rows, rows), 1)
    rank = _mm(jnp.where(c < r, 1.0, 0.0), sel) + run_sc[0:1, :]
    r1 = jnp.sum(jnp.where(lane == i1, rank, 0.0), axis=-1, keepdims=True)
    r2 = jnp.sum(jnp.where(lane == i2, rank, 0.0), axis=-1, keepdims=True)
    fields = (i1.astype(F32), i2.astype(F32), r1, r2, w1, w2)
    meta = jnp.zeros_like(logits)
    for k, v in enumerate(fields):
        meta = jnp.where(lane == k, v, meta)
    meta_ref[...] = meta
    meta_t_ref[...] = meta.T[0:SUBLANES, :]
    blkbase_ref[0] = run_sc[...]
    run_sc[...] = run_sc[...] + jnp.sum(sel, axis=0, keepdims=True)
    cnt_ref[...] = run_sc[...]


def _router(x, mod, w_router, base, *, nb, tt, n_experts):
    b, t, d = x.shape
    rows = nb * tt
    nt = t // tt
    nblk = (b // nb) * nt
    whole = lambda a: pl.BlockSpec(a.shape, lambda i: (0,) * a.ndim)
    return pl.pallas_call(
        functools.partial(_router_kernel, n_experts=n_experts),
        out_shape=(jax.ShapeDtypeStruct((nblk * rows, d), BF16),
                   jax.ShapeDtypeStruct((nblk * rows, LANES), F32),
                   jax.ShapeDtypeStruct((SUBLANES, nblk * rows), F32),
                   jax.ShapeDtypeStruct((nblk, SUBLANES, LANES), F32),
                   jax.ShapeDtypeStruct((SUBLANES, LANES), F32)),
        grid=(nblk,),
        in_specs=[pl.BlockSpec((nb, tt, d), lambda i: (i // nt, i % nt, 0)),
                  pl.BlockSpec((nb, 1, N_MOD * d), lambda i: (i // nt, 0, 0)),
                  whole(w_router), whole(base)],
        out_specs=(pl.BlockSpec((rows, d), lambda i: (i, 0)),
                   pl.BlockSpec((rows, LANES), lambda i: (i, 0)),
                   pl.BlockSpec((SUBLANES, rows), lambda i: (0, i)),
                   pl.BlockSpec((1, SUBLANES, LANES), lambda i: (i, 0, 0)),
                   pl.BlockSpec((SUBLANES, LANES), lambda i: (0, 0))),
        scratch_shapes=[pltpu.VMEM((SUBLANES, LANES), F32)],
        compiler_params=_params(("arbitrary",)),
        name="moe_router",
    )(x, mod, w_router, base)


def _slot_of(expert, rank, start_ref, n_experts):
    start = jnp.zeros_like(rank)
    for e in range(n_experts):
        start = jnp.where(expert == e, start_ref[e].astype(F32), start)
    return (start + rank).astype(jnp.int32)


def _gather_kernel(clo_ref, chi_ref, start_ref, *refs, group_chunks, n_experts):
    n_groups = len(group_chunks)
    m_refs = refs[:n_groups]
    h_refs = refs[n_groups:2 * n_groups]
    xs_ref, ws_ref, acc_sc, wacc_sc = refs[2 * n_groups:]
    g = pl.program_id(0)
    n_sub = MOE_SLOT_TILE // MOE_SUB
    for j in range(n_sub):
        q = g * n_sub + j
        ids = g * MOE_SLOT_TILE + j * MOE_SUB + lax.broadcasted_iota(jnp.int32, (MOE_SUB, MOE_CHUNK), 0)
        acc_sc[...] = jnp.zeros_like(acc_sc)
        wacc_sc[...] = jnp.zeros_like(wacc_sc)
        first = 0
        for m_ref, h_ref, n_chunks in zip(m_refs, h_refs, group_chunks):
            def body(c, carry, m_ref=m_ref, h_ref=h_ref):
                off = pl.multiple_of(c * MOE_CHUNK, MOE_CHUNK)
                rec = m_ref[:, pl.ds(off, MOE_CHUNK)]
                row = lambda k: rec[k:k + 1, :]
                hit1 = _slot_of(row(ROUTE_I1), row(ROUTE_R1), start_ref, n_experts) == ids
                hit2 = _slot_of(row(ROUTE_I2), row(ROUTE_R2), start_ref, n_experts) == ids
                p = jnp.where(hit1, 1.0, jnp.where(hit2, 1.0, 0.0)).astype(BF16)
                acc_sc[...] += jnp.dot(p, h_ref[pl.ds(off, MOE_CHUNK), :], preferred_element_type=F32)
                w = jnp.where(hit1, row(ROUTE_W1), 0.0) + jnp.where(hit2, row(ROUTE_W2), 0.0)
                wacc_sc[...] += jnp.broadcast_to(jnp.sum(w, axis=-1, keepdims=True), wacc_sc.shape)
                return carry

            lo = jnp.clip(clo_ref[q] - first, 0, n_chunks)
            hi = jnp.clip(chi_ref[q] - first, 0, n_chunks)
            lax.fori_loop(lo, hi, body, 0)
            first += n_chunks
        xs_ref[j * MOE_SUB:(j + 1) * MOE_SUB, :] = acc_sc[...].astype(BF16)
        ws_ref[j * MOE_SUB:(j + 1) * MOE_SUB, :] = wacc_sc[...]


def _gather_slots(c_lo, c_hi, run_start, metas_t, hs, n_tiles, n_experts):
    d = hs[0].shape[-1]
    vmem = pl.BlockSpec(memory_space=pltpu.VMEM)
    return pl.pallas_call(
        functools.partial(_gather_kernel, group_chunks=tuple(h.shape[0] // MOE_CHUNK for h in hs),
                          n_experts=n_experts),
        out_shape=(jax.ShapeDtypeStruct((n_tiles * MOE_SLOT_TILE, d), BF16),
                   jax.ShapeDtypeStruct((n_tiles * MOE_SLOT_TILE, LANES), F32)),
        grid_spec=pltpu.PrefetchScalarGridSpec(
            num_scalar_prefetch=3, grid=(n_tiles,),
            in_specs=[vmem] * (2 * len(hs)),
            out_specs=(pl.BlockSpec((MOE_SLOT_TILE, d), lambda g, lo, hi, st: (g, 0)),
                       pl.BlockSpec((MOE_SLOT_TILE, LANES), lambda g, lo, hi, st: (g, 0))),
            scratch_shapes=[pltpu.VMEM((MOE_SUB, d), F32), pltpu.VMEM((MOE_SUB, LANES), F32)]),
        compiler_params=_params(("arbitrary",)),
        name="moe_gather",
    )(c_lo, c_hi, run_start, *metas_t, *hs)


def _expert_kernel(te_ref, tv_ref, xs_ref, ws_ref, wg_ref, wu_ref, wd_ref, o_ref, *, ff_chunk):
    g = pl.program_id(0)

    @pl.when(tv_ref[g] != 0)
    def _():
        x = xs_ref[...]
        ff = wg_ref.shape[-1]
        acc = jnp.zeros(o_ref.shape, F32)
        for f0 in range(0, ff, ff_chunk):
            gate = jnp.dot(x, wg_ref[0, :, f0:f0 + ff_chunk], preferred_element_type=F32)
            up = jnp.dot(x, wu_ref[0, :, f0:f0 + ff_chunk], preferred_element_type=F32)
            acc = acc + _mm(_silu(gate) * up, wd_ref[0, f0:f0 + ff_chunk, :])
        o_ref[...] = (ws_ref[:, 0:1] * acc).astype(BF16)

    @pl.when(tv_ref[g] == 0)
    def _():
        o_ref[...] = jnp.zeros_like(o_ref)


def _expert_ffn(tile_expert, tile_valid, xs, ws, lw_):
    s_total, d = xs.shape
    _, _, ff = lw_["w_gate"].shape
    n_tiles = s_total // MOE_SLOT_TILE
    once = pl.Buffered(1)
    twice = pl.Buffered(2)
    return pl.pallas_call(
        functools.partial(_expert_kernel, ff_chunk=_ff_tile(ff)),
        out_shape=jax.ShapeDtypeStruct((s_total, d), BF16),
        grid_spec=pltpu.PrefetchScalarGridSpec(
            num_scalar_prefetch=2, grid=(n_tiles,),
            in_specs=[pl.BlockSpec((MOE_SLOT_TILE, d), lambda g, te, tv: (g, 0)),
                      pl.BlockSpec((MOE_SLOT_TILE, LANES), lambda g, te, tv: (g, 0)),
                      pl.BlockSpec((1, d, ff), lambda g, te, tv: (te[g], 0, 0), pipeline_mode=twice),
                      pl.BlockSpec((1, d, ff), lambda g, te, tv: (te[g], 0, 0), pipeline_mode=twice),
                      pl.BlockSpec((1, ff, d), lambda g, te, tv: (te[g], 0, 0), pipeline_mode=once)],
            out_specs=pl.BlockSpec((MOE_SLOT_TILE, d), lambda g, te, tv: (g, 0))),
        compiler_params=_params(("arbitrary",)),
        name="expert_ffn",
    )(tile_expert, tile_valid, xs, ws, lw_["w_gate"], lw_["w_up"], lw_["w_down"])


def _combine_kernel(ic_ref, ik_ref, ie_ref, if_ref, bf_ref, start_ref, x_ref, mod_ref, meta_ref, *refs, alpha,
                    n_experts):
    os_refs = refs[:MOE_COMBINE_FANIN]
    lng_ref, lnb_ref, y_ref, acc_sc, slot_sc = refs[MOE_COMBINE_FANIN:]
    nb, tt, d = x_ref.shape
    rows = nb * tt
    w = pl.program_id(0)
    flags = if_ref[w]

    @pl.when((flags & 1) != 0)
    def _():
        rec = meta_ref[...]
        col = lambda k: rec[:, k:k + 1]
        slot_sc[:, 0:1] = _slot_of(col(ROUTE_I1), col(ROUTE_R1), start_ref, n_experts)
        slot_sc[:, 1:2] = _slot_of(col(ROUTE_I2), col(ROUTE_R2), start_ref, n_experts)

    lane = lax.broadcasted_iota(jnp.int32, (rows, MOE_COMBINE_BLOCK), 1)
    s1 = slot_sc[:, 0:1]
    s2 = slot_sc[:, 1:2]
    total = None
    for j, os_ref in enumerate(os_refs):
        ids = ik_ref[w * MOE_COMBINE_FANIN + j] + lane
        ids = jnp.where(ids < ie_ref[w * MOE_COMBINE_FANIN + j], ids, -1)
        q = jnp.where(s1 == ids, 1.0, jnp.where(s2 == ids, 1.0, 0.0)).astype(BF16)
        part = jnp.dot(q, os_ref[...], preferred_element_type=F32)
        total = part if total is None else total + part

    @pl.when((flags & 1) != 0)
    def _():
        acc_sc[...] = total

    @pl.when((flags & 1) == 0)
    def _():
        acc_sc[...] += total

    @pl.when((flags & 2) != 0)
    def _():
        gate = mod_ref[...][:, :, 5 * d:6 * d]
        ff = acc_sc[...].reshape(nb, tt, d)
        y_ref[...] = _layer_norm(alpha * x_ref[...] + (1.0 + gate) * ff, lng_ref[...], lnb_ref[...])


def _combine(items, run_start, x, mod, meta, out_sorted, lw_, *, nb, tt, alpha, n_experts):
    step_chunk, win_start, win_end, step_flags, win_fetch = items
    b, t, d = x.shape
    rows = nb * tt
    nt = t // tt
    whole = lambda a: pl.BlockSpec(a.shape, lambda w, ic, ik, ie, fl, bf, st: (0,) * a.ndim)
    seq_blk = pl.BlockSpec((nb, tt, d), lambda w, ic, ik, ie, fl, bf, st: (ic[w] // nt, ic[w] % nt, 0))
    slot_blk = lambda j: pl.BlockSpec(
        (pl.Element(MOE_COMBINE_BLOCK), pl.Element(d)),
        lambda w, ic, ik, ie, fl, bf, st: (pl.multiple_of(bf[w * MOE_COMBINE_FANIN + j], 2 * SUBLANES), 0))
    return pl.pallas_call(
        functools.partial(_combine_kernel, alpha=alpha, n_experts=n_experts),
        out_shape=jax.ShapeDtypeStruct((b, t, d), F32),
        grid_spec=pltpu.PrefetchScalarGridSpec(
            num_scalar_prefetch=6, grid=(step_chunk.shape[0],),
            in_specs=[seq_blk,
                      pl.BlockSpec((nb, 1, N_MOD * d), lambda w, ic, ik, ie, fl, bf, st: (ic[w] // nt, 0, 0)),
                      pl.BlockSpec((rows, LANES), lambda w, ic, ik, ie, fl, bf, st: (ic[w], 0))]
            + [slot_blk(j) for j in range(MOE_COMBINE_FANIN)]
            + [whole(lw_["ln2_g"]), whole(lw_["ln2_b"])],
            out_specs=seq_blk,
            scratch_shapes=[pltpu.VMEM((rows, d), F32), pltpu.VMEM((rows, LANES), jnp.int32)]),
        compiler_params=_params(("arbitrary",)),
        name="moe_combine",
    )(step_chunk, win_start, win_end, step_flags, win_fetch, run_start, x, mod, meta,
      *([out_sorted] * MOE_COMBINE_FANIN), lw_["ln2_g"], lw_["ln2_b"])


def _count_le(sorted_vals, queries):
    return jnp.sum(sorted_vals[None, :] <= queries[:, None], axis=1).astype(jnp.int32)


def _combine_steps(lo, hi, n_steps):
    n_chunks, n_experts = lo.shape
    fan = MOE_COMBINE_FANIN
    align = 2 * SUBLANES
    lo_f, hi_f = lo.reshape(-1), hi.reshape(-1)
    first = (lo_f // align) * align
    count = jnp.where(hi_f > lo_f, (hi_f - first + MOE_COMBINE_BLOCK - 1) // MOE_COMBINE_BLOCK, 0)
    pair_end = jnp.cumsum(count)
    chunk_items = jnp.sum(count.reshape(n_chunks, n_experts), axis=1)
    chunk_item0 = jnp.cumsum(chunk_items) - chunk_items
    chunk_steps = (chunk_items + fan - 1) // fan
    step_end = jnp.cumsum(chunk_steps)
    total_steps = step_end[-1]
    s = jnp.arange(n_steps, dtype=jnp.int32)
    live = s < total_steps
    chunk = jnp.minimum(_count_le(step_end, s), n_chunks - 1)
    chunk = jnp.where(live, chunk, chunk[jnp.maximum(total_steps - 1, 0)])
    q = s - (step_end[chunk] - chunk_steps[chunk])
    j = q[:, None] * fan + jnp.arange(fan, dtype=jnp.int32)[None, :]
    used = live[:, None] & (j < chunk_items[chunk][:, None])
    item = jnp.where(used, chunk_item0[chunk][:, None] + j, 0).reshape(-1)
    pair = jnp.minimum(_count_le(pair_end, item), n_chunks * n_experts - 1)
    start = first[pair] + (item - (pair_end[pair] - count[pair])) * MOE_COMBINE_BLOCK
    fetch = jnp.where(used.reshape(-1), start, jnp.repeat(start.reshape(-1, fan)[:, 0], fan))
    end = jnp.where(used.reshape(-1), hi_f[pair], 0)
    flags = (jnp.where(live & (q == 0), 1, 0) + jnp.where(live & (q == chunk_steps[chunk] - 1), 2, 0))
    as_i32 = lambda a: a.astype(jnp.int32)
    return as_i32(chunk), as_i32(start), as_i32(end), as_i32(flags), as_i32(fetch)


def _moe_layer(xs_in, mods, lw_, tilings, alpha):
    n_experts = lw_["w_gate"].shape[0]
    base = jnp.zeros((SUBLANES, LANES), F32)
    hs, metas, metas_t, bases = [], [], [], []
    for x, mod, (nb, tt) in zip(xs_in, mods, tilings):
        assert nb * tt == MOE_CHUNK and (x.shape[0] * x.shape[1]) % MOE_CHUNK == 0
        h, meta, meta_t, blkbase, base = _router(x, mod, lw_["w_router"], base, nb=nb, tt=tt,
                                                 n_experts=n_experts)
        hs.append(h)
        metas.append(meta)
        metas_t.append(meta_t)
        bases.append(blkbase[:, 0, :n_experts])
    n = sum(m.shape[0] for m in metas)
    cum = jnp.concatenate(bases + [base[0:1, :n_experts]], axis=0).astype(jnp.int32)
    counts = cum[-1]
    sizes = ((counts + MOE_SLOT_TILE - 1) // MOE_SLOT_TILE) * MOE_SLOT_TILE
    run_end = jnp.cumsum(sizes)
    run_start = (run_end - sizes).astype(jnp.int32)

    n_tiles = (TOP_K * n + MOE_SLOT_TILE - 1) // MOE_SLOT_TILE + n_experts + 1
    tile_start = jnp.arange(n_tiles, dtype=jnp.int32) * MOE_SLOT_TILE
    tile_expert = jnp.minimum(_count_le(run_end, tile_start), n_experts - 1)
    tile_valid = (tile_start < run_end[-1]).astype(jnp.int32)
    n_sub = MOE_SLOT_TILE // MOE_SUB
    sub_expert = jnp.repeat(tile_expert, n_sub)
    sub_rank0 = jnp.arange(n_tiles * n_sub, dtype=jnp.int32) * MOE_SUB - run_start[sub_expert]
    cum_sub = cum[:, sub_expert]
    sub_valid = jnp.repeat(tile_valid, n_sub)
    c_lo = jnp.sum(cum_sub[1:] <= sub_rank0[None, :], axis=0).astype(jnp.int32) * sub_valid
    c_hi = jnp.sum(cum_sub[:-1] < sub_rank0[None, :] + MOE_SUB, axis=0).astype(jnp.int32) * sub_valid

    x_sorted, w_sorted = _gather_slots(c_lo, c_hi, run_start, metas_t, hs, n_tiles, n_experts)
    out_sorted = _expert_ffn(tile_expert, tile_valid, x_sorted, w_sorted, lw_)

    outs = []
    chunk0 = 0
    for x, mod, meta, (nb, tt) in zip(xs_in, mods, metas, tilings):
        nc = x.shape[0] * x.shape[1] // MOE_CHUNK
        lo = run_start[None, :] + cum[chunk0:chunk0 + nc]
        hi = run_start[None, :] + cum[chunk0 + 1:chunk0 + nc + 1]
        max_items = nc * n_experts + (TOP_K * nc * MOE_CHUNK) // MOE_COMBINE_BLOCK + 2 * n_experts
        items = _combine_steps(lo, hi, max_items // MOE_COMBINE_FANIN + nc)
        outs.append(_combine(items, run_start, x, mod, meta, out_sorted, lw_, nb=nb, tt=tt, alpha=alpha,
                             n_experts=n_experts))
        chunk0 += nc
    return outs


def _pad_lanes(v, width=LANES):
    return jnp.pad(v, ((0, 0), (0, width - v.shape[-1])))


def _block_diag_halves(w):
    nblk, c, _ = w.shape
    half = nblk // 2
    out = jnp.zeros((2, half * c, half * c), w.dtype)
    for i in range(nblk):
        j = i % half
        out = out.at[i // half, j * c:(j + 1) * c, j * c:(j + 1) * c].set(w[i])
    return out


def _layer_weights(l, p, d, lw):
    w_in = p["w_in"][l]
    a_off = DN_QKV
    z_off = a_off + 2 * DN_HEADS
    x_off = z_off + DN_WIDTH
    y_off = x_off + lw
    w_ab = _pad_lanes(w_in[:, a_off:z_off])
    w_cat = jnp.concatenate([w_in[:, :DN_QKV], w_in[:, z_off:y_off + lw], w_ab], axis=1).astype(BF16)
    out = {
        "w_cat": w_cat,
        "dn_conv_w": p["dn_conv_w"][l],
        "lru_conv_w": p["lru_conv_w"][l],
        "lru_conv_b": p["lru_conv_b"][l][None],
        "a_log": _pad_lanes(p["dn_a_log"][l][None]),
        "dt_bias": _pad_lanes(p["dn_dt_bias"][l][None]),
        "w_r": _block_diag_halves(p["lru_w_r"][l]).astype(BF16),
        "w_i": _block_diag_halves(p["lru_w_i"][l]).astype(BF16),
        "b_r": p["lru_b_r"][l][None],
        "b_i": p["lru_b_i"][l][None],
        "lam": p["lru_lambda"][l][None],
        "w_out": p["w_out"][l].astype(BF16),
        "dn_norm_w": p["dn_norm_w"][l][None],
        "ln1_g": p["ln1_g"][l][None],
        "ln1_b": p["ln1_b"][l][None],
        "ln2_g": p["ln2_g"][l][None],
        "ln2_b": p["ln2_b"][l][None],
    }
    j = l // 2
    if l % 2 == 0:
        out.update(w_gate=p["ffn_w_gate"][j], w_up=p["ffn_w_up"][j], w_down=p["ffn_w_down"][j])
    else:
        out.update(w_router=_pad_lanes(p["moe_w_router"][j]),
                   w_gate=p["moe_w_gate"][j], w_up=p["moe_w_up"][j], w_down=p["moe_w_down"][j])
    return out


CHANNEL_MIX_WEIGHTS = ("w_gate", "w_up", "w_down")


LARGE_SLAB_BYTES = 64 * 1024 * 1024


def _cast_plan(weights, host_steps):
    hosts = list(host_steps)
    plan = {h: [] for h in hosts}
    for k in reversed(range(len(weights))):
        dense = ("ffn", k) in host_steps
        last = hosts.index(("ffn", k)) if dense else hosts.index(("mixer", k)) + 1
        for n in CHANNEL_MIX_WEIGHTS:
            a = weights[k][n].reshape(-1, weights[k][n].shape[-1])
            large = a.size * 4 > LARGE_SLAB_BYTES
            for h in reversed(hosts[:last]):
                fits = a.shape[0] % (host_steps[h] * 2 * SUBLANES) == 0
                busy = large and any(s.size * 4 > LARGE_SLAB_BYTES for _, _, s in plan[h])
                if fits and not busy:
                    plan[h].append((k, n, a))
                    break
    return plan


def _tiling(b, t):
    if t >= MXU_DIM:
        tt = MXU_DIM
        return dict(mixer=(2 if b % 2 == 0 else 1, tt, min(PROMPT_CHUNK, tt)), ffn=(1, min(t, 2 * MXU_DIM)))
    assert t == SUBLANES, "short sequences must be exactly one sublane tile long"
    return dict(mixer=(min(b, 16), t, t), ffn=(min(b, 64), t))


def _token_mix(x, mod, conv_dn, s_dn_all, layer, conv_lru, s_lru, lw_, til, alpha, cast=()):
    pad_hist = lambda c: jnp.pad(c, ((0, 0), (SUBLANES - (CONV_W - 1), 0), (0, 0)))
    nb, tt, chunk = til["mixer"]
    x, tail_dn, tail_lru, s_new, h_new, *converted = _token_mix_call(
        x, mod, pad_hist(conv_dn), pad_hist(conv_lru), s_dn_all, layer, s_lru[:, None, :], lw_, cast,
        nb=nb, tt=tt, chunk=chunk, alpha=alpha)
    return (x, tail_dn[:, SUBLANES - (CONV_W - 1):, :], s_new, tail_lru[:, SUBLANES - (CONV_W - 1):, :],
            h_new[:, 0, :]), converted


def kernel(x_prompt, x_sample, cache_dn_conv, state_dn, cache_lru_conv, state_lru, c_prompt, c_sample,
           w_ada, b_ada, w_in, dn_conv_w, dn_a_log, dn_dt_bias, dn_norm_w,
           lru_conv_w, lru_conv_b, lru_w_r, lru_b_r, lru_w_i, lru_b_i, lru_lambda, w_out,
           ln1_g, ln1_b, ln2_g, ln2_b, ffn_w_gate, ffn_w_up, ffn_w_down,
           moe_w_router, moe_w_gate, moe_w_up, moe_w_down):
    p = dict(w_in=w_in, dn_conv_w=dn_conv_w, dn_a_log=dn_a_log, dn_dt_bias=dn_dt_bias, dn_norm_w=dn_norm_w,
             lru_conv_w=lru_conv_w, lru_conv_b=lru_conv_b, lru_w_r=lru_w_r, lru_b_r=lru_b_r,
             lru_w_i=lru_w_i, lru_b_i=lru_b_i, lru_lambda=lru_lambda, w_out=w_out,
             ln1_g=ln1_g, ln1_b=ln1_b, ln2_g=ln2_g, ln2_b=ln2_b,
             ffn_w_gate=ffn_w_gate, ffn_w_up=ffn_w_up, ffn_w_down=ffn_w_down,
             moe_w_router=moe_w_router, moe_w_gate=moe_w_gate, moe_w_up=moe_w_up, moe_w_down=moe_w_down)
    depth, d, _ = w_ada.shape
    bp = x_prompt.shape[0]
    bs = x_sample.shape[0]
    lw = cache_lru_conv.shape[-1]
    alpha = (2 * depth) ** 0.25
    weights = [_layer_weights(l, p, d, lw) for l in range(depth)]

    c_all = jnp.concatenate([c_prompt, c_sample], axis=0)
    mod_all = _modulation(c_all, w_ada, b_ada)
    groups = [
        dict(x=x_prompt, rows=slice(0, bp), conv_dn=jnp.zeros((depth, bp, CONV_W - 1, DN_QKV), F32),
             s_dn=jnp.zeros((depth, bp, DN_HEADS, DN_DK, DN_DV), F32),
             conv_lru=jnp.zeros((depth, bp, CONV_W - 1, lw), F32), s_lru=jnp.zeros((depth, bp, lw), F32)),
        dict(x=x_sample, rows=slice(bp, bp + bs), conv_dn=cache_dn_conv, s_dn=state_dn,
             conv_lru=cache_lru_conv, s_lru=state_lru),
    ]
    for g in groups:
        g["til"] = _tiling(g["x"].shape[0], g["x"].shape[1])
        g["new"] = [[], [], [], []]
    t_p = x_prompt.shape[1]
    nb_m, tt_m, _ = groups[0]["til"]["mixer"]
    nb_f, tt_f = groups[0]["til"]["ffn"]
    host_steps = {}
    for l in range(depth):
        host_steps[("mixer", l)] = (bp // nb_m) * (t_p // tt_m)
        if l % 2 == 0:
            host_steps[("ffn", l)] = (bp // nb_f) * (t_p // tt_f)
    plan = _cast_plan(weights, host_steps)
    planned = {(k, n) for slabs in plan.values() for k, n, _ in slabs}
    for k, w in enumerate(weights):
        for n in CHANNEL_MIX_WEIGHTS:
            if (k, n) not in planned:
                w[n] = w[n].astype(BF16)

    def adopt(slabs, converted):
        for (k, n, _), c in zip(slabs, converted):
            weights[k][n] = c.reshape(weights[k][n].shape)

    for l in range(depth):
        lw_ = weights[l]
        mods = [mod_all[l, g["rows"]][:, None, :] for g in groups]
        for gi, (g, mod) in enumerate(zip(groups, mods)):
            slabs = plan[("mixer", l)] if gi == 0 else []
            res, converted = _token_mix(g["x"], mod, g["conv_dn"][l], g["s_dn"], l, g["conv_lru"][l],
                                        g["s_lru"][l], lw_, g["til"], alpha,
                                        cast=tuple(a for _, _, a in slabs))
            adopt(slabs, converted)
            g["x"] = res[0]
            for acc, new in zip(g["new"], res[1:]):
                acc.append(new)
        if l % 2 == 0:
            for gi, (g, mod) in enumerate(zip(groups, mods)):
                nb, tt = g["til"]["ffn"]
                slabs = plan[("ffn", l)] if gi == 0 else []
                g["x"], *converted = _dense_ffn(g["x"], mod, lw_, tuple(a for _, _, a in slabs),
                                                nb=nb, tt=tt, alpha=alpha)
                adopt(slabs, converted)
        else:
            xs = _moe_layer([g["x"] for g in groups], mods, lw_, [g["til"]["ffn"] for g in groups], alpha)
            for g, x in zip(groups, xs):
                g["x"] = x
    states = [jnp.stack(acc) for g in groups for acc in g["new"]]
    return (groups[0]["x"], groups[1]["x"]) + tuple(states)
```

```python
import functools
import math

import jax
import jax.numpy as jnp
from jax import lax
from jax.experimental import pallas as pl
from jax.experimental.pallas import tpu as pltpu

F32 = jnp.float32
BF16 = jnp.bfloat16

DN_HEADS = 4
DN_DK = 128
DN_DV = 128
DN_WIDTH = DN_HEADS * DN_DV
DN_QKV = 3 * DN_WIDTH
LRU_BLOCKS = 8
LRU_C = 8.0
CONV_W = 4
N_MOD = 6
TOP_K = 2
LN_EPS = 1e-5
NORM_EPS = 1e-6

SUBLANES = 8
LANES = 128
MXU_DIM = 256
VMEM_LIMIT_BYTES = 56 * 1024 * 1024

PROMPT_CHUNK = 64
INV_BASE_BLOCK = 16


def _sigmoid(x):
    return 0.5 + 0.5 * jnp.tanh(0.5 * x)


def _silu(x):
    half = 0.5 * x
    return half + half * jnp.tanh(half)


def _softplus(x):
    return jnp.maximum(x, 0.0) + jnp.log1p(jnp.exp(-jnp.abs(x)))


def _gelu_tanh(x):
    return 0.5 * x * (1.0 + jnp.tanh(math.sqrt(2.0 / math.pi) * (x + 0.044715 * (x * x * x))))


def _mm(a, b):
    return jnp.dot(a.astype(BF16), b.astype(BF16), preferred_element_type=F32)


def _layer_norm(x, g, b):
    mu = jnp.mean(x, axis=-1, keepdims=True)
    xc = x - mu
    var = jnp.mean(xc * xc, axis=-1, keepdims=True)
    return xc * lax.rsqrt(var + LN_EPS) * g + b


def _params(sem):
    return pltpu.CompilerParams(dimension_semantics=sem, vmem_limit_bytes=VMEM_LIMIT_BYTES)


def _mod_kernel(c_ref, w_ref, b_ref, o_ref):
    sc = _silu(c_ref[...])
    o_ref[0] = _mm(sc, w_ref[0]) + b_ref[0]


def _modulation(c_all, w_ada, b_ada):
    depth, d, n = w_ada.shape
    rows = c_all.shape[0]
    tn = 1536 if n % 1536 == 0 else n
    return pl.pallas_call(
        _mod_kernel,
        out_shape=jax.ShapeDtypeStruct((depth, rows, n), F32),
        grid=(depth, n // tn),
        in_specs=[
            pl.BlockSpec((rows, d), lambda l, j: (0, 0)),
            pl.BlockSpec((1, d, tn), lambda l, j: (l, 0, j)),
            pl.BlockSpec((1, 1, tn), lambda l, j: (l, 0, j)),
        ],
        out_specs=pl.BlockSpec((1, rows, tn), lambda l, j: (l, 0, j)),
        compiler_params=_params(("arbitrary", "arbitrary")),
        name="adaln_modulation",
    )(c_all, w_ada, b_ada.reshape(depth, 1, n))


def _causal_conv(u, win_ref, w, nb, tt, c0):
    c = u.shape[-1]
    cols = slice(c0, c0 + c)
    win_ref[:, SUBLANES:, cols] = u.reshape(nb, tt, c)
    out = u * w[CONV_W - 1:CONV_W, cols]
    for j in range(1, CONV_W):
        prev = win_ref[:, SUBLANES - j:SUBLANES - j + tt, cols].reshape(nb * tt, c)
        out = out + prev * w[CONV_W - 1 - j:CONV_W - j, cols]
    tail = win_ref[:, tt:tt + SUBLANES, cols]
    win_ref[:, 0:SUBLANES, cols] = tail
    return out, tail


def _l2norm_heads(x, scale):
    outs = []
    for h in range(DN_HEADS):
        xh = x[:, h * DN_DK:(h + 1) * DN_DK]
        ss = jnp.sum(xh * xh, axis=-1, keepdims=True)
        outs.append(xh * (lax.rsqrt(ss + NORM_EPS) * scale))
    return jnp.concatenate(outs, axis=-1)


def _inproj_stage(x_ref, mod_ref, hdn_ref, hlru_ref, w_ref, cwdn_ref, cwlru_ref, cblru_ref,
                  alog_ref, dtb_ref, wr_ref, wi_ref, br_ref, bi_ref, lam_ref,
                  tdn_ref, tlru_ref, wdn_sc, wlru_sc, out, *, nb, tt):
    d = x_ref.shape[-1]
    lw = hlru_ref.shape[-1]
    rows = nb * tt

    @pl.when(pl.program_id(1) == 0)
    def _():
        wdn_sc[:, 0:SUBLANES, :] = hdn_ref[...]
        wlru_sc[:, 0:SUBLANES, :] = hlru_ref[...]

    m = mod_ref[...]
    shift = m[:, :, 0:d]
    scale = m[:, :, d:2 * d]
    h = (x_ref[...] * (1.0 + scale) + shift).reshape(rows, d)
    proj = _mm(h, w_ref[...])
    z = proj[:, DN_QKV:DN_QKV + DN_WIDTH]
    u_lru = proj[:, DN_QKV + DN_WIDTH:DN_QKV + DN_WIDTH + lw]
    y = proj[:, DN_QKV + DN_WIDTH + lw:DN_QKV + DN_WIDTH + 2 * lw]
    ab = proj[:, DN_QKV + DN_WIDTH + 2 * lw:]

    cw_dn = cwdn_ref[...]

    def dn_part(part, norm_scale):
        c0 = part * DN_WIDTH
        conv, tail = _causal_conv(proj[:, c0:c0 + DN_WIDTH], wdn_sc, cw_dn, nb, tt, c0)
        tdn_ref[:, :, c0:c0 + DN_WIDTH] = tail
        act = _silu(conv)
        return act if norm_scale is None else _l2norm_heads(act, norm_scale)

    out["q"] = dn_part(0, DN_DK ** -0.5)
    yield
    out["k"] = dn_part(1, 1.0)
    lane = lax.broadcasted_iota(jnp.int32, ab.shape, 1)
    g_full = -jnp.exp(alog_ref[...]) * _softplus(ab + dtb_ref[...])
    out["gb"] = jnp.where(lane < DN_HEADS, g_full, _sigmoid(ab))
    yield
    out["v"] = dn_part(2, None)
    yield
    out["gz"] = _silu(z)
    out["gy"] = _gelu_tanh(y)
    yield

    conv_lru, tail_lru = _causal_conv(u_lru, wlru_sc, cwlru_ref[...], nb, tt, 0)
    tlru_ref[...] = tail_lru
    xc = conv_lru + cblru_ref[...]
    half = lw // 2
    r_pre = jnp.concatenate([_mm(xc[:, :half], wr_ref[0]), _mm(xc[:, half:], wr_ref[1])], axis=-1)
    i_pre = jnp.concatenate([_mm(xc[:, :half], wi_ref[0]), _mm(xc[:, half:], wi_ref[1])], axis=-1)
    r = _sigmoid(r_pre + br_ref[...])
    i = _sigmoid(i_pre + bi_ref[...])
    log_a = -LRU_C * r * _softplus(-lam_ref[...])
    out["a"] = jnp.exp(log_a)
    th = jnp.tanh(log_a)
    out["inp"] = jnp.sqrt(-2.0 * th / (1.0 - th)) * (i * xc)
    yield


def _unit_lower_inverses(ls, row, col, chunk, base):
    def same_block(s):
        k = s.bit_length() - 1
        return (row >> k) == (col >> k)

    eye = jnp.where(row == col, 1.0, 0.0)
    base_mask = same_block(base)
    powers = [jnp.where(base_mask, l, 0.0) for l in ls]
    invs = [eye - d for d in powers]
    p = 2
    while p < base:
        powers = [_mm(d, d) for d in powers]
        invs = [t + _mm(t, d) for t, d in zip(invs, powers)]
        p *= 2
        yield
    s = base
    while s < chunk:
        off_mask = same_block(2 * s) & jnp.logical_not(same_block(s))
        tmp = [_mm(jnp.where(off_mask, l, 0.0), t) for l, t in zip(ls, invs)]
        invs = [t - _mm(t, x) for t, x in zip(invs, tmp)]
        s *= 2
        yield
    return invs


def _mixer_stage(vals, x_ref, mod_ref, wout_ref, nw_ref, lng_ref, lnb_ref,
                 y_ref, sout_ref, hout_ref, *, nb, tt, chunk, alpha):
    d = x_ref.shape[-1]
    lw = hout_ref.shape[-1]
    rows = nb * tt
    grows = min(rows, MXU_DIM)
    n_groups = rows // grows
    seqs_per_group = max(grows // tt, 1)
    n_chunks = grows // chunk
    chunks_per_seq = tt // chunk if tt >= chunk else 1
    log_chunk = chunk.bit_length() - 1

    row = lax.broadcasted_iota(jnp.int32, (grows, grows), 0)
    col = lax.broadcasted_iota(jnp.int32, (grows, grows), 1)
    incl = ((row >> log_chunk) == (col >> log_chunk)) & (col <= row)
    strict = incl & (col < row)
    gb = vals["gb"]
    pieces = []
    rest = gb
    for _ in range(3):
        piece = rest.astype(BF16)
        pieces.append(piece)
        rest = rest - piece.astype(F32)
    split = jnp.concatenate(pieces, axis=-1)
    incl_b = jnp.where(incl, 1.0, 0.0).astype(BF16)
    chains = [(g, h) for g in range(n_groups) for h in range(DN_HEADS)]
    gcs, gc_ts = [], []
    for g in range(n_groups):
        sums = jnp.dot(incl_b, split[g * grows:(g + 1) * grows], preferred_element_type=F32)
        gc = sums[:, 0:LANES] + sums[:, LANES:2 * LANES] + sums[:, 2 * LANES:3 * LANES]
        gcs.append(gc)
        gc_ts.append(gc.T)

    rsl = lambda g: slice(g * grows, (g + 1) * grows)
    qs = [vals["q"][rsl(g), h * DN_DK:(h + 1) * DN_DK] for g, h in chains]
    ks = [vals["k"][rsl(g), h * DN_DK:(h + 1) * DN_DK] for g, h in chains]
    gcols = [gcs[g][:, h:h + 1] for g, h in chains]
    betas = [gb[rsl(g), DN_HEADS + h:DN_HEADS + h + 1] for g, h in chains]
    decays = [jnp.where(incl, jnp.exp(jnp.where(incl, gcols[i] - gc_ts[g][h:h + 1, :], 0.0)), 0.0)
              for i, (g, h) in enumerate(chains)]
    idx = range(len(chains))
    kbs = [ks[i] * betas[i] for i in idx]
    qk_kks = [lax.dot_general(jnp.concatenate([qs[i], kbs[i]], axis=0).astype(BF16), ks[i].astype(BF16),
                              (((1,), (1,)), ((), ())), preferred_element_type=F32) for i in idx]
    qks = [qk_kks[i][:grows] * decays[i] for i in idx]
    lmats = [jnp.where(strict, qk_kks[i][grows:] * decays[i], 0.0) for i in idx]
    yield
    tmats = yield from _unit_lower_inverses(lmats, row, col, chunk, min(INV_BASE_BLOCK, chunk))
    egcs = [jnp.exp(gcol) for gcol in gcols]
    vs = [vals["v"][rsl(g), h * DN_DV:(h + 1) * DN_DV] for g, h in chains]
    uws = [_mm(tmats[i], jnp.concatenate([vs[i] * betas[i], kbs[i] * egcs[i]], axis=-1)) for i in idx]
    us = [uw[:, :DN_DV] for uw in uws]
    ws = [uw[:, DN_DV:] for uw in uws]
    qes = [qs[i] * egcs[i] for i in idx]
    yield

    a = vals["a"]
    bacc = vals["inp"]
    gy = vals["gy"]
    t = lax.broadcasted_iota(jnp.int32, (rows, lw), 0) & (SUBLANES - 1)
    s = 1
    while s < SUBLANES:
        keep = t >= s
        a_prev = jnp.where(keep, pltpu.roll(a, s, 0), 1.0)
        b_prev = jnp.where(keep, pltpu.roll(bacc, s, 0), 0.0)
        bacc = a * b_prev + bacc
        a = a * a_prev
        s *= 2
    tiles = tt // SUBLANES
    a4 = a.reshape(nb, tiles, SUBLANES, lw)
    b4 = bacc.reshape(nb, tiles, SUBLANES, lw)
    carry = hout_ref[...]
    h_tiles = []
    for k in range(tiles):
        hk = a4[:, k] * carry + b4[:, k]
        carry = hk[:, SUBLANES - 1:SUBLANES, :]
        h_tiles.append(hk)
    hout_ref[...] = carry
    hs = h_tiles[0] if tiles == 1 else jnp.stack(h_tiles, axis=1)
    o_b = hs.reshape(rows, lw) * gy
    yield

    v_new_parts = [[] for _ in idx]
    o_inter_parts = [[] for _ in idx]
    states = [None for _ in idx]
    for c in range(n_chunks):
        lo, hi = c * chunk, (c + 1) * chunk
        seq_of = lambda g: g * seqs_per_group + c // chunks_per_seq
        if c % chunks_per_seq == 0:
            states = [sout_ref[seq_of(g), h] for g, h in chains]
        wqs = [_mm(jnp.concatenate([ws[i][lo:hi], qes[i][lo:hi]], axis=0), states[i]) for i in idx]
        v_news = [us[i][lo:hi] - wqs[i][:chunk] for i in idx]
        new_states = []
        for i in idx:
            v_new_parts[i].append(v_news[i])
            o_inter_parts[i].append(wqs[i][chunk:])
            g_last = gcols[i][hi - 1:hi, :]
            k_dec = ks[i][lo:hi] * jnp.exp(g_last - gcols[i][lo:hi])
            new_states.append(states[i] * jnp.exp(g_last) + lax.dot_general(
                k_dec.astype(BF16), v_news[i].astype(BF16), (((0,), (0,)), ((), ())),
                preferred_element_type=F32))
        states = new_states
        if (c + 1) % chunks_per_seq == 0:
            for i, (g, h) in enumerate(chains):
                sout_ref[seq_of(g), h] = states[i]
        yield
    o_groups = []
    for g in range(n_groups):
        o_heads = []
        for h in range(DN_HEADS):
            i = g * DN_HEADS + h
            o = (jnp.concatenate(o_inter_parts[i], axis=0)
                 + _mm(qks[i], jnp.concatenate(v_new_parts[i], axis=0)))
            ms = jnp.mean(o * o, axis=-1, keepdims=True)
            o_heads.append(o * lax.rsqrt(ms + NORM_EPS) * nw_ref[...]
                           * vals["gz"][rsl(g), h * DN_DV:(h + 1) * DN_DV])
        o_groups.append(jnp.concatenate(o_heads, axis=-1))
    o_a = o_groups[0] if n_groups == 1 else jnp.concatenate(o_groups, axis=0)

    mixed = _mm(jnp.concatenate([o_a, o_b], axis=-1), wout_ref[...]).reshape(nb, tt, d)
    gate = mod_ref[...][:, :, 2 * d:3 * d]
    y_ref[...] = _layer_norm(alpha * x_ref[...] + (1.0 + gate) * mixed, lng_ref[...], lnb_ref[...])


N_TOKEN_MIX_INPUTS = 21
N_TOKEN_MIX_OUTPUTS = 5


def _token_mix_kernel(*refs, n_cast, nb, tt, chunk, alpha):
    (x_ref, mod_ref, hdn_ref, hlru_ref, s0_ref, h0_ref, w_ref, cwdn_ref, cwlru_ref, cblru_ref, alog_ref, dtb_ref,
     wr_ref, wi_ref, br_ref, bi_ref, lam_ref, wout_ref, nw_ref, lng_ref, lnb_ref) = refs[:N_TOKEN_MIX_INPUTS]
    cast_in = refs[N_TOKEN_MIX_INPUTS:N_TOKEN_MIX_INPUTS + n_cast]
    outs = refs[N_TOKEN_MIX_INPUTS + n_cast:]
    y_ref, tdn_ref, tlru_ref, sout_ref, hout_ref = outs[:N_TOKEN_MIX_OUTPUTS]
    cast_out = outs[N_TOKEN_MIX_OUTPUTS:N_TOKEN_MIX_OUTPUTS + n_cast]
    wdn_sc, wlru_sc = outs[N_TOKEN_MIX_OUTPUTS + n_cast:]

    for src, dst in zip(cast_in, cast_out):
        dst[...] = src[...].astype(BF16)

    @pl.when(pl.program_id(1) == 0)
    def _():
        sout_ref[...] = s0_ref[0]
        hout_ref[...] = h0_ref[...]

    vals = {}
    stage1 = _inproj_stage(
        x_ref, mod_ref, hdn_ref, hlru_ref, w_ref, cwdn_ref, cwlru_ref, cblru_ref, alog_ref, dtb_ref,
        wr_ref, wi_ref, br_ref, bi_ref, lam_ref, tdn_ref, tlru_ref, wdn_sc, wlru_sc, vals, nb=nb, tt=tt)
    stage2 = _mixer_stage(vals, x_ref, mod_ref, wout_ref, nw_ref, lng_ref, lnb_ref,
                          y_ref, sout_ref, hout_ref, nb=nb, tt=tt, chunk=chunk, alpha=alpha)
    while "gb" not in vals:
        next(stage1)
    pending = [stage2, stage1]
    while pending:
        pending = [g for g in pending if next(g, StopIteration) is not StopIteration]


def _token_mix_call(x, mod, hist_dn, hist_lru, s_dn_all, layer, h0, lw_, cast=(), *, nb, tt, chunk, alpha):
    b, t, d = x.shape
    lw = hist_lru.shape[-1]
    grid = (b // nb, t // tt)
    steps = grid[0] * grid[1]
    slab = lambda a: pl.BlockSpec((a.shape[0] // steps, a.shape[1]), lambda i, j: (i * grid[1] + j, 0))
    seq_blk = lambda c: pl.BlockSpec((nb, tt, c), lambda i, j: (i, j, 0))
    per_seq = lambda r, c: pl.BlockSpec((nb, r, c), lambda i, j: (i, 0, 0))
    whole = lambda a: pl.BlockSpec(a.shape, lambda i, j: (0,) * a.ndim, pipeline_mode=pl.Buffered(1))
    state_in = pl.BlockSpec((1, nb, DN_HEADS, DN_DK, DN_DV), lambda i, j: (layer, i, 0, 0, 0))
    state_out = pl.BlockSpec((nb, DN_HEADS, DN_DK, DN_DV), lambda i, j: (i, 0, 0, 0))
    weights = [lw_[n] for n in ("w_cat", "dn_conv_w", "lru_conv_w", "lru_conv_b", "a_log", "dt_bias",
                                "w_r", "w_i", "b_r", "b_i", "lam", "w_out", "dn_norm_w", "ln1_g", "ln1_b")]
    return pl.pallas_call(
        functools.partial(_token_mix_kernel, n_cast=len(cast), nb=nb, tt=tt, chunk=chunk, alpha=alpha),
        out_shape=(
            jax.ShapeDtypeStruct((b, t, d), F32),
            jax.ShapeDtypeStruct((b, SUBLANES, DN_QKV), F32),
            jax.ShapeDtypeStruct((b, SUBLANES, lw), F32),
            jax.ShapeDtypeStruct((b, DN_HEADS, DN_DK, DN_DV), F32),
            jax.ShapeDtypeStruct((b, 1, lw), F32),
        ) + tuple(jax.ShapeDtypeStruct(a.shape, BF16) for a in cast),
        grid=grid,
        in_specs=[seq_blk(d), per_seq(1, N_MOD * d), per_seq(SUBLANES, DN_QKV), per_seq(SUBLANES, lw),
                  state_in, per_seq(1, lw)] + [whole(a) for a in weights] + [slab(a) for a in cast],
        out_specs=(seq_blk(d), per_seq(SUBLANES, DN_QKV), per_seq(SUBLANES, lw), state_out, per_seq(1, lw))
        + tuple(slab(a) for a in cast),
        scratch_shapes=[pltpu.VMEM((nb, SUBLANES + tt, DN_QKV), F32), pltpu.VMEM((nb, SUBLANES + tt, lw), F32)],
        compiler_params=_params(("arbitrary", "arbitrary")),
        name="token_mixer",
    )(x, mod, hist_dn, hist_lru, s_dn_all, h0, *weights, *cast)


def _ffn_kernel(x_ref, mod_ref, wg_ref, wu_ref, wd_ref, lng_ref, lnb_ref, *refs, alpha):
    n_cast = (len(refs) - 1) // 2
    y_ref = refs[n_cast]
    for src, dst in zip(refs[:n_cast], refs[n_cast + 1:]):
        dst[...] = src[...].astype(BF16)
    nb, tt, d = x_ref.shape
    m = mod_ref[...]
    h = (x_ref[...] * (1.0 + m[:, :, 4 * d:5 * d]) + m[:, :, 3 * d:4 * d]).reshape(nb * tt, d).astype(BF16)
    act = _silu(jnp.dot(h, wg_ref[...], preferred_element_type=F32)) * jnp.dot(
        h, wu_ref[...], preferred_element_type=F32)
    ff = _mm(act, wd_ref[...]).reshape(nb, tt, d)
    y_ref[...] = _layer_norm(alpha * x_ref[...] + (1.0 + m[:, :, 5 * d:6 * d]) * ff, lng_ref[...], lnb_ref[...])


def _ff_tile(ff):
    for n in (2, 4, 7, 8, 11, 14, 16, 22, 28):
        if ff % n == 0 and (ff // n) % LANES == 0 and ff // n <= 2048:
            return ff // n
    return ff


def _dense_ffn(x, mod, lw_, cast=(), *, nb, tt, alpha):
    b, t, d = x.shape
    nt = t // tt
    steps = (b // nb) * nt
    slab = lambda a: pl.BlockSpec((a.shape[0] // steps, a.shape[1]), lambda i: (i, 0))
    seq_blk = pl.BlockSpec((nb, tt, d), lambda i: (i // nt, i % nt, 0))
    whole = lambda a: pl.BlockSpec(a.shape, lambda i: (0,) * a.ndim)
    resident = lambda a: pl.BlockSpec(a.shape, lambda i: (0,) * a.ndim, pipeline_mode=pl.Buffered(1))
    return pl.pallas_call(
        functools.partial(_ffn_kernel, alpha=alpha),
        out_shape=(jax.ShapeDtypeStruct((b, t, d), F32),) + tuple(jax.ShapeDtypeStruct(a.shape, BF16) for a in cast),
        grid=(steps,),
        in_specs=[seq_blk, pl.BlockSpec((nb, 1, N_MOD * d), lambda i: (i // nt, 0, 0)),
                  resident(lw_["w_gate"]), resident(lw_["w_up"]), resident(lw_["w_down"]),
                  whole(lw_["ln2_g"]), whole(lw_["ln2_b"])] + [slab(a) for a in cast],
        out_specs=(seq_blk,) + tuple(slab(a) for a in cast),
        compiler_params=_params(("arbitrary",)),
        name="dense_ffn",
    )(x, mod, lw_["w_gate"], lw_["w_up"], lw_["w_down"], lw_["ln2_g"], lw_["ln2_b"], *cast)


MOE_CHUNK = 512
MOE_SLOT_TILE = 512
MOE_SUB = 128
MOE_COMBINE_BLOCK = 256
MOE_COMBINE_FANIN = 4
ROUTE_I1, ROUTE_I2, ROUTE_R1, ROUTE_R2, ROUTE_W1, ROUTE_W2 = range(6)


def _router_kernel(x_ref, mod_ref, wr_ref, base_ref, h_ref, meta_ref, meta_t_ref, blkbase_ref, cnt_ref, run_sc,
                   *, n_experts):
    nb, tt, d = x_ref.shape
    rows = nb * tt

    @pl.when(pl.program_id(0) == 0)
    def _():
        run_sc[...] = base_ref[...]

    m = mod_ref[...]
    h = (x_ref[...] * (1.0 + m[:, :, 4 * d:5 * d]) + m[:, :, 3 * d:4 * d]).reshape(rows, d)
    h_hi = h.astype(BF16)
    h_ref[...] = h_hi
    w = wr_ref[...]
    w_hi = w.astype(BF16)
    h_lo = (h - h_hi.astype(F32)).astype(BF16)
    w_lo = (w - w_hi.astype(F32)).astype(BF16)
    logits = (jnp.dot(h_hi, w_hi, preferred_element_type=F32) + jnp.dot(h_lo, w_hi, preferred_element_type=F32)
              + jnp.dot(h_hi, w_lo, preferred_element_type=F32))
    lane = lax.broadcasted_iota(jnp.int32, logits.shape, 1)
    neg = jnp.float32(-jnp.inf)
    lg = jnp.where(lane < n_experts, logits, neg)
    m1 = jnp.max(lg, axis=-1, keepdims=True)
    i1 = jnp.min(jnp.where(lg == m1, lane, LANES), axis=-1, keepdims=True)
    lg2 = jnp.where(lane == i1, neg, lg)
    m2 = jnp.max(lg2, axis=-1, keepdims=True)
    i2 = jnp.min(jnp.where(lg2 == m2, lane, LANES), axis=-1, keepdims=True)
    e2 = jnp.exp(m2 - m1)
    w1 = 1.0 / (1.0 + e2)
    w2 = e2 / (1.0 + e2)
    sel = jnp.where(lane == i1, 1.0, jnp.where(lane == i2, 1.0, 0.0))
    r = lax.broadcasted_iota(jnp.int32, (rows, rows), 0)
    c = lax.broadcasted_iota(jnp.int32, (rows, rows), 1)
    rank = _mm(jnp.where(c < r, 1.0, 0.0), sel) + run_sc[0:1, :]
    r1 = jnp.sum(jnp.where(lane == i1, rank, 0.0), axis=-1, keepdims=True)
    r2 = jnp.sum(jnp.where(lane == i2, rank, 0.0), axis=-1, keepdims=True)
    fields = (i1.astype(F32), i2.astype(F32), r1, r2, w1, w2)
    meta = jnp.zeros_like(logits)
    for k, v in enumerate(fields):
        meta = jnp.where(lane == k, v, meta)
    meta_ref[...] = meta
    meta_t_ref[...] = meta.T[0:SUBLANES, :]
    blkbase_ref[0] = run_sc[...]
    run_sc[...] = run_sc[...] + jnp.sum(sel, axis=0, keepdims=True)
    cnt_ref[...] = run_sc[...]


def _router(x, mod, w_router, base, *, nb, tt, n_experts):
    b, t, d = x.shape
    rows = nb * tt
    nt = t // tt
    nblk = (b // nb) * nt
    whole = lambda a: pl.BlockSpec(a.shape, lambda i: (0,) * a.ndim)
    return pl.pallas_call(
        functools.partial(_router_kernel, n_experts=n_experts),
        out_shape=(jax.ShapeDtypeStruct((nblk * rows, d), BF16),
                   jax.ShapeDtypeStruct((nblk * rows, LANES), F32),
                   jax.ShapeDtypeStruct((SUBLANES, nblk * rows), F32),
                   jax.ShapeDtypeStruct((nblk, SUBLANES, LANES), F32),
                   jax.ShapeDtypeStruct((SUBLANES, LANES), F32)),
        grid=(nblk,),
        in_specs=[pl.BlockSpec((nb, tt, d), lambda i: (i // nt, i % nt, 0)),
                  pl.BlockSpec((nb, 1, N_MOD * d), lambda i: (i // nt, 0, 0)),
                  whole(w_router), whole(base)],
        out_specs=(pl.BlockSpec((rows, d), lambda i: (i, 0)),
                   pl.BlockSpec((rows, LANES), lambda i: (i, 0)),
                   pl.BlockSpec((SUBLANES, rows), lambda i: (0, i)),
                   pl.BlockSpec((1, SUBLANES, LANES), lambda i: (i, 0, 0)),
                   pl.BlockSpec((SUBLANES, LANES), lambda i: (0, 0))),
        scratch_shapes=[pltpu.VMEM((SUBLANES, LANES), F32)],
        compiler_params=_params(("arbitrary",)),
        name="moe_router",
    )(x, mod, w_router, base)


def _slot_of(expert, rank, start_ref, n_experts):
    start = jnp.zeros_like(rank)
    for e in range(n_experts):
        start = jnp.where(expert == e, start_ref[e].astype(F32), start)
    return (start + rank).astype(jnp.int32)


def _gather_kernel(clo_ref, chi_ref, start_ref, *refs, group_chunks, n_experts):
    n_groups = len(group_chunks)
    m_refs = refs[:n_groups]
    h_refs = refs[n_groups:2 * n_groups]
    xs_ref, ws_ref, acc_sc, wacc_sc = refs[2 * n_groups:]
    g = pl.program_id(0)
    n_sub = MOE_SLOT_TILE // MOE_SUB
    for j in range(n_sub):
        q = g * n_sub + j
        ids = g * MOE_SLOT_TILE + j * MOE_SUB + lax.broadcasted_iota(jnp.int32, (MOE_SUB, MOE_CHUNK), 0)
        acc_sc[...] = jnp.zeros_like(acc_sc)
        wacc_sc[...] = jnp.zeros_like(wacc_sc)
        first = 0
        for m_ref, h_ref, n_chunks in zip(m_refs, h_refs, group_chunks):
            def body(c, carry, m_ref=m_ref, h_ref=h_ref):
                off = pl.multiple_of(c * MOE_CHUNK, MOE_CHUNK)
                rec = m_ref[:, pl.ds(off, MOE_CHUNK)]
                row = lambda k: rec[k:k + 1, :]
                hit1 = _slot_of(row(ROUTE_I1), row(ROUTE_R1), start_ref, n_experts) == ids
                hit2 = _slot_of(row(ROUTE_I2), row(ROUTE_R2), start_ref, n_experts) == ids
                p = jnp.where(hit1, 1.0, jnp.where(hit2, 1.0, 0.0)).astype(BF16)
                acc_sc[...] += jnp.dot(p, h_ref[pl.ds(off, MOE_CHUNK), :], preferred_element_type=F32)
                w = jnp.where(hit1, row(ROUTE_W1), 0.0) + jnp.where(hit2, row(ROUTE_W2), 0.0)
                wacc_sc[...] += jnp.broadcast_to(jnp.sum(w, axis=-1, keepdims=True), wacc_sc.shape)
                return carry

            lo = jnp.clip(clo_ref[q] - first, 0, n_chunks)
            hi = jnp.clip(chi_ref[q] - first, 0, n_chunks)
            lax.fori_loop(lo, hi, body, 0)
            first += n_chunks
        xs_ref[j * MOE_SUB:(j + 1) * MOE_SUB, :] = acc_sc[...].astype(BF16)
        ws_ref[j * MOE_SUB:(j + 1) * MOE_SUB, :] = wacc_sc[...]


def _gather_slots(c_lo, c_hi, run_start, metas_t, hs, n_tiles, n_experts):
    d = hs[0].shape[-1]
    vmem = pl.BlockSpec(memory_space=pltpu.VMEM)
    return pl.pallas_call(
        functools.partial(_gather_kernel, group_chunks=tuple(h.shape[0] // MOE_CHUNK for h in hs),
                          n_experts=n_experts),
        out_shape=(jax.ShapeDtypeStruct((n_tiles * MOE_SLOT_TILE, d), BF16),
                   jax.ShapeDtypeStruct((n_tiles * MOE_SLOT_TILE, LANES), F32)),
        grid_spec=pltpu.PrefetchScalarGridSpec(
            num_scalar_prefetch=3, grid=(n_tiles,),
            in_specs=[vmem] * (2 * len(hs)),
            out_specs=(pl.BlockSpec((MOE_SLOT_TILE, d), lambda g, lo, hi, st: (g, 0)),
                       pl.BlockSpec((MOE_SLOT_TILE, LANES), lambda g, lo, hi, st: (g, 0))),
            scratch_shapes=[pltpu.VMEM((MOE_SUB, d), F32), pltpu.VMEM((MOE_SUB, LANES), F32)]),
        compiler_params=_params(("arbitrary",)),
        name="moe_gather",
    )(c_lo, c_hi, run_start, *metas_t, *hs)


def _expert_kernel(te_ref, tv_ref, xs_ref, ws_ref, wg_ref, wu_ref, wd_ref, o_ref, *, ff_chunk):
    g = pl.program_id(0)

    @pl.when(tv_ref[g] != 0)
    def _():
        x = xs_ref[...]
        ff = wg_ref.shape[-1]
        acc = jnp.zeros(o_ref.shape, F32)
        for f0 in range(0, ff, ff_chunk):
            gate = jnp.dot(x, wg_ref[0, :, f0:f0 + ff_chunk], preferred_element_type=F32)
            up = jnp.dot(x, wu_ref[0, :, f0:f0 + ff_chunk], preferred_element_type=F32)
            acc = acc + _mm(_silu(gate) * up, wd_ref[0, f0:f0 + ff_chunk, :])
        o_ref[...] = (ws_ref[:, 0:1] * acc).astype(BF16)

    @pl.when(tv_ref[g] == 0)
    def _():
        o_ref[...] = jnp.zeros_like(o_ref)


def _expert_ffn(tile_expert, tile_valid, xs, ws, lw_):
    s_total, d = xs.shape
    _, _, ff = lw_["w_gate"].shape
    n_tiles = s_total // MOE_SLOT_TILE
    once = pl.Buffered(1)
    twice = pl.Buffered(2)
    return pl.pallas_call(
        functools.partial(_expert_kernel, ff_chunk=_ff_tile(ff)),
        out_shape=jax.ShapeDtypeStruct((s_total, d), BF16),
        grid_spec=pltpu.PrefetchScalarGridSpec(
            num_scalar_prefetch=2, grid=(n_tiles,),
            in_specs=[pl.BlockSpec((MOE_SLOT_TILE, d), lambda g, te, tv: (g, 0)),
                      pl.BlockSpec((MOE_SLOT_TILE, LANES), lambda g, te, tv: (g, 0)),
                      pl.BlockSpec((1, d, ff), lambda g, te, tv: (te[g], 0, 0), pipeline_mode=twice),
                      pl.BlockSpec((1, d, ff), lambda g, te, tv: (te[g], 0, 0), pipeline_mode=twice),
                      pl.BlockSpec((1, ff, d), lambda g, te, tv: (te[g], 0, 0), pipeline_mode=once)],
            out_specs=pl.BlockSpec((MOE_SLOT_TILE, d), lambda g, te, tv: (g, 0))),
        compiler_params=_params(("arbitrary",)),
        name="expert_ffn",
    )(tile_expert, tile_valid, xs, ws, lw_["w_gate"], lw_["w_up"], lw_["w_down"])


def _combine_kernel(ic_ref, ik_ref, ie_ref, if_ref, bf_ref, start_ref, x_ref, mod_ref, meta_ref, *refs, alpha,
                    n_experts):
    os_refs = refs[:MOE_COMBINE_FANIN]
    lng_ref, lnb_ref, y_ref, acc_sc, slot_sc = refs[MOE_COMBINE_FANIN:]
    nb, tt, d = x_ref.shape
    rows = nb * tt
    w = pl.program_id(0)
    flags = if_ref[w]

    @pl.when((flags & 1) != 0)
    def _():
        rec = meta_ref[...]
        col = lambda k: rec[:, k:k + 1]
        slot_sc[:, 0:1] = _slot_of(col(ROUTE_I1), col(ROUTE_R1), start_ref, n_experts)
        slot_sc[:, 1:2] = _slot_of(col(ROUTE_I2), col(ROUTE_R2), start_ref, n_experts)

    def window_sum():
        lane = lax.broadcasted_iota(jnp.int32, (rows, MOE_COMBINE_BLOCK), 1)
        s1 = slot_sc[:, 0:1]
        s2 = slot_sc[:, 1:2]
        total = None
        for j, os_ref in enumerate(os_refs):
            ids = ik_ref[w * MOE_COMBINE_FANIN + j] + lane
            ids = jnp.where(ids < ie_ref[w * MOE_COMBINE_FANIN + j], ids, -1)
            q = jnp.where(s1 == ids, 1.0, jnp.where(s2 == ids, 1.0, 0.0)).astype(BF16)
            part = jnp.dot(q, os_ref[...], preferred_element_type=F32)
            total = part if total is None else total + part
        return total

    @pl.when((flags & 1) != 0)
    def _():
        acc_sc[...] = window_sum()

    @pl.when((flags & 5) == 4)
    def _():
        acc_sc[...] += window_sum()

    @pl.when((flags & 2) != 0)
    def _():
        gate = mod_ref[...][:, :, 5 * d:6 * d]
        ff = acc_sc[...].reshape(nb, tt, d)
        y_ref[...] = _layer_norm(alpha * x_ref[...] + (1.0 + gate) * ff, lng_ref[...], lnb_ref[...])


def _combine(items, run_start, x, mod, meta, out_sorted, lw_, *, nb, tt, alpha, n_experts):
    step_chunk, win_start, win_end, step_flags, win_fetch = items
    b, t, d = x.shape
    rows = nb * tt
    nt = t // tt
    whole = lambda a: pl.BlockSpec(a.shape, lambda w, ic, ik, ie, fl, bf, st: (0,) * a.ndim)
    seq_blk = pl.BlockSpec((nb, tt, d), lambda w, ic, ik, ie, fl, bf, st: (ic[w] // nt, ic[w] % nt, 0))
    slot_blk = lambda j: pl.BlockSpec(
        (pl.Element(MOE_COMBINE_BLOCK), pl.Element(d)),
        lambda w, ic, ik, ie, fl, bf, st: (pl.multiple_of(bf[w * MOE_COMBINE_FANIN + j], 2 * SUBLANES), 0))
    return pl.pallas_call(
        functools.partial(_combine_kernel, alpha=alpha, n_experts=n_experts),
        out_shape=jax.ShapeDtypeStruct((b, t, d), F32),
        grid_spec=pltpu.PrefetchScalarGridSpec(
            num_scalar_prefetch=6, grid=(step_chunk.shape[0],),
            in_specs=[seq_blk,
                      pl.BlockSpec((nb, 1, N_MOD * d), lambda w, ic, ik, ie, fl, bf, st: (ic[w] // nt, 0, 0)),
                      pl.BlockSpec((rows, LANES), lambda w, ic, ik, ie, fl, bf, st: (ic[w], 0))]
            + [slot_blk(j) for j in range(MOE_COMBINE_FANIN)]
            + [whole(lw_["ln2_g"]), whole(lw_["ln2_b"])],
            out_specs=seq_blk,
            scratch_shapes=[pltpu.VMEM((rows, d), F32), pltpu.VMEM((rows, LANES), jnp.int32)]),
        compiler_params=_params(("arbitrary",)),
        name="moe_combine",
    )(step_chunk, win_start, win_end, step_flags, win_fetch, run_start, x, mod, meta,
      *([out_sorted] * MOE_COMBINE_FANIN), lw_["ln2_g"], lw_["ln2_b"])


def _count_le(sorted_vals, queries):
    return jnp.sum(sorted_vals[None, :] <= queries[:, None], axis=1).astype(jnp.int32)


def _combine_steps(lo, hi, n_steps):
    n_chunks, n_experts = lo.shape
    fan = MOE_COMBINE_FANIN
    align = 2 * SUBLANES
    lo_f, hi_f = lo.reshape(-1), hi.reshape(-1)
    first = (lo_f // align) * align
    count = jnp.where(hi_f > lo_f, (hi_f - first + MOE_COMBINE_BLOCK - 1) // MOE_COMBINE_BLOCK, 0)
    pair_end = jnp.cumsum(count)
    chunk_items = jnp.sum(count.reshape(n_chunks, n_experts), axis=1)
    chunk_item0 = jnp.cumsum(chunk_items) - chunk_items
    chunk_steps = (chunk_items + fan - 1) // fan
    step_end = jnp.cumsum(chunk_steps)
    total_steps = step_end[-1]
    s = jnp.arange(n_steps, dtype=jnp.int32)
    live = s < total_steps
    chunk = jnp.minimum(_count_le(step_end, s), n_chunks - 1)
    chunk = jnp.where(live, chunk, chunk[jnp.maximum(total_steps - 1, 0)])
    q = s - (step_end[chunk] - chunk_steps[chunk])
    j = q[:, None] * fan + jnp.arange(fan, dtype=jnp.int32)[None, :]
    used = live[:, None] & (j < chunk_items[chunk][:, None])
    item = jnp.where(used, chunk_item0[chunk][:, None] + j, 0).reshape(-1)
    pair = jnp.minimum(_count_le(pair_end, item), n_chunks * n_experts - 1)
    start = first[pair] + (item - (pair_end[pair] - count[pair])) * MOE_COMBINE_BLOCK
    fetch = jnp.where(used.reshape(-1), start, jnp.repeat(start.reshape(-1, fan)[:, 0], fan))
    end = jnp.where(used.reshape(-1), hi_f[pair], 0)
    flags = (jnp.where(live & (q == 0), 1, 0) + jnp.where(live & (q == chunk_steps[chunk] - 1), 2, 0)
             + jnp.where(live, 4, 0))
    as_i32 = lambda a: a.astype(jnp.int32)
    return as_i32(chunk), as_i32(start), as_i32(end), as_i32(flags), as_i32(fetch)


def _moe_layer(xs_in, mods, lw_, tilings, alpha):
    n_experts = lw_["w_gate"].shape[0]
    base = jnp.zeros((SUBLANES, LANES), F32)
    hs, metas, metas_t, bases = [], [], [], []
    for x, mod, (nb, tt) in zip(xs_in, mods, tilings):
        assert nb * tt == MOE_CHUNK and (x.shape[0] * x.shape[1]) % MOE_CHUNK == 0
        h, meta, meta_t, blkbase, base = _router(x, mod, lw_["w_router"], base, nb=nb, tt=tt,
                                                 n_experts=n_experts)
        hs.append(h)
        metas.append(meta)
        metas_t.append(meta_t)
        bases.append(blkbase[:, 0, :n_experts])
    n = sum(m.shape[0] for m in metas)
    cum = jnp.concatenate(bases + [base[0:1, :n_experts]], axis=0).astype(jnp.int32)
    counts = cum[-1]
    sizes = ((counts + MOE_SLOT_TILE - 1) // MOE_SLOT_TILE) * MOE_SLOT_TILE
    run_end = jnp.cumsum(sizes)
    run_start = (run_end - sizes).astype(jnp.int32)

    n_tiles = (TOP_K * n + MOE_SLOT_TILE - 1) // MOE_SLOT_TILE + n_experts + 1
    tile_start = jnp.arange(n_tiles, dtype=jnp.int32) * MOE_SLOT_TILE
    tile_expert = jnp.minimum(_count_le(run_end, tile_start), n_experts - 1)
    tile_valid = (tile_start < run_end[-1]).astype(jnp.int32)
    n_sub = MOE_SLOT_TILE // MOE_SUB
    sub_expert = jnp.repeat(tile_expert, n_sub)
    sub_rank0 = jnp.arange(n_tiles * n_sub, dtype=jnp.int32) * MOE_SUB - run_start[sub_expert]
    cum_sub = cum[:, sub_expert]
    sub_valid = jnp.repeat(tile_valid, n_sub)
    c_lo = jnp.sum(cum_sub[1:] <= sub_rank0[None, :], axis=0).astype(jnp.int32) * sub_valid
    c_hi = jnp.sum(cum_sub[:-1] < sub_rank0[None, :] + MOE_SUB, axis=0).astype(jnp.int32) * sub_valid

    x_sorted, w_sorted = _gather_slots(c_lo, c_hi, run_start, metas_t, hs, n_tiles, n_experts)
    out_sorted = _expert_ffn(tile_expert, tile_valid, x_sorted, w_sorted, lw_)

    outs = []
    chunk0 = 0
    for x, mod, meta, (nb, tt) in zip(xs_in, mods, metas, tilings):
        nc = x.shape[0] * x.shape[1] // MOE_CHUNK
        lo = run_start[None, :] + cum[chunk0:chunk0 + nc]
        hi = run_start[None, :] + cum[chunk0 + 1:chunk0 + nc + 1]
        max_items = nc * n_experts + (TOP_K * nc * MOE_CHUNK) // MOE_COMBINE_BLOCK + 2 * n_experts
        items = _combine_steps(lo, hi, max_items // MOE_COMBINE_FANIN + nc)
        outs.append(_combine(items, run_start, x, mod, meta, out_sorted, lw_, nb=nb, tt=tt, alpha=alpha,
                             n_experts=n_experts))
        chunk0 += nc
    return outs


def _pad_lanes(v, width=LANES):
    return jnp.pad(v, ((0, 0), (0, width - v.shape[-1])))


def _block_diag_halves(w):
    nblk, c, _ = w.shape
    half = nblk // 2
    out = jnp.zeros((2, half * c, half * c), w.dtype)
    for i in range(nblk):
        j = i % half
        out = out.at[i // half, j * c:(j + 1) * c, j * c:(j + 1) * c].set(w[i])
    return out


def _layer_weights(l, p, d, lw):
    w_in = p["w_in"][l]
    a_off = DN_QKV
    z_off = a_off + 2 * DN_HEADS
    x_off = z_off + DN_WIDTH
    y_off = x_off + lw
    w_ab = _pad_lanes(w_in[:, a_off:z_off])
    w_cat = jnp.concatenate([w_in[:, :DN_QKV], w_in[:, z_off:y_off + lw], w_ab], axis=1).astype(BF16)
    out = {
        "w_cat": w_cat,
        "dn_conv_w": p["dn_conv_w"][l],
        "lru_conv_w": p["lru_conv_w"][l],
        "lru_conv_b": p["lru_conv_b"][l][None],
        "a_log": _pad_lanes(p["dn_a_log"][l][None]),
        "dt_bias": _pad_lanes(p["dn_dt_bias"][l][None]),
        "w_r": _block_diag_halves(p["lru_w_r"][l]).astype(BF16),
        "w_i": _block_diag_halves(p["lru_w_i"][l]).astype(BF16),
        "b_r": p["lru_b_r"][l][None],
        "b_i": p["lru_b_i"][l][None],
        "lam": p["lru_lambda"][l][None],
        "w_out": p["w_out"][l].astype(BF16),
        "dn_norm_w": p["dn_norm_w"][l][None],
        "ln1_g": p["ln1_g"][l][None],
        "ln1_b": p["ln1_b"][l][None],
        "ln2_g": p["ln2_g"][l][None],
        "ln2_b": p["ln2_b"][l][None],
    }
    j = l // 2
    if l % 2 == 0:
        out.update(w_gate=p["ffn_w_gate"][j], w_up=p["ffn_w_up"][j], w_down=p["ffn_w_down"][j])
    else:
        out.update(w_router=_pad_lanes(p["moe_w_router"][j]),
                   w_gate=p["moe_w_gate"][j], w_up=p["moe_w_up"][j], w_down=p["moe_w_down"][j])
    return out


CHANNEL_MIX_WEIGHTS = ("w_gate", "w_up", "w_down")


LARGE_SLAB_BYTES = 64 * 1024 * 1024


def _cast_plan(weights, host_steps):
    hosts = list(host_steps)
    plan = {h: [] for h in hosts}
    for k in reversed(range(len(weights))):
        dense = ("ffn", k) in host_steps
        last = hosts.index(("ffn", k)) if dense else hosts.index(("mixer", k)) + 1
        for n in CHANNEL_MIX_WEIGHTS:
            a = weights[k][n].reshape(-1, weights[k][n].shape[-1])
            large = a.size * 4 > LARGE_SLAB_BYTES
            for h in reversed(hosts[:last]):
                fits = a.shape[0] % (host_steps[h] * 2 * SUBLANES) == 0
                busy = large and any(s.size * 4 > LARGE_SLAB_BYTES for _, _, s in plan[h])
                if fits and not busy:
                    plan[h].append((k, n, a))
                    break
    return plan


def _tiling(b, t):
    if t >= MXU_DIM:
        tt = MXU_DIM
        return dict(mixer=(2 if b % 2 == 0 else 1, tt, min(PROMPT_CHUNK, tt)), ffn=(1, min(t, 2 * MXU_DIM)))
    assert t == SUBLANES, "short sequences must be exactly one sublane tile long"
    return dict(mixer=(min(b, 16), t, t), ffn=(min(b, 64), t))


def _token_mix(x, mod, conv_dn, s_dn_all, layer, conv_lru, s_lru, lw_, til, alpha, cast=()):
    pad_hist = lambda c: jnp.pad(c, ((0, 0), (SUBLANES - (CONV_W - 1), 0), (0, 0)))
    nb, tt, chunk = til["mixer"]
    x, tail_dn, tail_lru, s_new, h_new, *converted = _token_mix_call(
        x, mod, pad_hist(conv_dn), pad_hist(conv_lru), s_dn_all, layer, s_lru[:, None, :], lw_, cast,
        nb=nb, tt=tt, chunk=chunk, alpha=alpha)
    return (x, tail_dn[:, SUBLANES - (CONV_W - 1):, :], s_new, tail_lru[:, SUBLANES - (CONV_W - 1):, :],
            h_new[:, 0, :]), converted


def kernel(x_prompt, x_sample, cache_dn_conv, state_dn, cache_lru_conv, state_lru, c_prompt, c_sample,
           w_ada, b_ada, w_in, dn_conv_w, dn_a_log, dn_dt_bias, dn_norm_w,
           lru_conv_w, lru_conv_b, lru_w_r, lru_b_r, lru_w_i, lru_b_i, lru_lambda, w_out,
           ln1_g, ln1_b, ln2_g, ln2_b, ffn_w_gate, ffn_w_up, ffn_w_down,
           moe_w_router, moe_w_gate, moe_w_up, moe_w_down):
    p = dict(w_in=w_in, dn_conv_w=dn_conv_w, dn_a_log=dn_a_log, dn_dt_bias=dn_dt_bias, dn_norm_w=dn_norm_w,
             lru_conv_w=lru_conv_w, lru_conv_b=lru_conv_b, lru_w_r=lru_w_r, lru_b_r=lru_b_r,
             lru_w_i=lru_w_i, lru_b_i=lru_b_i, lru_lambda=lru_lambda, w_out=w_out,
             ln1_g=ln1_g, ln1_b=ln1_b, ln2_g=ln2_g, ln2_b=ln2_b,
             ffn_w_gate=ffn_w_gate, ffn_w_up=ffn_w_up, ffn_w_down=ffn_w_down,
             moe_w_router=moe_w_router, moe_w_gate=moe_w_gate, moe_w_up=moe_w_up, moe_w_down=moe_w_down)
    depth, d, _ = w_ada.shape
    bp = x_prompt.shape[0]
    bs = x_sample.shape[0]
    lw = cache_lru_conv.shape[-1]
    alpha = (2 * depth) ** 0.25
    weights = [_layer_weights(l, p, d, lw) for l in range(depth)]

    c_all = jnp.concatenate([c_prompt, c_sample], axis=0)
    mod_all = _modulation(c_all, w_ada, b_ada)
    groups = [
        dict(x=x_prompt, rows=slice(0, bp), conv_dn=jnp.zeros((depth, bp, CONV_W - 1, DN_QKV), F32),
             s_dn=jnp.zeros((depth, bp, DN_HEADS, DN_DK, DN_DV), F32),
             conv_lru=jnp.zeros((depth, bp, CONV_W - 1, lw), F32), s_lru=jnp.zeros((depth, bp, lw), F32)),
        dict(x=x_sample, rows=slice(bp, bp + bs), conv_dn=cache_dn_conv, s_dn=state_dn,
             conv_lru=cache_lru_conv, s_lru=state_lru),
    ]
    for g in groups:
        g["til"] = _tiling(g["x"].shape[0], g["x"].shape[1])
        g["new"] = [[], [], [], []]
    t_p = x_prompt.shape[1]
    nb_m, tt_m, _ = groups[0]["til"]["mixer"]
    nb_f, tt_f = groups[0]["til"]["ffn"]
    host_steps = {}
    for l in range(depth):
        host_steps[("mixer", l)] = (bp // nb_m) * (t_p // tt_m)
        if l % 2 == 0:
            host_steps[("ffn", l)] = (bp // nb_f) * (t_p // tt_f)
    plan = _cast_plan(weights, host_steps)
    planned = {(k, n) for slabs in plan.values() for k, n, _ in slabs}
    for k, w in enumerate(weights):
        for n in CHANNEL_MIX_WEIGHTS:
            if (k, n) not in planned:
                w[n] = w[n].astype(BF16)

    def adopt(slabs, converted):
        for (k, n, _), c in zip(slabs, converted):
            weights[k][n] = c.reshape(weights[k][n].shape)

    for l in range(depth):
        lw_ = weights[l]
        mods = [mod_all[l, g["rows"]][:, None, :] for g in groups]
        for gi, (g, mod) in enumerate(zip(groups, mods)):
            slabs = plan[("mixer", l)] if gi == 0 else []
            res, converted = _token_mix(g["x"], mod, g["conv_dn"][l], g["s_dn"], l, g["conv_lru"][l],
                                        g["s_lru"][l], lw_, g["til"], alpha,
                                        cast=tuple(a for _, _, a in slabs))
            adopt(slabs, converted)
            g["x"] = res[0]
            for acc, new in zip(g["new"], res[1:]):
                acc.append(new)
        if l % 2 == 0:
            for gi, (g, mod) in enumerate(zip(groups, mods)):
                nb, tt = g["til"]["ffn"]
                slabs = plan[("ffn", l)] if gi == 0 else []
                g["x"], *converted = _dense_ffn(g["x"], mod, lw_, tuple(a for _, _, a in slabs),
                                                nb=nb, tt=tt, alpha=alpha)
                adopt(slabs, converted)
        else:
            xs = _moe_layer([g["x"] for g in groups], mods, lw_, [g["til"]["ffn"] for g in groups], alpha)
            for g, x in zip(groups, xs):
                g["x"] = x
    states = [jnp.stack(acc) for g in groups for acc in g["new"]]
    return (groups[0]["x"], groups[1]["x"]) + tuple(states)
```

```python
import functools
import math

import jax
import jax.numpy as jnp
from jax import lax
from jax.experimental import pallas as pl
from jax.experimental.pallas import tpu as pltpu

F32 = jnp.float32
BF16 = jnp.bfloat16

DN_HEADS = 4
DN_DK = 128
DN_DV = 128
DN_WIDTH = DN_HEADS * DN_DV
DN_QKV = 3 * DN_WIDTH
LRU_BLOCKS = 8
LRU_C = 8.0
CONV_W = 4
N_MOD = 6
TOP_K = 2
LN_EPS = 1e-5
NORM_EPS = 1e-6

SUBLANES = 8
LANES = 128
MXU_DIM = 256
VMEM_LIMIT_BYTES = 56 * 1024 * 1024

PROMPT_CHUNK = 64
INV_BASE_BLOCK = 16


def _sigmoid(x):
    return 0.5 + 0.5 * jnp.tanh(0.5 * x)


def _silu(x):
    half = 0.5 * x
    return half + half * jnp.tanh(half)


def _softplus(x):
    return jnp.maximum(x, 0.0) + jnp.log1p(jnp.exp(-jnp.abs(x)))


def _gelu_tanh(x):
    return 0.5 * x * (1.0 + jnp.tanh(math.sqrt(2.0 / math.pi) * (x + 0.044715 * (x * x * x))))


def _mm(a, b):
    return jnp.dot(a.astype(BF16), b.astype(BF16), preferred_element_type=F32)


def _layer_norm(x, g, b):
    mu = jnp.mean(x, axis=-1, keepdims=True)
    xc = x - mu
    var = jnp.mean(xc * xc, axis=-1, keepdims=True)
    return xc * lax.rsqrt(var + LN_EPS) * g + b


def _params(sem):
    return pltpu.CompilerParams(dimension_semantics=sem, vmem_limit_bytes=VMEM_LIMIT_BYTES)


def _mod_kernel(c_ref, w_ref, b_ref, o_ref):
    sc = _silu(c_ref[...])
    o_ref[0] = _mm(sc, w_ref[0]) + b_ref[0]


def _modulation(c_all, w_ada, b_ada):
    depth, d, n = w_ada.shape
    rows = c_all.shape[0]
    tn = 1536 if n % 1536 == 0 else n
    return pl.pallas_call(
        _mod_kernel,
        out_shape=jax.ShapeDtypeStruct((depth, rows, n), F32),
        grid=(depth, n // tn),
        in_specs=[
            pl.BlockSpec((rows, d), lambda l, j: (0, 0)),
            pl.BlockSpec((1, d, tn), lambda l, j: (l, 0, j)),
            pl.BlockSpec((1, 1, tn), lambda l, j: (l, 0, j)),
        ],
        out_specs=pl.BlockSpec((1, rows, tn), lambda l, j: (l, 0, j)),
        compiler_params=_params(("arbitrary", "arbitrary")),
        name="adaln_modulation",
    )(c_all, w_ada, b_ada.reshape(depth, 1, n))


def _causal_conv(u, win_ref, w, nb, tt, c0):
    c = u.shape[-1]
    cols = slice(c0, c0 + c)
    win_ref[:, SUBLANES:, cols] = u.reshape(nb, tt, c)
    out = u * w[CONV_W - 1:CONV_W, cols]
    for j in range(1, CONV_W):
        prev = win_ref[:, SUBLANES - j:SUBLANES - j + tt, cols].reshape(nb * tt, c)
        out = out + prev * w[CONV_W - 1 - j:CONV_W - j, cols]
    tail = win_ref[:, tt:tt + SUBLANES, cols]
    win_ref[:, 0:SUBLANES, cols] = tail
    return out, tail


def _l2norm_heads(x, scale):
    outs = []
    for h in range(DN_HEADS):
        xh = x[:, h * DN_DK:(h + 1) * DN_DK]
        ss = jnp.sum(xh * xh, axis=-1, keepdims=True)
        outs.append(xh * (lax.rsqrt(ss + NORM_EPS) * scale))
    return jnp.concatenate(outs, axis=-1)


def _inproj_stage(x_ref, mod_ref, hdn_ref, hlru_ref, w_ref, cwdn_ref, cwlru_ref, cblru_ref,
                  alog_ref, dtb_ref, wr_ref, wi_ref, br_ref, bi_ref, lam_ref,
                  tdn_ref, tlru_ref, wdn_sc, wlru_sc, out, *, nb, tt):
    d = x_ref.shape[-1]
    lw = hlru_ref.shape[-1]
    rows = nb * tt

    @pl.when(pl.program_id(1) == 0)
    def _():
        wdn_sc[:, 0:SUBLANES, :] = hdn_ref[...]
        wlru_sc[:, 0:SUBLANES, :] = hlru_ref[...]

    m = mod_ref[...]
    shift = m[:, :, 0:d]
    scale = m[:, :, d:2 * d]
    h = (x_ref[...] * (1.0 + scale) + shift).reshape(rows, d)
    proj = _mm(h, w_ref[...])
    z = proj[:, DN_QKV:DN_QKV + DN_WIDTH]
    u_lru = proj[:, DN_QKV + DN_WIDTH:DN_QKV + DN_WIDTH + lw]
    y = proj[:, DN_QKV + DN_WIDTH + lw:DN_QKV + DN_WIDTH + 2 * lw]
    ab = proj[:, DN_QKV + DN_WIDTH + 2 * lw:]

    cw_dn = cwdn_ref[...]

    def dn_part(part, norm_scale):
        c0 = part * DN_WIDTH
        conv, tail = _causal_conv(proj[:, c0:c0 + DN_WIDTH], wdn_sc, cw_dn, nb, tt, c0)
        tdn_ref[:, :, c0:c0 + DN_WIDTH] = tail
        act = _silu(conv)
        return act if norm_scale is None else _l2norm_heads(act, norm_scale)

    out["q"] = dn_part(0, DN_DK ** -0.5)
    yield
    out["k"] = dn_part(1, 1.0)
    lane = lax.broadcasted_iota(jnp.int32, ab.shape, 1)
    g_full = -jnp.exp(alog_ref[...]) * _softplus(ab + dtb_ref[...])
    out["gb"] = jnp.where(lane < DN_HEADS, g_full, _sigmoid(ab))
    yield
    out["v"] = dn_part(2, None)
    yield
    out["gz"] = _silu(z)
    out["gy"] = _gelu_tanh(y)
    yield

    conv_lru, tail_lru = _causal_conv(u_lru, wlru_sc, cwlru_ref[...], nb, tt, 0)
    tlru_ref[...] = tail_lru
    xc = conv_lru + cblru_ref[...]
    half = lw // 2
    r_pre = jnp.concatenate([_mm(xc[:, :half], wr_ref[0]), _mm(xc[:, half:], wr_ref[1])], axis=-1)
    i_pre = jnp.concatenate([_mm(xc[:, :half], wi_ref[0]), _mm(xc[:, half:], wi_ref[1])], axis=-1)
    r = _sigmoid(r_pre + br_ref[...])
    i = _sigmoid(i_pre + bi_ref[...])
    log_a = -LRU_C * r * _softplus(-lam_ref[...])
    out["a"] = jnp.exp(log_a)
    th = jnp.tanh(log_a)
    out["inp"] = jnp.sqrt(-2.0 * th / (1.0 - th)) * (i * xc)
    yield


def _unit_lower_inverses(ls, row, col, chunk, base):
    def same_block(s):
        k = s.bit_length() - 1
        return (row >> k) == (col >> k)

    eye = jnp.where(row == col, 1.0, 0.0)
    base_mask = same_block(base)
    powers = [jnp.where(base_mask, l, 0.0) for l in ls]
    invs = [eye - d for d in powers]
    p = 2
    while p < base:
        powers = [_mm(d, d) for d in powers]
        invs = [t + _mm(t, d) for t, d in zip(invs, powers)]
        p *= 2
        yield
    s = base
    while s < chunk:
        off_mask = same_block(2 * s) & jnp.logical_not(same_block(s))
        tmp = [_mm(jnp.where(off_mask, l, 0.0), t) for l, t in zip(ls, invs)]
        invs = [t - _mm(t, x) for t, x in zip(invs, tmp)]
        s *= 2
        yield
    return invs


def _mixer_stage(vals, x_ref, mod_ref, wout_ref, nw_ref, lng_ref, lnb_ref,
                 y_ref, sout_ref, hout_ref, *, nb, tt, chunk, alpha):
    d = x_ref.shape[-1]
    lw = hout_ref.shape[-1]
    rows = nb * tt
    grows = min(rows, MXU_DIM)
    n_groups = rows // grows
    seqs_per_group = max(grows // tt, 1)
    n_chunks = grows // chunk
    chunks_per_seq = tt // chunk if tt >= chunk else 1
    log_chunk = chunk.bit_length() - 1

    row = lax.broadcasted_iota(jnp.int32, (grows, grows), 0)
    col = lax.broadcasted_iota(jnp.int32, (grows, grows), 1)
    incl = ((row >> log_chunk) == (col >> log_chunk)) & (col <= row)
    strict = incl & (col < row)
    gb = vals["gb"]
    pieces = []
    rest = gb
    for _ in range(3):
        piece = rest.astype(BF16)
        pieces.append(piece)
        rest = rest - piece.astype(F32)
    split = jnp.concatenate(pieces, axis=-1)
    incl_b = jnp.where(incl, 1.0, 0.0).astype(BF16)
    chains = [(g, h) for g in range(n_groups) for h in range(DN_HEADS)]
    gcs, gc_ts = [], []
    for g in range(n_groups):
        sums = jnp.dot(incl_b, split[g * grows:(g + 1) * grows], preferred_element_type=F32)
        gc = sums[:, 0:LANES] + sums[:, LANES:2 * LANES] + sums[:, 2 * LANES:3 * LANES]
        gcs.append(gc)
        gc_ts.append(gc.T)

    rsl = lambda g: slice(g * grows, (g + 1) * grows)
    qs = [vals["q"][rsl(g), h * DN_DK:(h + 1) * DN_DK] for g, h in chains]
    ks = [vals["k"][rsl(g), h * DN_DK:(h + 1) * DN_DK] for g, h in chains]
    gcols = [gcs[g][:, h:h + 1] for g, h in chains]
    betas = [gb[rsl(g), DN_HEADS + h:DN_HEADS + h + 1] for g, h in chains]
    decays = [jnp.where(incl, jnp.exp(jnp.where(incl, gcols[i] - gc_ts[g][h:h + 1, :], 0.0)), 0.0)
              for i, (g, h) in enumerate(chains)]
    idx = range(len(chains))
    kbs = [ks[i] * betas[i] for i in idx]
    qk_kks = [lax.dot_general(jnp.concatenate([qs[i], kbs[i]], axis=0).astype(BF16), ks[i].astype(BF16),
                              (((1,), (1,)), ((), ())), preferred_element_type=F32) for i in idx]
    qks = [qk_kks[i][:grows] * decays[i] for i in idx]
    lmats = [jnp.where(strict, qk_kks[i][grows:] * decays[i], 0.0) for i in idx]
    yield
    tmats = yield from _unit_lower_inverses(lmats, row, col, chunk, min(INV_BASE_BLOCK, chunk))
    egcs = [jnp.exp(gcol) for gcol in gcols]
    vs = [vals["v"][rsl(g), h * DN_DV:(h + 1) * DN_DV] for g, h in chains]
    uws = [_mm(tmats[i], jnp.concatenate([vs[i] * betas[i], kbs[i] * egcs[i]], axis=-1)) for i in idx]
    us = [uw[:, :DN_DV] for uw in uws]
    ws = [uw[:, DN_DV:] for uw in uws]
    qes = [qs[i] * egcs[i] for i in idx]
    yield

    a = vals["a"]
    bacc = vals["inp"]
    gy = vals["gy"]
    t = lax.broadcasted_iota(jnp.int32, (rows, lw), 0) & (SUBLANES - 1)
    s = 1
    while s < SUBLANES:
        keep = t >= s
        a_prev = jnp.where(keep, pltpu.roll(a, s, 0), 1.0)
        b_prev = jnp.where(keep, pltpu.roll(bacc, s, 0), 0.0)
        bacc = a * b_prev + bacc
        a = a * a_prev
        s *= 2
    tiles = tt // SUBLANES
    a4 = a.reshape(nb, tiles, SUBLANES, lw)
    b4 = bacc.reshape(nb, tiles, SUBLANES, lw)
    carry = hout_ref[...]
    h_tiles = []
    for k in range(tiles):
        hk = a4[:, k] * carry + b4[:, k]
        carry = hk[:, SUBLANES - 1:SUBLANES, :]
        h_tiles.append(hk)
    hout_ref[...] = carry
    hs = h_tiles[0] if tiles == 1 else jnp.stack(h_tiles, axis=1)
    o_b = hs.reshape(rows, lw) * gy
    yield

    v_new_parts = [[] for _ in idx]
    o_inter_parts = [[] for _ in idx]
    states = [None for _ in idx]
    for c in range(n_chunks):
        lo, hi = c * chunk, (c + 1) * chunk
        seq_of = lambda g: g * seqs_per_group + c // chunks_per_seq
        if c % chunks_per_seq == 0:
            states = [sout_ref[seq_of(g), h] for g, h in chains]
        wqs = [_mm(jnp.concatenate([ws[i][lo:hi], qes[i][lo:hi]], axis=0), states[i]) for i in idx]
        v_news = [us[i][lo:hi] - wqs[i][:chunk] for i in idx]
        new_states = []
        for i in idx:
            v_new_parts[i].append(v_news[i])
            o_inter_parts[i].append(wqs[i][chunk:])
            g_last = gcols[i][hi - 1:hi, :]
            k_dec = ks[i][lo:hi] * jnp.exp(g_last - gcols[i][lo:hi])
            new_states.append(states[i] * jnp.exp(g_last) + lax.dot_general(
                k_dec.astype(BF16), v_news[i].astype(BF16), (((0,), (0,)), ((), ())),
                preferred_element_type=F32))
        states = new_states
        if (c + 1) % chunks_per_seq == 0:
            for i, (g, h) in enumerate(chains):
                sout_ref[seq_of(g), h] = states[i]
        yield
    o_groups = []
    for g in range(n_groups):
        o_heads = []
        for h in range(DN_HEADS):
            i = g * DN_HEADS + h
            o = (jnp.concatenate(o_inter_parts[i], axis=0)
                 + _mm(qks[i], jnp.concatenate(v_new_parts[i], axis=0)))
            ms = jnp.mean(o * o, axis=-1, keepdims=True)
            o_heads.append(o * lax.rsqrt(ms + NORM_EPS) * nw_ref[...]
                           * vals["gz"][rsl(g), h * DN_DV:(h + 1) * DN_DV])
        o_groups.append(jnp.concatenate(o_heads, axis=-1))
    o_a = o_groups[0] if n_groups == 1 else jnp.concatenate(o_groups, axis=0)

    mixed = _mm(jnp.concatenate([o_a, o_b], axis=-1), wout_ref[...]).reshape(nb, tt, d)
    gate = mod_ref[...][:, :, 2 * d:3 * d]
    y_ref[...] = _layer_norm(alpha * x_ref[...] + (1.0 + gate) * mixed, lng_ref[...], lnb_ref[...])


N_TOKEN_MIX_INPUTS = 21
N_TOKEN_MIX_OUTPUTS = 5


def _token_mix_kernel(*refs, n_cast, nb, tt, chunk, alpha):
    (x_ref, mod_ref, hdn_ref, hlru_ref, s0_ref, h0_ref, w_ref, cwdn_ref, cwlru_ref, cblru_ref, alog_ref, dtb_ref,
     wr_ref, wi_ref, br_ref, bi_ref, lam_ref, wout_ref, nw_ref, lng_ref, lnb_ref) = refs[:N_TOKEN_MIX_INPUTS]
    cast_in = refs[N_TOKEN_MIX_INPUTS:N_TOKEN_MIX_INPUTS + n_cast]
    outs = refs[N_TOKEN_MIX_INPUTS + n_cast:]
    y_ref, tdn_ref, tlru_ref, sout_ref, hout_ref = outs[:N_TOKEN_MIX_OUTPUTS]
    cast_out = outs[N_TOKEN_MIX_OUTPUTS:N_TOKEN_MIX_OUTPUTS + n_cast]
    wdn_sc, wlru_sc = outs[N_TOKEN_MIX_OUTPUTS + n_cast:]

    for src, dst in zip(cast_in, cast_out):
        dst[...] = src[...].astype(BF16)

    @pl.when(pl.program_id(1) == 0)
    def _():
        sout_ref[...] = s0_ref[0]
        hout_ref[...] = h0_ref[...]

    vals = {}
    stage1 = _inproj_stage(
        x_ref, mod_ref, hdn_ref, hlru_ref, w_ref, cwdn_ref, cwlru_ref, cblru_ref, alog_ref, dtb_ref,
        wr_ref, wi_ref, br_ref, bi_ref, lam_ref, tdn_ref, tlru_ref, wdn_sc, wlru_sc, vals, nb=nb, tt=tt)
    stage2 = _mixer_stage(vals, x_ref, mod_ref, wout_ref, nw_ref, lng_ref, lnb_ref,
                          y_ref, sout_ref, hout_ref, nb=nb, tt=tt, chunk=chunk, alpha=alpha)
    while "gb" not in vals:
        next(stage1)
    pending = [stage2, stage1]
    while pending:
        pending = [g for g in pending if next(g, StopIteration) is not StopIteration]


def _token_mix_call(x, mod, hist_dn, hist_lru, s_dn_all, layer, h0, lw_, cast=(), *, nb, tt, chunk, alpha):
    b, t, d = x.shape
    lw = hist_lru.shape[-1]
    grid = (b // nb, t // tt)
    steps = grid[0] * grid[1]
    slab = lambda a: pl.BlockSpec((a.shape[0] // steps, a.shape[1]), lambda i, j: (i * grid[1] + j, 0))
    seq_blk = lambda c: pl.BlockSpec((nb, tt, c), lambda i, j: (i, j, 0))
    per_seq = lambda r, c: pl.BlockSpec((nb, r, c), lambda i, j: (i, 0, 0))
    whole = lambda a: pl.BlockSpec(a.shape, lambda i, j: (0,) * a.ndim, pipeline_mode=pl.Buffered(1))
    state_in = pl.BlockSpec((1, nb, DN_HEADS, DN_DK, DN_DV), lambda i, j: (layer, i, 0, 0, 0))
    state_out = pl.BlockSpec((nb, DN_HEADS, DN_DK, DN_DV), lambda i, j: (i, 0, 0, 0))
    weights = [lw_[n] for n in ("w_cat", "dn_conv_w", "lru_conv_w", "lru_conv_b", "a_log", "dt_bias",
                                "w_r", "w_i", "b_r", "b_i", "lam", "w_out", "dn_norm_w", "ln1_g", "ln1_b")]
    return pl.pallas_call(
        functools.partial(_token_mix_kernel, n_cast=len(cast), nb=nb, tt=tt, chunk=chunk, alpha=alpha),
        out_shape=(
            jax.ShapeDtypeStruct((b, t, d), F32),
            jax.ShapeDtypeStruct((b, SUBLANES, DN_QKV), F32),
            jax.ShapeDtypeStruct((b, SUBLANES, lw), F32),
            jax.ShapeDtypeStruct((b, DN_HEADS, DN_DK, DN_DV), F32),
            jax.ShapeDtypeStruct((b, 1, lw), F32),
        ) + tuple(jax.ShapeDtypeStruct(a.shape, BF16) for a in cast),
        grid=grid,
        in_specs=[seq_blk(d), per_seq(1, N_MOD * d), per_seq(SUBLANES, DN_QKV), per_seq(SUBLANES, lw),
                  state_in, per_seq(1, lw)] + [whole(a) for a in weights] + [slab(a) for a in cast],
        out_specs=(seq_blk(d), per_seq(SUBLANES, DN_QKV), per_seq(SUBLANES, lw), state_out, per_seq(1, lw))
        + tuple(slab(a) for a in cast),
        scratch_shapes=[pltpu.VMEM((nb, SUBLANES + tt, DN_QKV), F32), pltpu.VMEM((nb, SUBLANES + tt, lw), F32)],
        compiler_params=_params(("arbitrary", "arbitrary")),
        name="token_mixer",
    )(x, mod, hist_dn, hist_lru, s_dn_all, h0, *weights, *cast)


def _ffn_kernel(x_ref, mod_ref, wg_ref, wu_ref, wd_ref, lng_ref, lnb_ref, *refs, alpha):
    n_cast = (len(refs) - 1) // 2
    y_ref = refs[n_cast]
    for src, dst in zip(refs[:n_cast], refs[n_cast + 1:]):
        dst[...] = src[...].astype(BF16)
    nb, tt, d = x_ref.shape
    m = mod_ref[...]
    h = (x_ref[...] * (1.0 + m[:, :, 4 * d:5 * d]) + m[:, :, 3 * d:4 * d]).reshape(nb * tt, d).astype(BF16)
    act = _silu(jnp.dot(h, wg_ref[...], preferred_element_type=F32)) * jnp.dot(
        h, wu_ref[...], preferred_element_type=F32)
    ff = _mm(act, wd_ref[...]).reshape(nb, tt, d)
    y_ref[...] = _layer_norm(alpha * x_ref[...] + (1.0 + m[:, :, 5 * d:6 * d]) * ff, lng_ref[...], lnb_ref[...])


def _ff_tile(ff):
    for n in (2, 4, 7, 8, 11, 14, 16, 22, 28):
        if ff % n == 0 and (ff // n) % LANES == 0 and ff // n <= 2048:
            return ff // n
    return ff


def _dense_ffn(x, mod, lw_, cast=(), *, nb, tt, alpha):
    b, t, d = x.shape
    nt = t // tt
    steps = (b // nb) * nt
    slab = lambda a: pl.BlockSpec((a.shape[0] // steps, a.shape[1]), lambda i: (i, 0))
    seq_blk = pl.BlockSpec((nb, tt, d), lambda i: (i // nt, i % nt, 0))
    whole = lambda a: pl.BlockSpec(a.shape, lambda i: (0,) * a.ndim)
    resident = lambda a: pl.BlockSpec(a.shape, lambda i: (0,) * a.ndim, pipeline_mode=pl.Buffered(1))
    return pl.pallas_call(
        functools.partial(_ffn_kernel, alpha=alpha),
        out_shape=(jax.ShapeDtypeStruct((b, t, d), F32),) + tuple(jax.ShapeDtypeStruct(a.shape, BF16) for a in cast),
        grid=(steps,),
        in_specs=[seq_blk, pl.BlockSpec((nb, 1, N_MOD * d), lambda i: (i // nt, 0, 0)),
                  resident(lw_["w_gate"]), resident(lw_["w_up"]), resident(lw_["w_down"]),
                  whole(lw_["ln2_g"]), whole(lw_["ln2_b"])] + [slab(a) for a in cast],
        out_specs=(seq_blk,) + tuple(slab(a) for a in cast),
        compiler_params=_params(("arbitrary",)),
        name="dense_ffn",
    )(x, mod, lw_["w_gate"], lw_["w_up"], lw_["w_down"], lw_["ln2_g"], lw_["ln2_b"], *cast)


MOE_CHUNK = 512
MOE_SLOT_TILE = 512
MOE_SUB = 256
MOE_COMBINE_BLOCK = 256
MOE_COMBINE_FANIN = 8
ROUTE_I1, ROUTE_I2, ROUTE_R1, ROUTE_R2, ROUTE_W1, ROUTE_W2 = range(6)


def _router_kernel(x_ref, mod_ref, wr_ref, base_ref, h_ref, meta_ref, meta_t_ref, blkbase_ref, cnt_ref, run_sc,
                   *, n_experts):
    nb, tt, d = x_ref.shape
    rows = nb * tt

    @pl.when(pl.program_id(0) == 0)
    def _():
        run_sc[...] = base_ref[...]

    m = mod_ref[...]
    h = (x_ref[...] * (1.0 + m[:, :, 4 * d:5 * d]) + m[:, :, 3 * d:4 * d]).reshape(rows, d)
    h_hi = h.astype(BF16)
    h_ref[...] = h_hi
    w = wr_ref[...]
    w_hi = w.astype(BF16)
    h_lo = (h - h_hi.astype(F32)).astype(BF16)
    w_lo = (w - w_hi.astype(F32)).astype(BF16)
    logits = (jnp.dot(h_hi, w_hi, preferred_element_type=F32) + jnp.dot(h_lo, w_hi, preferred_element_type=F32)
              + jnp.dot(h_hi, w_lo, preferred_element_type=F32))
    lane = lax.broadcasted_iota(jnp.int32, logits.shape, 1)
    neg = jnp.float32(-jnp.inf)
    lg = jnp.where(lane < n_experts, logits, neg)
    m1 = jnp.max(lg, axis=-1, keepdims=True)
    i1 = jnp.min(jnp.where(lg == m1, lane, LANES), axis=-1, keepdims=True)
    lg2 = jnp.where(lane == i1, neg, lg)
    m2 = jnp.max(lg2, axis=-1, keepdims=True)
    i2 = jnp.min(jnp.where(lg2 == m2, lane, LANES), axis=-1, keepdims=True)
    e2 = jnp.exp(m2 - m1)
    w1 = 1.0 / (1.0 + e2)
    w2 = e2 / (1.0 + e2)
    sel = jnp.where(lane == i1, 1.0, jnp.where(lane == i2, 1.0, 0.0))
    r = lax.broadcasted_iota(jnp.int32, (rows, rows), 0)
    c = lax.broadcasted_iota(jnp.int32, (rows, rows), 1)
    rank = _mm(jnp.where(c < r, 1.0, 0.0), sel) + run_sc[0:1, :]
    r1 = jnp.sum(jnp.where(lane == i1, rank, 0.0), axis=-1, keepdims=True)
    r2 = jnp.sum(jnp.where(lane == i2, rank, 0.0), axis=-1, keepdims=True)
    fields = (i1.astype(F32), i2.astype(F32), r1, r2, w1, w2)
    meta = jnp.zeros_like(logits)
    for k, v in enumerate(fields):
        meta = jnp.where(lane == k, v, meta)
    meta_ref[...] = meta
    meta_t_ref[...] = meta.T[0:SUBLANES, :]
    blkbase_ref[0] = run_sc[...]
    run_sc[...] = run_sc[...] + jnp.sum(sel, axis=0, keepdims=True)
    cnt_ref[...] = run_sc[...]


def _router(x, mod, w_router, base, *, nb, tt, n_experts):
    b, t, d = x.shape
    rows = nb * tt
    nt = t // tt
    nblk = (b // nb) * nt
    whole = lambda a: pl.BlockSpec(a.shape, lambda i: (0,) * a.ndim)
    return pl.pallas_call(
        functools.partial(_router_kernel, n_experts=n_experts),
        out_shape=(jax.ShapeDtypeStruct((nblk * rows, d), BF16),
                   jax.ShapeDtypeStruct((nblk * rows, LANES), F32),
                   jax.ShapeDtypeStruct((SUBLANES, nblk * rows), F32),
                   jax.ShapeDtypeStruct((nblk, SUBLANES, LANES), F32),
                   jax.ShapeDtypeStruct((SUBLANES, LANES), F32)),
        grid=(nblk,),
        in_specs=[pl.BlockSpec((nb, tt, d), lambda i: (i // nt, i % nt, 0)),
                  pl.BlockSpec((nb, 1, N_MOD * d), lambda i: (i // nt, 0, 0)),
                  whole(w_router), whole(base)],
        out_specs=(pl.BlockSpec((rows, d), lambda i: (i, 0)),
                   pl.BlockSpec((rows, LANES), lambda i: (i, 0)),
                   pl.BlockSpec((SUBLANES, rows), lambda i: (0, i)),
                   pl.BlockSpec((1, SUBLANES, LANES), lambda i: (i, 0, 0)),
                   pl.BlockSpec((SUBLANES, LANES), lambda i: (0, 0))),
        scratch_shapes=[pltpu.VMEM((SUBLANES, LANES), F32)],
        compiler_params=_params(("arbitrary",)),
        name="moe_router",
    )(x, mod, w_router, base)


def _slot_of(expert, rank, start_ref, n_experts):
    start = jnp.zeros_like(rank)
    for e in range(n_experts):
        start = jnp.where(expert == e, start_ref[e].astype(F32), start)
    return (start + rank).astype(jnp.int32)


def _gather_kernel(clo_ref, chi_ref, start_ref, *refs, group_chunks, n_experts):
    n_groups = len(group_chunks)
    m_refs = refs[:n_groups]
    h_refs = refs[n_groups:2 * n_groups]
    xs_ref, ws_ref, acc_sc, wacc_sc = refs[2 * n_groups:]
    g = pl.program_id(0)
    n_sub = MOE_SLOT_TILE // MOE_SUB
    for j in range(n_sub):
        q = g * n_sub + j
        ids = g * MOE_SLOT_TILE + j * MOE_SUB + lax.broadcasted_iota(jnp.int32, (MOE_SUB, MOE_CHUNK), 0)
        acc_sc[...] = jnp.zeros_like(acc_sc)
        wacc_sc[...] = jnp.zeros_like(wacc_sc)
        first = 0
        for m_ref, h_ref, n_chunks in zip(m_refs, h_refs, group_chunks):
            def body(c, carry, m_ref=m_ref, h_ref=h_ref):
                off = pl.multiple_of(c * MOE_CHUNK, MOE_CHUNK)
                rec = m_ref[:, pl.ds(off, MOE_CHUNK)]
                row = lambda k: rec[k:k + 1, :]
                hit1 = _slot_of(row(ROUTE_I1), row(ROUTE_R1), start_ref, n_experts) == ids
                hit2 = _slot_of(row(ROUTE_I2), row(ROUTE_R2), start_ref, n_experts) == ids
                p = jnp.where(hit1, 1.0, jnp.where(hit2, 1.0, 0.0)).astype(BF16)
                acc_sc[...] += jnp.dot(p, h_ref[pl.ds(off, MOE_CHUNK), :], preferred_element_type=F32)
                w = jnp.where(hit1, row(ROUTE_W1), 0.0) + jnp.where(hit2, row(ROUTE_W2), 0.0)
                wacc_sc[...] += jnp.broadcast_to(jnp.sum(w, axis=-1, keepdims=True), wacc_sc.shape)
                return carry

            lo = jnp.clip(clo_ref[q] - first, 0, n_chunks)
            hi = jnp.clip(chi_ref[q] - first, 0, n_chunks)
            lax.fori_loop(lo, hi, body, 0)
            first += n_chunks
        xs_ref[j * MOE_SUB:(j + 1) * MOE_SUB, :] = acc_sc[...].astype(BF16)
        ws_ref[j * MOE_SUB:(j + 1) * MOE_SUB, :] = wacc_sc[...]


def _gather_slots(c_lo, c_hi, run_start, metas_t, hs, n_tiles, n_experts):
    d = hs[0].shape[-1]
    vmem = pl.BlockSpec(memory_space=pltpu.VMEM)
    return pl.pallas_call(
        functools.partial(_gather_kernel, group_chunks=tuple(h.shape[0] // MOE_CHUNK for h in hs),
                          n_experts=n_experts),
        out_shape=(jax.ShapeDtypeStruct((n_tiles * MOE_SLOT_TILE, d), BF16),
                   jax.ShapeDtypeStruct((n_tiles * MOE_SLOT_TILE, LANES), F32)),
        grid_spec=pltpu.PrefetchScalarGridSpec(
            num_scalar_prefetch=3, grid=(n_tiles,),
            in_specs=[vmem] * (2 * len(hs)),
            out_specs=(pl.BlockSpec((MOE_SLOT_TILE, d), lambda g, lo, hi, st: (g, 0)),
                       pl.BlockSpec((MOE_SLOT_TILE, LANES), lambda g, lo, hi, st: (g, 0))),
            scratch_shapes=[pltpu.VMEM((MOE_SUB, d), F32), pltpu.VMEM((MOE_SUB, LANES), F32)]),
        compiler_params=_params(("arbitrary",)),
        name="moe_gather",
    )(c_lo, c_hi, run_start, *metas_t, *hs)


def _expert_kernel(te_ref, tv_ref, xs_ref, ws_ref, wg_ref, wu_ref, wd_ref, o_ref, *, ff_chunk):
    g = pl.program_id(0)

    @pl.when(tv_ref[g] != 0)
    def _():
        x = xs_ref[...]
        ff = wg_ref.shape[-1]
        acc = jnp.zeros(o_ref.shape, F32)
        for f0 in range(0, ff, ff_chunk):
            gate = jnp.dot(x, wg_ref[0, :, f0:f0 + ff_chunk], preferred_element_type=F32)
            up = jnp.dot(x, wu_ref[0, :, f0:f0 + ff_chunk], preferred_element_type=F32)
            acc = acc + _mm(_silu(gate) * up, wd_ref[0, f0:f0 + ff_chunk, :])
        o_ref[...] = (ws_ref[:, 0:1] * acc).astype(BF16)

    @pl.when(tv_ref[g] == 0)
    def _():
        o_ref[...] = jnp.zeros_like(o_ref)


def _expert_ffn(tile_expert, tile_valid, xs, ws, lw_):
    s_total, d = xs.shape
    _, _, ff = lw_["w_gate"].shape
    n_tiles = s_total // MOE_SLOT_TILE
    once = pl.Buffered(1)
    twice = pl.Buffered(2)
    return pl.pallas_call(
        functools.partial(_expert_kernel, ff_chunk=_ff_tile(ff)),
        out_shape=jax.ShapeDtypeStruct((s_total, d), BF16),
        grid_spec=pltpu.PrefetchScalarGridSpec(
            num_scalar_prefetch=2, grid=(n_tiles,),
            in_specs=[pl.BlockSpec((MOE_SLOT_TILE, d), lambda g, te, tv: (g, 0)),
                      pl.BlockSpec((MOE_SLOT_TILE, LANES), lambda g, te, tv: (g, 0)),
                      pl.BlockSpec((1, d, ff), lambda g, te, tv: (te[g], 0, 0), pipeline_mode=twice),
                      pl.BlockSpec((1, d, ff), lambda g, te, tv: (te[g], 0, 0), pipeline_mode=twice),
                      pl.BlockSpec((1, ff, d), lambda g, te, tv: (te[g], 0, 0), pipeline_mode=once)],
            out_specs=pl.BlockSpec((MOE_SLOT_TILE, d), lambda g, te, tv: (g, 0))),
        compiler_params=_params(("arbitrary",)),
        name="expert_ffn",
    )(tile_expert, tile_valid, xs, ws, lw_["w_gate"], lw_["w_up"], lw_["w_down"])


def _combine_kernel(ic_ref, ik_ref, ie_ref, if_ref, bf_ref, start_ref, x_ref, mod_ref, meta_ref, *refs, alpha,
                    n_experts):
    os_refs = refs[:MOE_COMBINE_FANIN]
    lng_ref, lnb_ref, y_ref, acc_sc, slot_sc = refs[MOE_COMBINE_FANIN:]
    nb, tt, d = x_ref.shape
    rows = nb * tt
    w = pl.program_id(0)
    flags = if_ref[w]

    @pl.when((flags & 1) != 0)
    def _():
        rec = meta_ref[...]
        col = lambda k: rec[:, k:k + 1]
        slot_sc[:, 0:1] = _slot_of(col(ROUTE_I1), col(ROUTE_R1), start_ref, n_experts)
        slot_sc[:, 1:2] = _slot_of(col(ROUTE_I2), col(ROUTE_R2), start_ref, n_experts)

    def window_sum():
        lane = lax.broadcasted_iota(jnp.int32, (rows, MOE_COMBINE_BLOCK), 1)
        s1 = slot_sc[:, 0:1]
        s2 = slot_sc[:, 1:2]
        total = None
        for j, os_ref in enumerate(os_refs):
            ids = ik_ref[w * MOE_COMBINE_FANIN + j] + lane
            ids = jnp.where(ids < ie_ref[w * MOE_COMBINE_FANIN + j], ids, -1)
            q = jnp.where(s1 == ids, 1.0, jnp.where(s2 == ids, 1.0, 0.0)).astype(BF16)
            part = jnp.dot(q, os_ref[...], preferred_element_type=F32)
            total = part if total is None else total + part
        return total

    @pl.when((flags & 1) != 0)
    def _():
        acc_sc[...] = window_sum()

    @pl.when((flags & 5) == 4)
    def _():
        acc_sc[...] += window_sum()

    @pl.when((flags & 2) != 0)
    def _():
        gate = mod_ref[...][:, :, 5 * d:6 * d]
        ff = acc_sc[...].reshape(nb, tt, d)
        y_ref[...] = _layer_norm(alpha * x_ref[...] + (1.0 + gate) * ff, lng_ref[...], lnb_ref[...])


def _combine(items, run_start, x, mod, meta, out_sorted, lw_, *, nb, tt, alpha, n_experts):
    step_chunk, win_start, win_end, step_flags, win_fetch = items
    b, t, d = x.shape
    rows = nb * tt
    nt = t // tt
    whole = lambda a: pl.BlockSpec(a.shape, lambda w, ic, ik, ie, fl, bf, st: (0,) * a.ndim)
    seq_blk = pl.BlockSpec((nb, tt, d), lambda w, ic, ik, ie, fl, bf, st: (ic[w] // nt, ic[w] % nt, 0))
    slot_blk = lambda j: pl.BlockSpec(
        (pl.Element(MOE_COMBINE_BLOCK), pl.Element(d)),
        lambda w, ic, ik, ie, fl, bf, st: (pl.multiple_of(bf[w * MOE_COMBINE_FANIN + j], 2 * SUBLANES), 0))
    return pl.pallas_call(
        functools.partial(_combine_kernel, alpha=alpha, n_experts=n_experts),
        out_shape=jax.ShapeDtypeStruct((b, t, d), F32),
        grid_spec=pltpu.PrefetchScalarGridSpec(
            num_scalar_prefetch=6, grid=(step_chunk.shape[0],),
            in_specs=[seq_blk,
                      pl.BlockSpec((nb, 1, N_MOD * d), lambda w, ic, ik, ie, fl, bf, st: (ic[w] // nt, 0, 0)),
                      pl.BlockSpec((rows, LANES), lambda w, ic, ik, ie, fl, bf, st: (ic[w], 0))]
            + [slot_blk(j) for j in range(MOE_COMBINE_FANIN)]
            + [whole(lw_["ln2_g"]), whole(lw_["ln2_b"])],
            out_specs=seq_blk,
            scratch_shapes=[pltpu.VMEM((rows, d), F32), pltpu.VMEM((rows, LANES), jnp.int32)]),
        compiler_params=_params(("arbitrary",)),
        name="moe_combine",
    )(step_chunk, win_start, win_end, step_flags, win_fetch, run_start, x, mod, meta,
      *([out_sorted] * MOE_COMBINE_FANIN), lw_["ln2_g"], lw_["ln2_b"])


def _count_le(sorted_vals, queries):
    return jnp.sum(sorted_vals[None, :] <= queries[:, None], axis=1).astype(jnp.int32)


def _combine_steps(lo, hi, n_steps):
    n_chunks, n_experts = lo.shape
    fan = MOE_COMBINE_FANIN
    align = 2 * SUBLANES
    lo_f, hi_f = lo.reshape(-1), hi.reshape(-1)
    first = (lo_f // align) * align
    count = jnp.where(hi_f > lo_f, (hi_f - first + MOE_COMBINE_BLOCK - 1) // MOE_COMBINE_BLOCK, 0)
    pair_end = jnp.cumsum(count)
    chunk_items = jnp.sum(count.reshape(n_chunks, n_experts), axis=1)
    chunk_item0 = jnp.cumsum(chunk_items) - chunk_items
    chunk_steps = (chunk_items + fan - 1) // fan
    step_end = jnp.cumsum(chunk_steps)
    total_steps = step_end[-1]
    s = jnp.arange(n_steps, dtype=jnp.int32)
    live = s < total_steps
    chunk = jnp.minimum(_count_le(step_end, s), n_chunks - 1)
    chunk = jnp.where(live, chunk, chunk[jnp.maximum(total_steps - 1, 0)])
    q = s - (step_end[chunk] - chunk_steps[chunk])
    j = q[:, None] * fan + jnp.arange(fan, dtype=jnp.int32)[None, :]
    used = live[:, None] & (j < chunk_items[chunk][:, None])
    item = jnp.where(used, chunk_item0[chunk][:, None] + j, 0).reshape(-1)
    pair = jnp.minimum(_count_le(pair_end, item), n_chunks * n_experts - 1)
    start = first[pair] + (item - (pair_end[pair] - count[pair])) * MOE_COMBINE_BLOCK
    fetch = jnp.where(used.reshape(-1), start, jnp.repeat(start.reshape(-1, fan)[:, 0], fan))
    end = jnp.where(used.reshape(-1), hi_f[pair], 0)
    flags = (jnp.where(live & (q == 0), 1, 0) + jnp.where(live & (q == chunk_steps[chunk] - 1), 2, 0)
             + jnp.where(live, 4, 0))
    as_i32 = lambda a: a.astype(jnp.int32)
    return as_i32(chunk), as_i32(start), as_i32(end), as_i32(flags), as_i32(fetch)


def _moe_layer(xs_in, mods, lw_, tilings, alpha):
    n_experts = lw_["w_gate"].shape[0]
    base = jnp.zeros((SUBLANES, LANES), F32)
    hs, metas, metas_t, bases = [], [], [], []
    for x, mod, (nb, tt) in zip(xs_in, mods, tilings):
        assert nb * tt == MOE_CHUNK and (x.shape[0] * x.shape[1]) % MOE_CHUNK == 0
        h, meta, meta_t, blkbase, base = _router(x, mod, lw_["w_router"], base, nb=nb, tt=tt,
                                                 n_experts=n_experts)
        hs.append(h)
        metas.append(meta)
        metas_t.append(meta_t)
        bases.append(blkbase[:, 0, :n_experts])
    n = sum(m.shape[0] for m in metas)
    cum = jnp.concatenate(bases + [base[0:1, :n_experts]], axis=0).astype(jnp.int32)
    counts = cum[-1]
    sizes = ((counts + MOE_SLOT_TILE - 1) // MOE_SLOT_TILE) * MOE_SLOT_TILE
    run_end = jnp.cumsum(sizes)
    run_start = (run_end - sizes).astype(jnp.int32)

    n_tiles = (TOP_K * n + MOE_SLOT_TILE - 1) // MOE_SLOT_TILE + n_experts + 1
    tile_start = jnp.arange(n_tiles, dtype=jnp.int32) * MOE_SLOT_TILE
    tile_expert = jnp.minimum(_count_le(run_end, tile_start), n_experts - 1)
    tile_valid = (tile_start < run_end[-1]).astype(jnp.int32)
    n_sub = MOE_SLOT_TILE // MOE_SUB
    sub_expert = jnp.repeat(tile_expert, n_sub)
    sub_rank0 = jnp.arange(n_tiles * n_sub, dtype=jnp.int32) * MOE_SUB - run_start[sub_expert]
    cum_sub = cum[:, sub_expert]
    sub_valid = jnp.repeat(tile_valid, n_sub)
    c_lo = jnp.sum(cum_sub[1:] <= sub_rank0[None, :], axis=0).astype(jnp.int32) * sub_valid
    c_hi = jnp.sum(cum_sub[:-1] < sub_rank0[None, :] + MOE_SUB, axis=0).astype(jnp.int32) * sub_valid

    x_sorted, w_sorted = _gather_slots(c_lo, c_hi, run_start, metas_t, hs, n_tiles, n_experts)
    out_sorted = _expert_ffn(tile_expert, tile_valid, x_sorted, w_sorted, lw_)

    outs = []
    chunk0 = 0
    for x, mod, meta, (nb, tt) in zip(xs_in, mods, metas, tilings):
        nc = x.shape[0] * x.shape[1] // MOE_CHUNK
        lo = run_start[None, :] + cum[chunk0:chunk0 + nc]
        hi = run_start[None, :] + cum[chunk0 + 1:chunk0 + nc + 1]
        max_items = nc * n_experts + (TOP_K * nc * MOE_CHUNK) // MOE_COMBINE_BLOCK + 2 * n_experts
        items = _combine_steps(lo, hi, max_items // MOE_COMBINE_FANIN + nc)
        outs.append(_combine(items, run_start, x, mod, meta, out_sorted, lw_, nb=nb, tt=tt, alpha=alpha,
                             n_experts=n_experts))
        chunk0 += nc
    return outs


def _pad_lanes(v, width=LANES):
    return jnp.pad(v, ((0, 0), (0, width - v.shape[-1])))


def _block_diag_halves(w):
    nblk, c, _ = w.shape
    half = nblk // 2
    out = jnp.zeros((2, half * c, half * c), w.dtype)
    for i in range(nblk):
        j = i % half
        out = out.at[i // half, j * c:(j + 1) * c, j * c:(j + 1) * c].set(w[i])
    return out


def _layer_weights(l, p, d, lw):
    w_in = p["w_in"][l]
    a_off = DN_QKV
    z_off = a_off + 2 * DN_HEADS
    x_off = z_off + DN_WIDTH
    y_off = x_off + lw
    w_ab = _pad_lanes(w_in[:, a_off:z_off])
    w_cat = jnp.concatenate([w_in[:, :DN_QKV], w_in[:, z_off:y_off + lw], w_ab], axis=1).astype(BF16)
    out = {
        "w_cat": w_cat,
        "dn_conv_w": p["dn_conv_w"][l],
        "lru_conv_w": p["lru_conv_w"][l],
        "lru_conv_b": p["lru_conv_b"][l][None],
        "a_log": _pad_lanes(p["dn_a_log"][l][None]),
        "dt_bias": _pad_lanes(p["dn_dt_bias"][l][None]),
        "w_r": _block_diag_halves(p["lru_w_r"][l]).astype(BF16),
        "w_i": _block_diag_halves(p["lru_w_i"][l]).astype(BF16),
        "b_r": p["lru_b_r"][l][None],
        "b_i": p["lru_b_i"][l][None],
        "lam": p["lru_lambda"][l][None],
        "w_out": p["w_out"][l].astype(BF16),
        "dn_norm_w": p["dn_norm_w"][l][None],
        "ln1_g": p["ln1_g"][l][None],
        "ln1_b": p["ln1_b"][l][None],
        "ln2_g": p["ln2_g"][l][None],
        "ln2_b": p["ln2_b"][l][None],
    }
    j = l // 2
    if l % 2 == 0:
        out.update(w_gate=p["ffn_w_gate"][j], w_up=p["ffn_w_up"][j], w_down=p["ffn_w_down"][j])
    else:
        out.update(w_router=_pad_lanes(p["moe_w_router"][j]),
                   w_gate=p["moe_w_gate"][j], w_up=p["moe_w_up"][j], w_down=p["moe_w_down"][j])
    return out


CHANNEL_MIX_WEIGHTS = ("w_gate", "w_up", "w_down")


LARGE_SLAB_BYTES = 64 * 1024 * 1024


def _cast_plan(weights, host_steps):
    hosts = list(host_steps)
    plan = {h: [] for h in hosts}
    for k in reversed(range(len(weights))):
        dense = ("ffn", k) in host_steps
        last = hosts.index(("ffn", k)) if dense else hosts.index(("mixer", k)) + 1
        for n in CHANNEL_MIX_WEIGHTS:
            a = weights[k][n].reshape(-1, weights[k][n].shape[-1])
            large = a.size * 4 > LARGE_SLAB_BYTES
            for h in reversed(hosts[:last]):
                fits = a.shape[0] % (host_steps[h] * 2 * SUBLANES) == 0
                busy = large and any(s.size * 4 > LARGE_SLAB_BYTES for _, _, s in plan[h])
                if fits and not busy:
                    plan[h].append((k, n, a))
                    break
    return plan


def _tiling(b, t):
    if t >= MXU_DIM:
        tt = MXU_DIM
        return dict(mixer=(2 if b % 2 == 0 else 1, tt, min(PROMPT_CHUNK, tt)), ffn=(1, min(t, 2 * MXU_DIM)))
    assert t == SUBLANES, "short sequences must be exactly one sublane tile long"
    return dict(mixer=(min(b, 16), t, t), ffn=(min(b, 64), t))


def _token_mix(x, mod, conv_dn, s_dn_all, layer, conv_lru, s_lru, lw_, til, alpha, cast=()):
    pad_hist = lambda c: jnp.pad(c, ((0, 0), (SUBLANES - (CONV_W - 1), 0), (0, 0)))
    nb, tt, chunk = til["mixer"]
    x, tail_dn, tail_lru, s_new, h_new, *converted = _token_mix_call(
        x, mod, pad_hist(conv_dn), pad_hist(conv_lru), s_dn_all, layer, s_lru[:, None, :], lw_, cast,
        nb=nb, tt=tt, chunk=chunk, alpha=alpha)
    return (x, tail_dn[:, SUBLANES - (CONV_W - 1):, :], s_new, tail_lru[:, SUBLANES - (CONV_W - 1):, :],
            h_new[:, 0, :]), converted


def kernel(x_prompt, x_sample, cache_dn_conv, state_dn, cache_lru_conv, state_lru, c_prompt, c_sample,
           w_ada, b_ada, w_in, dn_conv_w, dn_a_log, dn_dt_bias, dn_norm_w,
           lru_conv_w, lru_conv_b, lru_w_r, lru_b_r, lru_w_i, lru_b_i, lru_lambda, w_out,
           ln1_g, ln1_b, ln2_g, ln2_b, ffn_w_gate, ffn_w_up, ffn_w_down,
           moe_w_router, moe_w_gate, moe_w_up, moe_w_down):
    p = dict(w_in=w_in, dn_conv_w=dn_conv_w, dn_a_log=dn_a_log, dn_dt_bias=dn_dt_bias, dn_norm_w=dn_norm_w,
             lru_conv_w=lru_conv_w, lru_conv_b=lru_conv_b, lru_w_r=lru_w_r, lru_b_r=lru_b_r,
             lru_w_i=lru_w_i, lru_b_i=lru_b_i, lru_lambda=lru_lambda, w_out=w_out,
             ln1_g=ln1_g, ln1_b=ln1_b, ln2_g=ln2_g, ln2_b=ln2_b,
             ffn_w_gate=ffn_w_gate, ffn_w_up=ffn_w_up, ffn_w_down=ffn_w_down,
             moe_w_router=moe_w_router, moe_w_gate=moe_w_gate, moe_w_up=moe_w_up, moe_w_down=moe_w_down)
    depth, d, _ = w_ada.shape
    bp = x_prompt.shape[0]
    bs = x_sample.shape[0]
    lw = cache_lru_conv.shape[-1]
    alpha = (2 * depth) ** 0.25
    weights = [_layer_weights(l, p, d, lw) for l in range(depth)]

    c_all = jnp.concatenate([c_prompt, c_sample], axis=0)
    mod_all = _modulation(c_all, w_ada, b_ada)
    groups = [
        dict(x=x_prompt, rows=slice(0, bp), conv_dn=jnp.zeros((depth, bp, CONV_W - 1, DN_QKV), F32),
             s_dn=jnp.zeros((depth, bp, DN_HEADS, DN_DK, DN_DV), F32),
             conv_lru=jnp.zeros((depth, bp, CONV_W - 1, lw), F32), s_lru=jnp.zeros((depth, bp, lw), F32)),
        dict(x=x_sample, rows=slice(bp, bp + bs), conv_dn=cache_dn_conv, s_dn=state_dn,
             conv_lru=cache_lru_conv, s_lru=state_lru),
    ]
    for g in groups:
        g["til"] = _tiling(g["x"].shape[0], g["x"].shape[1])
        g["new"] = [[], [], [], []]
    t_p = x_prompt.shape[1]
    nb_m, tt_m, _ = groups[0]["til"]["mixer"]
    nb_f, tt_f = groups[0]["til"]["ffn"]
    host_steps = {}
    for l in range(depth):
        host_steps[("mixer", l)] = (bp // nb_m) * (t_p // tt_m)
        if l % 2 == 0:
            host_steps[("ffn", l)] = (bp // nb_f) * (t_p // tt_f)
    plan = _cast_plan(weights, host_steps)
    planned = {(k, n) for slabs in plan.values() for k, n, _ in slabs}
    for k, w in enumerate(weights):
        for n in CHANNEL_MIX_WEIGHTS:
            if (k, n) not in planned:
                w[n] = w[n].astype(BF16)

    def adopt(slabs, converted):
        for (k, n, _), c in zip(slabs, converted):
            weights[k][n] = c.reshape(weights[k][n].shape)

    for l in range(depth):
        lw_ = weights[l]
        mods = [mod_all[l, g["rows"]][:, None, :] for g in groups]
        for gi, (g, mod) in enumerate(zip(groups, mods)):
            slabs = plan[("mixer", l)] if gi == 0 else []
            res, converted = _token_mix(g["x"], mod, g["conv_dn"][l], g["s_dn"], l, g["conv_lru"][l],
                                        g["s_lru"][l], lw_, g["til"], alpha,
                                        cast=tuple(a for _, _, a in slabs))
            adopt(slabs, converted)
            g["x"] = res[0]
            for acc, new in zip(g["new"], res[1:]):
                acc.append(new)
        if l % 2 == 0:
            for gi, (g, mod) in enumerate(zip(groups, mods)):
                nb, tt = g["til"]["ffn"]
                slabs = plan[("ffn", l)] if gi == 0 else []
                g["x"], *converted = _dense_ffn(g["x"], mod, lw_, tuple(a for _, _, a in slabs),
                                                nb=nb, tt=tt, alpha=alpha)
                adopt(slabs, converted)
        else:
            xs = _moe_layer([g["x"] for g in groups], mods, lw_, [g["til"]["ffn"] for g in groups], alpha)
            for g, x in zip(groups, xs):
                g["x"] = x
    states = [jnp.stack(acc) for g in groups for acc in g["new"]]
    return (groups[0]["x"], groups[1]["x"]) + tuple(states)
```

```python
import functools
import math

import jax
import jax.numpy as jnp
from jax import lax
from jax.experimental import pallas as pl
from jax.experimental.pallas import tpu as pltpu

F32 = jnp.float32
BF16 = jnp.bfloat16

DN_HEADS = 4
DN_DK = 128
DN_DV = 128
DN_WIDTH = DN_HEADS * DN_DV
DN_QKV = 3 * DN_WIDTH
LRU_BLOCKS = 8
LRU_C = 8.0
CONV_W = 4
N_MOD = 6
TOP_K = 2
LN_EPS = 1e-5
NORM_EPS = 1e-6

SUBLANES = 8
LANES = 128
MXU_DIM = 256
VMEM_LIMIT_BYTES = 56 * 1024 * 1024

PROMPT_CHUNK = 64
INV_BASE_BLOCK = 16


def _sigmoid(x):
    return 0.5 + 0.5 * jnp.tanh(0.5 * x)


def _silu(x):
    half = 0.5 * x
    return half + half * jnp.tanh(half)


def _softplus(x):
    return jnp.maximum(x, 0.0) + jnp.log1p(jnp.exp(-jnp.abs(x)))


def _gelu_tanh(x):
    return 0.5 * x * (1.0 + jnp.tanh(math.sqrt(2.0 / math.pi) * (x + 0.044715 * (x * x * x))))


def _mm(a, b):
    return jnp.dot(a.astype(BF16), b.astype(BF16), preferred_element_type=F32)


def _layer_norm(x, g, b):
    mu = jnp.mean(x, axis=-1, keepdims=True)
    xc = x - mu
    var = jnp.mean(xc * xc, axis=-1, keepdims=True)
    return xc * lax.rsqrt(var + LN_EPS) * g + b


def _params(sem):
    return pltpu.CompilerParams(dimension_semantics=sem, vmem_limit_bytes=VMEM_LIMIT_BYTES)


def _mod_kernel(c_ref, w_ref, b_ref, o_ref):
    sc = _silu(c_ref[...])
    o_ref[0] = _mm(sc, w_ref[0]) + b_ref[0]


def _modulation(c_all, w_ada, b_ada):
    depth, d, n = w_ada.shape
    rows = c_all.shape[0]
    tn = 1536 if n % 1536 == 0 else n
    return pl.pallas_call(
        _mod_kernel,
        out_shape=jax.ShapeDtypeStruct((depth, rows, n), F32),
        grid=(depth, n // tn),
        in_specs=[
            pl.BlockSpec((rows, d), lambda l, j: (0, 0)),
            pl.BlockSpec((1, d, tn), lambda l, j: (l, 0, j)),
            pl.BlockSpec((1, 1, tn), lambda l, j: (l, 0, j)),
        ],
        out_specs=pl.BlockSpec((1, rows, tn), lambda l, j: (l, 0, j)),
        compiler_params=_params(("arbitrary", "arbitrary")),
        name="adaln_modulation",
    )(c_all, w_ada, b_ada.reshape(depth, 1, n))


def _causal_conv(u, win_ref, w, nb, tt, c0):
    c = u.shape[-1]
    cols = slice(c0, c0 + c)
    win_ref[:, SUBLANES:, cols] = u.reshape(nb, tt, c)
    out = u * w[CONV_W - 1:CONV_W, cols]
    for j in range(1, CONV_W):
        prev = win_ref[:, SUBLANES - j:SUBLANES - j + tt, cols].reshape(nb * tt, c)
        out = out + prev * w[CONV_W - 1 - j:CONV_W - j, cols]
    tail = win_ref[:, tt:tt + SUBLANES, cols]
    win_ref[:, 0:SUBLANES, cols] = tail
    return out, tail


def _l2norm_heads(x, scale):
    outs = []
    for h in range(DN_HEADS):
        xh = x[:, h * DN_DK:(h + 1) * DN_DK]
        ss = jnp.sum(xh * xh, axis=-1, keepdims=True)
        outs.append(xh * (lax.rsqrt(ss + NORM_EPS) * scale))
    return jnp.concatenate(outs, axis=-1)


def _inproj_stage(x_ref, mod_ref, hdn_ref, hlru_ref, w_ref, cwdn_ref, cwlru_ref, cblru_ref,
                  alog_ref, dtb_ref, wr_ref, wi_ref, br_ref, bi_ref, lam_ref,
                  tdn_ref, tlru_ref, wdn_sc, wlru_sc, out, *, nb, tt):
    d = x_ref.shape[-1]
    lw = hlru_ref.shape[-1]
    rows = nb * tt

    @pl.when(pl.program_id(1) == 0)
    def _():
        wdn_sc[:, 0:SUBLANES, :] = hdn_ref[...]
        wlru_sc[:, 0:SUBLANES, :] = hlru_ref[...]

    m = mod_ref[...]
    shift = m[:, :, 0:d]
    scale = m[:, :, d:2 * d]
    h = (x_ref[...] * (1.0 + scale) + shift).reshape(rows, d)
    proj = _mm(h, w_ref[...])
    z = proj[:, DN_QKV:DN_QKV + DN_WIDTH]
    u_lru = proj[:, DN_QKV + DN_WIDTH:DN_QKV + DN_WIDTH + lw]
    y = proj[:, DN_QKV + DN_WIDTH + lw:DN_QKV + DN_WIDTH + 2 * lw]
    ab = proj[:, DN_QKV + DN_WIDTH + 2 * lw:]

    cw_dn = cwdn_ref[...]

    def dn_part(part, norm_scale):
        c0 = part * DN_WIDTH
        conv, tail = _causal_conv(proj[:, c0:c0 + DN_WIDTH], wdn_sc, cw_dn, nb, tt, c0)
        tdn_ref[:, :, c0:c0 + DN_WIDTH] = tail
        act = _silu(conv)
        return act if norm_scale is None else _l2norm_heads(act, norm_scale)

    out["q"] = dn_part(0, DN_DK ** -0.5)
    yield
    out["k"] = dn_part(1, 1.0)
    lane = lax.broadcasted_iota(jnp.int32, ab.shape, 1)
    g_full = -jnp.exp(alog_ref[...]) * _softplus(ab + dtb_ref[...])
    out["gb"] = jnp.where(lane < DN_HEADS, g_full, _sigmoid(ab))
    yield
    out["v"] = dn_part(2, None)
    yield
    out["gz"] = _silu(z)
    out["gy"] = _gelu_tanh(y)
    yield

    conv_lru, tail_lru = _causal_conv(u_lru, wlru_sc, cwlru_ref[...], nb, tt, 0)
    tlru_ref[...] = tail_lru
    xc = conv_lru + cblru_ref[...]
    half = lw // 2
    r_pre = jnp.concatenate([_mm(xc[:, :half], wr_ref[0]), _mm(xc[:, half:], wr_ref[1])], axis=-1)
    i_pre = jnp.concatenate([_mm(xc[:, :half], wi_ref[0]), _mm(xc[:, half:], wi_ref[1])], axis=-1)
    r = _sigmoid(r_pre + br_ref[...])
    i = _sigmoid(i_pre + bi_ref[...])
    log_a = -LRU_C * r * _softplus(-lam_ref[...])
    out["a"] = jnp.exp(log_a)
    th = jnp.tanh(log_a)
    out["inp"] = jnp.sqrt(-2.0 * th / (1.0 - th)) * (i * xc)
    yield


def _unit_lower_inverses(ls, row, col, chunk, base):
    def same_block(s):
        k = s.bit_length() - 1
        return (row >> k) == (col >> k)

    eye = jnp.where(row == col, 1.0, 0.0)
    base_mask = same_block(base)
    powers = [jnp.where(base_mask, l, 0.0) for l in ls]
    invs = [eye - d for d in powers]
    p = 2
    while p < base:
        powers = [_mm(d, d) for d in powers]
        invs = [t + _mm(t, d) for t, d in zip(invs, powers)]
        p *= 2
        yield
    s = base
    while s < chunk:
        off_mask = same_block(2 * s) & jnp.logical_not(same_block(s))
        tmp = [_mm(jnp.where(off_mask, l, 0.0), t) for l, t in zip(ls, invs)]
        invs = [t - _mm(t, x) for t, x in zip(invs, tmp)]
        s *= 2
        yield
    return invs


def _mixer_stage(vals, x_ref, mod_ref, wout_ref, nw_ref, lng_ref, lnb_ref,
                 y_ref, sout_ref, hout_ref, *, nb, tt, chunk, alpha):
    d = x_ref.shape[-1]
    lw = hout_ref.shape[-1]
    rows = nb * tt
    grows = min(rows, MXU_DIM)
    n_groups = rows // grows
    seqs_per_group = max(grows // tt, 1)
    n_chunks = grows // chunk
    chunks_per_seq = tt // chunk if tt >= chunk else 1
    log_chunk = chunk.bit_length() - 1

    row = lax.broadcasted_iota(jnp.int32, (grows, grows), 0)
    col = lax.broadcasted_iota(jnp.int32, (grows, grows), 1)
    incl = ((row >> log_chunk) == (col >> log_chunk)) & (col <= row)
    strict = incl & (col < row)
    gb = vals["gb"]
    pieces = []
    rest = gb
    for _ in range(3):
        piece = rest.astype(BF16)
        pieces.append(piece)
        rest = rest - piece.astype(F32)
    split = jnp.concatenate(pieces, axis=-1)
    incl_b = jnp.where(incl, 1.0, 0.0).astype(BF16)
    chains = [(g, h) for g in range(n_groups) for h in range(DN_HEADS)]
    gcs, gc_ts = [], []
    for g in range(n_groups):
        sums = jnp.dot(incl_b, split[g * grows:(g + 1) * grows], preferred_element_type=F32)
        gc = sums[:, 0:LANES] + sums[:, LANES:2 * LANES] + sums[:, 2 * LANES:3 * LANES]
        gcs.append(gc)
        gc_ts.append(gc.T)

    rsl = lambda g: slice(g * grows, (g + 1) * grows)
    qs = [vals["q"][rsl(g), h * DN_DK:(h + 1) * DN_DK] for g, h in chains]
    ks = [vals["k"][rsl(g), h * DN_DK:(h + 1) * DN_DK] for g, h in chains]
    gcols = [gcs[g][:, h:h + 1] for g, h in chains]
    betas = [gb[rsl(g), DN_HEADS + h:DN_HEADS + h + 1] for g, h in chains]
    decays = [jnp.where(incl, jnp.exp(jnp.where(incl, gcols[i] - gc_ts[g][h:h + 1, :], 0.0)), 0.0)
              for i, (g, h) in enumerate(chains)]
    idx = range(len(chains))
    kbs = [ks[i] * betas[i] for i in idx]
    qk_kks = [lax.dot_general(jnp.concatenate([qs[i], kbs[i]], axis=0).astype(BF16), ks[i].astype(BF16),
                              (((1,), (1,)), ((), ())), preferred_element_type=F32) for i in idx]
    qks = [qk_kks[i][:grows] * decays[i] for i in idx]
    lmats = [jnp.where(strict, qk_kks[i][grows:] * decays[i], 0.0) for i in idx]
    yield
    tmats = yield from _unit_lower_inverses(lmats, row, col, chunk, min(INV_BASE_BLOCK, chunk))
    egcs = [jnp.exp(gcol) for gcol in gcols]
    vs = [vals["v"][rsl(g), h * DN_DV:(h + 1) * DN_DV] for g, h in chains]
    uws = [_mm(tmats[i], jnp.concatenate([vs[i] * betas[i], kbs[i] * egcs[i]], axis=-1)) for i in idx]
    us = [uw[:, :DN_DV] for uw in uws]
    ws = [uw[:, DN_DV:] for uw in uws]
    qes = [qs[i] * egcs[i] for i in idx]
    yield

    a = vals["a"]
    bacc = vals["inp"]
    gy = vals["gy"]
    t = lax.broadcasted_iota(jnp.int32, (rows, lw), 0) & (SUBLANES - 1)
    s = 1
    while s < SUBLANES:
        keep = t >= s
        a_prev = jnp.where(keep, pltpu.roll(a, s, 0), 1.0)
        b_prev = jnp.where(keep, pltpu.roll(bacc, s, 0), 0.0)
        bacc = a * b_prev + bacc
        a = a * a_prev
        s *= 2
    tiles = tt // SUBLANES
    a4 = a.reshape(nb, tiles, SUBLANES, lw)
    b4 = bacc.reshape(nb, tiles, SUBLANES, lw)
    carry = hout_ref[...]
    h_tiles = []
    for k in range(tiles):
        hk = a4[:, k] * carry + b4[:, k]
        carry = hk[:, SUBLANES - 1:SUBLANES, :]
        h_tiles.append(hk)
    hout_ref[...] = carry
    hs = h_tiles[0] if tiles == 1 else jnp.stack(h_tiles, axis=1)
    o_b = hs.reshape(rows, lw) * gy
    yield

    v_new_parts = [[] for _ in idx]
    o_inter_parts = [[] for _ in idx]
    states = [None for _ in idx]
    for c in range(n_chunks):
        lo, hi = c * chunk, (c + 1) * chunk
        seq_of = lambda g: g * seqs_per_group + c // chunks_per_seq
        if c % chunks_per_seq == 0:
            states = [sout_ref[seq_of(g), h] for g, h in chains]
        wqs = [_mm(jnp.concatenate([ws[i][lo:hi], qes[i][lo:hi]], axis=0), states[i]) for i in idx]
        v_news = [us[i][lo:hi] - wqs[i][:chunk] for i in idx]
        new_states = []
        for i in idx:
            v_new_parts[i].append(v_news[i])
            o_inter_parts[i].append(wqs[i][chunk:])
            g_last = gcols[i][hi - 1:hi, :]
            k_dec = ks[i][lo:hi] * jnp.exp(g_last - gcols[i][lo:hi])
            new_states.append(states[i] * jnp.exp(g_last) + lax.dot_general(
                k_dec.astype(BF16), v_news[i].astype(BF16), (((0,), (0,)), ((), ())),
                preferred_element_type=F32))
        states = new_states
        if (c + 1) % chunks_per_seq == 0:
            for i, (g, h) in enumerate(chains):
                sout_ref[seq_of(g), h] = states[i]
        yield
    o_groups = []
    for g in range(n_groups):
        o_heads = []
        for h in range(DN_HEADS):
            i = g * DN_HEADS + h
            o = (jnp.concatenate(o_inter_parts[i], axis=0)
                 + _mm(qks[i], jnp.concatenate(v_new_parts[i], axis=0)))
            ms = jnp.mean(o * o, axis=-1, keepdims=True)
            o_heads.append(o * lax.rsqrt(ms + NORM_EPS) * nw_ref[...]
                           * vals["gz"][rsl(g), h * DN_DV:(h + 1) * DN_DV])
        o_groups.append(jnp.concatenate(o_heads, axis=-1))
    o_a = o_groups[0] if n_groups == 1 else jnp.concatenate(o_groups, axis=0)

    mixed = _mm(jnp.concatenate([o_a, o_b], axis=-1), wout_ref[...]).reshape(nb, tt, d)
    gate = mod_ref[...][:, :, 2 * d:3 * d]
    y_ref[...] = _layer_norm(alpha * x_ref[...] + (1.0 + gate) * mixed, lng_ref[...], lnb_ref[...])


N_TOKEN_MIX_INPUTS = 21
N_TOKEN_MIX_OUTPUTS = 5


def _token_mix_kernel(*refs, n_cast, nb, tt, chunk, alpha):
    (x_ref, mod_ref, hdn_ref, hlru_ref, s0_ref, h0_ref, w_ref, cwdn_ref, cwlru_ref, cblru_ref, alog_ref, dtb_ref,
     wr_ref, wi_ref, br_ref, bi_ref, lam_ref, wout_ref, nw_ref, lng_ref, lnb_ref) = refs[:N_TOKEN_MIX_INPUTS]
    cast_in = refs[N_TOKEN_MIX_INPUTS:N_TOKEN_MIX_INPUTS + n_cast]
    outs = refs[N_TOKEN_MIX_INPUTS + n_cast:]
    y_ref, tdn_ref, tlru_ref, sout_ref, hout_ref = outs[:N_TOKEN_MIX_OUTPUTS]
    cast_out = outs[N_TOKEN_MIX_OUTPUTS:N_TOKEN_MIX_OUTPUTS + n_cast]
    wdn_sc, wlru_sc = outs[N_TOKEN_MIX_OUTPUTS + n_cast:]

    for src, dst in zip(cast_in, cast_out):
        dst[...] = src[...].astype(BF16)

    @pl.when(pl.program_id(1) == 0)
    def _():
        sout_ref[...] = s0_ref[0]
        hout_ref[...] = h0_ref[...]

    vals = {}
    stage1 = _inproj_stage(
        x_ref, mod_ref, hdn_ref, hlru_ref, w_ref, cwdn_ref, cwlru_ref, cblru_ref, alog_ref, dtb_ref,
        wr_ref, wi_ref, br_ref, bi_ref, lam_ref, tdn_ref, tlru_ref, wdn_sc, wlru_sc, vals, nb=nb, tt=tt)
    stage2 = _mixer_stage(vals, x_ref, mod_ref, wout_ref, nw_ref, lng_ref, lnb_ref,
                          y_ref, sout_ref, hout_ref, nb=nb, tt=tt, chunk=chunk, alpha=alpha)
    while "gb" not in vals:
        next(stage1)
    pending = [stage2, stage1]
    while pending:
        pending = [g for g in pending if next(g, StopIteration) is not StopIteration]


def _token_mix_call(x, mod, hist_dn, hist_lru, s_dn_all, layer, h0, lw_, cast=(), *, nb, tt, chunk, alpha):
    b, t, d = x.shape
    lw = hist_lru.shape[-1]
    grid = (b // nb, t // tt)
    steps = grid[0] * grid[1]
    slab = lambda a: pl.BlockSpec((a.shape[0] // steps, a.shape[1]), lambda i, j: (i * grid[1] + j, 0))
    seq_blk = lambda c: pl.BlockSpec((nb, tt, c), lambda i, j: (i, j, 0))
    per_seq = lambda r, c: pl.BlockSpec((nb, r, c), lambda i, j: (i, 0, 0))
    whole = lambda a: pl.BlockSpec(a.shape, lambda i, j: (0,) * a.ndim, pipeline_mode=pl.Buffered(1))
    state_in = pl.BlockSpec((1, nb, DN_HEADS, DN_DK, DN_DV), lambda i, j: (layer, i, 0, 0, 0))
    state_out = pl.BlockSpec((nb, DN_HEADS, DN_DK, DN_DV), lambda i, j: (i, 0, 0, 0))
    weights = [lw_[n] for n in ("w_cat", "dn_conv_w", "lru_conv_w", "lru_conv_b", "a_log", "dt_bias",
                                "w_r", "w_i", "b_r", "b_i", "lam", "w_out", "dn_norm_w", "ln1_g", "ln1_b")]
    return pl.pallas_call(
        functools.partial(_token_mix_kernel, n_cast=len(cast), nb=nb, tt=tt, chunk=chunk, alpha=alpha),
        out_shape=(
            jax.ShapeDtypeStruct((b, t, d), F32),
            jax.ShapeDtypeStruct((b, SUBLANES, DN_QKV), F32),
            jax.ShapeDtypeStruct((b, SUBLANES, lw), F32),
            jax.ShapeDtypeStruct((b, DN_HEADS, DN_DK, DN_DV), F32),
            jax.ShapeDtypeStruct((b, 1, lw), F32),
        ) + tuple(jax.ShapeDtypeStruct(a.shape, BF16) for a in cast),
        grid=grid,
        in_specs=[seq_blk(d), per_seq(1, N_MOD * d), per_seq(SUBLANES, DN_QKV), per_seq(SUBLANES, lw),
                  state_in, per_seq(1, lw)] + [whole(a) for a in weights] + [slab(a) for a in cast],
        out_specs=(seq_blk(d), per_seq(SUBLANES, DN_QKV), per_seq(SUBLANES, lw), state_out, per_seq(1, lw))
        + tuple(slab(a) for a in cast),
        scratch_shapes=[pltpu.VMEM((nb, SUBLANES + tt, DN_QKV), F32), pltpu.VMEM((nb, SUBLANES + tt, lw), F32)],
        compiler_params=_params(("arbitrary", "arbitrary")),
        name="token_mixer",
    )(x, mod, hist_dn, hist_lru, s_dn_all, h0, *weights, *cast)


def _ffn_kernel(x_ref, mod_ref, wg_ref, wu_ref, wd_ref, lng_ref, lnb_ref, *refs, alpha):
    n_cast = (len(refs) - 1) // 2
    y_ref = refs[n_cast]
    for src, dst in zip(refs[:n_cast], refs[n_cast + 1:]):
        dst[...] = src[...].astype(BF16)
    nb, tt, d = x_ref.shape
    m = mod_ref[...]
    h = (x_ref[...] * (1.0 + m[:, :, 4 * d:5 * d]) + m[:, :, 3 * d:4 * d]).reshape(nb * tt, d).astype(BF16)
    act = _silu(jnp.dot(h, wg_ref[...], preferred_element_type=F32)) * jnp.dot(
        h, wu_ref[...], preferred_element_type=F32)
    ff = _mm(act, wd_ref[...]).reshape(nb, tt, d)
    y_ref[...] = _layer_norm(alpha * x_ref[...] + (1.0 + m[:, :, 5 * d:6 * d]) * ff, lng_ref[...], lnb_ref[...])


def _ff_tile(ff):
    for n in (2, 4, 7, 8, 11, 14, 16, 22, 28):
        if ff % n == 0 and (ff // n) % LANES == 0 and ff // n <= 2048:
            return ff // n
    return ff


def _dense_ffn(x, mod, lw_, cast=(), *, nb, tt, alpha):
    b, t, d = x.shape
    nt = t // tt
    steps = (b // nb) * nt
    slab = lambda a: pl.BlockSpec((a.shape[0] // steps, a.shape[1]), lambda i: (i, 0))
    seq_blk = pl.BlockSpec((nb, tt, d), lambda i: (i // nt, i % nt, 0))
    whole = lambda a: pl.BlockSpec(a.shape, lambda i: (0,) * a.ndim)
    resident = lambda a: pl.BlockSpec(a.shape, lambda i: (0,) * a.ndim, pipeline_mode=pl.Buffered(1))
    return pl.pallas_call(
        functools.partial(_ffn_kernel, alpha=alpha),
        out_shape=(jax.ShapeDtypeStruct((b, t, d), F32),) + tuple(jax.ShapeDtypeStruct(a.shape, BF16) for a in cast),
        grid=(steps,),
        in_specs=[seq_blk, pl.BlockSpec((nb, 1, N_MOD * d), lambda i: (i // nt, 0, 0)),
                  resident(lw_["w_gate"]), resident(lw_["w_up"]), resident(lw_["w_down"]),
                  whole(lw_["ln2_g"]), whole(lw_["ln2_b"])] + [slab(a) for a in cast],
        out_specs=(seq_blk,) + tuple(slab(a) for a in cast),
        compiler_params=_params(("arbitrary",)),
        name="dense_ffn",
    )(x, mod, lw_["w_gate"], lw_["w_up"], lw_["w_down"], lw_["ln2_g"], lw_["ln2_b"], *cast)


MOE_CHUNK = 512
MOE_SLOT_TILE = 512
MOE_SUB = 128
MOE_COMBINE_BLOCK = 256
MOE_COMBINE_FANIN = 4
ROUTE_I1, ROUTE_I2, ROUTE_R1, ROUTE_R2, ROUTE_W1, ROUTE_W2 = range(6)


def _router_kernel(x_ref, mod_ref, wr_ref, base_ref, h_ref, meta_ref, meta_t_ref, blkbase_ref, cnt_ref, run_sc,
                   *, n_experts):
    nb, tt, d = x_ref.shape
    rows = nb * tt

    @pl.when(pl.program_id(0) == 0)
    def _():
        run_sc[...] = base_ref[...]

    m = mod_ref[...]
    h = (x_ref[...] * (1.0 + m[:, :, 4 * d:5 * d]) + m[:, :, 3 * d:4 * d]).reshape(rows, d)
    h_hi = h.astype(BF16)
    h_ref[...] = h_hi
    w = wr_ref[...]
    w_hi = w.astype(BF16)
    h_lo = (h - h_hi.astype(F32)).astype(BF16)
    w_lo = (w - w_hi.astype(F32)).astype(BF16)
    logits = (jnp.dot(h_hi, w_hi, preferred_element_type=F32) + jnp.dot(h_lo, w_hi, preferred_element_type=F32)
              + jnp.dot(h_hi, w_lo, preferred_element_type=F32))
    lane = lax.broadcasted_iota(jnp.int32, logits.shape, 1)
    neg = jnp.float32(-jnp.inf)
    lg = jnp.where(lane < n_experts, logits, neg)
    m1 = jnp.max(lg, axis=-1, keepdims=True)
    i1 = jnp.min(jnp.where(lg == m1, lane, LANES), axis=-1, keepdims=True)
    lg2 = jnp.where(lane == i1, neg, lg)
    m2 = jnp.max(lg2, axis=-1, keepdims=True)
    i2 = jnp.min(jnp.where(lg2 == m2, lane, LANES), axis=-1, keepdims=True)
    e2 = jnp.exp(m2 - m1)
    w1 = 1.0 / (1.0 + e2)
    w2 = e2 / (1.0 + e2)
    sel = jnp.where(lane == i1, 1.0, jnp.where(lane == i2, 1.0, 0.0))
    r = lax.broadcasted_iota(jnp.int32, (rows, rows), 0)
    c = lax.broadcasted_iota(jnp.int32, (rows, rows), 1)
    rank = _mm(jnp.where(c < r, 1.0, 0.0), sel) + run_sc[0:1, :]
    r1 = jnp.sum(jnp.where(lane == i1, rank, 0.0), axis=-1, keepdims=True)
    r2 = jnp.sum(jnp.where(lane == i2, rank, 0.0), axis=-1, keepdims=True)
    fields = (i1.astype(F32), i2.astype(F32), r1, r2, w1, w2)
    meta = jnp.zeros_like(logits)
    for k, v in enumerate(fields):
        meta = jnp.where(lane == k, v, meta)
    meta_ref[...] = meta
    meta_t_ref[...] = meta.T[0:SUBLANES, :]
    blkbase_ref[0] = run_sc[...]
    run_sc[...] = run_sc[...] + jnp.sum(sel, axis=0, keepdims=True)
    cnt_ref[...] = run_sc[...]


def _router(x, mod, w_router, base, *, nb, tt, n_experts):
    b, t, d = x.shape
    rows = nb * tt
    nt = t // tt
    nblk = (b // nb) * nt
    whole = lambda a: pl.BlockSpec(a.shape, lambda i: (0,) * a.ndim)
    return pl.pallas_call(
        functools.partial(_router_kernel, n_experts=n_experts),
        out_shape=(jax.ShapeDtypeStruct((nblk * rows, d), BF16),
                   jax.ShapeDtypeStruct((nblk * rows, LANES), F32),
                   jax.ShapeDtypeStruct((SUBLANES, nblk * rows), F32),
                   jax.ShapeDtypeStruct((nblk, SUBLANES, LANES), F32),
                   jax.ShapeDtypeStruct((SUBLANES, LANES), F32)),
        grid=(nblk,),
        in_specs=[pl.BlockSpec((nb, tt, d), lambda i: (i // nt, i % nt, 0)),
                  pl.BlockSpec((nb, 1, N_MOD * d), lambda i: (i // nt, 0, 0)),
                  whole(w_router), whole(base)],
        out_specs=(pl.BlockSpec((rows, d), lambda i: (i, 0)),
                   pl.BlockSpec((rows, LANES), lambda i: (i, 0)),
                   pl.BlockSpec((SUBLANES, rows), lambda i: (0, i)),
                   pl.BlockSpec((1, SUBLANES, LANES), lambda i: (i, 0, 0)),
                   pl.BlockSpec((SUBLANES, LANES), lambda i: (0, 0))),
        scratch_shapes=[pltpu.VMEM((SUBLANES, LANES), F32)],
        compiler_params=_params(("arbitrary",)),
        name="moe_router",
    )(x, mod, w_router, base)


def _slot_of(expert, rank, start_ref, n_experts):
    start = jnp.zeros_like(rank)
    for e in range(n_experts):
        start = jnp.where(expert == e, start_ref[e].astype(F32), start)
    return (start + rank).astype(jnp.int32)


def _gather_kernel(clo_ref, chi_ref, start_ref, *refs, group_chunks, n_experts):
    n_groups = len(group_chunks)
    m_refs = refs[:n_groups]
    h_refs = refs[n_groups:2 * n_groups]
    xs_ref, ws_ref, acc_sc, wacc_sc = refs[2 * n_groups:]
    g = pl.program_id(0)
    n_sub = MOE_SLOT_TILE // MOE_SUB
    for j in range(n_sub):
        q = g * n_sub + j
        ids = g * MOE_SLOT_TILE + j * MOE_SUB + lax.broadcasted_iota(jnp.int32, (MOE_SUB, MOE_CHUNK), 0)
        acc_sc[...] = jnp.zeros_like(acc_sc)
        wacc_sc[...] = jnp.zeros_like(wacc_sc)
        first = 0
        for m_ref, h_ref, n_chunks in zip(m_refs, h_refs, group_chunks):
            def body(c, carry, m_ref=m_ref, h_ref=h_ref):
                off = pl.multiple_of(c * MOE_CHUNK, MOE_CHUNK)
                rec = m_ref[:, pl.ds(off, MOE_CHUNK)]
                row = lambda k: rec[k:k + 1, :]
                hit1 = _slot_of(row(ROUTE_I1), row(ROUTE_R1), start_ref, n_experts) == ids
                hit2 = _slot_of(row(ROUTE_I2), row(ROUTE_R2), start_ref, n_experts) == ids
                p = jnp.where(hit1, 1.0, jnp.where(hit2, 1.0, 0.0)).astype(BF16)
                acc_sc[...] += jnp.dot(p, h_ref[pl.ds(off, MOE_CHUNK), :], preferred_element_type=F32)
                w = jnp.where(hit1, row(ROUTE_W1), 0.0) + jnp.where(hit2, row(ROUTE_W2), 0.0)
                wacc_sc[...] += jnp.broadcast_to(jnp.sum(w, axis=-1, keepdims=True), wacc_sc.shape)
                return carry

            lo = jnp.clip(clo_ref[q] - first, 0, n_chunks)
            hi = jnp.clip(chi_ref[q] - first, 0, n_chunks)
            lax.fori_loop(lo, hi, body, 0)
            first += n_chunks
        xs_ref[j * MOE_SUB:(j + 1) * MOE_SUB, :] = acc_sc[...].astype(BF16)
        ws_ref[j * MOE_SUB:(j + 1) * MOE_SUB, :] = wacc_sc[...]


def _gather_slots(c_lo, c_hi, run_start, metas_t, hs, n_tiles, n_experts):
    d = hs[0].shape[-1]
    vmem = pl.BlockSpec(memory_space=pltpu.VMEM)
    return pl.pallas_call(
        functools.partial(_gather_kernel, group_chunks=tuple(h.shape[0] // MOE_CHUNK for h in hs),
                          n_experts=n_experts),
        out_shape=(jax.ShapeDtypeStruct((n_tiles * MOE_SLOT_TILE, d), BF16),
                   jax.ShapeDtypeStruct((n_tiles * MOE_SLOT_TILE, LANES), F32)),
        grid_spec=pltpu.PrefetchScalarGridSpec(
            num_scalar_prefetch=3, grid=(n_tiles,),
            in_specs=[vmem] * (2 * len(hs)),
            out_specs=(pl.BlockSpec((MOE_SLOT_TILE, d), lambda g, lo, hi, st: (g, 0)),
                       pl.BlockSpec((MOE_SLOT_TILE, LANES), lambda g, lo, hi, st: (g, 0))),
            scratch_shapes=[pltpu.VMEM((MOE_SUB, d), F32), pltpu.VMEM((MOE_SUB, LANES), F32)]),
        compiler_params=_params(("arbitrary",)),
        name="moe_gather",
    )(c_lo, c_hi, run_start, *metas_t, *hs)


def _expert_kernel(te_ref, tv_ref, xs_ref, ws_ref, wg_ref, wu_ref, wd_ref, o_ref, *, ff_chunk):
    g = pl.program_id(0)

    @pl.when(tv_ref[g] != 0)
    def _():
        x = xs_ref[...]
        ff = wg_ref.shape[-1]
        acc = jnp.zeros(o_ref.shape, F32)
        for f0 in range(0, ff, ff_chunk):
            gate = jnp.dot(x, wg_ref[0, :, f0:f0 + ff_chunk], preferred_element_type=F32)
            up = jnp.dot(x, wu_ref[0, :, f0:f0 + ff_chunk], preferred_element_type=F32)
            acc = acc + _mm(_silu(gate) * up, wd_ref[0, f0:f0 + ff_chunk, :])
        o_ref[...] = (ws_ref[:, 0:1] * acc).astype(BF16)

    @pl.when(tv_ref[g] == 0)
    def _():
        o_ref[...] = jnp.zeros_like(o_ref)


def _expert_ffn(tile_expert, tile_valid, xs, ws, lw_):
    s_total, d = xs.shape
    _, _, ff = lw_["w_gate"].shape
    n_tiles = s_total // MOE_SLOT_TILE
    expert_blk = lambda shape: pl.BlockSpec((1,) + shape, lambda g, te, tv: (te[g], 0, 0))
    return pl.pallas_call(
        functools.partial(_expert_kernel, ff_chunk=_ff_tile(ff)),
        out_shape=jax.ShapeDtypeStruct((s_total, d), BF16),
        grid_spec=pltpu.PrefetchScalarGridSpec(
            num_scalar_prefetch=2, grid=(n_tiles,),
            in_specs=[pl.BlockSpec((MOE_SLOT_TILE, d), lambda g, te, tv: (g, 0)),
                      pl.BlockSpec((MOE_SLOT_TILE, LANES), lambda g, te, tv: (g, 0)),
                      expert_blk((d, ff)), expert_blk((d, ff)), expert_blk((ff, d))],
            out_specs=pl.BlockSpec((MOE_SLOT_TILE, d), lambda g, te, tv: (g, 0))),
        compiler_params=_params(("arbitrary",)),
        name="expert_ffn",
    )(tile_expert, tile_valid, xs, ws, lw_["w_gate"], lw_["w_up"], lw_["w_down"])


def _combine_kernel(ic_ref, ik_ref, ie_ref, if_ref, bf_ref, start_ref, x_ref, mod_ref, meta_ref, *refs, alpha,
                    n_experts):
    os_refs = refs[:MOE_COMBINE_FANIN]
    lng_ref, lnb_ref, y_ref, acc_sc, slot_sc = refs[MOE_COMBINE_FANIN:]
    nb, tt, d = x_ref.shape
    rows = nb * tt
    w = pl.program_id(0)
    flags = if_ref[w]

    @pl.when((flags & 1) != 0)
    def _():
        rec = meta_ref[...]
        col = lambda k: rec[:, k:k + 1]
        slot_sc[:, 0:1] = _slot_of(col(ROUTE_I1), col(ROUTE_R1), start_ref, n_experts)
        slot_sc[:, 1:2] = _slot_of(col(ROUTE_I2), col(ROUTE_R2), start_ref, n_experts)

    def window_sum():
        lane = lax.broadcasted_iota(jnp.int32, (rows, MOE_COMBINE_BLOCK), 1)
        s1 = slot_sc[:, 0:1]
        s2 = slot_sc[:, 1:2]
        total = None
        for j, os_ref in enumerate(os_refs):
            ids = ik_ref[w * MOE_COMBINE_FANIN + j] + lane
            ids = jnp.where(ids < ie_ref[w * MOE_COMBINE_FANIN + j], ids, -1)
            q = jnp.where(s1 == ids, 1.0, jnp.where(s2 == ids, 1.0, 0.0)).astype(BF16)
            part = jnp.dot(q, os_ref[...], preferred_element_type=F32)
            total = part if total is None else total + part
        return total

    @pl.when((flags & 1) != 0)
    def _():
        acc_sc[...] = window_sum()

    @pl.when((flags & 5) == 4)
    def _():
        acc_sc[...] += window_sum()

    @pl.when((flags & 2) != 0)
    def _():
        gate = mod_ref[...][:, :, 5 * d:6 * d]
        ff = acc_sc[...].reshape(nb, tt, d)
        y_ref[...] = _layer_norm(alpha * x_ref[...] + (1.0 + gate) * ff, lng_ref[...], lnb_ref[...])


def _combine(items, run_start, x, mod, meta, out_sorted, lw_, *, nb, tt, alpha, n_experts):
    step_chunk, win_start, win_end, step_flags, win_fetch = items
    b, t, d = x.shape
    rows = nb * tt
    nt = t // tt
    whole = lambda a: pl.BlockSpec(a.shape, lambda w, ic, ik, ie, fl, bf, st: (0,) * a.ndim)
    seq_blk = pl.BlockSpec((nb, tt, d), lambda w, ic, ik, ie, fl, bf, st: (ic[w] // nt, ic[w] % nt, 0))
    slot_blk = lambda j: pl.BlockSpec(
        (pl.Element(MOE_COMBINE_BLOCK), pl.Element(d)),
        lambda w, ic, ik, ie, fl, bf, st: (pl.multiple_of(bf[w * MOE_COMBINE_FANIN + j], 2 * SUBLANES), 0))
    return pl.pallas_call(
        functools.partial(_combine_kernel, alpha=alpha, n_experts=n_experts),
        out_shape=jax.ShapeDtypeStruct((b, t, d), F32),
        grid_spec=pltpu.PrefetchScalarGridSpec(
            num_scalar_prefetch=6, grid=(step_chunk.shape[0],),
            in_specs=[seq_blk,
                      pl.BlockSpec((nb, 1, N_MOD * d), lambda w, ic, ik, ie, fl, bf, st: (ic[w] // nt, 0, 0)),
                      pl.BlockSpec((rows, LANES), lambda w, ic, ik, ie, fl, bf, st: (ic[w], 0))]
            + [slot_blk(j) for j in range(MOE_COMBINE_FANIN)]
            + [whole(lw_["ln2_g"]), whole(lw_["ln2_b"])],
            out_specs=seq_blk,
            scratch_shapes=[pltpu.VMEM((rows, d), F32), pltpu.VMEM((rows, LANES), jnp.int32)]),
        compiler_params=_params(("arbitrary",)),
        name="moe_combine",
    )(step_chunk, win_start, win_end, step_flags, win_fetch, run_start, x, mod, meta,
      *([out_sorted] * MOE_COMBINE_FANIN), lw_["ln2_g"], lw_["ln2_b"])


def _count_le(sorted_vals, queries):
    return jnp.sum(sorted_vals[None, :] <= queries[:, None], axis=1).astype(jnp.int32)


def _combine_steps(lo, hi, n_steps):
    n_chunks, n_experts = lo.shape
    fan = MOE_COMBINE_FANIN
    align = 2 * SUBLANES
    lo_f, hi_f = lo.reshape(-1), hi.reshape(-1)
    first = (lo_f // align) * align
    count = jnp.where(hi_f > lo_f, (hi_f - first + MOE_COMBINE_BLOCK - 1) // MOE_COMBINE_BLOCK, 0)
    pair_end = jnp.cumsum(count)
    chunk_items = jnp.sum(count.reshape(n_chunks, n_experts), axis=1)
    chunk_item0 = jnp.cumsum(chunk_items) - chunk_items
    chunk_steps = (chunk_items + fan - 1) // fan
    step_end = jnp.cumsum(chunk_steps)
    total_steps = step_end[-1]
    s = jnp.arange(n_steps, dtype=jnp.int32)
    live = s < total_steps
    chunk = jnp.minimum(_count_le(step_end, s), n_chunks - 1)
    chunk = jnp.where(live, chunk, chunk[jnp.maximum(total_steps - 1, 0)])
    q = s - (step_end[chunk] - chunk_steps[chunk])
    j = q[:, None] * fan + jnp.arange(fan, dtype=jnp.int32)[None, :]
    used = live[:, None] & (j < chunk_items[chunk][:, None])
    item = jnp.where(used, chunk_item0[chunk][:, None] + j, 0).reshape(-1)
    pair = jnp.minimum(_count_le(pair_end, item), n_chunks * n_experts - 1)
    start = first[pair] + (item - (pair_end[pair] - count[pair])) * MOE_COMBINE_BLOCK
    fetch = jnp.where(used.reshape(-1), start, jnp.repeat(start.reshape(-1, fan)[:, 0], fan))
    end = jnp.where(used.reshape(-1), hi_f[pair], 0)
    flags = (jnp.where(live & (q == 0), 1, 0) + jnp.where(live & (q == chunk_steps[chunk] - 1), 2, 0)
             + jnp.where(live, 4, 0))
    as_i32 = lambda a: a.astype(jnp.int32)
    return as_i32(chunk), as_i32(start), as_i32(end), as_i32(flags), as_i32(fetch)


def _moe_layer(xs_in, mods, lw_, tilings, alpha):
    n_experts = lw_["w_gate"].shape[0]
    base = jnp.zeros((SUBLANES, LANES), F32)
    hs, metas, metas_t, bases = [], [], [], []
    for x, mod, (nb, tt) in zip(xs_in, mods, tilings):
        assert nb * tt == MOE_CHUNK and (x.shape[0] * x.shape[1]) % MOE_CHUNK == 0
        h, meta, meta_t, blkbase, base = _router(x, mod, lw_["w_router"], base, nb=nb, tt=tt,
                                                 n_experts=n_experts)
        hs.append(h)
        metas.append(meta)
        metas_t.append(meta_t)
        bases.append(blkbase[:, 0, :n_experts])
    n = sum(m.shape[0] for m in metas)
    cum = jnp.concatenate(bases + [base[0:1, :n_experts]], axis=0).astype(jnp.int32)
    counts = cum[-1]
    sizes = ((counts + MOE_SLOT_TILE - 1) // MOE_SLOT_TILE) * MOE_SLOT_TILE
    run_end = jnp.cumsum(sizes)
    run_start = (run_end - sizes).astype(jnp.int32)

    n_tiles = (TOP_K * n + MOE_SLOT_TILE - 1) // MOE_SLOT_TILE + n_experts + 1
    tile_start = jnp.arange(n_tiles, dtype=jnp.int32) * MOE_SLOT_TILE
    tile_expert = jnp.minimum(_count_le(run_end, tile_start), n_experts - 1)
    tile_valid = (tile_start < run_end[-1]).astype(jnp.int32)
    n_sub = MOE_SLOT_TILE // MOE_SUB
    sub_expert = jnp.repeat(tile_expert, n_sub)
    sub_rank0 = jnp.arange(n_tiles * n_sub, dtype=jnp.int32) * MOE_SUB - run_start[sub_expert]
    cum_sub = cum[:, sub_expert]
    sub_valid = jnp.repeat(tile_valid, n_sub)
    c_lo = jnp.sum(cum_sub[1:] <= sub_rank0[None, :], axis=0).astype(jnp.int32) * sub_valid
    c_hi = jnp.sum(cum_sub[:-1] < sub_rank0[None, :] + MOE_SUB, axis=0).astype(jnp.int32) * sub_valid

    x_sorted, w_sorted = _gather_slots(c_lo, c_hi, run_start, metas_t, hs, n_tiles, n_experts)
    out_sorted = _expert_ffn(tile_expert, tile_valid, x_sorted, w_sorted, lw_)

    outs = []
    chunk0 = 0
    for x, mod, meta, (nb, tt) in zip(xs_in, mods, metas, tilings):
        nc = x.shape[0] * x.shape[1] // MOE_CHUNK
        lo = run_start[None, :] + cum[chunk0:chunk0 + nc]
        hi = run_start[None, :] + cum[chunk0 + 1:chunk0 + nc + 1]
        max_items = nc * n_experts + (TOP_K * nc * MOE_CHUNK) // MOE_COMBINE_BLOCK + 2 * n_experts
        items = _combine_steps(lo, hi, max_items // MOE_COMBINE_FANIN + nc)
        outs.append(_combine(items, run_start, x, mod, meta, out_sorted, lw_, nb=nb, tt=tt, alpha=alpha,
                             n_experts=n_experts))
        chunk0 += nc
    return outs


def _pad_lanes(v, width=LANES):
    return jnp.pad(v, ((0, 0), (0, width - v.shape[-1])))


def _block_diag_halves(w):
    nblk, c, _ = w.shape
    half = nblk // 2
    out = jnp.zeros((2, half * c, half * c), w.dtype)
    for i in range(nblk):
        j = i % half
        out = out.at[i // half, j * c:(j + 1) * c, j * c:(j + 1) * c].set(w[i])
    return out


def _layer_weights(l, p, d, lw):
    w_in = p["w_in"][l]
    a_off = DN_QKV
    z_off = a_off + 2 * DN_HEADS
    x_off = z_off + DN_WIDTH
    y_off = x_off + lw
    w_ab = _pad_lanes(w_in[:, a_off:z_off])
    w_cat = jnp.concatenate([w_in[:, :DN_QKV], w_in[:, z_off:y_off + lw], w_ab], axis=1).astype(BF16)
    out = {
        "w_cat": w_cat,
        "dn_conv_w": p["dn_conv_w"][l],
        "lru_conv_w": p["lru_conv_w"][l],
        "lru_conv_b": p["lru_conv_b"][l][None],
        "a_log": _pad_lanes(p["dn_a_log"][l][None]),
        "dt_bias": _pad_lanes(p["dn_dt_bias"][l][None]),
        "w_r": _block_diag_halves(p["lru_w_r"][l]).astype(BF16),
        "w_i": _block_diag_halves(p["lru_w_i"][l]).astype(BF16),
        "b_r": p["lru_b_r"][l][None],
        "b_i": p["lru_b_i"][l][None],
        "lam": p["lru_lambda"][l][None],
        "w_out": p["w_out"][l].astype(BF16),
        "dn_norm_w": p["dn_norm_w"][l][None],
        "ln1_g": p["ln1_g"][l][None],
        "ln1_b": p["ln1_b"][l][None],
        "ln2_g": p["ln2_g"][l][None],
        "ln2_b": p["ln2_b"][l][None],
    }
    j = l // 2
    if l % 2 == 0:
        out.update(w_gate=p["ffn_w_gate"][j], w_up=p["ffn_w_up"][j], w_down=p["ffn_w_down"][j])
    else:
        out.update(w_router=_pad_lanes(p["moe_w_router"][j]),
                   w_gate=p["moe_w_gate"][j], w_up=p["moe_w_up"][j], w_down=p["moe_w_down"][j])
    return out


CHANNEL_MIX_WEIGHTS = ("w_gate", "w_up", "w_down")


LARGE_SLAB_BYTES = 64 * 1024 * 1024


def _cast_plan(weights, host_steps):
    hosts = list(host_steps)
    plan = {h: [] for h in hosts}
    for k in reversed(range(len(weights))):
        dense = ("ffn", k) in host_steps
        last = hosts.index(("ffn", k)) if dense else hosts.index(("mixer", k)) + 1
        for n in CHANNEL_MIX_WEIGHTS:
            a = weights[k][n].reshape(-1, weights[k][n].shape[-1])
            large = a.size * 4 > LARGE_SLAB_BYTES
            for h in reversed(hosts[:last]):
                fits = a.shape[0] % (host_steps[h] * 2 * SUBLANES) == 0
                busy = large and any(s.size * 4 > LARGE_SLAB_BYTES for _, _, s in plan[h])
                if fits and not busy:
                    plan[h].append((k, n, a))
                    break
    return plan


def _tiling(b, t):
    if t >= MXU_DIM:
        tt = MXU_DIM
        return dict(mixer=(2 if b % 2 == 0 else 1, tt, min(PROMPT_CHUNK, tt)), ffn=(1, min(t, 2 * MXU_DIM)))
    assert t == SUBLANES, "short sequences must be exactly one sublane tile long"
    return dict(mixer=(min(b, 16), t, t), ffn=(min(b, 64), t))


def _token_mix(x, mod, conv_dn, s_dn_all, layer, conv_lru, s_lru, lw_, til, alpha, cast=()):
    pad_hist = lambda c: jnp.pad(c, ((0, 0), (SUBLANES - (CONV_W - 1), 0), (0, 0)))
    nb, tt, chunk = til["mixer"]
    x, tail_dn, tail_lru, s_new, h_new, *converted = _token_mix_call(
        x, mod, pad_hist(conv_dn), pad_hist(conv_lru), s_dn_all, layer, s_lru[:, None, :], lw_, cast,
        nb=nb, tt=tt, chunk=chunk, alpha=alpha)
    return (x, tail_dn[:, SUBLANES - (CONV_W - 1):, :], s_new, tail_lru[:, SUBLANES - (CONV_W - 1):, :],
            h_new[:, 0, :]), converted


def kernel(x_prompt, x_sample, cache_dn_conv, state_dn, cache_lru_conv, state_lru, c_prompt, c_sample,
           w_ada, b_ada, w_in, dn_conv_w, dn_a_log, dn_dt_bias, dn_norm_w,
           lru_conv_w, lru_conv_b, lru_w_r, lru_b_r, lru_w_i, lru_b_i, lru_lambda, w_out,
           ln1_g, ln1_b, ln2_g, ln2_b, ffn_w_gate, ffn_w_up, ffn_w_down,
           moe_w_router, moe_w_gate, moe_w_up, moe_w_down):
    p = dict(w_in=w_in, dn_conv_w=dn_conv_w, dn_a_log=dn_a_log, dn_dt_bias=dn_dt_bias, dn_norm_w=dn_norm_w,
             lru_conv_w=lru_conv_w, lru_conv_b=lru_conv_b, lru_w_r=lru_w_r, lru_b_r=lru_b_r,
             lru_w_i=lru_w_i, lru_b_i=lru_b_i, lru_lambda=lru_lambda, w_out=w_out,
             ln1_g=ln1_g, ln1_b=ln1_b, ln2_g=ln2_g, ln2_b=ln2_b,
             ffn_w_gate=ffn_w_gate, ffn_w_up=ffn_w_up, ffn_w_down=ffn_w_down,
             moe_w_router=moe_w_router, moe_w_gate=moe_w_gate, moe_w_up=moe_w_up, moe_w_down=moe_w_down)
    depth, d, _ = w_ada.shape
    bp = x_prompt.shape[0]
    bs = x_sample.shape[0]
    lw = cache_lru_conv.shape[-1]
    alpha = (2 * depth) ** 0.25
    weights = [_layer_weights(l, p, d, lw) for l in range(depth)]

    c_all = jnp.concatenate([c_prompt, c_sample], axis=0)
    mod_all = _modulation(c_all, w_ada, b_ada)
    groups = [
        dict(x=x_prompt, rows=slice(0, bp), conv_dn=jnp.zeros((depth, bp, CONV_W - 1, DN_QKV), F32),
             s_dn=jnp.zeros((depth, bp, DN_HEADS, DN_DK, DN_DV), F32),
             conv_lru=jnp.zeros((depth, bp, CONV_W - 1, lw), F32), s_lru=jnp.zeros((depth, bp, lw), F32)),
        dict(x=x_sample, rows=slice(bp, bp + bs), conv_dn=cache_dn_conv, s_dn=state_dn,
             conv_lru=cache_lru_conv, s_lru=state_lru),
    ]
    for g in groups:
        g["til"] = _tiling(g["x"].shape[0], g["x"].shape[1])
        g["new"] = [[], [], [], []]
    t_p = x_prompt.shape[1]
    nb_m, tt_m, _ = groups[0]["til"]["mixer"]
    nb_f, tt_f = groups[0]["til"]["ffn"]
    host_steps = {}
    for l in range(depth):
        host_steps[("mixer", l)] = (bp // nb_m) * (t_p // tt_m)
        if l % 2 == 0:
            host_steps[("ffn", l)] = (bp // nb_f) * (t_p // tt_f)
    plan = _cast_plan(weights, host_steps)
    planned = {(k, n) for slabs in plan.values() for k, n, _ in slabs}
    for k, w in enumerate(weights):
        for n in CHANNEL_MIX_WEIGHTS:
            if (k, n) not in planned:
                w[n] = w[n].astype(BF16)

    def adopt(slabs, converted):
        for (k, n, _), c in zip(slabs, converted):
            weights[k][n] = c.reshape(weights[k][n].shape)

    for l in range(depth):
        lw_ = weights[l]
        mods = [mod_all[l, g["rows"]][:, None, :] for g in groups]
        for gi, (g, mod) in enumerate(zip(groups, mods)):
            slabs = plan[("mixer", l)] if gi == 0 else []
            res, converted = _token_mix(g["x"], mod, g["conv_dn"][l], g["s_dn"], l, g["conv_lru"][l],
                                        g["s_lru"][l], lw_, g["til"], alpha,
                                        cast=tuple(a for _, _, a in slabs))
            adopt(slabs, converted)
            g["x"] = res[0]
            for acc, new in zip(g["new"], res[1:]):
                acc.append(new)
        if l % 2 == 0:
            for gi, (g, mod) in enumerate(zip(groups, mods)):
                nb, tt = g["til"]["ffn"]
                slabs = plan[("ffn", l)] if gi == 0 else []
                g["x"], *converted = _dense_ffn(g["x"], mod, lw_, tuple(a for _, _, a in slabs),
                                                nb=nb, tt=tt, alpha=alpha)
                adopt(slabs, converted)
        else:
            xs = _moe_layer([g["x"] for g in groups], mods, lw_, [g["til"]["ffn"] for g in groups], alpha)
            for g, x in zip(groups, xs):
                g["x"] = x
    states = [jnp.stack(acc) for g in groups for acc in g["new"]]
    return (groups[0]["x"], groups[1]["x"]) + tuple(states)
```

```python
import functools
import math

import jax
import jax.numpy as jnp
from jax import lax
from jax.experimental import pallas as pl
from jax.experimental.pallas import tpu as pltpu

F32 = jnp.float32
BF16 = jnp.bfloat16

DN_HEADS = 4
DN_DK = 128
DN_DV = 128
DN_WIDTH = DN_HEADS * DN_DV
DN_QKV = 3 * DN_WIDTH
LRU_BLOCKS = 8
LRU_C = 8.0
CONV_W = 4
N_MOD = 6
TOP_K = 2
LN_EPS = 1e-5
NORM_EPS = 1e-6

SUBLANES = 8
LANES = 128
MXU_DIM = 256
VMEM_LIMIT_BYTES = 56 * 1024 * 1024

PROMPT_CHUNK = 128
INV_BASE_BLOCK = 16


def _sigmoid(x):
    return 0.5 + 0.5 * jnp.tanh(0.5 * x)


def _silu(x):
    half = 0.5 * x
    return half + half * jnp.tanh(half)


def _softplus(x):
    return jnp.maximum(x, 0.0) + jnp.log1p(jnp.exp(-jnp.abs(x)))


def _gelu_tanh(x):
    return 0.5 * x * (1.0 + jnp.tanh(math.sqrt(2.0 / math.pi) * (x + 0.044715 * (x * x * x))))


def _mm(a, b):
    return jnp.dot(a.astype(BF16), b.astype(BF16), preferred_element_type=F32)


def _layer_norm(x, g, b):
    mu = jnp.mean(x, axis=-1, keepdims=True)
    xc = x - mu
    var = jnp.mean(xc * xc, axis=-1, keepdims=True)
    return xc * lax.rsqrt(var + LN_EPS) * g + b


def _params(sem):
    return pltpu.CompilerParams(dimension_semantics=sem, vmem_limit_bytes=VMEM_LIMIT_BYTES)


def _mod_kernel(c_ref, w_ref, b_ref, o_ref):
    sc = _silu(c_ref[...])
    o_ref[0] = _mm(sc, w_ref[0]) + b_ref[0]


def _modulation(c_all, w_ada, b_ada):
    depth, d, n = w_ada.shape
    rows = c_all.shape[0]
    tn = 1536 if n % 1536 == 0 else n
    return pl.pallas_call(
        _mod_kernel,
        out_shape=jax.ShapeDtypeStruct((depth, rows, n), F32),
        grid=(depth, n // tn),
        in_specs=[
            pl.BlockSpec((rows, d), lambda l, j: (0, 0)),
            pl.BlockSpec((1, d, tn), lambda l, j: (l, 0, j)),
            pl.BlockSpec((1, 1, tn), lambda l, j: (l, 0, j)),
        ],
        out_specs=pl.BlockSpec((1, rows, tn), lambda l, j: (l, 0, j)),
        compiler_params=_params(("arbitrary", "arbitrary")),
        name="adaln_modulation",
    )(c_all, w_ada, b_ada.reshape(depth, 1, n))


def _causal_conv(u, win_ref, w, nb, tt, c0):
    c = u.shape[-1]
    cols = slice(c0, c0 + c)
    win_ref[:, SUBLANES:, cols] = u.reshape(nb, tt, c)
    out = u * w[CONV_W - 1:CONV_W, cols]
    for j in range(1, CONV_W):
        prev = win_ref[:, SUBLANES - j:SUBLANES - j + tt, cols].reshape(nb * tt, c)
        out = out + prev * w[CONV_W - 1 - j:CONV_W - j, cols]
    tail = win_ref[:, tt:tt + SUBLANES, cols]
    win_ref[:, 0:SUBLANES, cols] = tail
    return out, tail


def _l2norm_heads(x, scale):
    outs = []
    for h in range(DN_HEADS):
        xh = x[:, h * DN_DK:(h + 1) * DN_DK]
        ss = jnp.sum(xh * xh, axis=-1, keepdims=True)
        outs.append(xh * (lax.rsqrt(ss + NORM_EPS) * scale))
    return jnp.concatenate(outs, axis=-1)


def _inproj_stage(x_ref, mod_ref, hdn_ref, hlru_ref, w_ref, cwdn_ref, cwlru_ref, cblru_ref,
                  alog_ref, dtb_ref, wr_ref, wi_ref, br_ref, bi_ref, lam_ref,
                  tdn_ref, tlru_ref, wdn_sc, wlru_sc, out, *, nb, tt):
    d = x_ref.shape[-1]
    lw = hlru_ref.shape[-1]
    rows = nb * tt

    @pl.when(pl.program_id(1) == 0)
    def _():
        wdn_sc[:, 0:SUBLANES, :] = hdn_ref[...]
        wlru_sc[:, 0:SUBLANES, :] = hlru_ref[...]

    m = mod_ref[...]
    shift = m[:, :, 0:d]
    scale = m[:, :, d:2 * d]
    h = (x_ref[...] * (1.0 + scale) + shift).reshape(rows, d)
    proj = _mm(h, w_ref[...])
    z = proj[:, DN_QKV:DN_QKV + DN_WIDTH]
    u_lru = proj[:, DN_QKV + DN_WIDTH:DN_QKV + DN_WIDTH + lw]
    y = proj[:, DN_QKV + DN_WIDTH + lw:DN_QKV + DN_WIDTH + 2 * lw]
    ab = proj[:, DN_QKV + DN_WIDTH + 2 * lw:]

    cw_dn = cwdn_ref[...]

    def dn_part(part, norm_scale):
        c0 = part * DN_WIDTH
        conv, tail = _causal_conv(proj[:, c0:c0 + DN_WIDTH], wdn_sc, cw_dn, nb, tt, c0)
        tdn_ref[:, :, c0:c0 + DN_WIDTH] = tail
        act = _silu(conv)
        return act if norm_scale is None else _l2norm_heads(act, norm_scale)

    out["q"] = dn_part(0, DN_DK ** -0.5)
    yield
    out["k"] = dn_part(1, 1.0)
    lane = lax.broadcasted_iota(jnp.int32, ab.shape, 1)
    g_full = -jnp.exp(alog_ref[...]) * _softplus(ab + dtb_ref[...])
    out["gb"] = jnp.where(lane < DN_HEADS, g_full, _sigmoid(ab))
    yield
    out["v"] = dn_part(2, None)
    yield
    out["gz"] = _silu(z)
    out["gy"] = _gelu_tanh(y)
    yield

    conv_lru, tail_lru = _causal_conv(u_lru, wlru_sc, cwlru_ref[...], nb, tt, 0)
    tlru_ref[...] = tail_lru
    xc = conv_lru + cblru_ref[...]
    half = lw // 2
    r_pre = jnp.concatenate([_mm(xc[:, :half], wr_ref[0]), _mm(xc[:, half:], wr_ref[1])], axis=-1)
    i_pre = jnp.concatenate([_mm(xc[:, :half], wi_ref[0]), _mm(xc[:, half:], wi_ref[1])], axis=-1)
    r = _sigmoid(r_pre + br_ref[...])
    i = _sigmoid(i_pre + bi_ref[...])
    log_a = -LRU_C * r * _softplus(-lam_ref[...])
    out["a"] = jnp.exp(log_a)
    th = jnp.tanh(log_a)
    out["inp"] = jnp.sqrt(-2.0 * th / (1.0 - th)) * (i * xc)
    yield


def _unit_lower_inverses(ls, row, col, chunk, base):
    def same_block(s):
        k = s.bit_length() - 1
        return (row >> k) == (col >> k)

    eye = jnp.where(row == col, 1.0, 0.0)
    base_mask = same_block(base)
    powers = [jnp.where(base_mask, l, 0.0) for l in ls]
    invs = [eye - d for d in powers]
    p = 2
    while p < base:
        powers = [_mm(d, d) for d in powers]
        invs = [t + _mm(t, d) for t, d in zip(invs, powers)]
        p *= 2
        yield
    s = base
    while s < chunk:
        off_mask = same_block(2 * s) & jnp.logical_not(same_block(s))
        tmp = [_mm(jnp.where(off_mask, l, 0.0), t) for l, t in zip(ls, invs)]
        invs = [t - _mm(t, x) for t, x in zip(invs, tmp)]
        s *= 2
        yield
    return invs


def _mixer_stage(vals, x_ref, mod_ref, wout_ref, nw_ref, lng_ref, lnb_ref,
                 y_ref, sout_ref, hout_ref, *, nb, tt, chunk, alpha):
    d = x_ref.shape[-1]
    lw = hout_ref.shape[-1]
    rows = nb * tt
    grows = min(rows, MXU_DIM)
    n_groups = rows // grows
    seqs_per_group = max(grows // tt, 1)
    n_chunks = grows // chunk
    chunks_per_seq = tt // chunk if tt >= chunk else 1
    log_chunk = chunk.bit_length() - 1

    row = lax.broadcasted_iota(jnp.int32, (grows, grows), 0)
    col = lax.broadcasted_iota(jnp.int32, (grows, grows), 1)
    incl = ((row >> log_chunk) == (col >> log_chunk)) & (col <= row)
    strict = incl & (col < row)
    gb = vals["gb"]
    pieces = []
    rest = gb
    for _ in range(3):
        piece = rest.astype(BF16)
        pieces.append(piece)
        rest = rest - piece.astype(F32)
    split = jnp.concatenate(pieces, axis=-1)
    incl_b = jnp.where(incl, 1.0, 0.0).astype(BF16)
    chains = [(g, h) for g in range(n_groups) for h in range(DN_HEADS)]
    gcs, gc_ts = [], []
    for g in range(n_groups):
        sums = jnp.dot(incl_b, split[g * grows:(g + 1) * grows], preferred_element_type=F32)
        gc = sums[:, 0:LANES] + sums[:, LANES:2 * LANES] + sums[:, 2 * LANES:3 * LANES]
        gcs.append(gc)
        gc_ts.append(gc.T)

    rsl = lambda g: slice(g * grows, (g + 1) * grows)
    qs = [vals["q"][rsl(g), h * DN_DK:(h + 1) * DN_DK] for g, h in chains]
    ks = [vals["k"][rsl(g), h * DN_DK:(h + 1) * DN_DK] for g, h in chains]
    gcols = [gcs[g][:, h:h + 1] for g, h in chains]
    betas = [gb[rsl(g), DN_HEADS + h:DN_HEADS + h + 1] for g, h in chains]
    decays = [jnp.where(incl, jnp.exp(jnp.where(incl, gcols[i] - gc_ts[g][h:h + 1, :], 0.0)), 0.0)
              for i, (g, h) in enumerate(chains)]
    idx = range(len(chains))
    kbs = [ks[i] * betas[i] for i in idx]
    qk_kks = [lax.dot_general(jnp.concatenate([qs[i], kbs[i]], axis=0).astype(BF16), ks[i].astype(BF16),
                              (((1,), (1,)), ((), ())), preferred_element_type=F32) for i in idx]
    qks = [qk_kks[i][:grows] * decays[i] for i in idx]
    lmats = [jnp.where(strict, qk_kks[i][grows:] * decays[i], 0.0) for i in idx]
    yield
    tmats = yield from _unit_lower_inverses(lmats, row, col, chunk, min(INV_BASE_BLOCK, chunk))
    egcs = [jnp.exp(gcol) for gcol in gcols]
    vs = [vals["v"][rsl(g), h * DN_DV:(h + 1) * DN_DV] for g, h in chains]
    uws = [_mm(tmats[i], jnp.concatenate([vs[i] * betas[i], kbs[i] * egcs[i]], axis=-1)) for i in idx]
    us = [uw[:, :DN_DV] for uw in uws]
    ws = [uw[:, DN_DV:] for uw in uws]
    qes = [qs[i] * egcs[i] for i in idx]
    yield

    a = vals["a"]
    bacc = vals["inp"]
    gy = vals["gy"]
    t = lax.broadcasted_iota(jnp.int32, (rows, lw), 0) & (SUBLANES - 1)
    s = 1
    while s < SUBLANES:
        keep = t >= s
        a_prev = jnp.where(keep, pltpu.roll(a, s, 0), 1.0)
        b_prev = jnp.where(keep, pltpu.roll(bacc, s, 0), 0.0)
        bacc = a * b_prev + bacc
        a = a * a_prev
        s *= 2
    tiles = tt // SUBLANES
    a4 = a.reshape(nb, tiles, SUBLANES, lw)
    b4 = bacc.reshape(nb, tiles, SUBLANES, lw)
    carry = hout_ref[...]
    h_tiles = []
    for k in range(tiles):
        hk = a4[:, k] * carry + b4[:, k]
        carry = hk[:, SUBLANES - 1:SUBLANES, :]
        h_tiles.append(hk)
    hout_ref[...] = carry
    hs = h_tiles[0] if tiles == 1 else jnp.stack(h_tiles, axis=1)
    o_b = hs.reshape(rows, lw) * gy
    yield

    v_new_parts = [[] for _ in idx]
    o_inter_parts = [[] for _ in idx]
    states = [None for _ in idx]
    for c in range(n_chunks):
        lo, hi = c * chunk, (c + 1) * chunk
        seq_of = lambda g: g * seqs_per_group + c // chunks_per_seq
        if c % chunks_per_seq == 0:
            states = [sout_ref[seq_of(g), h] for g, h in chains]
        wqs = [_mm(jnp.concatenate([ws[i][lo:hi], qes[i][lo:hi]], axis=0), states[i]) for i in idx]
        v_news = [us[i][lo:hi] - wqs[i][:chunk] for i in idx]
        new_states = []
        for i in idx:
            v_new_parts[i].append(v_news[i])
            o_inter_parts[i].append(wqs[i][chunk:])
            g_last = gcols[i][hi - 1:hi, :]
            k_dec = ks[i][lo:hi] * jnp.exp(g_last - gcols[i][lo:hi])
            new_states.append(states[i] * jnp.exp(g_last) + lax.dot_general(
                k_dec.astype(BF16), v_news[i].astype(BF16), (((0,), (0,)), ((), ())),
                preferred_element_type=F32))
        states = new_states
        if (c + 1) % chunks_per_seq == 0:
            for i, (g, h) in enumerate(chains):
                sout_ref[seq_of(g), h] = states[i]
        yield
    o_groups = []
    for g in range(n_groups):
        o_heads = []
        for h in range(DN_HEADS):
            i = g * DN_HEADS + h
            o = (jnp.concatenate(o_inter_parts[i], axis=0)
                 + _mm(qks[i], jnp.concatenate(v_new_parts[i], axis=0)))
            ms = jnp.mean(o * o, axis=-1, keepdims=True)
            o_heads.append(o * lax.rsqrt(ms + NORM_EPS) * nw_ref[...]
                           * vals["gz"][rsl(g), h * DN_DV:(h + 1) * DN_DV])
        o_groups.append(jnp.concatenate(o_heads, axis=-1))
    o_a = o_groups[0] if n_groups == 1 else jnp.concatenate(o_groups, axis=0)

    mixed = _mm(jnp.concatenate([o_a, o_b], axis=-1), wout_ref[...]).reshape(nb, tt, d)
    gate = mod_ref[...][:, :, 2 * d:3 * d]
    y_ref[...] = _layer_norm(alpha * x_ref[...] + (1.0 + gate) * mixed, lng_ref[...], lnb_ref[...])


N_TOKEN_MIX_INPUTS = 21
N_TOKEN_MIX_OUTPUTS = 5


def _token_mix_kernel(*refs, n_cast, nb, tt, chunk, alpha):
    (x_ref, mod_ref, hdn_ref, hlru_ref, s0_ref, h0_ref, w_ref, cwdn_ref, cwlru_ref, cblru_ref, alog_ref, dtb_ref,
     wr_ref, wi_ref, br_ref, bi_ref, lam_ref, wout_ref, nw_ref, lng_ref, lnb_ref) = refs[:N_TOKEN_MIX_INPUTS]
    cast_in = refs[N_TOKEN_MIX_INPUTS:N_TOKEN_MIX_INPUTS + n_cast]
    outs = refs[N_TOKEN_MIX_INPUTS + n_cast:]
    y_ref, tdn_ref, tlru_ref, sout_ref, hout_ref = outs[:N_TOKEN_MIX_OUTPUTS]
    cast_out = outs[N_TOKEN_MIX_OUTPUTS:N_TOKEN_MIX_OUTPUTS + n_cast]
    wdn_sc, wlru_sc = outs[N_TOKEN_MIX_OUTPUTS + n_cast:]

    for src, dst in zip(cast_in, cast_out):
        dst[...] = src[...].astype(BF16)

    @pl.when(pl.program_id(1) == 0)
    def _():
        sout_ref[...] = s0_ref[0]
        hout_ref[...] = h0_ref[...]

    vals = {}
    stage1 = _inproj_stage(
        x_ref, mod_ref, hdn_ref, hlru_ref, w_ref, cwdn_ref, cwlru_ref, cblru_ref, alog_ref, dtb_ref,
        wr_ref, wi_ref, br_ref, bi_ref, lam_ref, tdn_ref, tlru_ref, wdn_sc, wlru_sc, vals, nb=nb, tt=tt)
    stage2 = _mixer_stage(vals, x_ref, mod_ref, wout_ref, nw_ref, lng_ref, lnb_ref,
                          y_ref, sout_ref, hout_ref, nb=nb, tt=tt, chunk=chunk, alpha=alpha)
    while "gb" not in vals:
        next(stage1)
    pending = [stage2, stage1]
    while pending:
        pending = [g for g in pending if next(g, StopIteration) is not StopIteration]


def _token_mix_call(x, mod, hist_dn, hist_lru, s_dn_all, layer, h0, lw_, cast=(), *, nb, tt, chunk, alpha):
    b, t, d = x.shape
    lw = hist_lru.shape[-1]
    grid = (b // nb, t // tt)
    steps = grid[0] * grid[1]
    slab = lambda a: pl.BlockSpec((a.shape[0] // steps, a.shape[1]), lambda i, j: (i * grid[1] + j, 0))
    seq_blk = lambda c: pl.BlockSpec((nb, tt, c), lambda i, j: (i, j, 0))
    per_seq = lambda r, c: pl.BlockSpec((nb, r, c), lambda i, j: (i, 0, 0))
    whole = lambda a: pl.BlockSpec(a.shape, lambda i, j: (0,) * a.ndim, pipeline_mode=pl.Buffered(1))
    state_in = pl.BlockSpec((1, nb, DN_HEADS, DN_DK, DN_DV), lambda i, j: (layer, i, 0, 0, 0))
    state_out = pl.BlockSpec((nb, DN_HEADS, DN_DK, DN_DV), lambda i, j: (i, 0, 0, 0))
    weights = [lw_[n] for n in ("w_cat", "dn_conv_w", "lru_conv_w", "lru_conv_b", "a_log", "dt_bias",
                                "w_r", "w_i", "b_r", "b_i", "lam", "w_out", "dn_norm_w", "ln1_g", "ln1_b")]
    return pl.pallas_call(
        functools.partial(_token_mix_kernel, n_cast=len(cast), nb=nb, tt=tt, chunk=chunk, alpha=alpha),
        out_shape=(
            jax.ShapeDtypeStruct((b, t, d), F32),
            jax.ShapeDtypeStruct((b, SUBLANES, DN_QKV), F32),
            jax.ShapeDtypeStruct((b, SUBLANES, lw), F32),
            jax.ShapeDtypeStruct((b, DN_HEADS, DN_DK, DN_DV), F32),
            jax.ShapeDtypeStruct((b, 1, lw), F32),
        ) + tuple(jax.ShapeDtypeStruct(a.shape, BF16) for a in cast),
        grid=grid,
        in_specs=[seq_blk(d), per_seq(1, N_MOD * d), per_seq(SUBLANES, DN_QKV), per_seq(SUBLANES, lw),
                  state_in, per_seq(1, lw)] + [whole(a) for a in weights] + [slab(a) for a in cast],
        out_specs=(seq_blk(d), per_seq(SUBLANES, DN_QKV), per_seq(SUBLANES, lw), state_out, per_seq(1, lw))
        + tuple(slab(a) for a in cast),
        scratch_shapes=[pltpu.VMEM((nb, SUBLANES + tt, DN_QKV), F32), pltpu.VMEM((nb, SUBLANES + tt, lw), F32)],
        compiler_params=_params(("arbitrary", "arbitrary")),
        name="token_mixer",
    )(x, mod, hist_dn, hist_lru, s_dn_all, h0, *weights, *cast)


def _ffn_kernel(x_ref, mod_ref, wg_ref, wu_ref, wd_ref, lng_ref, lnb_ref, *refs, alpha):
    n_cast = (len(refs) - 1) // 2
    y_ref = refs[n_cast]
    for src, dst in zip(refs[:n_cast], refs[n_cast + 1:]):
        dst[...] = src[...].astype(BF16)
    nb, tt, d = x_ref.shape
    m = mod_ref[...]
    h = (x_ref[...] * (1.0 + m[:, :, 4 * d:5 * d]) + m[:, :, 3 * d:4 * d]).reshape(nb * tt, d).astype(BF16)
    act = _silu(jnp.dot(h, wg_ref[...], preferred_element_type=F32)) * jnp.dot(
        h, wu_ref[...], preferred_element_type=F32)
    ff = _mm(act, wd_ref[...]).reshape(nb, tt, d)
    y_ref[...] = _layer_norm(alpha * x_ref[...] + (1.0 + m[:, :, 5 * d:6 * d]) * ff, lng_ref[...], lnb_ref[...])


def _ff_tile(ff):
    for n in (2, 4, 7, 8, 11, 14, 16, 22, 28):
        if ff % n == 0 and (ff // n) % LANES == 0 and ff // n <= 2048:
            return ff // n
    return ff


def _dense_ffn(x, mod, lw_, cast=(), *, nb, tt, alpha):
    b, t, d = x.shape
    nt = t // tt
    steps = (b // nb) * nt
    slab = lambda a: pl.BlockSpec((a.shape[0] // steps, a.shape[1]), lambda i: (i, 0))
    seq_blk = pl.BlockSpec((nb, tt, d), lambda i: (i // nt, i % nt, 0))
    whole = lambda a: pl.BlockSpec(a.shape, lambda i: (0,) * a.ndim)
    resident = lambda a: pl.BlockSpec(a.shape, lambda i: (0,) * a.ndim, pipeline_mode=pl.Buffered(1))
    return pl.pallas_call(
        functools.partial(_ffn_kernel, alpha=alpha),
        out_shape=(jax.ShapeDtypeStruct((b, t, d), F32),) + tuple(jax.ShapeDtypeStruct(a.shape, BF16) for a in cast),
        grid=(steps,),
        in_specs=[seq_blk, pl.BlockSpec((nb, 1, N_MOD * d), lambda i: (i // nt, 0, 0)),
                  resident(lw_["w_gate"]), resident(lw_["w_up"]), resident(lw_["w_down"]),
                  whole(lw_["ln2_g"]), whole(lw_["ln2_b"])] + [slab(a) for a in cast],
        out_specs=(seq_blk,) + tuple(slab(a) for a in cast),
        compiler_params=_params(("arbitrary",)),
        name="dense_ffn",
    )(x, mod, lw_["w_gate"], lw_["w_up"], lw_["w_down"], lw_["ln2_g"], lw_["ln2_b"], *cast)


MOE_CHUNK = 512
MOE_SLOT_TILE = 512
MOE_SUB = 128
MOE_COMBINE_BLOCK = 256
MOE_COMBINE_FANIN = 4
ROUTE_I1, ROUTE_I2, ROUTE_R1, ROUTE_R2, ROUTE_W1, ROUTE_W2 = range(6)


def _router_kernel(x_ref, mod_ref, wr_ref, base_ref, h_ref, meta_ref, meta_t_ref, blkbase_ref, cnt_ref, run_sc,
                   *, n_experts):
    nb, tt, d = x_ref.shape
    rows = nb * tt

    @pl.when(pl.program_id(0) == 0)
    def _():
        run_sc[...] = base_ref[...]

    m = mod_ref[...]
    h = (x_ref[...] * (1.0 + m[:, :, 4 * d:5 * d]) + m[:, :, 3 * d:4 * d]).reshape(rows, d)
    h_hi = h.astype(BF16)
    h_ref[...] = h_hi
    w = wr_ref[...]
    w_hi = w.astype(BF16)
    h_lo = (h - h_hi.astype(F32)).astype(BF16)
    w_lo = (w - w_hi.astype(F32)).astype(BF16)
    logits = (jnp.dot(h_hi, w_hi, preferred_element_type=F32) + jnp.dot(h_lo, w_hi, preferred_element_type=F32)
              + jnp.dot(h_hi, w_lo, preferred_element_type=F32))
    lane = lax.broadcasted_iota(jnp.int32, logits.shape, 1)
    neg = jnp.float32(-jnp.inf)
    lg = jnp.where(lane < n_experts, logits, neg)
    m1 = jnp.max(lg, axis=-1, keepdims=True)
    i1 = jnp.min(jnp.where(lg == m1, lane, LANES), axis=-1, keepdims=True)
    lg2 = jnp.where(lane == i1, neg, lg)
    m2 = jnp.max(lg2, axis=-1, keepdims=True)
    i2 = jnp.min(jnp.where(lg2 == m2, lane, LANES), axis=-1, keepdims=True)
    e2 = jnp.exp(m2 - m1)
    w1 = 1.0 / (1.0 + e2)
    w2 = e2 / (1.0 + e2)
    sel = jnp.where(lane == i1, 1.0, jnp.where(lane == i2, 1.0, 0.0))
    r = lax.broadcasted_iota(jnp.int32, (rows, rows), 0)
    c = lax.broadcasted_iota(jnp.int32, (rows, rows), 1)
    rank = _mm(jnp.where(c < r, 1.0, 0.0), sel) + run_sc[0:1, :]
    r1 = jnp.sum(jnp.where(lane == i1, rank, 0.0), axis=-1, keepdims=True)
    r2 = jnp.sum(jnp.where(lane == i2, rank, 0.0), axis=-1, keepdims=True)
    fields = (i1.astype(F32), i2.astype(F32), r1, r2, w1, w2)
    meta = jnp.zeros_like(logits)
    for k, v in enumerate(fields):
        meta = jnp.where(lane == k, v, meta)
    meta_ref[...] = meta
    meta_t_ref[...] = meta.T[0:SUBLANES, :]
    blkbase_ref[0] = run_sc[...]
    run_sc[...] = run_sc[...] + jnp.sum(sel, axis=0, keepdims=True)
    cnt_ref[...] = run_sc[...]


def _router(x, mod, w_router, base, *, nb, tt, n_experts):
    b, t, d = x.shape
    rows = nb * tt
    nt = t // tt
    nblk = (b // nb) * nt
    whole = lambda a: pl.BlockSpec(a.shape, lambda i: (0,) * a.ndim)
    return pl.pallas_call(
        functools.partial(_router_kernel, n_experts=n_experts),
        out_shape=(jax.ShapeDtypeStruct((nblk * rows, d), BF16),
                   jax.ShapeDtypeStruct((nblk * rows, LANES), F32),
                   jax.ShapeDtypeStruct((SUBLANES, nblk * rows), F32),
                   jax.ShapeDtypeStruct((nblk, SUBLANES, LANES), F32),
                   jax.ShapeDtypeStruct((SUBLANES, LANES), F32)),
        grid=(nblk,),
        in_specs=[pl.BlockSpec((nb, tt, d), lambda i: (i // nt, i % nt, 0)),
                  pl.BlockSpec((nb, 1, N_MOD * d), lambda i: (i // nt, 0, 0)),
                  whole(w_router), whole(base)],
        out_specs=(pl.BlockSpec((rows, d), lambda i: (i, 0)),
                   pl.BlockSpec((rows, LANES), lambda i: (i, 0)),
                   pl.BlockSpec((SUBLANES, rows), lambda i: (0, i)),
                   pl.BlockSpec((1, SUBLANES, LANES), lambda i: (i, 0, 0)),
                   pl.BlockSpec((SUBLANES, LANES), lambda i: (0, 0))),
        scratch_shapes=[pltpu.VMEM((SUBLANES, LANES), F32)],
        compiler_params=_params(("arbitrary",)),
        name="moe_router",
    )(x, mod, w_router, base)


def _slot_of(expert, rank, start_ref, n_experts):
    start = jnp.zeros_like(rank)
    for e in range(n_experts):
        start = jnp.where(expert == e, start_ref[e].astype(F32), start)
    return (start + rank).astype(jnp.int32)


def _gather_kernel(clo_ref, chi_ref, start_ref, *refs, group_chunks, n_experts):
    n_groups = len(group_chunks)
    m_refs = refs[:n_groups]
    h_refs = refs[n_groups:2 * n_groups]
    xs_ref, ws_ref, acc_sc, wacc_sc = refs[2 * n_groups:]
    g = pl.program_id(0)
    n_sub = MOE_SLOT_TILE // MOE_SUB
    for j in range(n_sub):
        q = g * n_sub + j
        ids = g * MOE_SLOT_TILE + j * MOE_SUB + lax.broadcasted_iota(jnp.int32, (MOE_SUB, MOE_CHUNK), 0)
        acc_sc[...] = jnp.zeros_like(acc_sc)
        wacc_sc[...] = jnp.zeros_like(wacc_sc)
        first = 0
        for m_ref, h_ref, n_chunks in zip(m_refs, h_refs, group_chunks):
            def body(c, carry, m_ref=m_ref, h_ref=h_ref):
                off = pl.multiple_of(c * MOE_CHUNK, MOE_CHUNK)
                rec = m_ref[:, pl.ds(off, MOE_CHUNK)]
                row = lambda k: rec[k:k + 1, :]
                hit1 = _slot_of(row(ROUTE_I1), row(ROUTE_R1), start_ref, n_experts) == ids
                hit2 = _slot_of(row(ROUTE_I2), row(ROUTE_R2), start_ref, n_experts) == ids
                p = jnp.where(hit1, 1.0, jnp.where(hit2, 1.0, 0.0)).astype(BF16)
                acc_sc[...] += jnp.dot(p, h_ref[pl.ds(off, MOE_CHUNK), :], preferred_element_type=F32)
                w = jnp.where(hit1, row(ROUTE_W1), 0.0) + jnp.where(hit2, row(ROUTE_W2), 0.0)
                wacc_sc[...] += jnp.broadcast_to(jnp.sum(w, axis=-1, keepdims=True), wacc_sc.shape)
                return carry

            lo = jnp.clip(clo_ref[q] - first, 0, n_chunks)
            hi = jnp.clip(chi_ref[q] - first, 0, n_chunks)
            lax.fori_loop(lo, hi, body, 0)
            first += n_chunks
        xs_ref[j * MOE_SUB:(j + 1) * MOE_SUB, :] = acc_sc[...].astype(BF16)
        ws_ref[j * MOE_SUB:(j + 1) * MOE_SUB, :] = wacc_sc[...]


def _gather_slots(c_lo, c_hi, run_start, metas_t, hs, n_tiles, n_experts):
    d = hs[0].shape[-1]
    vmem = pl.BlockSpec(memory_space=pltpu.VMEM)
    return pl.pallas_call(
        functools.partial(_gather_kernel, group_chunks=tuple(h.shape[0] // MOE_CHUNK for h in hs),
                          n_experts=n_experts),
        out_shape=(jax.ShapeDtypeStruct((n_tiles * MOE_SLOT_TILE, d), BF16),
                   jax.ShapeDtypeStruct((n_tiles * MOE_SLOT_TILE, LANES), F32)),
        grid_spec=pltpu.PrefetchScalarGridSpec(
            num_scalar_prefetch=3, grid=(n_tiles,),
            in_specs=[vmem] * (2 * len(hs)),
            out_specs=(pl.BlockSpec((MOE_SLOT_TILE, d), lambda g, lo, hi, st: (g, 0)),
                       pl.BlockSpec((MOE_SLOT_TILE, LANES), lambda g, lo, hi, st: (g, 0))),
            scratch_shapes=[pltpu.VMEM((MOE_SUB, d), F32), pltpu.VMEM((MOE_SUB, LANES), F32)]),
        compiler_params=_params(("arbitrary",)),
        name="moe_gather",
    )(c_lo, c_hi, run_start, *metas_t, *hs)


def _expert_kernel(te_ref, tv_ref, xs_ref, ws_ref, wg_ref, wu_ref, wd_ref, o_ref, *, ff_chunk):
    g = pl.program_id(0)

    @pl.when(tv_ref[g] != 0)
    def _():
        x = xs_ref[...]
        ff = wg_ref.shape[-1]
        acc = jnp.zeros(o_ref.shape, F32)
        for f0 in range(0, ff, ff_chunk):
            gate = jnp.dot(x, wg_ref[0, :, f0:f0 + ff_chunk], preferred_element_type=F32)
            up = jnp.dot(x, wu_ref[0, :, f0:f0 + ff_chunk], preferred_element_type=F32)
            acc = acc + _mm(_silu(gate) * up, wd_ref[0, f0:f0 + ff_chunk, :])
        o_ref[...] = (ws_ref[:, 0:1] * acc).astype(BF16)

    @pl.when(tv_ref[g] == 0)
    def _():
        o_ref[...] = jnp.zeros_like(o_ref)


def _expert_ffn(tile_expert, tile_valid, xs, ws, lw_):
    s_total, d = xs.shape
    _, _, ff = lw_["w_gate"].shape
    n_tiles = s_total // MOE_SLOT_TILE
    expert_blk = lambda shape: pl.BlockSpec((1,) + shape, lambda g, te, tv: (te[g], 0, 0))
    return pl.pallas_call(
        functools.partial(_expert_kernel, ff_chunk=_ff_tile(ff)),
        out_shape=jax.ShapeDtypeStruct((s_total, d), BF16),
        grid_spec=pltpu.PrefetchScalarGridSpec(
            num_scalar_prefetch=2, grid=(n_tiles,),
            in_specs=[pl.BlockSpec((MOE_SLOT_TILE, d), lambda g, te, tv: (g, 0)),
                      pl.BlockSpec((MOE_SLOT_TILE, LANES), lambda g, te, tv: (g, 0)),
                      expert_blk((d, ff)), expert_blk((d, ff)), expert_blk((ff, d))],
            out_specs=pl.BlockSpec((MOE_SLOT_TILE, d), lambda g, te, tv: (g, 0))),
        compiler_params=_params(("arbitrary",)),
        name="expert_ffn",
    )(tile_expert, tile_valid, xs, ws, lw_["w_gate"], lw_["w_up"], lw_["w_down"])


def _combine_kernel(ic_ref, ik_ref, ie_ref, if_ref, bf_ref, start_ref, x_ref, mod_ref, meta_ref, *refs, alpha,
                    n_experts):
    os_refs = refs[:MOE_COMBINE_FANIN]
    lng_ref, lnb_ref, y_ref, acc_sc, slot_sc = refs[MOE_COMBINE_FANIN:]
    nb, tt, d = x_ref.shape
    rows = nb * tt
    w = pl.program_id(0)
    flags = if_ref[w]

    @pl.when((flags & 1) != 0)
    def _():
        rec = meta_ref[...]
        col = lambda k: rec[:, k:k + 1]
        slot_sc[:, 0:1] = _slot_of(col(ROUTE_I1), col(ROUTE_R1), start_ref, n_experts)
        slot_sc[:, 1:2] = _slot_of(col(ROUTE_I2), col(ROUTE_R2), start_ref, n_experts)

    def window_sum():
        lane = lax.broadcasted_iota(jnp.int32, (rows, MOE_COMBINE_BLOCK), 1)
        s1 = slot_sc[:, 0:1]
        s2 = slot_sc[:, 1:2]
        total = None
        for j, os_ref in enumerate(os_refs):
            ids = ik_ref[w * MOE_COMBINE_FANIN + j] + lane
            ids = jnp.where(ids < ie_ref[w * MOE_COMBINE_FANIN + j], ids, -1)
            q = jnp.where(s1 == ids, 1.0, jnp.where(s2 == ids, 1.0, 0.0)).astype(BF16)
            part = jnp.dot(q, os_ref[...], preferred_element_type=F32)
            total = part if total is None else total + part
        return total

    @pl.when((flags & 1) != 0)
    def _():
        acc_sc[...] = window_sum()

    @pl.when((flags & 5) == 4)
    def _():
        acc_sc[...] += window_sum()

    @pl.when((flags & 2) != 0)
    def _():
        gate = mod_ref[...][:, :, 5 * d:6 * d]
        ff = acc_sc[...].reshape(nb, tt, d)
        y_ref[...] = _layer_norm(alpha * x_ref[...] + (1.0 + gate) * ff, lng_ref[...], lnb_ref[...])


def _combine(items, run_start, x, mod, meta, out_sorted, lw_, *, nb, tt, alpha, n_experts):
    step_chunk, win_start, win_end, step_flags, win_fetch = items
    b, t, d = x.shape
    rows = nb * tt
    nt = t // tt
    whole = lambda a: pl.BlockSpec(a.shape, lambda w, ic, ik, ie, fl, bf, st: (0,) * a.ndim)
    seq_blk = pl.BlockSpec((nb, tt, d), lambda w, ic, ik, ie, fl, bf, st: (ic[w] // nt, ic[w] % nt, 0))
    slot_blk = lambda j: pl.BlockSpec(
        (pl.Element(MOE_COMBINE_BLOCK), pl.Element(d)),
        lambda w, ic, ik, ie, fl, bf, st: (pl.multiple_of(bf[w * MOE_COMBINE_FANIN + j], 2 * SUBLANES), 0))
    return pl.pallas_call(
        functools.partial(_combine_kernel, alpha=alpha, n_experts=n_experts),
        out_shape=jax.ShapeDtypeStruct((b, t, d), F32),
        grid_spec=pltpu.PrefetchScalarGridSpec(
            num_scalar_prefetch=6, grid=(step_chunk.shape[0],),
            in_specs=[seq_blk,
                      pl.BlockSpec((nb, 1, N_MOD * d), lambda w, ic, ik, ie, fl, bf, st: (ic[w] // nt, 0, 0)),
                      pl.BlockSpec((rows, LANES), lambda w, ic, ik, ie, fl, bf, st: (ic[w], 0))]
            + [slot_blk(j) for j in range(MOE_COMBINE_FANIN)]
            + [whole(lw_["ln2_g"]), whole(lw_["ln2_b"])],
            out_specs=seq_blk,
            scratch_shapes=[pltpu.VMEM((rows, d), F32), pltpu.VMEM((rows, LANES), jnp.int32)]),
        compiler_params=_params(("arbitrary",)),
        name="moe_combine",
    )(step_chunk, win_start, win_end, step_flags, win_fetch, run_start, x, mod, meta,
      *([out_sorted] * MOE_COMBINE_FANIN), lw_["ln2_g"], lw_["ln2_b"])


def _count_le(sorted_vals, queries):
    return jnp.sum(sorted_vals[None, :] <= queries[:, None], axis=1).astype(jnp.int32)


def _combine_steps(lo, hi, n_steps):
    n_chunks, n_experts = lo.shape
    fan = MOE_COMBINE_FANIN
    align = 2 * SUBLANES
    lo_f, hi_f = lo.reshape(-1), hi.reshape(-1)
    first = (lo_f // align) * align
    count = jnp.where(hi_f > lo_f, (hi_f - first + MOE_COMBINE_BLOCK - 1) // MOE_COMBINE_BLOCK, 0)
    pair_end = jnp.cumsum(count)
    chunk_items = jnp.sum(count.reshape(n_chunks, n_experts), axis=1)
    chunk_item0 = jnp.cumsum(chunk_items) - chunk_items
    chunk_steps = (chunk_items + fan - 1) // fan
    step_end = jnp.cumsum(chunk_steps)
    total_steps = step_end[-1]
    s = jnp.arange(n_steps, dtype=jnp.int32)
    live = s < total_steps
    chunk = jnp.minimum(_count_le(step_end, s), n_chunks - 1)
    chunk = jnp.where(live, chunk, chunk[jnp.maximum(total_steps - 1, 0)])
    q = s - (step_end[chunk] - chunk_steps[chunk])
    j = q[:, None] * fan + jnp.arange(fan, dtype=jnp.int32)[None, :]
    used = live[:, None] & (j < chunk_items[chunk][:, None])
    item = jnp.where(used, chunk_item0[chunk][:, None] + j, 0).reshape(-1)
    pair = jnp.minimum(_count_le(pair_end, item), n_chunks * n_experts - 1)
    start = first[pair] + (item - (pair_end[pair] - count[pair])) * MOE_COMBINE_BLOCK
    fetch = jnp.where(used.reshape(-1), start, jnp.repeat(start.reshape(-1, fan)[:, 0], fan))
    end = jnp.where(used.reshape(-1), hi_f[pair], 0)
    flags = (jnp.where(live & (q == 0), 1, 0) + jnp.where(live & (q == chunk_steps[chunk] - 1), 2, 0)
             + jnp.where(live, 4, 0))
    as_i32 = lambda a: a.astype(jnp.int32)
    return as_i32(chunk), as_i32(start), as_i32(end), as_i32(flags), as_i32(fetch)


def _moe_layer(xs_in, mods, lw_, tilings, alpha):
    n_experts = lw_["w_gate"].shape[0]
    base = jnp.zeros((SUBLANES, LANES), F32)
    hs, metas, metas_t, bases = [], [], [], []
    for x, mod, (nb, tt) in zip(xs_in, mods, tilings):
        assert nb * tt == MOE_CHUNK and (x.shape[0] * x.shape[1]) % MOE_CHUNK == 0
        h, meta, meta_t, blkbase, base = _router(x, mod, lw_["w_router"], base, nb=nb, tt=tt,
                                                 n_experts=n_experts)
        hs.append(h)
        metas.append(meta)
        metas_t.append(meta_t)
        bases.append(blkbase[:, 0, :n_experts])
    n = sum(m.shape[0] for m in metas)
    cum = jnp.concatenate(bases + [base[0:1, :n_experts]], axis=0).astype(jnp.int32)
    counts = cum[-1]
    sizes = ((counts + MOE_SLOT_TILE - 1) // MOE_SLOT_TILE) * MOE_SLOT_TILE
    run_end = jnp.cumsum(sizes)
    run_start = (run_end - sizes).astype(jnp.int32)

    n_tiles = (TOP_K * n + MOE_SLOT_TILE - 1) // MOE_SLOT_TILE + n_experts + 1
    tile_start = jnp.arange(n_tiles, dtype=jnp.int32) * MOE_SLOT_TILE
    tile_expert = jnp.minimum(_count_le(run_end, tile_start), n_experts - 1)
    tile_valid = (tile_start < run_end[-1]).astype(jnp.int32)
    n_sub = MOE_SLOT_TILE // MOE_SUB
    sub_expert = jnp.repeat(tile_expert, n_sub)
    sub_rank0 = jnp.arange(n_tiles * n_sub, dtype=jnp.int32) * MOE_SUB - run_start[sub_expert]
    cum_sub = cum[:, sub_expert]
    sub_valid = jnp.repeat(tile_valid, n_sub)
    c_lo = jnp.sum(cum_sub[1:] <= sub_rank0[None, :], axis=0).astype(jnp.int32) * sub_valid
    c_hi = jnp.sum(cum_sub[:-1] < sub_rank0[None, :] + MOE_SUB, axis=0).astype(jnp.int32) * sub_valid

    x_sorted, w_sorted = _gather_slots(c_lo, c_hi, run_start, metas_t, hs, n_tiles, n_experts)
    out_sorted = _expert_ffn(tile_expert, tile_valid, x_sorted, w_sorted, lw_)

    outs = []
    chunk0 = 0
    for x, mod, meta, (nb, tt) in zip(xs_in, mods, metas, tilings):
        nc = x.shape[0] * x.shape[1] // MOE_CHUNK
        lo = run_start[None, :] + cum[chunk0:chunk0 + nc]
        hi = run_start[None, :] + cum[chunk0 + 1:chunk0 + nc + 1]
        max_items = nc * n_experts + (TOP_K * nc * MOE_CHUNK) // MOE_COMBINE_BLOCK + 2 * n_experts
        items = _combine_steps(lo, hi, max_items // MOE_COMBINE_FANIN + nc)
        outs.append(_combine(items, run_start, x, mod, meta, out_sorted, lw_, nb=nb, tt=tt, alpha=alpha,
                             n_experts=n_experts))
        chunk0 += nc
    return outs


def _pad_lanes(v, width=LANES):
    return jnp.pad(v, ((0, 0), (0, width - v.shape[-1])))


def _block_diag_halves(w):
    nblk, c, _ = w.shape
    half = nblk // 2
    out = jnp.zeros((2, half * c, half * c), w.dtype)
    for i in range(nblk):
        j = i % half
        out = out.at[i // half, j * c:(j + 1) * c, j * c:(j + 1) * c].set(w[i])
    return out


def _layer_weights(l, p, d, lw):
    w_in = p["w_in"][l]
    a_off = DN_QKV
    z_off = a_off + 2 * DN_HEADS
    x_off = z_off + DN_WIDTH
    y_off = x_off + lw
    w_ab = _pad_lanes(w_in[:, a_off:z_off])
    w_cat = jnp.concatenate([w_in[:, :DN_QKV], w_in[:, z_off:y_off + lw], w_ab], axis=1).astype(BF16)
    out = {
        "w_cat": w_cat,
        "dn_conv_w": p["dn_conv_w"][l],
        "lru_conv_w": p["lru_conv_w"][l],
        "lru_conv_b": p["lru_conv_b"][l][None],
        "a_log": _pad_lanes(p["dn_a_log"][l][None]),
        "dt_bias": _pad_lanes(p["dn_dt_bias"][l][None]),
        "w_r": _block_diag_halves(p["lru_w_r"][l]).astype(BF16),
        "w_i": _block_diag_halves(p["lru_w_i"][l]).astype(BF16),
        "b_r": p["lru_b_r"][l][None],
        "b_i": p["lru_b_i"][l][None],
        "lam": p["lru_lambda"][l][None],
        "w_out": p["w_out"][l].astype(BF16),
        "dn_norm_w": p["dn_norm_w"][l][None],
        "ln1_g": p["ln1_g"][l][None],
        "ln1_b": p["ln1_b"][l][None],
        "ln2_g": p["ln2_g"][l][None],
        "ln2_b": p["ln2_b"][l][None],
    }
    j = l // 2
    if l % 2 == 0:
        out.update(w_gate=p["ffn_w_gate"][j], w_up=p["ffn_w_up"][j], w_down=p["ffn_w_down"][j])
    else:
        out.update(w_router=_pad_lanes(p["moe_w_router"][j]),
                   w_gate=p["moe_w_gate"][j], w_up=p["moe_w_up"][j], w_down=p["moe_w_down"][j])
    return out


CHANNEL_MIX_WEIGHTS = ("w_gate", "w_up", "w_down")


LARGE_SLAB_BYTES = 64 * 1024 * 1024


def _cast_plan(weights, host_steps):
    hosts = list(host_steps)
    plan = {h: [] for h in hosts}
    for k in reversed(range(len(weights))):
        dense = ("ffn", k) in host_steps
        last = hosts.index(("ffn", k)) if dense else hosts.index(("mixer", k)) + 1
        for n in CHANNEL_MIX_WEIGHTS:
            a = weights[k][n].reshape(-1, weights[k][n].shape[-1])
            large = a.size * 4 > LARGE_SLAB_BYTES
            for h in reversed(hosts[:last]):
                fits = a.shape[0] % (host_steps[h] * 2 * SUBLANES) == 0
                busy = large and any(s.size * 4 > LARGE_SLAB_BYTES for _, _, s in plan[h])
                if fits and not busy:
                    plan[h].append((k, n, a))
                    break
    return plan


def _tiling(b, t):
    if t >= MXU_DIM:
        tt = MXU_DIM
        return dict(mixer=(2 if b % 2 == 0 else 1, tt, min(PROMPT_CHUNK, tt)), ffn=(1, min(t, 2 * MXU_DIM)))
    assert t == SUBLANES, "short sequences must be exactly one sublane tile long"
    return dict(mixer=(min(b, 16), t, t), ffn=(min(b, 64), t))


def _token_mix(x, mod, conv_dn, s_dn_all, layer, conv_lru, s_lru, lw_, til, alpha, cast=()):
    pad_hist = lambda c: jnp.pad(c, ((0, 0), (SUBLANES - (CONV_W - 1), 0), (0, 0)))
    nb, tt, chunk = til["mixer"]
    x, tail_dn, tail_lru, s_new, h_new, *converted = _token_mix_call(
        x, mod, pad_hist(conv_dn), pad_hist(conv_lru), s_dn_all, layer, s_lru[:, None, :], lw_, cast,
        nb=nb, tt=tt, chunk=chunk, alpha=alpha)
    return (x, tail_dn[:, SUBLANES - (CONV_W - 1):, :], s_new, tail_lru[:, SUBLANES - (CONV_W - 1):, :],
            h_new[:, 0, :]), converted


def kernel(x_prompt, x_sample, cache_dn_conv, state_dn, cache_lru_conv, state_lru, c_prompt, c_sample,
           w_ada, b_ada, w_in, dn_conv_w, dn_a_log, dn_dt_bias, dn_norm_w,
           lru_conv_w, lru_conv_b, lru_w_r, lru_b_r, lru_w_i, lru_b_i, lru_lambda, w_out,
           ln1_g, ln1_b, ln2_g, ln2_b, ffn_w_gate, ffn_w_up, ffn_w_down,
           moe_w_router, moe_w_gate, moe_w_up, moe_w_down):
    p = dict(w_in=w_in, dn_conv_w=dn_conv_w, dn_a_log=dn_a_log, dn_dt_bias=dn_dt_bias, dn_norm_w=dn_norm_w,
             lru_conv_w=lru_conv_w, lru_conv_b=lru_conv_b, lru_w_r=lru_w_r, lru_b_r=lru_b_r,
             lru_w_i=lru_w_i, lru_b_i=lru_b_i, lru_lambda=lru_lambda, w_out=w_out,
             ln1_g=ln1_g, ln1_b=ln1_b, ln2_g=ln2_g, ln2_b=ln2_b,
             ffn_w_gate=ffn_w_gate, ffn_w_up=ffn_w_up, ffn_w_down=ffn_w_down,
             moe_w_router=moe_w_router, moe_w_gate=moe_w_gate, moe_w_up=moe_w_up, moe_w_down=moe_w_down)
    depth, d, _ = w_ada.shape
    bp = x_prompt.shape[0]
    bs = x_sample.shape[0]
    lw = cache_lru_conv.shape[-1]
    alpha = (2 * depth) ** 0.25
    weights = [_layer_weights(l, p, d, lw) for l in range(depth)]

    c_all = jnp.concatenate([c_prompt, c_sample], axis=0)
    mod_all = _modulation(c_all, w_ada, b_ada)
    groups = [
        dict(x=x_prompt, rows=slice(0, bp), conv_dn=jnp.zeros((depth, bp, CONV_W - 1, DN_QKV), F32),
             s_dn=jnp.zeros((depth, bp, DN_HEADS, DN_DK, DN_DV), F32),
             conv_lru=jnp.zeros((depth, bp, CONV_W - 1, lw), F32), s_lru=jnp.zeros((depth, bp, lw), F32)),
        dict(x=x_sample, rows=slice(bp, bp + bs), conv_dn=cache_dn_conv, s_dn=state_dn,
             conv_lru=cache_lru_conv, s_lru=state_lru),
    ]
    for g in groups:
        g["til"] = _tiling(g["x"].shape[0], g["x"].shape[1])
        g["new"] = [[], [], [], []]
    t_p = x_prompt.shape[1]
    nb_m, tt_m, _ = groups[0]["til"]["mixer"]
    nb_f, tt_f = groups[0]["til"]["ffn"]
    host_steps = {}
    for l in range(depth):
        host_steps[("mixer", l)] = (bp // nb_m) * (t_p // tt_m)
        if l % 2 == 0:
            host_steps[("ffn", l)] = (bp // nb_f) * (t_p // tt_f)
    plan = _cast_plan(weights, host_steps)
    planned = {(k, n) for slabs in plan.values() for k, n, _ in slabs}
    for k, w in enumerate(weights):
        for n in CHANNEL_MIX_WEIGHTS:
            if (k, n) not in planned:
                w[n] = w[n].astype(BF16)

    def adopt(slabs, converted):
        for (k, n, _), c in zip(slabs, converted):
            weights[k][n] = c.reshape(weights[k][n].shape)

    for l in range(depth):
        lw_ = weights[l]
        mods = [mod_all[l, g["rows"]][:, None, :] for g in groups]
        for gi, (g, mod) in enumerate(zip(groups, mods)):
            slabs = plan[("mixer", l)] if gi == 0 else []
            res, converted = _token_mix(g["x"], mod, g["conv_dn"][l], g["s_dn"], l, g["conv_lru"][l],
                                        g["s_lru"][l], lw_, g["til"], alpha,
                                        cast=tuple(a for _, _, a in slabs))
            adopt(slabs, converted)
            g["x"] = res[0]
            for acc, new in zip(g["new"], res[1:]):
                acc.append(new)
        if l % 2 == 0:
            for gi, (g, mod) in enumerate(zip(groups, mods)):
                nb, tt = g["til"]["ffn"]
                slabs = plan[("ffn", l)] if gi == 0 else []
                g["x"], *converted = _dense_ffn(g["x"], mod, lw_, tuple(a for _, _, a in slabs),
                                                nb=nb, tt=tt, alpha=alpha)
                adopt(slabs, converted)
        else:
            xs = _moe_layer([g["x"] for g in groups], mods, lw_, [g["til"]["ffn"] for g in groups], alpha)
            for g, x in zip(groups, xs):
                g["x"] = x
    states = [jnp.stack(acc) for g in groups for acc in g["new"]]
    return (groups[0]["x"], groups[1]["x"]) + tuple(states)
```
